```python
import math, functools
import jax, jax.numpy as jnp
from jax import lax
import numpy as np

D_MODEL = 1024
BATCH = 16
SEQ = 256
DEPTH = 2
DEC_BATCH = 8
DEC_SEQ = 2048
PAST_LEN = 512

GRID_W = 64
HEAD_DIM = 64
AXIS_DIM = HEAD_DIM // 2
A_HEADS = 4
A_VDIM = 2 * HEAD_DIM
CONV_DIM = 512
CONV_WIDTH = 3
C_HEADS = 16
C_KV_HEADS = 4
WINDOW = 128
Q_BLOCK = 128
D_FF = 2816
ROPE_BASE = 10000.0
N_MOD = 9
EPS = 1e-6
N_EVEN = (DEPTH + 1) // 2
N_ODD = DEPTH // 2
A_QK = A_HEADS * 2 * HEAD_DIM
A_V = A_HEADS * A_VDIM
EVEN_SPLITS = [A_QK, 2 * A_QK, 2 * A_QK + A_V, 2 * A_QK + A_V + CONV_DIM, 2 * A_QK + A_V + 2 * CONV_DIM]
EVEN_IN = 2 * A_QK + A_V + 3 * CONV_DIM
EVEN_OUT = A_V + CONV_DIM
ODD_IN = (C_HEADS + 2 * C_KV_HEADS) * HEAD_DIM
ODD_OUT = C_HEADS * HEAD_DIM

kernel_name = "hybrid_diff_conv_swa_prefix_dit_step"


def _rms_norm(x, g):
    xf = x.astype(jnp.float32)
    y = xf * lax.rsqrt(jnp.mean(xf * xf, axis=-1, keepdims=True) + EPS)
    return (y * g.astype(jnp.float32)).astype(x.dtype)


def _adaln(cvec, w, b):
    m = jax.nn.silu(cvec) @ w + b
    return m.reshape(cvec.shape[0], N_MOD, D_MODEL)


def _swiglu(h, w13, w2):
    gate, up = jnp.split(h @ w13, 2, axis=-1)
    return (jax.nn.silu(gate) * up) @ w2


def _axial_rope(n):
    rows = n // GRID_W
    row = jnp.repeat(jnp.arange(rows, dtype=jnp.float32), GRID_W)
    col = jnp.tile(jnp.arange(GRID_W, dtype=jnp.float32), rows)
    inv = ROPE_BASE ** (-jnp.arange(0, AXIS_DIM, 2, dtype=jnp.float32) / AXIS_DIM)
    ang_r = row[:, None] * inv[None, :]
    ang_c = col[:, None] * inv[None, :]
    ang = jnp.concatenate([ang_r, ang_r, ang_c, ang_c], axis=-1)
    return jnp.cos(ang), jnp.sin(ang)


def _rope(x, cos, sin):
    shape = (1, x.shape[1]) + (1,) * (x.ndim - 3) + (HEAD_DIM,)
    cos = cos.reshape(shape).astype(x.dtype)
    sin = sin.reshape(shape).astype(x.dtype)
    x1, x2, x3, x4 = jnp.split(x, 4, axis=-1)
    rot = jnp.concatenate([-x2, x1, -x4, x3], axis=-1)
    return x * cos + rot * sin


def _short_conv(u, w):
    up = jnp.pad(u, ((0, 0), (1, 1), (0, 0)))
    return w[0] * up[:, :-2] + w[1] * up[:, 1:-1] + w[2] * up[:, 2:]


def _query_blocks(q):
    b, s = q.shape[:2]
    nb = s // Q_BLOCK
    return jnp.moveaxis(q.reshape((b, nb, Q_BLOCK) + q.shape[2:]), 1, 0)


def _merge_blocks(o):
    nb, b = o.shape[:2]
    return jnp.moveaxis(o, 0, 1).reshape((b, nb * Q_BLOCK) + o.shape[3:])


def _diff_attention(q, k, v, lam):
    scale = HEAD_DIM ** -0.5

    def block(qi):
        s = jnp.einsum("bqhmd,bkhmd->bhmqk", qi, k, preferred_element_type=jnp.float32) * scale
        p = jax.nn.softmax(s, axis=-1)
        a = p[:, :, 0] - lam * p[:, :, 1]
        return jnp.einsum("bhqk,bkhe->bqhe", a.astype(v.dtype), v, preferred_element_type=jnp.float32).astype(v.dtype)

    return _merge_blocks(lax.map(block, _query_blocks(q)))


def _sink_column(sink, s):
    col = sink.astype(jnp.float32).reshape(1, C_KV_HEADS, -1, 1, 1)
    return jnp.broadcast_to(col, s.shape[:-1] + (1,))


def _dense_sink_attention(q, k, v, sink):
    b, s_len, h, dh = q.shape
    qg = q.reshape(b, s_len, C_KV_HEADS, h // C_KV_HEADS, dh)
    scale = HEAD_DIM ** -0.5

    def block(qi):
        s = jnp.einsum("bqgrd,btgd->bgrqt", qi, k, preferred_element_type=jnp.float32) * scale
        p = jax.nn.softmax(jnp.concatenate([s, _sink_column(sink, s)], axis=-1), axis=-1)[..., :-1]
        return jnp.einsum("bgrqt,btgd->bqgrd", p.astype(v.dtype), v, preferred_element_type=jnp.float32).astype(v.dtype)

    return _merge_blocks(lax.map(block, _query_blocks(qg))).reshape(b, s_len, h, dh)


def _window_sink_attention(q, k, v, k_ctx, v_ctx, sink):
    b, n, h, dh = q.shape
    p_len = k_ctx.shape[1]
    nb = n // Q_BLOCK
    qg = q.reshape(b, n, C_KV_HEADS, h // C_KV_HEADS, dh)
    pad = ((0, 0), (Q_BLOCK, Q_BLOCK), (0, 0), (0, 0))
    k_pad = jnp.pad(k, pad)
    v_pad = jnp.pad(v, pad)
    offs_q = jnp.arange(Q_BLOCK)
    offs_k = jnp.arange(3 * Q_BLOCK) - Q_BLOCK
    scale = HEAD_DIM ** -0.5

    def block(args):
        i, qi = args
        start = i * Q_BLOCK
        ks = lax.dynamic_slice_in_dim(k_pad, start, 3 * Q_BLOCK, axis=1)
        vs = lax.dynamic_slice_in_dim(v_pad, start, 3 * Q_BLOCK, axis=1)
        qpos = start + offs_q
        kpos = start + offs_k
        valid = (jnp.abs(qpos[:, None] - kpos[None, :]) <= WINDOW) & (kpos[None, :] >= 0) & (kpos[None, :] < n)
        s_c = jnp.einsum("bqgrd,bcgd->bgrqc", qi, k_ctx, preferred_element_type=jnp.float32) * scale
        s_l = jnp.einsum("bqgrd,bkgd->bgrqk", qi, ks, preferred_element_type=jnp.float32) * scale
        s_l = jnp.where(valid, s_l, -jnp.inf)
        p = jax.nn.softmax(jnp.concatenate([s_c, s_l, _sink_column(sink, s_c)], axis=-1), axis=-1)
        o = (jnp.einsum("bgrqc,bcgd->bqgrd", p[..., :p_len].astype(v.dtype), v_ctx, preferred_element_type=jnp.float32)
             + jnp.einsum("bgrqk,bkgd->bqgrd", p[..., p_len:-1].astype(v.dtype), vs, preferred_element_type=jnp.float32))
        return o.astype(v.dtype)

    o = lax.map(block, (jnp.arange(nb), _query_blocks(qg)))
    return _merge_blocks(o).reshape(b, n, h, dh)


def _even_mixer(h, w_in, w_out, qk_g, lam_vec, subln_g, conv_w, lam_init, rope=None, kv_ctx=None):
    b, s, _ = h.shape
    q, k, v, bg, cg, xin = jnp.split(h @ w_in, EVEN_SPLITS, axis=-1)
    q = _rms_norm(q.reshape(b, s, A_HEADS, 2, HEAD_DIM), qk_g[0])
    k = _rms_norm(k.reshape(b, s, A_HEADS, 2, HEAD_DIM), qk_g[1])
    v = v.reshape(b, s, A_HEADS, A_VDIM)
    lf = lam_vec.astype(jnp.float32)
    lam = jnp.exp(jnp.sum(lf[0] * lf[1])) - jnp.exp(jnp.sum(lf[2] * lf[3])) + lam_init
    if kv_ctx is None:
        aux = (k.reshape(b, s, A_HEADS, 2 * HEAD_DIM), v)
        o = _diff_attention(q, k, v, lam)
    else:
        cos, sin = rope
        k_ctx, v_ctx = kv_ctx
        k_all = jnp.concatenate([k_ctx.reshape(b, -1, A_HEADS, 2, HEAD_DIM).astype(k.dtype), _rope(k, cos, sin)], axis=1)
        v_all = jnp.concatenate([v_ctx.astype(v.dtype), v], axis=1)
        o = _diff_attention(_rope(q, cos, sin), k_all, v_all, lam)
        aux = None
    o = _rms_norm(o, subln_g) * (1.0 - lam_init)
    y = bg * _short_conv(cg * xin, conv_w)
    out = jnp.concatenate([o.reshape(b, s, A_V), y], axis=-1) @ w_out
    return out, aux


def _odd_mixer(h, w_in, w_out, qk_g, sink, rope=None, kv_ctx=None):
    b, s, _ = h.shape
    q, k, v = jnp.split(h @ w_in, [C_HEADS * HEAD_DIM, (C_HEADS + C_KV_HEADS) * HEAD_DIM], axis=-1)
    q = _rms_norm(q.reshape(b, s, C_HEADS, HEAD_DIM), qk_g[0])
    k = _rms_norm(k.reshape(b, s, C_KV_HEADS, HEAD_DIM), qk_g[1])
    v = v.reshape(b, s, C_KV_HEADS, HEAD_DIM)
    if kv_ctx is None:
        aux = (k, v)
        o = _dense_sink_attention(q, k, v, sink)
    else:
        cos, sin = rope
        k_ctx, v_ctx = kv_ctx
        o = _window_sink_attention(_rope(q, cos, sin), _rope(k, cos, sin), v,
                                   k_ctx.astype(k.dtype), v_ctx.astype(v.dtype), sink)
        aux = None
    return o.reshape(b, s, ODD_OUT) @ w_out, aux


def _layer(x, m, g, w13, w2, mixer):
    m = m[:, None]
    h = _rms_norm(x, g[0]) * (1 + m[:, :, 1]) + m[:, :, 0]
    x = x + 0.5 * m[:, :, 2] * _swiglu(h, w13[0], w2[0])
    h = _rms_norm(x, g[1]) * (1 + m[:, :, 4]) + m[:, :, 3]
    mix, aux = mixer(h)
    x = x + m[:, :, 5] * mix
    h = _rms_norm(x, g[2]) * (1 + m[:, :, 7]) + m[:, :, 6]
    x = x + 0.5 * m[:, :, 8] * _swiglu(h, w13[1], w2[1])
    return x, aux


def setup_inputs(seed: int = 0) -> dict:
    key = jax.random.key(seed)
    ks = jax.random.split(key, 24)

    def nrm(k, shape, s=1.0):
        return jax.random.normal(k, shape, jnp.float32) * s

    return {
        "x_prompt": nrm(ks[0], (BATCH, SEQ, D_MODEL)),
        "x_sample": nrm(ks[1], (DEC_BATCH, DEC_SEQ, D_MODEL)),
        "cache_even_k": nrm(ks[2], (DEC_BATCH, N_EVEN, PAST_LEN, A_HEADS, 2 * HEAD_DIM)),
        "cache_even_v": nrm(ks[3], (DEC_BATCH, N_EVEN, PAST_LEN, A_HEADS, A_VDIM)),
        "cache_odd_k": nrm(ks[4], (DEC_BATCH, N_ODD, PAST_LEN, C_KV_HEADS, HEAD_DIM)),
        "cache_odd_v": nrm(ks[5], (DEC_BATCH, N_ODD, PAST_LEN, C_KV_HEADS, HEAD_DIM)),
        "c": nrm(ks[6], (DEC_BATCH, D_MODEL)),
        "c_ctx": nrm(ks[7], (D_MODEL,)),
        "w_mod": nrm(ks[8], (DEPTH, D_MODEL, N_MOD * D_MODEL), 0.5 * D_MODEL ** -0.5),
        "b_mod": nrm(ks[9], (DEPTH, N_MOD * D_MODEL), 0.02),
        "norm_g": 1.0 + nrm(ks[10], (DEPTH, 3, D_MODEL), 0.02),
        "ffn_w13": nrm(ks[11], (DEPTH, 2, D_MODEL, 2 * D_FF), D_MODEL ** -0.5),
        "ffn_w2": nrm(ks[12], (DEPTH, 2, D_FF, D_MODEL), D_FF ** -0.5),
        "even_w_in": nrm(ks[13], (N_EVEN, D_MODEL, EVEN_IN), D_MODEL ** -0.5),
        "even_w_out": nrm(ks[14], (N_EVEN, EVEN_OUT, D_MODEL), EVEN_OUT ** -0.5),
        "even_qk_norm": 1.0 + nrm(ks[15], (N_EVEN, 2, HEAD_DIM), 0.02),
        "even_lambda": nrm(ks[16], (N_EVEN, 4, HEAD_DIM), 0.1),
        "even_subln": 1.0 + nrm(ks[17], (N_EVEN, A_VDIM), 0.02),
        "even_conv_w": nrm(ks[18], (N_EVEN, CONV_WIDTH, CONV_DIM), CONV_WIDTH ** -0.5),
        "odd_w_in": nrm(ks[19], (N_ODD, D_MODEL, ODD_IN), D_MODEL ** -0.5),
        "odd_w_out": nrm(ks[20], (N_ODD, ODD_OUT, D_MODEL), ODD_OUT ** -0.5),
        "odd_qk_norm": 1.0 + nrm(ks[21], (N_ODD, 2, HEAD_DIM), 0.02),
        "odd_sink": nrm(ks[22], (N_ODD, C_HEADS), 0.5),
    }


def reference(x_prompt, x_sample, cache_even_k, cache_even_v, cache_odd_k, cache_odd_v, c, c_ctx,
              w_mod, b_mod, norm_g, ffn_w13, ffn_w2,
              even_w_in, even_w_out, even_qk_norm, even_lambda, even_subln, even_conv_w,
              odd_w_in, odd_w_out, odd_qk_norm, odd_sink):
    rope = _axial_rope(x_sample.shape[1])
    yp, ys = x_prompt, x_sample
    even_k, even_v, odd_k, odd_v = [], [], [], []
    for l in range(DEPTH):
        m_ctx = _adaln(c_ctx[None, :], w_mod[l], b_mod[l])
        m_lat = _adaln(c, w_mod[l], b_mod[l])
        if l % 2 == 0:
            e = l // 2
            lam_init = 0.8 - 0.6 * math.exp(-0.3 * l)
            mix = functools.partial(_even_mixer, w_in=even_w_in[e], w_out=even_w_out[e], qk_g=even_qk_norm[e],
                                    lam_vec=even_lambda[e], subln_g=even_subln[e], conv_w=even_conv_w[e],
                                    lam_init=lam_init)
            yp, (kc, vc) = _layer(yp, m_ctx, norm_g[l], ffn_w13[l], ffn_w2[l], mix)
            ys, _ = _layer(ys, m_lat, norm_g[l], ffn_w13[l], ffn_w2[l],
                           functools.partial(mix, rope=rope, kv_ctx=(cache_even_k[:, e], cache_even_v[:, e])))
            even_k.append(kc)
            even_v.append(vc)
        else:
            o = l // 2
            mix = functools.partial(_odd_mixer, w_in=odd_w_in[o], w_out=odd_w_out[o], qk_g=odd_qk_norm[o],
                                    sink=odd_sink[o])
            yp, (kc, vc) = _layer(yp, m_ctx, norm_g[l], ffn_w13[l], ffn_w2[l], mix)
            ys, _ = _layer(ys, m_lat, norm_g[l], ffn_w13[l], ffn_w2[l],
                           functools.partial(mix, rope=rope, kv_ctx=(cache_odd_k[:, o], cache_odd_v[:, o])))
            odd_k.append(kc)
            odd_v.append(vc)
    return (yp, ys, jnp.stack(even_k, axis=1), jnp.stack(even_v, axis=1), jnp.stack(odd_k, axis=1), jnp.stack(odd_v, axis=1))
```

```python
import functools
import math

import jax
import jax.numpy as jnp
from jax import lax
from jax.experimental import pallas as pl
from jax.experimental.pallas import tpu as pltpu

F32 = jnp.float32
BF16 = jnp.bfloat16

D_MODEL = 1024
GRID_W = 64
HEAD_DIM = 64
AXIS_DIM = HEAD_DIM // 2
A_HEADS = 4
A_VDIM = 2 * HEAD_DIM
CONV_DIM = 512
C_HEADS = 16
C_KV_HEADS = 4
GROUP = C_HEADS // C_KV_HEADS
WINDOW = 128
D_FF = 2816
ROPE_BASE = 10000.0
N_MOD = 9
EPS = 1e-6
A_QK = A_HEADS * 2 * HEAD_DIM
A_V = A_HEADS * A_VDIM
EVEN_IN = 2 * A_QK + A_V + 3 * CONV_DIM
ODD_IN = (C_HEADS + 2 * C_KV_HEADS) * HEAD_DIM
QK_SCALE = HEAD_DIM ** -0.5

LANES = 128
SUBLANES = 8
MOD_ROWS = 16
VMEM_LIMIT_BYTES = 56 * 2 ** 20

TOKEN_TILE = 512
FF_TILE = 1408
EVEN_Q_TILE = 256
ODD_Q_TILE = 128


def _params(*semantics):
    return pltpu.CompilerParams(dimension_semantics=semantics, vmem_limit_bytes=VMEM_LIMIT_BYTES)


def _sigmoid(x):
    return 1.0 / (1.0 + jnp.exp(-x))


def _mod_norm(x, g, shift, scale):
    y = x * lax.rsqrt(jnp.mean(x * x, axis=-1, keepdims=True) + EPS)
    return (y * g) * (1.0 + scale) + shift


def _mod_row(mod_ref, k, r):
    return mod_ref[k, pl.ds(r, 1), :]


def _row_fn(row0, tokens_per_row, tile):
    return lambda i: row0 + (i * tile) // tokens_per_row


def _lane_lo(shape):
    return lax.broadcasted_iota(jnp.int32, shape, len(shape) - 1) < HEAD_DIM


def _seg_rms(xs, gain):
    lo = _lane_lo(xs.shape)
    sq = xs * xs
    s_lo = jnp.sum(jnp.where(lo, sq, 0.0), axis=-1, keepdims=True)
    s_hi = jnp.sum(jnp.where(lo, 0.0, sq), axis=-1, keepdims=True)
    inv = jnp.where(lo, lax.rsqrt(s_lo * (1.0 / HEAD_DIM) + EPS), lax.rsqrt(s_hi * (1.0 / HEAD_DIM) + EPS))
    return (xs * inv) * gain


def _rope(xs, cos, sin_signed):
    lane = lax.broadcasted_iota(jnp.int32, xs.shape, 1)
    first = (lane & (AXIS_DIM // 2)) == 0
    partner = jnp.where(first, pltpu.roll(xs, LANES - AXIS_DIM // 2, 1), pltpu.roll(xs, AXIS_DIM // 2, 1))
    return xs * cos + partner * sin_signed


def _dot_t(a, b):
    return lax.dot_general(a, b, (((1,), (1,)), ((), ())), preferred_element_type=F32)


def _dot(a, b):
    return jnp.dot(a, b, preferred_element_type=F32)


def _adaln_kernel(c_ref, w_ref, b_ref, o_ref):
    c = c_ref[...]
    s = c * _sigmoid(c)
    o_ref[0, 0] = _dot(s.astype(BF16), w_ref[0].astype(BF16)) + b_ref[0]


def _adaln(c_rows, w_mod, b_mod):
    depth = w_mod.shape[0]
    b3 = b_mod.reshape(depth * N_MOD, 1, D_MODEL)
    return pl.pallas_call(
        _adaln_kernel,
        out_shape=jax.ShapeDtypeStruct((depth, N_MOD, MOD_ROWS, D_MODEL), F32),
        grid=(depth, N_MOD),
        in_specs=[
            pl.BlockSpec((MOD_ROWS, D_MODEL), lambda l, j: (0, 0)),
            pl.BlockSpec((1, D_MODEL, D_MODEL), lambda l, j: (l, 0, j)),
            pl.BlockSpec((1, 1, D_MODEL), lambda l, j: (l * N_MOD + j, 0, 0)),
        ],
        out_specs=pl.BlockSpec((1, 1, MOD_ROWS, D_MODEL), lambda l, j: (l, j, 0, 0)),
        compiler_params=_params("parallel", "parallel"),
        name="adaln",
    )(c_rows, w_mod, b3)


def _ffn_kernel(x_ref, mod_ref, g_ref, w13_ref, w2_ref, o_ref, h_scr, acc_scr, *, row_fn, k0, tf):
    j = pl.program_id(1)
    r = row_fn(pl.program_id(0))

    @pl.when(j == 0)
    def _():
        h = _mod_norm(x_ref[...], g_ref[...], _mod_row(mod_ref, k0, r), _mod_row(mod_ref, k0 + 1, r))
        h_scr[...] = h.astype(BF16)

    a = _dot(h_scr[...], w13_ref[0])
    gate = a[:, :tf]
    up = a[:, tf:]
    act = (gate * _sigmoid(gate)) * up
    contrib = _dot(act.astype(BF16), w2_ref[...])

    @pl.when(j == 0)
    def _():
        acc_scr[...] = contrib

    @pl.when(j > 0)
    def _():
        acc_scr[...] += contrib

    @pl.when(j == pl.num_programs(1) - 1)
    def _():
        o_ref[...] = x_ref[...] + (0.5 * _mod_row(mod_ref, k0 + 2, r)) * acc_scr[...]


def _ffn(x, mod, g, w13r, w2, *, k0, row_fn, tm):
    t = x.shape[0]
    nch, _, tf2 = w13r.shape
    tf = tf2 // 2
    return pl.pallas_call(
        functools.partial(_ffn_kernel, row_fn=row_fn, k0=k0, tf=tf),
        out_shape=jax.ShapeDtypeStruct((t, D_MODEL), F32),
        grid=(t // tm, nch),
        in_specs=[
            pl.BlockSpec((tm, D_MODEL), lambda i, j: (i, 0)),
            pl.BlockSpec((N_MOD, MOD_ROWS, D_MODEL), lambda i, j: (0, 0, 0)),
            pl.BlockSpec((1, D_MODEL), lambda i, j: (0, 0)),
            pl.BlockSpec((1, D_MODEL, tf2), lambda i, j: (j, 0, 0)),
            pl.BlockSpec((tf, D_MODEL), lambda i, j: (j, 0)),
        ],
        out_specs=pl.BlockSpec((tm, D_MODEL), lambda i, j: (i, 0)),
        scratch_shapes=[pltpu.VMEM((tm, D_MODEL), BF16), pltpu.VMEM((tm, D_MODEL), F32)],
        compiler_params=_params("parallel", "arbitrary"),
        name="ffn",
    )(x, mod, g, w13r, w2)


def _even_in_kernel(x_ref, mod_ref, g_ref, w_ref, qkg_ref, *rest, row_fn, rope):
    if rope:
        cos_ref, sin_ref, q_ref, k_ref, v_ref, bg_ref, u_ref = rest
    else:
        q_ref, k_ref, v_ref, bg_ref, u_ref = rest
    r = row_fn(pl.program_id(0))
    h = _mod_norm(x_ref[...], g_ref[...], _mod_row(mod_ref, 3, r), _mod_row(mod_ref, 4, r))
    y = _dot(h.astype(BF16), w_ref[...])
    for hh in range(A_HEADS):
        sl = slice(hh * LANES, (hh + 1) * LANES)
        qs = _seg_rms(y[:, sl], qkg_ref[0:1, :])
        ks = _seg_rms(y[:, A_QK + hh * LANES:A_QK + (hh + 1) * LANES], qkg_ref[1:2, :])
        if rope:
            qs = _rope(qs, cos_ref[...], sin_ref[...])
            ks = _rope(ks, cos_ref[...], sin_ref[...])
        q_ref[:, sl] = (qs * QK_SCALE).astype(q_ref.dtype)
        k_ref[:, sl] = ks.astype(k_ref.dtype)
    v_ref[...] = y[:, 2 * A_QK:2 * A_QK + A_V].astype(v_ref.dtype)
    o = 2 * A_QK + A_V
    bg_ref[...] = y[:, o:o + CONV_DIM]
    u_ref[...] = y[:, o + CONV_DIM:o + 2 * CONV_DIM] * y[:, o + 2 * CONV_DIM:o + 3 * CONV_DIM]


def _even_in(x, mod, g, w_in, qkg, rope_tabs, *, row_fn, tm, kv_dtype):
    t = x.shape[0]
    rope = rope_tabs is not None
    in_specs = [
        pl.BlockSpec((tm, D_MODEL), lambda i: (i, 0)),
        pl.BlockSpec((N_MOD, MOD_ROWS, D_MODEL), lambda i: (0, 0, 0)),
        pl.BlockSpec((1, D_MODEL), lambda i: (0, 0)),
        pl.BlockSpec((D_MODEL, EVEN_IN), lambda i: (0, 0)),
        pl.BlockSpec((2, LANES), lambda i: (0, 0)),
    ]
    args = [x, mod, g, w_in, qkg]
    if rope:
        nt = rope_tabs[0].shape[0] // tm
        in_specs += [pl.BlockSpec((tm, LANES), lambda i: (i % nt, 0))] * 2
        args += list(rope_tabs)
    wide = lambda dt: jax.ShapeDtypeStruct((t, A_QK), dt)
    spec = pl.BlockSpec((tm, A_QK), lambda i: (i, 0))
    return pl.pallas_call(
        functools.partial(_even_in_kernel, row_fn=row_fn, rope=rope),
        out_shape=(wide(BF16), wide(kv_dtype), wide(kv_dtype), wide(F32), wide(F32)),
        grid=(t // tm,),
        in_specs=in_specs,
        out_specs=(spec,) * 5,
        compiler_params=_params("parallel"),
        name="even_in",
    )(*args)


def _lambda(lam_ref, lam_init):
    lf = lam_ref[...]
    a = jnp.sum(lf[0:1] * lf[1:2], axis=-1, keepdims=True)
    b = jnp.sum(lf[2:3] * lf[3:4], axis=-1, keepdims=True)
    return jnp.exp(a) - jnp.exp(b) + lam_init


def _subln(o, sub, lam_init):
    y = o * lax.rsqrt(jnp.mean(o * o, axis=-1, keepdims=True) + EPS)
    return (y * sub) * (1.0 - lam_init)


def _softmax_pv(q, keys, vals):
    scores = [_dot_t(q, k) for k in keys]
    m = functools.reduce(jnp.maximum, [jnp.max(s, axis=-1, keepdims=True) for s in scores])
    es = [jnp.exp(s - m) for s in scores]
    denom = functools.reduce(jnp.add, [jnp.sum(e, axis=-1, keepdims=True) for e in es])
    o = functools.reduce(jnp.add, [_dot(e.astype(BF16), v) for e, v in zip(es, vals)])
    return o * (1.0 / denom)


def _diff_attn(q, keys, vals, lam_ref, sub_ref, lam_init):
    lo = _lane_lo(q.shape)
    zero = jnp.zeros_like(q)
    o1 = _softmax_pv(jnp.where(lo, q, zero), keys, vals)
    o2 = _softmax_pv(jnp.where(lo, zero, q), keys, vals)
    o = o1 - _lambda(lam_ref, lam_init) * o2
    return _subln(o, sub_ref[...], lam_init)


def _even_attn_p_kernel(q_ref, k_ref, v_ref, lam_ref, sub_ref, o_ref, *, lam_init):
    keys = [k_ref[...].astype(BF16)]
    vals = [v_ref[...].astype(BF16)]
    o_ref[...] = _diff_attn(q_ref[...], keys, vals, lam_ref, sub_ref, lam_init).astype(o_ref.dtype)


def _even_attn_p(q, k, v, lam_vec, sub, *, seq, lam_init):
    t = q.shape[0]
    blk = pl.BlockSpec((seq, LANES), lambda b, h: (b, h))
    return pl.pallas_call(
        functools.partial(_even_attn_p_kernel, lam_init=lam_init),
        out_shape=jax.ShapeDtypeStruct((t, A_V), BF16),
        grid=(t // seq, A_HEADS),
        in_specs=[blk, blk, blk,
                  pl.BlockSpec((4, HEAD_DIM), lambda b, h: (0, 0)),
                  pl.BlockSpec((1, A_VDIM), lambda b, h: (0, 0))],
        out_specs=blk,
        compiler_params=_params("parallel", "parallel"),
        name="even_attn_prompt",
    )(q, k, v, lam_vec, sub)


def _even_attn_s_kernel(q_ref, kl_ref, vl_ref, kc_ref, vc_ref, lam_ref, sub_ref, o_ref, *, lam_init):
    keys = [kc_ref[...].astype(BF16), kl_ref[...]]
    vals = [vc_ref[...].astype(BF16), vl_ref[...]]
    o_ref[...] = _diff_attn(q_ref[...], keys, vals, lam_ref, sub_ref, lam_init).astype(o_ref.dtype)


def _even_attn_s(q, kl, vl, kc, vc, lam_vec, sub, *, seq, past, tq, lam_init):
    t = q.shape[0]
    nq = seq // tq
    qblk = pl.BlockSpec((tq, LANES), lambda b, h, i: (b * nq + i, h))
    lat = pl.BlockSpec((seq, LANES), lambda b, h, i: (b, h))
    ctx = pl.BlockSpec((past, LANES), lambda b, h, i: (b, h))
    return pl.pallas_call(
        functools.partial(_even_attn_s_kernel, lam_init=lam_init),
        out_shape=jax.ShapeDtypeStruct((t, A_V), BF16),
        grid=(t // seq, A_HEADS, nq),
        in_specs=[qblk, lat, lat, ctx, ctx,
                  pl.BlockSpec((4, HEAD_DIM), lambda b, h, i: (0, 0)),
                  pl.BlockSpec((1, A_VDIM), lambda b, h, i: (0, 0))],
        out_specs=qblk,
        compiler_params=_params("parallel", "parallel", "arbitrary"),
        name="even_attn_sample",
    )(q, kl, vl, kc, vc, lam_vec, sub)


def _even_out_kernel(x_ref, o_ref, bg_ref, u_ref, up_ref, un_ref, cw_ref, w_ref, mod_ref, out_ref, *, row_fn, seq, tm):
    i = pl.program_id(0)
    r = row_fn(i)
    u = u_ref[...]
    row = lax.broadcasted_iota(jnp.int32, (tm, 1), 0)
    pos = (row + i * tm) % seq
    u_dn = jnp.where(row == 0, up_ref[SUBLANES - 1:SUBLANES, :], pltpu.roll(u, 1, 0))
    u_dn = jnp.where(pos == 0, 0.0, u_dn)
    u_up = jnp.where(row == tm - 1, un_ref[0:1, :], pltpu.roll(u, tm - 1, 0))
    u_up = jnp.where(pos == seq - 1, 0.0, u_up)
    y = bg_ref[...] * (cw_ref[0:1, :] * u_dn + cw_ref[1:2, :] * u + cw_ref[2:3, :] * u_up)
    cat = jnp.concatenate([o_ref[...], y.astype(BF16)], axis=-1)
    mix = _dot(cat, w_ref[...])
    out_ref[...] = x_ref[...] + _mod_row(mod_ref, 5, r) * mix


def _even_out(x, o, bg, u, conv_w, w_out, mod, *, row_fn, seq, tm):
    t = x.shape[0]
    nb = tm // SUBLANES
    last = t // SUBLANES - 1
    return pl.pallas_call(
        functools.partial(_even_out_kernel, row_fn=row_fn, seq=seq, tm=tm),
        out_shape=jax.ShapeDtypeStruct((t, D_MODEL), F32),
        grid=(t // tm,),
        in_specs=[
            pl.BlockSpec((tm, D_MODEL), lambda i: (i, 0)),
            pl.BlockSpec((tm, A_V), lambda i: (i, 0)),
            pl.BlockSpec((tm, CONV_DIM), lambda i: (i, 0)),
            pl.BlockSpec((tm, CONV_DIM), lambda i: (i, 0)),
            pl.BlockSpec((SUBLANES, CONV_DIM), lambda i: (jnp.maximum(i * nb - 1, 0), 0)),
            pl.BlockSpec((SUBLANES, CONV_DIM), lambda i: (jnp.minimum((i + 1) * nb, last), 0)),
            pl.BlockSpec((3, CONV_DIM), lambda i: (0, 0)),
            pl.BlockSpec((D_MODEL, D_MODEL), lambda i: (0, 0)),
            pl.BlockSpec((N_MOD, MOD_ROWS, D_MODEL), lambda i: (0, 0, 0)),
        ],
        out_specs=pl.BlockSpec((tm, D_MODEL), lambda i: (i, 0)),
        compiler_params=_params("parallel"),
        name="even_out",
    )(x, o, bg, u, u, u, conv_w, w_out, mod)


def _dup_halves(x):
    lo = _lane_lo(x.shape)
    sw = pltpu.roll(x, HEAD_DIM, 1)
    return jnp.where(lo, x, sw), jnp.where(lo, sw, x)


def _odd_in_kernel(x_ref, mod_ref, g_ref, w_ref, qkg_ref, *rest, row_fn, rope, keep_kv):
    rest = list(rest)
    if rope:
        cos_ref, sin_ref = rest[:2]
        rest = rest[2:]
    if keep_kv:
        q_ref, kd_ref, vd_ref, k_ref, v_ref = rest
    else:
        q_ref, kd_ref, vd_ref = rest
    r = row_fn(pl.program_id(0))
    h = _mod_norm(x_ref[...], g_ref[...], _mod_row(mod_ref, 3, r), _mod_row(mod_ref, 4, r))
    y = _dot(h.astype(BF16), w_ref[...])
    nq = C_HEADS * HEAD_DIM
    nk = C_KV_HEADS * HEAD_DIM
    for c in range(nq // LANES):
        sl = slice(c * LANES, (c + 1) * LANES)
        qs = _seg_rms(y[:, sl], qkg_ref[0:1, :])
        if rope:
            qs = _rope(qs, cos_ref[...], sin_ref[...])
        q_ref[:, sl] = (qs * QK_SCALE).astype(q_ref.dtype)
    for c in range(nk // LANES):
        sl = slice(c * LANES, (c + 1) * LANES)
        ks = _seg_rms(y[:, nq + c * LANES:nq + (c + 1) * LANES], qkg_ref[1:2, :])
        vs = y[:, nq + nk + c * LANES:nq + nk + (c + 1) * LANES]
        if keep_kv:
            k_ref[:, sl] = ks
            v_ref[:, sl] = vs
        if rope:
            ks = _rope(ks, cos_ref[...], sin_ref[...])
        for arr, ref in ((ks, kd_ref), (vs, vd_ref)):
            d0, d1 = _dup_halves(arr)
            ref[:, 2 * c * LANES:(2 * c + 1) * LANES] = d0.astype(ref.dtype)
            ref[:, (2 * c + 1) * LANES:(2 * c + 2) * LANES] = d1.astype(ref.dtype)


def _odd_in(x, mod, g, w_in, qkg, rope_tabs, *, row_fn, tm, keep_kv):
    t = x.shape[0]
    rope = rope_tabs is not None
    in_specs = [
        pl.BlockSpec((tm, D_MODEL), lambda i: (i, 0)),
        pl.BlockSpec((N_MOD, MOD_ROWS, D_MODEL), lambda i: (0, 0, 0)),
        pl.BlockSpec((1, D_MODEL), lambda i: (0, 0)),
        pl.BlockSpec((D_MODEL, ODD_IN), lambda i: (0, 0)),
        pl.BlockSpec((2, LANES), lambda i: (0, 0)),
    ]
    args = [x, mod, g, w_in, qkg]
    if rope:
        nt = rope_tabs[0].shape[0] // tm
        in_specs += [pl.BlockSpec((tm, LANES), lambda i: (i % nt, 0))] * 2
        args += list(rope_tabs)
    nq = C_HEADS * HEAD_DIM
    nd = C_KV_HEADS * LANES
    nk = C_KV_HEADS * HEAD_DIM
    out_shape = [jax.ShapeDtypeStruct((t, nq), BF16), jax.ShapeDtypeStruct((t, nd), BF16),
                 jax.ShapeDtypeStruct((t, nd), BF16)]
    out_specs = [pl.BlockSpec((tm, nq), lambda i: (i, 0)), pl.BlockSpec((tm, nd), lambda i: (i, 0)),
                 pl.BlockSpec((tm, nd), lambda i: (i, 0))]
    if keep_kv:
        out_shape += [jax.ShapeDtypeStruct((t, nk), F32)] * 2
        out_specs += [pl.BlockSpec((tm, nk), lambda i: (i, 0))] * 2
    return pl.pallas_call(
        functools.partial(_odd_in_kernel, row_fn=row_fn, rope=rope, keep_kv=keep_kv),
        out_shape=tuple(out_shape),
        grid=(t // tm,),
        in_specs=in_specs,
        out_specs=tuple(out_specs),
        compiler_params=_params("parallel"),
        name="odd_in",
    )(*args)


def _stack_heads(q_ref, g, rows):
    parts = []
    for rr in range(GROUP):
        c = g * (GROUP // 2) + rr // 2
        qc = q_ref[:, c * LANES:(c + 1) * LANES]
        lo = _lane_lo(qc.shape)
        keep = lo if rr % 2 == 0 else jnp.logical_not(lo)
        parts.append(jnp.where(keep, qc, jnp.zeros_like(qc)))
    return jnp.concatenate(parts, axis=0)


def _sink_col(sink_ref, g, rows):
    return jnp.concatenate([jnp.full((rows, 1), sink_ref[g * GROUP + rr], F32) for rr in range(GROUP)], axis=0)


def _unstack_heads(o4, rows):
    lo = _lane_lo((rows, LANES))
    outs = []
    for c in range(GROUP // 2):
        a = o4[(2 * c) * rows:(2 * c + 1) * rows]
        b = o4[(2 * c + 1) * rows:(2 * c + 2) * rows]
        outs.append(jnp.where(lo, a, b))
    return jnp.concatenate(outs, axis=-1)


def _sink_softmax_pv(scores, vals, sink):
    m = functools.reduce(jnp.maximum, [jnp.max(s, axis=-1, keepdims=True) for s in scores] + [sink])
    es = [jnp.exp(s - m) for s in scores]
    denom = functools.reduce(jnp.add, [jnp.sum(e, axis=-1, keepdims=True) for e in es]) + jnp.exp(sink - m)
    o = functools.reduce(jnp.add, [_dot(e.astype(BF16), v) for e, v in zip(es, vals)])
    return o * (1.0 / denom)


def _odd_attn_p_kernel(sink_ref, q_ref, kd_ref, vd_ref, o_ref, *, seq):
    g = pl.program_id(1)
    q4 = _stack_heads(q_ref, 0, seq)
    o4 = _sink_softmax_pv([_dot_t(q4, kd_ref[...])], [vd_ref[...]], _sink_col(sink_ref, g, seq))
    o_ref[...] = _unstack_heads(o4, seq).astype(o_ref.dtype)


def _odd_attn_p(q, kd, vd, sink, *, seq):
    t = q.shape[0]
    gw = GROUP * HEAD_DIM
    return pl.pallas_call(
        functools.partial(_odd_attn_p_kernel, seq=seq),
        out_shape=jax.ShapeDtypeStruct((t, C_HEADS * HEAD_DIM), BF16),
        grid=(t // seq, C_KV_HEADS),
        in_specs=[
            pl.BlockSpec(memory_space=pltpu.SMEM),
            pl.BlockSpec((seq, gw), lambda b, g: (b, g)),
            pl.BlockSpec((seq, LANES), lambda b, g: (b, g)),
            pl.BlockSpec((seq, LANES), lambda b, g: (b, g)),
        ],
        out_specs=pl.BlockSpec((seq, gw), lambda b, g: (b, g)),
        compiler_params=_params("parallel", "parallel"),
        name="odd_attn_prompt",
    )(sink, q, kd, vd)


def _odd_attn_s_kernel(sink_ref, q_ref, kd_ref, vd_ref, kc_ref, vc_ref, o_ref, kcd_scr, vcd_scr, *, seq, tq):
    i = pl.program_id(1)
    span = tq + 2 * WINDOW

    @pl.when(i == 0)
    def _():
        for c in range(C_KV_HEADS // 2):
            sl = slice(c * LANES, (c + 1) * LANES)
            for src, dst in ((kc_ref, kcd_scr), (vc_ref, vcd_scr)):
                d0, d1 = _dup_halves(src[:, sl])
                dst[:, 2 * c * LANES:(2 * c + 1) * LANES] = d0.astype(BF16)
                dst[:, (2 * c + 1) * LANES:(2 * c + 2) * LANES] = d1.astype(BF16)

    start = pl.multiple_of(jnp.clip(i * tq - WINDOW, 0, seq - span), WINDOW)
    rows = GROUP * tq
    qpos = i * tq + lax.broadcasted_iota(jnp.int32, (rows, span), 0) % tq
    kpos = start + lax.broadcasted_iota(jnp.int32, (rows, span), 1)
    valid = jnp.abs(qpos - kpos) <= WINDOW
    for g in range(C_KV_HEADS):
        gl = slice(g * LANES, (g + 1) * LANES)
        q4 = _stack_heads(q_ref, g, tq)
        s_c = _dot_t(q4, kcd_scr[:, gl])
        s_l = jnp.where(valid, _dot_t(q4, kd_ref[pl.ds(start, span), gl]), -jnp.inf)
        o4 = _sink_softmax_pv([s_c, s_l], [vcd_scr[:, gl], vd_ref[pl.ds(start, span), gl]],
                              _sink_col(sink_ref, g, tq))
        o_ref[:, g * GROUP * HEAD_DIM:(g + 1) * GROUP * HEAD_DIM] = _unstack_heads(o4, tq).astype(o_ref.dtype)


def _odd_attn_s(q, kd, vd, kc, vc, sink, *, seq, past, tq):
    t = q.shape[0]
    nq = seq // tq
    nqw = C_HEADS * HEAD_DIM
    nd = C_KV_HEADS * LANES
    nk = C_KV_HEADS * HEAD_DIM
    return pl.pallas_call(
        functools.partial(_odd_attn_s_kernel, seq=seq, tq=tq),
        out_shape=jax.ShapeDtypeStruct((t, nqw), BF16),
        grid=(t // seq, nq),
        in_specs=[
            pl.BlockSpec(memory_space=pltpu.SMEM),
            pl.BlockSpec((tq, nqw), lambda b, i: (b * nq + i, 0)),
            pl.BlockSpec((seq, nd), lambda b, i: (b, 0)),
            pl.BlockSpec((seq, nd), lambda b, i: (b, 0)),
            pl.BlockSpec((past, nk), lambda b, i: (b, 0)),
            pl.BlockSpec((past, nk), lambda b, i: (b, 0)),
        ],
        out_specs=pl.BlockSpec((tq, nqw), lambda b, i: (b * nq + i, 0)),
        scratch_shapes=[pltpu.VMEM((past, nd), BF16), pltpu.VMEM((past, nd), BF16)],
        compiler_params=_params("parallel", "arbitrary"),
        name="odd_attn_sample",
    )(sink, q, kd, vd, kc, vc)


def _odd_out_kernel(x_ref, o_ref, w_ref, mod_ref, out_ref, *, row_fn):
    r = row_fn(pl.program_id(0))
    out_ref[...] = x_ref[...] + _mod_row(mod_ref, 5, r) * _dot(o_ref[...], w_ref[...])


def _odd_out(x, o, w_out, mod, *, row_fn, tm):
    t = x.shape[0]
    return pl.pallas_call(
        functools.partial(_odd_out_kernel, row_fn=row_fn),
        out_shape=jax.ShapeDtypeStruct((t, D_MODEL), F32),
        grid=(t // tm,),
        in_specs=[
            pl.BlockSpec((tm, D_MODEL), lambda i: (i, 0)),
            pl.BlockSpec((tm, D_MODEL), lambda i: (i, 0)),
            pl.BlockSpec((D_MODEL, D_MODEL), lambda i: (0, 0)),
            pl.BlockSpec((N_MOD, MOD_ROWS, D_MODEL), lambda i: (0, 0, 0)),
        ],
        out_specs=pl.BlockSpec((tm, D_MODEL), lambda i: (i, 0)),
        compiler_params=_params("parallel"),
        name="odd_out",
    )(x, o, w_out, mod)


def _rope_tables(n):
    rows = n // GRID_W
    row = jnp.repeat(jnp.arange(rows, dtype=F32), GRID_W)
    col = jnp.tile(jnp.arange(GRID_W, dtype=F32), rows)
    inv = ROPE_BASE ** (-jnp.arange(0, AXIS_DIM, 2, dtype=F32) / AXIS_DIM)
    ang_r = row[:, None] * inv[None, :]
    ang_c = col[:, None] * inv[None, :]
    ang = jnp.concatenate([ang_r, ang_r, ang_c, ang_c], axis=-1)
    sign = jnp.where((jnp.arange(HEAD_DIM) & (AXIS_DIM // 2)) == 0, -1.0, 1.0).astype(F32)
    cos = jnp.cos(ang)
    sin = jnp.sin(ang) * sign[None, :]
    return jnp.tile(cos, (1, 2)), jnp.tile(sin, (1, 2))


def _prep_w13(w13, tf):
    nch = D_FF // tf
    w = w13.reshape(D_MODEL, 2, nch, tf).transpose(2, 0, 1, 3).reshape(nch, D_MODEL, 2 * tf)
    return w.astype(BF16)


def kernel(x_prompt, x_sample, cache_even_k, cache_even_v, cache_odd_k, cache_odd_v, c, c_ctx, w_mod, b_mod, norm_g, ffn_w13, ffn_w2, even_w_in, even_w_out, even_qk_norm, even_lambda, even_subln, even_conv_w, odd_w_in, odd_w_out, odd_qk_norm, odd_sink):
    batch, seq, _ = x_prompt.shape
    dec_batch, dec_seq, _ = x_sample.shape
    past = cache_even_k.shape[2]
    depth = w_mod.shape[0]
    assert 1 + dec_batch <= MOD_ROWS and seq % SUBLANES == 0 and dec_seq % TOKEN_TILE == 0

    xp = x_prompt.reshape(batch * seq, D_MODEL)
    xs = x_sample.reshape(dec_batch * dec_seq, D_MODEL)
    tm_p = min(TOKEN_TILE, batch * seq)
    row_p = _row_fn(0, batch * seq, tm_p)
    row_s = _row_fn(1, dec_seq, TOKEN_TILE)

    c_rows = jnp.zeros((MOD_ROWS, D_MODEL), F32).at[0].set(c_ctx).at[1:1 + dec_batch].set(c)
    mods = _adaln(c_rows, w_mod, b_mod)
    rope_tabs = _rope_tables(dec_seq)

    even_k, even_v, odd_k, odd_v = [], [], [], []
    for l in range(depth):
        mod = mods[l]
        g = [norm_g[l, s].reshape(1, D_MODEL) for s in range(3)]
        w13 = [_prep_w13(ffn_w13[l, s], FF_TILE) for s in range(2)]
        w2 = [ffn_w2[l, s].astype(BF16) for s in range(2)]

        xp = _ffn(xp, mod, g[0], w13[0], w2[0], k0=0, row_fn=row_p, tm=tm_p)
        xs = _ffn(xs, mod, g[0], w13[0], w2[0], k0=0, row_fn=row_s, tm=TOKEN_TILE)

        if l % 2 == 0:
            e = l // 2
            lam_init = 0.8 - 0.6 * math.exp(-0.3 * l)
            w_in = even_w_in[e].astype(BF16)
            w_out = even_w_out[e].astype(BF16)
            qkg = jnp.tile(even_qk_norm[e], (1, 2))
            sub = even_subln[e].reshape(1, A_VDIM)

            q, k, v, bg, u = _even_in(xp, mod, g[1], w_in, qkg, None, row_fn=row_p, tm=tm_p, kv_dtype=F32)
            o = _even_attn_p(q, k, v, even_lambda[e], sub, seq=seq, lam_init=lam_init)
            xp = _even_out(xp, o, bg, u, even_conv_w[e], w_out, mod, row_fn=row_p, seq=seq, tm=tm_p)
            even_k.append(k.reshape(batch, seq, A_HEADS, 2 * HEAD_DIM))
            even_v.append(v.reshape(batch, seq, A_HEADS, A_VDIM))

            q, k, v, bg, u = _even_in(xs, mod, g[1], w_in, qkg, rope_tabs, row_fn=row_s, tm=TOKEN_TILE, kv_dtype=BF16)
            kc = cache_even_k[:, e].reshape(dec_batch * past, A_QK)
            vc = cache_even_v[:, e].reshape(dec_batch * past, A_V)
            o = _even_attn_s(q, k, v, kc, vc, even_lambda[e], sub, seq=dec_seq, past=past, tq=EVEN_Q_TILE,
                             lam_init=lam_init)
            xs = _even_out(xs, o, bg, u, even_conv_w[e], w_out, mod, row_fn=row_s, seq=dec_seq, tm=TOKEN_TILE)
        else:
            e = l // 2
            w_in = odd_w_in[e].astype(BF16)
            w_out = odd_w_out[e].astype(BF16)
            qkg = jnp.tile(odd_qk_norm[e], (1, 2))
            sink = odd_sink[e]

            q, kd, vd, k, v = _odd_in(xp, mod, g[1], w_in, qkg, None, row_fn=row_p, tm=tm_p, keep_kv=True)
            o = _odd_attn_p(q, kd, vd, sink, seq=seq)
            xp = _odd_out(xp, o, w_out, mod, row_fn=row_p, tm=tm_p)
            odd_k.append(k.reshape(batch, seq, C_KV_HEADS, HEAD_DIM))
            odd_v.append(v.reshape(batch, seq, C_KV_HEADS, HEAD_DIM))

            q, kd, vd = _odd_in(xs, mod, g[1], w_in, qkg, rope_tabs, row_fn=row_s, tm=TOKEN_TILE, keep_kv=False)
            kc = cache_odd_k[:, e].reshape(dec_batch * past, C_KV_HEADS * HEAD_DIM)
            vc = cache_odd_v[:, e].reshape(dec_batch * past, C_KV_HEADS * HEAD_DIM)
            o = _odd_attn_s(q, kd, vd, kc, vc, sink, seq=dec_seq, past=past, tq=ODD_Q_TILE)
            xs = _odd_out(xs, o, w_out, mod, row_fn=row_s, tm=TOKEN_TILE)

        xp = _ffn(xp, mod, g[2], w13[1], w2[1], k0=6, row_fn=row_p, tm=tm_p)
        xs = _ffn(xs, mod, g[2], w13[1], w2[1], k0=6, row_fn=row_s, tm=TOKEN_TILE)

    return (xp.reshape(batch, seq, D_MODEL), xs.reshape(dec_batch, dec_seq, D_MODEL),
            jnp.stack(even_k, axis=1), jnp.stack(even_v, axis=1),
            jnp.stack(odd_k, axis=1), jnp.stack(odd_v, axis=1))
```

```python
import functools
import math

import jax
import jax.numpy as jnp
from jax import lax
from jax.experimental import pallas as pl
from jax.experimental.pallas import tpu as pltpu

F32 = jnp.float32
BF16 = jnp.bfloat16

D_MODEL = 1024
GRID_W = 64
HEAD_DIM = 64
AXIS_DIM = HEAD_DIM // 2
A_HEADS = 4
A_VDIM = 2 * HEAD_DIM
CONV_DIM = 512
C_HEADS = 16
C_KV_HEADS = 4
GROUP = C_HEADS // C_KV_HEADS
WINDOW = 128
D_FF = 2816
ROPE_BASE = 10000.0
N_MOD = 9
EPS = 1e-6
A_QK = A_HEADS * 2 * HEAD_DIM
A_V = A_HEADS * A_VDIM
EVEN_IN = 2 * A_QK + A_V + 3 * CONV_DIM
ODD_IN = (C_HEADS + 2 * C_KV_HEADS) * HEAD_DIM
QK_SCALE = HEAD_DIM ** -0.5

LANES = 128
SUBLANES = 8
MOD_ROWS = 16
VMEM_LIMIT_BYTES = 56 * 2 ** 20

TOKEN_TILE = 512
EVEN_Q_TILE = 256
ODD_Q_TILE = 128


def _params(*semantics):
    return pltpu.CompilerParams(dimension_semantics=semantics, vmem_limit_bytes=VMEM_LIMIT_BYTES)


def _sigmoid(x):
    return 1.0 / (1.0 + jnp.exp(-x))


def _mod_norm(x, g, shift, scale):
    y = x * lax.rsqrt(jnp.mean(x * x, axis=-1, keepdims=True) + EPS)
    return (y * g) * (1.0 + scale) + shift


def _mod_row(mod_ref, k, r):
    return mod_ref[k, pl.ds(r, 1), :]


def _row_fn(row0, tokens_per_row, tile):
    return lambda i: row0 + (i * tile) // tokens_per_row


def _lane_lo(shape):
    return lax.broadcasted_iota(jnp.int32, shape, len(shape) - 1) < HEAD_DIM


def _seg_rms(xs, gain):
    lo = _lane_lo(xs.shape)
    sq = xs * xs
    s_lo = jnp.sum(jnp.where(lo, sq, 0.0), axis=-1, keepdims=True)
    s_hi = jnp.sum(jnp.where(lo, 0.0, sq), axis=-1, keepdims=True)
    inv = jnp.where(lo, lax.rsqrt(s_lo * (1.0 / HEAD_DIM) + EPS), lax.rsqrt(s_hi * (1.0 / HEAD_DIM) + EPS))
    return (xs * inv) * gain


def _rope(xs, cos, sin_signed):
    lane = lax.broadcasted_iota(jnp.int32, xs.shape, 1)
    first = (lane & (AXIS_DIM // 2)) == 0
    partner = jnp.where(first, pltpu.roll(xs, LANES - AXIS_DIM // 2, 1), pltpu.roll(xs, AXIS_DIM // 2, 1))
    return xs * cos + partner * sin_signed


def _dot_t(a, b):
    return lax.dot_general(a, b, (((1,), (1,)), ((), ())), preferred_element_type=F32)


def _dot(a, b):
    return jnp.dot(a, b, preferred_element_type=F32)


def _adaln_kernel(c_ref, w_ref, b_ref, o_ref):
    c = c_ref[...]
    s = c * _sigmoid(c)
    o_ref[0, 0] = _dot(s.astype(BF16), w_ref[0].astype(BF16)) + b_ref[0]


def _adaln(c_rows, w_mod, b_mod):
    depth = w_mod.shape[0]
    b3 = b_mod.reshape(depth * N_MOD, 1, D_MODEL)
    return pl.pallas_call(
        _adaln_kernel,
        out_shape=jax.ShapeDtypeStruct((depth, N_MOD, MOD_ROWS, D_MODEL), F32),
        grid=(depth, N_MOD),
        in_specs=[
            pl.BlockSpec((MOD_ROWS, D_MODEL), lambda l, j: (0, 0)),
            pl.BlockSpec((1, D_MODEL, D_MODEL), lambda l, j: (l, 0, j)),
            pl.BlockSpec((1, 1, D_MODEL), lambda l, j: (l * N_MOD + j, 0, 0)),
        ],
        out_specs=pl.BlockSpec((1, 1, MOD_ROWS, D_MODEL), lambda l, j: (l, j, 0, 0)),
        compiler_params=_params("parallel", "parallel"),
        name="adaln",
    )(c_rows, w_mod, b3)


def _ffn_kernel(x_ref, mod_ref, g_ref, w13_ref, w2_ref, o_ref, *, row_fn, k0):
    r = row_fn(pl.program_id(0))
    x = x_ref[...]
    h = _mod_norm(x, g_ref[...], _mod_row(mod_ref, k0, r), _mod_row(mod_ref, k0 + 1, r))
    a = _dot(h.astype(BF16), w13_ref[...])
    gate = a[:, :D_FF]
    up = a[:, D_FF:]
    act = (gate * _sigmoid(gate)) * up
    y = _dot(act.astype(BF16), w2_ref[...])
    o_ref[...] = x + (0.5 * _mod_row(mod_ref, k0 + 2, r)) * y


def _resident(shape):
    return pl.BlockSpec(shape, lambda *_: (0,) * len(shape), pipeline_mode=pl.Buffered(1))


def _ffn(x, mod, g, w13, w2, *, k0, row_fn, tm):
    t = x.shape[0]
    return pl.pallas_call(
        functools.partial(_ffn_kernel, row_fn=row_fn, k0=k0),
        out_shape=jax.ShapeDtypeStruct((t, D_MODEL), F32),
        grid=(t // tm,),
        in_specs=[
            pl.BlockSpec((tm, D_MODEL), lambda i: (i, 0)),
            _resident((N_MOD, MOD_ROWS, D_MODEL)),
            _resident((1, D_MODEL)),
            _resident((D_MODEL, 2 * D_FF)),
            _resident((D_FF, D_MODEL)),
        ],
        out_specs=pl.BlockSpec((tm, D_MODEL), lambda i: (i, 0)),
        compiler_params=_params("parallel"),
        name="ffn",
    )(x, mod, g, w13, w2)


def _even_in_kernel(x_ref, mod_ref, g_ref, w_ref, qkg_ref, *rest, row_fn, rope):
    if rope:
        cos_ref, sin_ref, q_ref, k_ref, v_ref, bg_ref, u_ref = rest
    else:
        q_ref, k_ref, v_ref, bg_ref, u_ref = rest
    r = row_fn(pl.program_id(0))
    h = _mod_norm(x_ref[...], g_ref[...], _mod_row(mod_ref, 3, r), _mod_row(mod_ref, 4, r))
    y = _dot(h.astype(BF16), w_ref[...])
    for hh in range(A_HEADS):
        sl = slice(hh * LANES, (hh + 1) * LANES)
        qs = _seg_rms(y[:, sl], qkg_ref[0:1, :])
        ks = _seg_rms(y[:, A_QK + hh * LANES:A_QK + (hh + 1) * LANES], qkg_ref[1:2, :])
        if rope:
            qs = _rope(qs, cos_ref[...], sin_ref[...])
            ks = _rope(ks, cos_ref[...], sin_ref[...])
        q_ref[:, sl] = (qs * QK_SCALE).astype(q_ref.dtype)
        k_ref[:, sl] = ks.astype(k_ref.dtype)
    v_ref[...] = y[:, 2 * A_QK:2 * A_QK + A_V].astype(v_ref.dtype)
    o = 2 * A_QK + A_V
    bg_ref[...] = y[:, o:o + CONV_DIM]
    u_ref[...] = y[:, o + CONV_DIM:o + 2 * CONV_DIM] * y[:, o + 2 * CONV_DIM:o + 3 * CONV_DIM]


def _even_in(x, mod, g, w_in, qkg, rope_tabs, *, row_fn, tm, kv_dtype):
    t = x.shape[0]
    rope = rope_tabs is not None
    in_specs = [
        pl.BlockSpec((tm, D_MODEL), lambda i: (i, 0)),
        pl.BlockSpec((N_MOD, MOD_ROWS, D_MODEL), lambda i: (0, 0, 0)),
        pl.BlockSpec((1, D_MODEL), lambda i: (0, 0)),
        pl.BlockSpec((D_MODEL, EVEN_IN), lambda i: (0, 0)),
        pl.BlockSpec((2, LANES), lambda i: (0, 0)),
    ]
    args = [x, mod, g, w_in, qkg]
    if rope:
        nt = rope_tabs[0].shape[0] // tm
        in_specs += [pl.BlockSpec((tm, LANES), lambda i: (i % nt, 0))] * 2
        args += list(rope_tabs)
    wide = lambda dt: jax.ShapeDtypeStruct((t, A_QK), dt)
    spec = pl.BlockSpec((tm, A_QK), lambda i: (i, 0))
    return pl.pallas_call(
        functools.partial(_even_in_kernel, row_fn=row_fn, rope=rope),
        out_shape=(wide(BF16), wide(kv_dtype), wide(kv_dtype), wide(F32), wide(F32)),
        grid=(t // tm,),
        in_specs=in_specs,
        out_specs=(spec,) * 5,
        compiler_params=_params("parallel"),
        name="even_in",
    )(*args)


def _lambda(lam_ref, lam_init):
    lf = lam_ref[...]
    a = jnp.sum(lf[0:1] * lf[1:2], axis=-1, keepdims=True)
    b = jnp.sum(lf[2:3] * lf[3:4], axis=-1, keepdims=True)
    return jnp.exp(a) - jnp.exp(b) + lam_init


def _subln(o, sub, lam_init):
    y = o * lax.rsqrt(jnp.mean(o * o, axis=-1, keepdims=True) + EPS)
    return (y * sub) * (1.0 - lam_init)


def _softmax_pv(q, keys, vals):
    scores = [_dot_t(q, k) for k in keys]
    m = functools.reduce(jnp.maximum, [jnp.max(s, axis=-1, keepdims=True) for s in scores])
    es = [jnp.exp(s - m) for s in scores]
    denom = functools.reduce(jnp.add, [jnp.sum(e, axis=-1, keepdims=True) for e in es])
    o = functools.reduce(jnp.add, [_dot(e.astype(BF16), v) for e, v in zip(es, vals)])
    return o * (1.0 / denom)


def _diff_attn(q, keys, vals, lam_ref, sub_ref, lam_init):
    lo = _lane_lo(q.shape)
    zero = jnp.zeros_like(q)
    o1 = _softmax_pv(jnp.where(lo, q, zero), keys, vals)
    o2 = _softmax_pv(jnp.where(lo, zero, q), keys, vals)
    o = o1 - _lambda(lam_ref, lam_init) * o2
    return _subln(o, sub_ref[...], lam_init)


def _even_attn_p_kernel(q_ref, k_ref, v_ref, lam_ref, sub_ref, o_ref, *, lam_init):
    keys = [k_ref[...].astype(BF16)]
    vals = [v_ref[...].astype(BF16)]
    o_ref[...] = _diff_attn(q_ref[...], keys, vals, lam_ref, sub_ref, lam_init).astype(o_ref.dtype)


def _even_attn_p(q, k, v, lam_vec, sub, *, seq, lam_init):
    t = q.shape[0]
    blk = pl.BlockSpec((seq, LANES), lambda b, h: (b, h))
    return pl.pallas_call(
        functools.partial(_even_attn_p_kernel, lam_init=lam_init),
        out_shape=jax.ShapeDtypeStruct((t, A_V), BF16),
        grid=(t // seq, A_HEADS),
        in_specs=[blk, blk, blk,
                  pl.BlockSpec((4, HEAD_DIM), lambda b, h: (0, 0)),
                  pl.BlockSpec((1, A_VDIM), lambda b, h: (0, 0))],
        out_specs=blk,
        compiler_params=_params("parallel", "parallel"),
        name="even_attn_prompt",
    )(q, k, v, lam_vec, sub)


def _even_attn_s_kernel(q_ref, kl_ref, vl_ref, kc_ref, vc_ref, lam_ref, sub_ref, o_ref, *, lam_init):
    keys = [kc_ref[...].astype(BF16), kl_ref[...]]
    vals = [vc_ref[...].astype(BF16), vl_ref[...]]
    o_ref[...] = _diff_attn(q_ref[...], keys, vals, lam_ref, sub_ref, lam_init).astype(o_ref.dtype)


def _even_attn_s(q, kl, vl, kc, vc, lam_vec, sub, *, seq, past, tq, lam_init):
    t = q.shape[0]
    nq = seq // tq
    qblk = pl.BlockSpec((tq, LANES), lambda b, h, i: (b * nq + i, h))
    lat = pl.BlockSpec((seq, LANES), lambda b, h, i: (b, h))
    ctx = pl.BlockSpec((past, LANES), lambda b, h, i: (b, h))
    return pl.pallas_call(
        functools.partial(_even_attn_s_kernel, lam_init=lam_init),
        out_shape=jax.ShapeDtypeStruct((t, A_V), BF16),
        grid=(t // seq, A_HEADS, nq),
        in_specs=[qblk, lat, lat, ctx, ctx,
                  pl.BlockSpec((4, HEAD_DIM), lambda b, h, i: (0, 0)),
                  pl.BlockSpec((1, A_VDIM), lambda b, h, i: (0, 0))],
        out_specs=qblk,
        compiler_params=_params("parallel", "parallel", "arbitrary"),
        name="even_attn_sample",
    )(q, kl, vl, kc, vc, lam_vec, sub)


def _even_out_kernel(x_ref, o_ref, bg_ref, u_ref, up_ref, un_ref, cw_ref, w_ref, mod_ref, out_ref, *, row_fn, seq, tm):
    i = pl.program_id(0)
    r = row_fn(i)
    u = u_ref[...]
    row = lax.broadcasted_iota(jnp.int32, (tm, 1), 0)
    pos = (row + i * tm) % seq
    u_dn = jnp.where(row == 0, up_ref[SUBLANES - 1:SUBLANES, :], pltpu.roll(u, 1, 0))
    u_dn = jnp.where(pos == 0, 0.0, u_dn)
    u_up = jnp.where(row == tm - 1, un_ref[0:1, :], pltpu.roll(u, tm - 1, 0))
    u_up = jnp.where(pos == seq - 1, 0.0, u_up)
    y = bg_ref[...] * (cw_ref[0:1, :] * u_dn + cw_ref[1:2, :] * u + cw_ref[2:3, :] * u_up)
    cat = jnp.concatenate([o_ref[...], y.astype(BF16)], axis=-1)
    mix = _dot(cat, w_ref[...])
    out_ref[...] = x_ref[...] + _mod_row(mod_ref, 5, r) * mix


def _even_out(x, o, bg, u, conv_w, w_out, mod, *, row_fn, seq, tm):
    t = x.shape[0]
    nb = tm // SUBLANES
    last = t // SUBLANES - 1
    return pl.pallas_call(
        functools.partial(_even_out_kernel, row_fn=row_fn, seq=seq, tm=tm),
        out_shape=jax.ShapeDtypeStruct((t, D_MODEL), F32),
        grid=(t // tm,),
        in_specs=[
            pl.BlockSpec((tm, D_MODEL), lambda i: (i, 0)),
            pl.BlockSpec((tm, A_V), lambda i: (i, 0)),
            pl.BlockSpec((tm, CONV_DIM), lambda i: (i, 0)),
            pl.BlockSpec((tm, CONV_DIM), lambda i: (i, 0)),
            pl.BlockSpec((SUBLANES, CONV_DIM), lambda i: (jnp.maximum(i * nb - 1, 0), 0)),
            pl.BlockSpec((SUBLANES, CONV_DIM), lambda i: (jnp.minimum((i + 1) * nb, last), 0)),
            pl.BlockSpec((3, CONV_DIM), lambda i: (0, 0)),
            pl.BlockSpec((D_MODEL, D_MODEL), lambda i: (0, 0)),
            pl.BlockSpec((N_MOD, MOD_ROWS, D_MODEL), lambda i: (0, 0, 0)),
        ],
        out_specs=pl.BlockSpec((tm, D_MODEL), lambda i: (i, 0)),
        compiler_params=_params("parallel"),
        name="even_out",
    )(x, o, bg, u, u, u, conv_w, w_out, mod)


def _dup_halves(x):
    lo = _lane_lo(x.shape)
    sw = pltpu.roll(x, HEAD_DIM, 1)
    return jnp.where(lo, x, sw), jnp.where(lo, sw, x)


def _odd_in_kernel(x_ref, mod_ref, g_ref, w_ref, qkg_ref, *rest, row_fn, rope, keep_kv):
    rest = list(rest)
    if rope:
        cos_ref, sin_ref = rest[:2]
        rest = rest[2:]
    if keep_kv:
        q_ref, kd_ref, vd_ref, k_ref, v_ref = rest
    else:
        q_ref, kd_ref, vd_ref = rest
    r = row_fn(pl.program_id(0))
    h = _mod_norm(x_ref[...], g_ref[...], _mod_row(mod_ref, 3, r), _mod_row(mod_ref, 4, r))
    y = _dot(h.astype(BF16), w_ref[...])
    nq = C_HEADS * HEAD_DIM
    nk = C_KV_HEADS * HEAD_DIM
    for c in range(nq // LANES):
        sl = slice(c * LANES, (c + 1) * LANES)
        qs = _seg_rms(y[:, sl], qkg_ref[0:1, :])
        if rope:
            qs = _rope(qs, cos_ref[...], sin_ref[...])
        q_ref[:, sl] = (qs * QK_SCALE).astype(q_ref.dtype)
    for c in range(nk // LANES):
        sl = slice(c * LANES, (c + 1) * LANES)
        ks = _seg_rms(y[:, nq + c * LANES:nq + (c + 1) * LANES], qkg_ref[1:2, :])
        vs = y[:, nq + nk + c * LANES:nq + nk + (c + 1) * LANES]
        if keep_kv:
            k_ref[:, sl] = ks
            v_ref[:, sl] = vs
        if rope:
            ks = _rope(ks, cos_ref[...], sin_ref[...])
        for arr, ref in ((ks, kd_ref), (vs, vd_ref)):
            d0, d1 = _dup_halves(arr)
            ref[:, 2 * c * LANES:(2 * c + 1) * LANES] = d0.astype(ref.dtype)
            ref[:, (2 * c + 1) * LANES:(2 * c + 2) * LANES] = d1.astype(ref.dtype)


def _odd_in(x, mod, g, w_in, qkg, rope_tabs, *, row_fn, tm, keep_kv):
    t = x.shape[0]
    rope = rope_tabs is not None
    in_specs = [
        pl.BlockSpec((tm, D_MODEL), lambda i: (i, 0)),
        pl.BlockSpec((N_MOD, MOD_ROWS, D_MODEL), lambda i: (0, 0, 0)),
        pl.BlockSpec((1, D_MODEL), lambda i: (0, 0)),
        pl.BlockSpec((D_MODEL, ODD_IN), lambda i: (0, 0)),
        pl.BlockSpec((2, LANES), lambda i: (0, 0)),
    ]
    args = [x, mod, g, w_in, qkg]
    if rope:
        nt = rope_tabs[0].shape[0] // tm
        in_specs += [pl.BlockSpec((tm, LANES), lambda i: (i % nt, 0))] * 2
        args += list(rope_tabs)
    nq = C_HEADS * HEAD_DIM
    nd = C_KV_HEADS * LANES
    nk = C_KV_HEADS * HEAD_DIM
    out_shape = [jax.ShapeDtypeStruct((t, nq), BF16), jax.ShapeDtypeStruct((t, nd), BF16),
                 jax.ShapeDtypeStruct((t, nd), BF16)]
    out_specs = [pl.BlockSpec((tm, nq), lambda i: (i, 0)), pl.BlockSpec((tm, nd), lambda i: (i, 0)),
                 pl.BlockSpec((tm, nd), lambda i: (i, 0))]
    if keep_kv:
        out_shape += [jax.ShapeDtypeStruct((t, nk), F32)] * 2
        out_specs += [pl.BlockSpec((tm, nk), lambda i: (i, 0))] * 2
    return pl.pallas_call(
        functools.partial(_odd_in_kernel, row_fn=row_fn, rope=rope, keep_kv=keep_kv),
        out_shape=tuple(out_shape),
        grid=(t // tm,),
        in_specs=in_specs,
        out_specs=tuple(out_specs),
        compiler_params=_params("parallel"),
        name="odd_in",
    )(*args)


def _stack_heads(q_ref, g, rows):
    parts = []
    for rr in range(GROUP):
        c = g * (GROUP // 2) + rr // 2
        qc = q_ref[:, c * LANES:(c + 1) * LANES]
        lo = _lane_lo(qc.shape)
        keep = lo if rr % 2 == 0 else jnp.logical_not(lo)
        parts.append(jnp.where(keep, qc, jnp.zeros_like(qc)))
    return jnp.concatenate(parts, axis=0)


def _sink_col(sink_ref, g, rows):
    return jnp.concatenate([jnp.full((rows, 1), sink_ref[g * GROUP + rr], F32) for rr in range(GROUP)], axis=0)


def _unstack_heads(o4, rows):
    lo = _lane_lo((rows, LANES))
    outs = []
    for c in range(GROUP // 2):
        a = o4[(2 * c) * rows:(2 * c + 1) * rows]
        b = o4[(2 * c + 1) * rows:(2 * c + 2) * rows]
        outs.append(jnp.where(lo, a, b))
    return jnp.concatenate(outs, axis=-1)


def _sink_softmax_pv(scores, vals, sink):
    m = functools.reduce(jnp.maximum, [jnp.max(s, axis=-1, keepdims=True) for s in scores] + [sink])
    es = [jnp.exp(s - m) for s in scores]
    denom = functools.reduce(jnp.add, [jnp.sum(e, axis=-1, keepdims=True) for e in es]) + jnp.exp(sink - m)
    o = functools.reduce(jnp.add, [_dot(e.astype(BF16), v) for e, v in zip(es, vals)])
    return o * (1.0 / denom)


def _odd_attn_p_kernel(sink_ref, q_ref, kd_ref, vd_ref, o_ref, *, seq):
    g = pl.program_id(1)
    q4 = _stack_heads(q_ref, 0, seq)
    o4 = _sink_softmax_pv([_dot_t(q4, kd_ref[...])], [vd_ref[...]], _sink_col(sink_ref, g, seq))
    o_ref[...] = _unstack_heads(o4, seq).astype(o_ref.dtype)


def _odd_attn_p(q, kd, vd, sink, *, seq):
    t = q.shape[0]
    gw = GROUP * HEAD_DIM
    return pl.pallas_call(
        functools.partial(_odd_attn_p_kernel, seq=seq),
        out_shape=jax.ShapeDtypeStruct((t, C_HEADS * HEAD_DIM), BF16),
        grid=(t // seq, C_KV_HEADS),
        in_specs=[
            pl.BlockSpec(memory_space=pltpu.SMEM),
            pl.BlockSpec((seq, gw), lambda b, g: (b, g)),
            pl.BlockSpec((seq, LANES), lambda b, g: (b, g)),
            pl.BlockSpec((seq, LANES), lambda b, g: (b, g)),
        ],
        out_specs=pl.BlockSpec((seq, gw), lambda b, g: (b, g)),
        compiler_params=_params("parallel", "parallel"),
        name="odd_attn_prompt",
    )(sink, q, kd, vd)


def _odd_attn_s_kernel(sink_ref, q_ref, kd_ref, vd_ref, kc_ref, vc_ref, o_ref, kcd_scr, vcd_scr, *, seq, tq):
    i = pl.program_id(1)
    span = tq + 2 * WINDOW

    @pl.when(i == 0)
    def _():
        for c in range(C_KV_HEADS // 2):
            sl = slice(c * LANES, (c + 1) * LANES)
            for src, dst in ((kc_ref, kcd_scr), (vc_ref, vcd_scr)):
                d0, d1 = _dup_halves(src[:, sl])
                dst[:, 2 * c * LANES:(2 * c + 1) * LANES] = d0.astype(BF16)
                dst[:, (2 * c + 1) * LANES:(2 * c + 2) * LANES] = d1.astype(BF16)

    start = pl.multiple_of(jnp.clip(i * tq - WINDOW, 0, seq - span), WINDOW)
    rows = GROUP * tq
    qpos = i * tq + lax.broadcasted_iota(jnp.int32, (rows, span), 0) % tq
    kpos = start + lax.broadcasted_iota(jnp.int32, (rows, span), 1)
    valid = jnp.abs(qpos - kpos) <= WINDOW
    for g in range(C_KV_HEADS):
        gl = slice(g * LANES, (g + 1) * LANES)
        q4 = _stack_heads(q_ref, g, tq)
        s_c = _dot_t(q4, kcd_scr[:, gl])
        s_l = jnp.where(valid, _dot_t(q4, kd_ref[pl.ds(start, span), gl]), -jnp.inf)
        o4 = _sink_softmax_pv([s_c, s_l], [vcd_scr[:, gl], vd_ref[pl.ds(start, span), gl]],
                              _sink_col(sink_ref, g, tq))
        o_ref[:, g * GROUP * HEAD_DIM:(g + 1) * GROUP * HEAD_DIM] = _unstack_heads(o4, tq).astype(o_ref.dtype)


def _odd_attn_s(q, kd, vd, kc, vc, sink, *, seq, past, tq):
    t = q.shape[0]
    nq = seq // tq
    nqw = C_HEADS * HEAD_DIM
    nd = C_KV_HEADS * LANES
    nk = C_KV_HEADS * HEAD_DIM
    return pl.pallas_call(
        functools.partial(_odd_attn_s_kernel, seq=seq, tq=tq),
        out_shape=jax.ShapeDtypeStruct((t, nqw), BF16),
        grid=(t // seq, nq),
        in_specs=[
            pl.BlockSpec(memory_space=pltpu.SMEM),
            pl.BlockSpec((tq, nqw), lambda b, i: (b * nq + i, 0)),
            pl.BlockSpec((seq, nd), lambda b, i: (b, 0)),
            pl.BlockSpec((seq, nd), lambda b, i: (b, 0)),
            pl.BlockSpec((past, nk), lambda b, i: (b, 0)),
            pl.BlockSpec((past, nk), lambda b, i: (b, 0)),
        ],
        out_specs=pl.BlockSpec((tq, nqw), lambda b, i: (b * nq + i, 0)),
        scratch_shapes=[pltpu.VMEM((past, nd), BF16), pltpu.VMEM((past, nd), BF16)],
        compiler_params=_params("parallel", "arbitrary"),
        name="odd_attn_sample",
    )(sink, q, kd, vd, kc, vc)


def _odd_out_kernel(x_ref, o_ref, w_ref, mod_ref, out_ref, *, row_fn):
    r = row_fn(pl.program_id(0))
    out_ref[...] = x_ref[...] + _mod_row(mod_ref, 5, r) * _dot(o_ref[...], w_ref[...])


def _odd_out(x, o, w_out, mod, *, row_fn, tm):
    t = x.shape[0]
    return pl.pallas_call(
        functools.partial(_odd_out_kernel, row_fn=row_fn),
        out_shape=jax.ShapeDtypeStruct((t, D_MODEL), F32),
        grid=(t // tm,),
        in_specs=[
            pl.BlockSpec((tm, D_MODEL), lambda i: (i, 0)),
            pl.BlockSpec((tm, D_MODEL), lambda i: (i, 0)),
            pl.BlockSpec((D_MODEL, D_MODEL), lambda i: (0, 0)),
            pl.BlockSpec((N_MOD, MOD_ROWS, D_MODEL), lambda i: (0, 0, 0)),
        ],
        out_specs=pl.BlockSpec((tm, D_MODEL), lambda i: (i, 0)),
        compiler_params=_params("parallel"),
        name="odd_out",
    )(x, o, w_out, mod)


def _rope_tables(n):
    rows = n // GRID_W
    row = jnp.repeat(jnp.arange(rows, dtype=F32), GRID_W)
    col = jnp.tile(jnp.arange(GRID_W, dtype=F32), rows)
    inv = ROPE_BASE ** (-jnp.arange(0, AXIS_DIM, 2, dtype=F32) / AXIS_DIM)
    ang_r = row[:, None] * inv[None, :]
    ang_c = col[:, None] * inv[None, :]
    ang = jnp.concatenate([ang_r, ang_r, ang_c, ang_c], axis=-1)
    sign = jnp.where((jnp.arange(HEAD_DIM) & (AXIS_DIM // 2)) == 0, -1.0, 1.0).astype(F32)
    cos = jnp.cos(ang)
    sin = jnp.sin(ang) * sign[None, :]
    return jnp.tile(cos, (1, 2)), jnp.tile(sin, (1, 2))


def kernel(x_prompt, x_sample, cache_even_k, cache_even_v, cache_odd_k, cache_odd_v, c, c_ctx, w_mod, b_mod, norm_g, ffn_w13, ffn_w2, even_w_in, even_w_out, even_qk_norm, even_lambda, even_subln, even_conv_w, odd_w_in, odd_w_out, odd_qk_norm, odd_sink):
    batch, seq, _ = x_prompt.shape
    dec_batch, dec_seq, _ = x_sample.shape
    past = cache_even_k.shape[2]
    depth = w_mod.shape[0]
    assert 1 + dec_batch <= MOD_ROWS and seq % SUBLANES == 0 and dec_seq % TOKEN_TILE == 0

    xp = x_prompt.reshape(batch * seq, D_MODEL)
    xs = x_sample.reshape(dec_batch * dec_seq, D_MODEL)
    tm_p = min(TOKEN_TILE, batch * seq)
    row_p = _row_fn(0, batch * seq, tm_p)
    row_s = _row_fn(1, dec_seq, TOKEN_TILE)

    c_rows = jnp.zeros((MOD_ROWS, D_MODEL), F32).at[0].set(c_ctx).at[1:1 + dec_batch].set(c)
    mods = _adaln(c_rows, w_mod, b_mod)
    rope_tabs = _rope_tables(dec_seq)

    even_k, even_v, odd_k, odd_v = [], [], [], []
    for l in range(depth):
        mod = mods[l]
        g = [norm_g[l, s].reshape(1, D_MODEL) for s in range(3)]
        w13 = [ffn_w13[l, s].astype(BF16) for s in range(2)]
        w2 = [ffn_w2[l, s].astype(BF16) for s in range(2)]

        xp = _ffn(xp, mod, g[0], w13[0], w2[0], k0=0, row_fn=row_p, tm=tm_p)
        xs = _ffn(xs, mod, g[0], w13[0], w2[0], k0=0, row_fn=row_s, tm=TOKEN_TILE)

        if l % 2 == 0:
            e = l // 2
            lam_init = 0.8 - 0.6 * math.exp(-0.3 * l)
            w_in = even_w_in[e].astype(BF16)
            w_out = even_w_out[e].astype(BF16)
            qkg = jnp.tile(even_qk_norm[e], (1, 2))
            sub = even_subln[e].reshape(1, A_VDIM)

            q, k, v, bg, u = _even_in(xp, mod, g[1], w_in, qkg, None, row_fn=row_p, tm=tm_p, kv_dtype=F32)
            o = _even_attn_p(q, k, v, even_lambda[e], sub, seq=seq, lam_init=lam_init)
            xp = _even_out(xp, o, bg, u, even_conv_w[e], w_out, mod, row_fn=row_p, seq=seq, tm=tm_p)
            even_k.append(k.reshape(batch, seq, A_HEADS, 2 * HEAD_DIM))
            even_v.append(v.reshape(batch, seq, A_HEADS, A_VDIM))

            q, k, v, bg, u = _even_in(xs, mod, g[1], w_in, qkg, rope_tabs, row_fn=row_s, tm=TOKEN_TILE, kv_dtype=BF16)
            kc = cache_even_k[:, e].reshape(dec_batch * past, A_QK)
            vc = cache_even_v[:, e].reshape(dec_batch * past, A_V)
            o = _even_attn_s(q, k, v, kc, vc, even_lambda[e], sub, seq=dec_seq, past=past, tq=EVEN_Q_TILE,
                             lam_init=lam_init)
            xs = _even_out(xs, o, bg, u, even_conv_w[e], w_out, mod, row_fn=row_s, seq=dec_seq, tm=TOKEN_TILE)
        else:
            e = l // 2
            w_in = odd_w_in[e].astype(BF16)
            w_out = odd_w_out[e].astype(BF16)
            qkg = jnp.tile(odd_qk_norm[e], (1, 2))
            sink = odd_sink[e]

            q, kd, vd, k, v = _odd_in(xp, mod, g[1], w_in, qkg, None, row_fn=row_p, tm=tm_p, keep_kv=True)
            o = _odd_attn_p(q, kd, vd, sink, seq=seq)
            xp = _odd_out(xp, o, w_out, mod, row_fn=row_p, tm=tm_p)
            odd_k.append(k.reshape(batch, seq, C_KV_HEADS, HEAD_DIM))
            odd_v.append(v.reshape(batch, seq, C_KV_HEADS, HEAD_DIM))

            q, kd, vd = _odd_in(xs, mod, g[1], w_in, qkg, rope_tabs, row_fn=row_s, tm=TOKEN_TILE, keep_kv=False)
            kc = cache_odd_k[:, e].reshape(dec_batch * past, C_KV_HEADS * HEAD_DIM)
            vc = cache_odd_v[:, e].reshape(dec_batch * past, C_KV_HEADS * HEAD_DIM)
            o = _odd_attn_s(q, kd, vd, kc, vc, sink, seq=dec_seq, past=past, tq=ODD_Q_TILE)
            xs = _odd_out(xs, o, w_out, mod, row_fn=row_s, tm=TOKEN_TILE)

        xp = _ffn(xp, mod, g[2], w13[1], w2[1], k0=6, row_fn=row_p, tm=tm_p)
        xs = _ffn(xs, mod, g[2], w13[1], w2[1], k0=6, row_fn=row_s, tm=TOKEN_TILE)

    return (xp.reshape(batch, seq, D_MODEL), xs.reshape(dec_batch, dec_seq, D_MODEL),
            jnp.stack(even_k, axis=1), jnp.stack(even_v, axis=1),
            jnp.stack(odd_k, axis=1), jnp.stack(odd_v, axis=1))
```

```python
import functools
import math

import jax
import jax.numpy as jnp
from jax import lax
from jax.experimental import pallas as pl
from jax.experimental.pallas import tpu as pltpu

F32 = jnp.float32
BF16 = jnp.bfloat16

D_MODEL = 1024
GRID_W = 64
HEAD_DIM = 64
AXIS_DIM = HEAD_DIM // 2
A_HEADS = 4
A_VDIM = 2 * HEAD_DIM
CONV_DIM = 512
C_HEADS = 16
C_KV_HEADS = 4
GROUP = C_HEADS // C_KV_HEADS
WINDOW = 128
D_FF = 2816
ROPE_BASE = 10000.0
N_MOD = 9
EPS = 1e-6
A_QK = A_HEADS * 2 * HEAD_DIM
A_V = A_HEADS * A_VDIM
EVEN_IN = 2 * A_QK + A_V + 3 * CONV_DIM
ODD_IN = (C_HEADS + 2 * C_KV_HEADS) * HEAD_DIM
LOG2E = math.log2(math.e)
Q_PRESCALE = HEAD_DIM ** -0.5 * LOG2E

LANES = 128
SUBLANES = 8
MOD_ROWS = 16
VMEM_LIMIT_BYTES = 56 * 2 ** 20

TOKEN_TILE = 512
EVEN_Q_TILE = 256
ODD_Q_TILE = 128


def _params(*semantics):
    return pltpu.CompilerParams(dimension_semantics=semantics, vmem_limit_bytes=VMEM_LIMIT_BYTES)


def _sigmoid(x):
    return 1.0 / (1.0 + jnp.exp(-x))


def _mod_norm(x, g, shift, scale):
    y = x * lax.rsqrt(jnp.mean(x * x, axis=-1, keepdims=True) + EPS)
    return (y * g) * (1.0 + scale) + shift


def _mod_row(mod_ref, k, r):
    return mod_ref[k, pl.ds(r, 1), :]


def _row_fn(row0, tokens_per_row, tile):
    return lambda i: row0 + (i * tile) // tokens_per_row


def _lane_lo(shape):
    return lax.broadcasted_iota(jnp.int32, shape, len(shape) - 1) < HEAD_DIM


def _seg_rms(xs, gain):
    lo = _lane_lo(xs.shape)
    sq = xs * xs
    s_lo = jnp.sum(jnp.where(lo, sq, 0.0), axis=-1, keepdims=True)
    s_hi = jnp.sum(jnp.where(lo, 0.0, sq), axis=-1, keepdims=True)
    inv = jnp.where(lo, lax.rsqrt(s_lo * (1.0 / HEAD_DIM) + EPS), lax.rsqrt(s_hi * (1.0 / HEAD_DIM) + EPS))
    return (xs * inv) * gain


def _rope(xs, cos, sin_signed):
    lane = lax.broadcasted_iota(jnp.int32, xs.shape, 1)
    first = (lane & (AXIS_DIM // 2)) == 0
    partner = jnp.where(first, pltpu.roll(xs, LANES - AXIS_DIM // 2, 1), pltpu.roll(xs, AXIS_DIM // 2, 1))
    return xs * cos + partner * sin_signed


def _dot_t(a, b):
    return lax.dot_general(a, b, (((1,), (1,)), ((), ())), preferred_element_type=F32)


def _dot(a, b):
    return jnp.dot(a, b, preferred_element_type=F32)


def _adaln_kernel(c_ref, w_ref, b_ref, o_ref):
    c = c_ref[...]
    s = c * _sigmoid(c)
    o_ref[0, 0] = _dot(s.astype(BF16), w_ref[0].astype(BF16)) + b_ref[0]


def _adaln(c_rows, w_mod, b_mod):
    depth = w_mod.shape[0]
    b3 = b_mod.reshape(depth * N_MOD, 1, D_MODEL)
    return pl.pallas_call(
        _adaln_kernel,
        out_shape=jax.ShapeDtypeStruct((depth, N_MOD, MOD_ROWS, D_MODEL), F32),
        grid=(depth, N_MOD),
        in_specs=[
            pl.BlockSpec((MOD_ROWS, D_MODEL), lambda l, j: (0, 0)),
            pl.BlockSpec((1, D_MODEL, D_MODEL), lambda l, j: (l, 0, j)),
            pl.BlockSpec((1, 1, D_MODEL), lambda l, j: (l * N_MOD + j, 0, 0)),
        ],
        out_specs=pl.BlockSpec((1, 1, MOD_ROWS, D_MODEL), lambda l, j: (l, j, 0, 0)),
        compiler_params=_params("parallel", "parallel"),
        name="adaln",
    )(c_rows, w_mod, b3)


def _ffn_kernel(x_ref, mod_ref, g_ref, w13_ref, w2_ref, o_ref, *, row_fn, k0):
    r = row_fn(pl.program_id(0))
    x = x_ref[...]
    h = _mod_norm(x, g_ref[...], _mod_row(mod_ref, k0, r), _mod_row(mod_ref, k0 + 1, r))
    a = _dot(h.astype(BF16), w13_ref[0, 0])
    gate = a[:, :D_FF]
    up = a[:, D_FF:]
    act = (gate * _sigmoid(gate)) * up
    y = _dot(act.astype(BF16), w2_ref[0, 0])
    o_ref[...] = x + (0.5 * _mod_row(mod_ref, k0 + 2, r)) * y


def _resident(shape, index=None):
    index = (0,) * len(shape) if index is None else index
    return pl.BlockSpec(shape, lambda *_: index, pipeline_mode=pl.Buffered(1))


def _ffn(x, mod, g, w13, w2, *, layer, slot, k0, row_fn, tm):
    t = x.shape[0]
    return pl.pallas_call(
        functools.partial(_ffn_kernel, row_fn=row_fn, k0=k0),
        out_shape=jax.ShapeDtypeStruct((t, D_MODEL), F32),
        grid=(t // tm,),
        in_specs=[
            pl.BlockSpec((tm, D_MODEL), lambda i: (i, 0)),
            _resident((N_MOD, MOD_ROWS, D_MODEL)),
            _resident((1, D_MODEL)),
            _resident((1, 1, D_MODEL, 2 * D_FF), (layer, slot, 0, 0)),
            _resident((1, 1, D_FF, D_MODEL), (layer, slot, 0, 0)),
        ],
        out_specs=pl.BlockSpec((tm, D_MODEL), lambda i: (i, 0)),
        compiler_params=_params("parallel"),
        name="ffn",
    )(x, mod, g, w13, w2)


def _even_in_kernel(x_ref, mod_ref, g_ref, w_ref, qkg_ref, *rest, row_fn, rope):
    if rope:
        cos_ref, sin_ref, q_ref, k_ref, v_ref, bg_ref, u_ref = rest
    else:
        q_ref, k_ref, v_ref, bg_ref, u_ref = rest
    r = row_fn(pl.program_id(0))
    h = _mod_norm(x_ref[...], g_ref[...], _mod_row(mod_ref, 3, r), _mod_row(mod_ref, 4, r))
    y = _dot(h.astype(BF16), w_ref[...])
    for hh in range(A_HEADS):
        sl = slice(hh * LANES, (hh + 1) * LANES)
        qs = _seg_rms(y[:, sl], qkg_ref[0:1, :])
        ks = _seg_rms(y[:, A_QK + hh * LANES:A_QK + (hh + 1) * LANES], qkg_ref[1:2, :])
        if rope:
            qs = _rope(qs, cos_ref[...], sin_ref[...])
            ks = _rope(ks, cos_ref[...], sin_ref[...])
        q_ref[:, sl] = (qs * Q_PRESCALE).astype(q_ref.dtype)
        k_ref[:, sl] = ks.astype(k_ref.dtype)
    v_ref[...] = y[:, 2 * A_QK:2 * A_QK + A_V].astype(v_ref.dtype)
    o = 2 * A_QK + A_V
    bg_ref[...] = y[:, o:o + CONV_DIM]
    u_ref[...] = y[:, o + CONV_DIM:o + 2 * CONV_DIM] * y[:, o + 2 * CONV_DIM:o + 3 * CONV_DIM]


def _even_in(x, mod, g, w_in, qkg, rope_tabs, *, row_fn, tm, kv_dtype):
    t = x.shape[0]
    rope = rope_tabs is not None
    in_specs = [
        pl.BlockSpec((tm, D_MODEL), lambda i: (i, 0)),
        pl.BlockSpec((N_MOD, MOD_ROWS, D_MODEL), lambda i: (0, 0, 0)),
        pl.BlockSpec((1, D_MODEL), lambda i: (0, 0)),
        pl.BlockSpec((D_MODEL, EVEN_IN), lambda i: (0, 0)),
        pl.BlockSpec((2, LANES), lambda i: (0, 0)),
    ]
    args = [x, mod, g, w_in, qkg]
    if rope:
        nt = rope_tabs[0].shape[0] // tm
        in_specs += [pl.BlockSpec((tm, LANES), lambda i: (i % nt, 0))] * 2
        args += list(rope_tabs)
    wide = lambda dt: jax.ShapeDtypeStruct((t, A_QK), dt)
    spec = pl.BlockSpec((tm, A_QK), lambda i: (i, 0))
    return pl.pallas_call(
        functools.partial(_even_in_kernel, row_fn=row_fn, rope=rope),
        out_shape=(wide(BF16), wide(kv_dtype), wide(kv_dtype), wide(F32), wide(F32)),
        grid=(t // tm,),
        in_specs=in_specs,
        out_specs=(spec,) * 5,
        compiler_params=_params("parallel"),
        name="even_in",
    )(*args)


def _lambda(lam_ref, lam_init):
    lf = lam_ref[...]
    a = jnp.sum(lf[0:1] * lf[1:2], axis=-1, keepdims=True)
    b = jnp.sum(lf[2:3] * lf[3:4], axis=-1, keepdims=True)
    return jnp.exp(a) - jnp.exp(b) + lam_init


def _subln(o, sub, lam_init):
    y = o * lax.rsqrt(jnp.mean(o * o, axis=-1, keepdims=True) + EPS)
    return (y * sub) * (1.0 - lam_init)


def _softmax_pv(q, k, v1):
    s = _dot_t(q, k)
    m = jnp.max(s, axis=-1, keepdims=True)
    r = _dot(jnp.exp2(s - m).astype(BF16), v1)
    return r[:, :LANES] * (1.0 / r[:, LANES:])


def _diff_attn_heads(q_ref, head_kv, lam_ref, sub_ref, o_ref, lam_init):
    lam = _lambda(lam_ref, lam_init)
    for h in range(A_HEADS):
        sl = slice(h * LANES, (h + 1) * LANES)
        q = q_ref[:, sl]
        k, v1 = head_kv(h)
        lo = _lane_lo(q.shape)
        zero = jnp.zeros_like(q)
        o1 = _softmax_pv(jnp.where(lo, q, zero), k, v1)
        o2 = _softmax_pv(jnp.where(lo, zero, q), k, v1)
        o_ref[:, sl] = _subln(o1 - lam * o2, sub_ref[...], lam_init).astype(o_ref.dtype)


def _even_attn_p_kernel(q_ref, k_ref, v_ref, lam_ref, sub_ref, o_ref, *, lam_init):
    ones = jnp.ones((k_ref.shape[0], LANES), BF16)

    def head_kv(h):
        sl = slice(h * LANES, (h + 1) * LANES)
        return k_ref[:, sl].astype(BF16), jnp.concatenate([v_ref[:, sl].astype(BF16), ones], axis=-1)

    _diff_attn_heads(q_ref, head_kv, lam_ref, sub_ref, o_ref, lam_init)


def _even_attn_p(q, k, v, lam_vec, sub, *, seq, lam_init):
    t = q.shape[0]
    blk = pl.BlockSpec((seq, A_QK), lambda b: (b, 0))
    return pl.pallas_call(
        functools.partial(_even_attn_p_kernel, lam_init=lam_init),
        out_shape=jax.ShapeDtypeStruct((t, A_V), BF16),
        grid=(t // seq,),
        in_specs=[blk, blk, blk, _resident((4, HEAD_DIM)), _resident((1, A_VDIM))],
        out_specs=blk,
        compiler_params=_params("parallel"),
        name="even_attn_prompt",
    )(q, k, v, lam_vec, sub)


def _even_attn_s_kernel(q_ref, kl_ref, vl_ref, kc_ref, vc_ref, lam_ref, sub_ref, o_ref, k_scr, v1_scr, *, past, lam_init):
    @pl.when(pl.program_id(1) == 0)
    def _():
        k_scr[:past, :] = kc_ref[...].astype(BF16)
        k_scr[past:, :] = kl_ref[...]
        for h in range(A_HEADS):
            sl = slice(h * LANES, (h + 1) * LANES)
            v1_scr[h, :past, :LANES] = vc_ref[:, sl].astype(BF16)
            v1_scr[h, past:, :LANES] = vl_ref[:, sl]
            v1_scr[h, :, LANES:] = jnp.ones((v1_scr.shape[1], LANES), BF16)

    def head_kv(h):
        return k_scr[:, h * LANES:(h + 1) * LANES], v1_scr[h]

    _diff_attn_heads(q_ref, head_kv, lam_ref, sub_ref, o_ref, lam_init)


def _even_attn_s(q, kl, vl, kc, vc, lam_vec, sub, *, seq, past, tq, lam_init):
    t = q.shape[0]
    nq = seq // tq
    qblk = pl.BlockSpec((tq, A_QK), lambda b, i: (b * nq + i, 0))
    lat = pl.BlockSpec((seq, A_QK), lambda b, i: (b, 0))
    ctx = pl.BlockSpec((past, A_QK), lambda b, i: (b, 0))
    return pl.pallas_call(
        functools.partial(_even_attn_s_kernel, past=past, lam_init=lam_init),
        out_shape=jax.ShapeDtypeStruct((t, A_V), BF16),
        grid=(t // seq, nq),
        in_specs=[qblk, lat, lat, ctx, ctx, _resident((4, HEAD_DIM)), _resident((1, A_VDIM))],
        out_specs=qblk,
        scratch_shapes=[pltpu.VMEM((past + seq, A_QK), BF16),
                        pltpu.VMEM((A_HEADS, past + seq, 2 * LANES), BF16)],
        compiler_params=_params("parallel", "arbitrary"),
        name="even_attn_sample",
    )(q, kl, vl, kc, vc, lam_vec, sub)


def _even_out_kernel(x_ref, o_ref, bg_ref, u_ref, up_ref, un_ref, cw_ref, w_ref, mod_ref, out_ref, *, row_fn, seq, tm):
    i = pl.program_id(0)
    r = row_fn(i)
    u = u_ref[...]
    row = lax.broadcasted_iota(jnp.int32, (tm, 1), 0)
    pos = (row + i * tm) % seq
    u_dn = jnp.where(row == 0, up_ref[SUBLANES - 1:SUBLANES, :], pltpu.roll(u, 1, 0))
    u_dn = jnp.where(pos == 0, 0.0, u_dn)
    u_up = jnp.where(row == tm - 1, un_ref[0:1, :], pltpu.roll(u, tm - 1, 0))
    u_up = jnp.where(pos == seq - 1, 0.0, u_up)
    y = bg_ref[...] * (cw_ref[0:1, :] * u_dn + cw_ref[1:2, :] * u + cw_ref[2:3, :] * u_up)
    cat = jnp.concatenate([o_ref[...], y.astype(BF16)], axis=-1)
    mix = _dot(cat, w_ref[...])
    out_ref[...] = x_ref[...] + _mod_row(mod_ref, 5, r) * mix


def _even_out(x, o, bg, u, conv_w, w_out, mod, *, row_fn, seq, tm):
    t = x.shape[0]
    nb = tm // SUBLANES
    last = t // SUBLANES - 1
    return pl.pallas_call(
        functools.partial(_even_out_kernel, row_fn=row_fn, seq=seq, tm=tm),
        out_shape=jax.ShapeDtypeStruct((t, D_MODEL), F32),
        grid=(t // tm,),
        in_specs=[
            pl.BlockSpec((tm, D_MODEL), lambda i: (i, 0)),
            pl.BlockSpec((tm, A_V), lambda i: (i, 0)),
            pl.BlockSpec((tm, CONV_DIM), lambda i: (i, 0)),
            pl.BlockSpec((tm, CONV_DIM), lambda i: (i, 0)),
            pl.BlockSpec((SUBLANES, CONV_DIM), lambda i: (jnp.maximum(i * nb - 1, 0), 0)),
            pl.BlockSpec((SUBLANES, CONV_DIM), lambda i: (jnp.minimum((i + 1) * nb, last), 0)),
            pl.BlockSpec((3, CONV_DIM), lambda i: (0, 0)),
            pl.BlockSpec((D_MODEL, D_MODEL), lambda i: (0, 0)),
            pl.BlockSpec((N_MOD, MOD_ROWS, D_MODEL), lambda i: (0, 0, 0)),
        ],
        out_specs=pl.BlockSpec((tm, D_MODEL), lambda i: (i, 0)),
        compiler_params=_params("parallel"),
        name="even_out",
    )(x, o, bg, u, u, u, conv_w, w_out, mod)


def _dup_halves(x):
    lo = _lane_lo(x.shape)
    sw = pltpu.roll(x, HEAD_DIM, 1)
    return jnp.where(lo, x, sw), jnp.where(lo, sw, x)


def _ones_halves(x):
    lo = _lane_lo(x.shape)
    return jnp.where(lo, x, 1.0), jnp.where(lo, pltpu.roll(x, HEAD_DIM, 1), 1.0)


def _odd_in_kernel(x_ref, mod_ref, g_ref, w_ref, qkg_ref, *rest, row_fn, rope, keep_kv):
    rest = list(rest)
    if rope:
        cos_ref, sin_ref = rest[:2]
        rest = rest[2:]
    if keep_kv:
        q_ref, kd_ref, vd_ref, k_ref, v_ref = rest
    else:
        q_ref, kd_ref, vd_ref = rest
    r = row_fn(pl.program_id(0))
    h = _mod_norm(x_ref[...], g_ref[...], _mod_row(mod_ref, 3, r), _mod_row(mod_ref, 4, r))
    y = _dot(h.astype(BF16), w_ref[...])
    nq = C_HEADS * HEAD_DIM
    nk = C_KV_HEADS * HEAD_DIM
    for c in range(nq // LANES):
        sl = slice(c * LANES, (c + 1) * LANES)
        qs = _seg_rms(y[:, sl], qkg_ref[0:1, :])
        if rope:
            qs = _rope(qs, cos_ref[...], sin_ref[...])
        q_ref[:, sl] = (qs * Q_PRESCALE).astype(q_ref.dtype)
    for c in range(nk // LANES):
        sl = slice(c * LANES, (c + 1) * LANES)
        ks = _seg_rms(y[:, nq + c * LANES:nq + (c + 1) * LANES], qkg_ref[1:2, :])
        vs = y[:, nq + nk + c * LANES:nq + nk + (c + 1) * LANES]
        if keep_kv:
            k_ref[:, sl] = ks
            v_ref[:, sl] = vs
        if rope:
            ks = _rope(ks, cos_ref[...], sin_ref[...])
        for (d0, d1), ref in ((_dup_halves(ks), kd_ref), (_ones_halves(vs), vd_ref)):
            ref[:, 2 * c * LANES:(2 * c + 1) * LANES] = d0.astype(ref.dtype)
            ref[:, (2 * c + 1) * LANES:(2 * c + 2) * LANES] = d1.astype(ref.dtype)


def _odd_in(x, mod, g, w_in, qkg, rope_tabs, *, row_fn, tm, keep_kv):
    t = x.shape[0]
    rope = rope_tabs is not None
    in_specs = [
        pl.BlockSpec((tm, D_MODEL), lambda i: (i, 0)),
        pl.BlockSpec((N_MOD, MOD_ROWS, D_MODEL), lambda i: (0, 0, 0)),
        pl.BlockSpec((1, D_MODEL), lambda i: (0, 0)),
        pl.BlockSpec((D_MODEL, ODD_IN), lambda i: (0, 0)),
        pl.BlockSpec((2, LANES), lambda i: (0, 0)),
    ]
    args = [x, mod, g, w_in, qkg]
    if rope:
        nt = rope_tabs[0].shape[0] // tm
        in_specs += [pl.BlockSpec((tm, LANES), lambda i: (i % nt, 0))] * 2
        args += list(rope_tabs)
    nq = C_HEADS * HEAD_DIM
    nd = C_KV_HEADS * LANES
    nk = C_KV_HEADS * HEAD_DIM
    out_shape = [jax.ShapeDtypeStruct((t, nq), BF16), jax.ShapeDtypeStruct((t, nd), BF16),
                 jax.ShapeDtypeStruct((t, nd), BF16)]
    out_specs = [pl.BlockSpec((tm, nq), lambda i: (i, 0)), pl.BlockSpec((tm, nd), lambda i: (i, 0)),
                 pl.BlockSpec((tm, nd), lambda i: (i, 0))]
    if keep_kv:
        out_shape += [jax.ShapeDtypeStruct((t, nk), F32)] * 2
        out_specs += [pl.BlockSpec((tm, nk), lambda i: (i, 0))] * 2
    return pl.pallas_call(
        functools.partial(_odd_in_kernel, row_fn=row_fn, rope=rope, keep_kv=keep_kv),
        out_shape=tuple(out_shape),
        grid=(t // tm,),
        in_specs=in_specs,
        out_specs=tuple(out_specs),
        compiler_params=_params("parallel"),
        name="odd_in",
    )(*args)


def _stack_heads(q_ref, g, rows):
    parts = []
    for rr in range(GROUP):
        c = g * (GROUP // 2) + rr // 2
        qc = q_ref[:, c * LANES:(c + 1) * LANES]
        lo = _lane_lo(qc.shape)
        keep = lo if rr % 2 == 0 else jnp.logical_not(lo)
        parts.append(jnp.where(keep, qc, jnp.zeros_like(qc)))
    return jnp.concatenate(parts, axis=0)


def _gqa_group(q4, scores, vals, sinks, rows):
    es = [[] for _ in scores]
    sink_terms = []
    for rr in range(GROUP):
        rs = slice(rr * rows, (rr + 1) * rows)
        m = functools.reduce(jnp.maximum, [jnp.max(s[rs], axis=-1, keepdims=True) for s in scores])
        m = jnp.maximum(m, sinks[rr])
        for j, s in enumerate(scores):
            es[j].append(jnp.exp2(s[rs] - m).astype(BF16))
        sink_terms.append(jnp.exp2(sinks[rr] - m))
    r = functools.reduce(jnp.add, [_dot(jnp.concatenate(e, axis=0), v) for e, v in zip(es, vals)])
    heads = []
    for rr in range(GROUP):
        rh = r[rr * rows:(rr + 1) * rows]
        heads.append(rh * (1.0 / (pltpu.roll(rh, HEAD_DIM, 1) + sink_terms[rr])))
    lo = _lane_lo((rows, LANES))
    slabs = [jnp.where(lo, heads[2 * c], pltpu.roll(heads[2 * c + 1], HEAD_DIM, 1)) for c in range(GROUP // 2)]
    return jnp.concatenate(slabs, axis=-1)


def _odd_attn_p_kernel(sink_ref, q_ref, kd_ref, vd_ref, o_ref, *, seq):
    g = pl.program_id(1)
    q4 = _stack_heads(q_ref, 0, seq)
    sinks = [sink_ref[g * GROUP + rr] * LOG2E for rr in range(GROUP)]
    o_ref[...] = _gqa_group(q4, [_dot_t(q4, kd_ref[...])], [vd_ref[...]], sinks, seq).astype(o_ref.dtype)


def _odd_attn_p(q, kd, vd, sink, *, seq):
    t = q.shape[0]
    gw = GROUP * HEAD_DIM
    return pl.pallas_call(
        functools.partial(_odd_attn_p_kernel, seq=seq),
        out_shape=jax.ShapeDtypeStruct((t, C_HEADS * HEAD_DIM), BF16),
        grid=(t // seq, C_KV_HEADS),
        in_specs=[
            pl.BlockSpec(memory_space=pltpu.SMEM),
            pl.BlockSpec((seq, gw), lambda b, g: (b, g)),
            pl.BlockSpec((seq, LANES), lambda b, g: (b, g)),
            pl.BlockSpec((seq, LANES), lambda b, g: (b, g)),
        ],
        out_specs=pl.BlockSpec((seq, gw), lambda b, g: (b, g)),
        compiler_params=_params("parallel", "parallel"),
        name="odd_attn_prompt",
    )(sink, q, kd, vd)


def _odd_attn_s_kernel(sink_ref, q_ref, kd_ref, vd_ref, kc_ref, vc_ref, o_ref, kcd_scr, vcd_scr, *, seq, tq):
    i = pl.program_id(1)
    span = tq + 2 * WINDOW

    @pl.when(i == 0)
    def _():
        for c in range(C_KV_HEADS // 2):
            sl = slice(c * LANES, (c + 1) * LANES)
            for (d0, d1), dst in ((_dup_halves(kc_ref[:, sl]), kcd_scr), (_ones_halves(vc_ref[:, sl]), vcd_scr)):
                dst[:, 2 * c * LANES:(2 * c + 1) * LANES] = d0.astype(BF16)
                dst[:, (2 * c + 1) * LANES:(2 * c + 2) * LANES] = d1.astype(BF16)

    start = pl.multiple_of(jnp.clip(i * tq - WINDOW, 0, seq - span), WINDOW)
    rows = GROUP * tq
    qpos = i * tq + lax.broadcasted_iota(jnp.int32, (rows, span), 0) % tq
    kpos = start + lax.broadcasted_iota(jnp.int32, (rows, span), 1)
    valid = jnp.abs(qpos - kpos) <= WINDOW
    for g in range(C_KV_HEADS):
        gl = slice(g * LANES, (g + 1) * LANES)
        q4 = _stack_heads(q_ref, g, tq)
        s_c = _dot_t(q4, kcd_scr[:, gl])
        s_l = jnp.where(valid, _dot_t(q4, kd_ref[pl.ds(start, span), gl]), -jnp.inf)
        sinks = [sink_ref[g * GROUP + rr] * LOG2E for rr in range(GROUP)]
        og = _gqa_group(q4, [s_c, s_l], [vcd_scr[:, gl], vd_ref[pl.ds(start, span), gl]], sinks, tq)
        o_ref[:, g * GROUP * HEAD_DIM:(g + 1) * GROUP * HEAD_DIM] = og.astype(o_ref.dtype)


def _odd_attn_s(q, kd, vd, kc, vc, sink, *, seq, past, tq):
    t = q.shape[0]
    nq = seq // tq
    nqw = C_HEADS * HEAD_DIM
    nd = C_KV_HEADS * LANES
    nk = C_KV_HEADS * HEAD_DIM
    return pl.pallas_call(
        functools.partial(_odd_attn_s_kernel, seq=seq, tq=tq),
        out_shape=jax.ShapeDtypeStruct((t, nqw), BF16),
        grid=(t // seq, nq),
        in_specs=[
            pl.BlockSpec(memory_space=pltpu.SMEM),
            pl.BlockSpec((tq, nqw), lambda b, i: (b * nq + i, 0)),
            pl.BlockSpec((seq, nd), lambda b, i: (b, 0)),
            pl.BlockSpec((seq, nd), lambda b, i: (b, 0)),
            pl.BlockSpec((past, nk), lambda b, i: (b, 0)),
            pl.BlockSpec((past, nk), lambda b, i: (b, 0)),
        ],
        out_specs=pl.BlockSpec((tq, nqw), lambda b, i: (b * nq + i, 0)),
        scratch_shapes=[pltpu.VMEM((past, nd), BF16), pltpu.VMEM((past, nd), BF16)],
        compiler_params=_params("parallel", "arbitrary"),
        name="odd_attn_sample",
    )(sink, q, kd, vd, kc, vc)


def _odd_out_kernel(x_ref, o_ref, w_ref, mod_ref, out_ref, *, row_fn):
    r = row_fn(pl.program_id(0))
    out_ref[...] = x_ref[...] + _mod_row(mod_ref, 5, r) * _dot(o_ref[...], w_ref[...])


def _odd_out(x, o, w_out, mod, *, row_fn, tm):
    t = x.shape[0]
    return pl.pallas_call(
        functools.partial(_odd_out_kernel, row_fn=row_fn),
        out_shape=jax.ShapeDtypeStruct((t, D_MODEL), F32),
        grid=(t // tm,),
        in_specs=[
            pl.BlockSpec((tm, D_MODEL), lambda i: (i, 0)),
            pl.BlockSpec((tm, D_MODEL), lambda i: (i, 0)),
            pl.BlockSpec((D_MODEL, D_MODEL), lambda i: (0, 0)),
            pl.BlockSpec((N_MOD, MOD_ROWS, D_MODEL), lambda i: (0, 0, 0)),
        ],
        out_specs=pl.BlockSpec((tm, D_MODEL), lambda i: (i, 0)),
        compiler_params=_params("parallel"),
        name="odd_out",
    )(x, o, w_out, mod)


def _rope_tables(n):
    rows = n // GRID_W
    row = jnp.repeat(jnp.arange(rows, dtype=F32), GRID_W)
    col = jnp.tile(jnp.arange(GRID_W, dtype=F32), rows)
    inv = ROPE_BASE ** (-jnp.arange(0, AXIS_DIM, 2, dtype=F32) / AXIS_DIM)
    ang_r = row[:, None] * inv[None, :]
    ang_c = col[:, None] * inv[None, :]
    ang = jnp.concatenate([ang_r, ang_r, ang_c, ang_c], axis=-1)
    sign = jnp.where((jnp.arange(HEAD_DIM) & (AXIS_DIM // 2)) == 0, -1.0, 1.0).astype(F32)
    cos = jnp.cos(ang)
    sin = jnp.sin(ang) * sign[None, :]
    return jnp.tile(cos, (1, 2)), jnp.tile(sin, (1, 2))


def kernel(x_prompt, x_sample, cache_even_k, cache_even_v, cache_odd_k, cache_odd_v, c, c_ctx, w_mod, b_mod, norm_g, ffn_w13, ffn_w2, even_w_in, even_w_out, even_qk_norm, even_lambda, even_subln, even_conv_w, odd_w_in, odd_w_out, odd_qk_norm, odd_sink):
    batch, seq, _ = x_prompt.shape
    dec_batch, dec_seq, _ = x_sample.shape
    past = cache_even_k.shape[2]
    depth = w_mod.shape[0]
    assert 1 + dec_batch <= MOD_ROWS and seq % SUBLANES == 0 and dec_seq % TOKEN_TILE == 0

    xp = x_prompt.reshape(batch * seq, D_MODEL)
    xs = x_sample.reshape(dec_batch * dec_seq, D_MODEL)
    tm_p = min(TOKEN_TILE, batch * seq)
    row_p = _row_fn(0, batch * seq, tm_p)
    row_s = _row_fn(1, dec_seq, TOKEN_TILE)

    c_rows = jnp.zeros((MOD_ROWS, D_MODEL), F32).at[0].set(c_ctx).at[1:1 + dec_batch].set(c)
    mods = _adaln(c_rows, w_mod, b_mod)
    rope_tabs = _rope_tables(dec_seq)
    w13 = ffn_w13.astype(BF16)
    w2 = ffn_w2.astype(BF16)

    even_k, even_v, odd_k, odd_v = [], [], [], []
    for l in range(depth):
        mod = mods[l]
        g = [norm_g[l, s].reshape(1, D_MODEL) for s in range(3)]

        xp = _ffn(xp, mod, g[0], w13, w2, layer=l, slot=0, k0=0, row_fn=row_p, tm=tm_p)
        xs = _ffn(xs, mod, g[0], w13, w2, layer=l, slot=0, k0=0, row_fn=row_s, tm=TOKEN_TILE)

        if l % 2 == 0:
            e = l // 2
            lam_init = 0.8 - 0.6 * math.exp(-0.3 * l)
            w_in = even_w_in[e].astype(BF16)
            w_out = even_w_out[e].astype(BF16)
            qkg = jnp.tile(even_qk_norm[e], (1, 2))
            sub = even_subln[e].reshape(1, A_VDIM)

            q, k, v, bg, u = _even_in(xp, mod, g[1], w_in, qkg, None, row_fn=row_p, tm=tm_p, kv_dtype=F32)
            o = _even_attn_p(q, k, v, even_lambda[e], sub, seq=seq, lam_init=lam_init)
            xp = _even_out(xp, o, bg, u, even_conv_w[e], w_out, mod, row_fn=row_p, seq=seq, tm=tm_p)
            even_k.append(k.reshape(batch, seq, A_HEADS, 2 * HEAD_DIM))
            even_v.append(v.reshape(batch, seq, A_HEADS, A_VDIM))

            q, k, v, bg, u = _even_in(xs, mod, g[1], w_in, qkg, rope_tabs, row_fn=row_s, tm=TOKEN_TILE, kv_dtype=BF16)
            kc = cache_even_k[:, e].reshape(dec_batch * past, A_QK)
            vc = cache_even_v[:, e].reshape(dec_batch * past, A_V)
            o = _even_attn_s(q, k, v, kc, vc, even_lambda[e], sub, seq=dec_seq, past=past, tq=EVEN_Q_TILE,
                             lam_init=lam_init)
            xs = _even_out(xs, o, bg, u, even_conv_w[e], w_out, mod, row_fn=row_s, seq=dec_seq, tm=TOKEN_TILE)
        else:
            e = l // 2
            w_in = odd_w_in[e].astype(BF16)
            w_out = odd_w_out[e].astype(BF16)
            qkg = jnp.tile(odd_qk_norm[e], (1, 2))
            sink = odd_sink[e]

            q, kd, vd, k, v = _odd_in(xp, mod, g[1], w_in, qkg, None, row_fn=row_p, tm=tm_p, keep_kv=True)
            o = _odd_attn_p(q, kd, vd, sink, seq=seq)
            xp = _odd_out(xp, o, w_out, mod, row_fn=row_p, tm=tm_p)
            odd_k.append(k.reshape(batch, seq, C_KV_HEADS, HEAD_DIM))
            odd_v.append(v.reshape(batch, seq, C_KV_HEADS, HEAD_DIM))

            q, kd, vd = _odd_in(xs, mod, g[1], w_in, qkg, rope_tabs, row_fn=row_s, tm=TOKEN_TILE, keep_kv=False)
            kc = cache_odd_k[:, e].reshape(dec_batch * past, C_KV_HEADS * HEAD_DIM)
            vc = cache_odd_v[:, e].reshape(dec_batch * past, C_KV_HEADS * HEAD_DIM)
            o = _odd_attn_s(q, kd, vd, kc, vc, sink, seq=dec_seq, past=past, tq=ODD_Q_TILE)
            xs = _odd_out(xs, o, w_out, mod, row_fn=row_s, tm=TOKEN_TILE)

        xp = _ffn(xp, mod, g[2], w13, w2, layer=l, slot=1, k0=6, row_fn=row_p, tm=tm_p)
        xs = _ffn(xs, mod, g[2], w13, w2, layer=l, slot=1, k0=6, row_fn=row_s, tm=TOKEN_TILE)

    return (xp.reshape(batch, seq, D_MODEL), xs.reshape(dec_batch, dec_seq, D_MODEL),
            jnp.stack(even_k, axis=1), jnp.stack(even_v, axis=1),
            jnp.stack(odd_k, axis=1), jnp.stack(odd_v, axis=1))
```

```python
import functools
import math

import jax
import jax.numpy as jnp
from jax import lax
from jax.experimental import pallas as pl
from jax.experimental.pallas import tpu as pltpu

F32 = jnp.float32
BF16 = jnp.bfloat16

D_MODEL = 1024
GRID_W = 64
HEAD_DIM = 64
AXIS_DIM = HEAD_DIM // 2
A_HEADS = 4
A_VDIM = 2 * HEAD_DIM
CONV_DIM = 512
C_HEADS = 16
C_KV_HEADS = 4
GROUP = C_HEADS // C_KV_HEADS
WINDOW = 128
D_FF = 2816
ROPE_BASE = 10000.0
N_MOD = 9
EPS = 1e-6
A_QK = A_HEADS * 2 * HEAD_DIM
A_V = A_HEADS * A_VDIM
EVEN_IN = 2 * A_QK + A_V + 3 * CONV_DIM
ODD_IN = (C_HEADS + 2 * C_KV_HEADS) * HEAD_DIM
LOG2E = math.log2(math.e)
Q_PRESCALE = HEAD_DIM ** -0.5 * LOG2E

LANES = 128
SUBLANES = 8
MOD_ROWS = 16
VMEM_LIMIT_BYTES = 56 * 2 ** 20

TOKEN_TILE = 512
EVEN_Q_TILE = 256
ODD_Q_TILE = 128


def _params(*semantics):
    return pltpu.CompilerParams(dimension_semantics=semantics, vmem_limit_bytes=VMEM_LIMIT_BYTES)


def _sigmoid(x):
    return 1.0 / (1.0 + jnp.exp(-x))


def _mod_norm(x, g, shift, scale):
    y = x * lax.rsqrt(jnp.mean(x * x, axis=-1, keepdims=True) + EPS)
    return (y * g) * (1.0 + scale) + shift


def _mod_row(mod_ref, k, r):
    return mod_ref[k, pl.ds(r, 1), :]


def _row_fn(row0, tokens_per_row, tile):
    return lambda i: row0 + (i * tile) // tokens_per_row


def _lane_lo(shape):
    return lax.broadcasted_iota(jnp.int32, shape, len(shape) - 1) < HEAD_DIM


def _seg_rms(xs, gain):
    lo = _lane_lo(xs.shape)
    sq = xs * xs
    s_lo = jnp.sum(jnp.where(lo, sq, 0.0), axis=-1, keepdims=True)
    s_hi = jnp.sum(jnp.where(lo, 0.0, sq), axis=-1, keepdims=True)
    inv = lax.rsqrt(jnp.where(lo, s_lo, s_hi) * (1.0 / HEAD_DIM) + EPS)
    return (xs * inv) * gain


def _half_sum_matrix():
    r = lax.broadcasted_iota(jnp.int32, (2 * LANES, LANES), 0)
    c = lax.broadcasted_iota(jnp.int32, (2 * LANES, LANES), 1)
    return jnp.where((r & HEAD_DIM) == (c & HEAD_DIM), 1.0, 0.0).astype(BF16)


def _seg_rms_mxu(xs, gain, ones2):
    sq = xs * xs
    head = sq.astype(BF16)
    rest = (sq - head.astype(F32)).astype(BF16)
    ssum = _dot(jnp.concatenate([head, rest], axis=-1), ones2)
    return (xs * lax.rsqrt(ssum * (1.0 / HEAD_DIM) + EPS)) * gain


def _rope(xs, cos, sin_signed):
    lane = lax.broadcasted_iota(jnp.int32, xs.shape, 1)
    first = (lane & (AXIS_DIM // 2)) == 0
    partner = jnp.where(first, pltpu.roll(xs, LANES - AXIS_DIM // 2, 1), pltpu.roll(xs, AXIS_DIM // 2, 1))
    return xs * cos + partner * sin_signed


def _dot_t(a, b):
    return lax.dot_general(a, b, (((1,), (1,)), ((), ())), preferred_element_type=F32)


def _dot(a, b):
    return jnp.dot(a, b, preferred_element_type=F32)


def _adaln_kernel(c_ref, w_ref, b_ref, o_ref):
    c = c_ref[...]
    s = c * _sigmoid(c)
    o_ref[0, 0] = _dot(s.astype(BF16), w_ref[0].astype(BF16)) + b_ref[0]


def _adaln(c_rows, w_mod, b_mod):
    depth = w_mod.shape[0]
    b3 = b_mod.reshape(depth * N_MOD, 1, D_MODEL)
    return pl.pallas_call(
        _adaln_kernel,
        out_shape=jax.ShapeDtypeStruct((depth, N_MOD, MOD_ROWS, D_MODEL), F32),
        grid=(depth, N_MOD),
        in_specs=[
            pl.BlockSpec((MOD_ROWS, D_MODEL), lambda l, j: (0, 0)),
            pl.BlockSpec((1, D_MODEL, D_MODEL), lambda l, j: (l, 0, j)),
            pl.BlockSpec((1, 1, D_MODEL), lambda l, j: (l * N_MOD + j, 0, 0)),
        ],
        out_specs=pl.BlockSpec((1, 1, MOD_ROWS, D_MODEL), lambda l, j: (l, j, 0, 0)),
        compiler_params=_params("parallel", "parallel"),
        name="adaln",
    )(c_rows, w_mod, b3)


def _ffn_tail(x, r, mod_ref, g_ref, w13_ref, w2_ref, o_ref, k0):
    h = _mod_norm(x, g_ref[...], _mod_row(mod_ref, k0, r), _mod_row(mod_ref, k0 + 1, r))
    a = _dot(h.astype(BF16), w13_ref[0, 0])
    gate = a[:, :D_FF]
    up = a[:, D_FF:]
    act = (gate * _sigmoid(gate)) * up
    y = _dot(act.astype(BF16), w2_ref[0, 0])
    o_ref[...] = x + (0.5 * _mod_row(mod_ref, k0 + 2, r)) * y


def _ffn_kernel(x_ref, mod_ref, g_ref, w13_ref, w2_ref, o_ref, *, row_fn):
    _ffn_tail(x_ref[...], row_fn(pl.program_id(0)), mod_ref, g_ref, w13_ref, w2_ref, o_ref, 0)


def _conv_gate(bg_ref, u_ref, up_ref, un_ref, cw_ref, i, seq, tm):
    u = u_ref[...]
    row = lax.broadcasted_iota(jnp.int32, (tm, 1), 0)
    pos = (row + i * tm) % seq
    u_dn = jnp.where(row == 0, up_ref[SUBLANES - 1:SUBLANES, :], pltpu.roll(u, 1, 0))
    u_dn = jnp.where(pos == 0, 0.0, u_dn)
    u_up = jnp.where(row == tm - 1, un_ref[0:1, :], pltpu.roll(u, tm - 1, 0))
    u_up = jnp.where(pos == seq - 1, 0.0, u_up)
    return bg_ref[...] * (cw_ref[0:1, :] * u_dn + cw_ref[1:2, :] * u + cw_ref[2:3, :] * u_up)


def _even_out_ffn_kernel(x_ref, a_ref, bg_ref, u_ref, up_ref, un_ref, cw_ref, wo_ref, mod_ref, g_ref, w13_ref, w2_ref,
                         o_ref, *, row_fn, seq, tm):
    i = pl.program_id(0)
    r = row_fn(i)
    y = _conv_gate(bg_ref, u_ref, up_ref, un_ref, cw_ref, i, seq, tm)
    mix = _dot(jnp.concatenate([a_ref[...], y.astype(BF16)], axis=-1), wo_ref[...])
    x = x_ref[...] + _mod_row(mod_ref, 5, r) * mix
    _ffn_tail(x, r, mod_ref, g_ref, w13_ref, w2_ref, o_ref, 6)


def _odd_out_ffn_kernel(x_ref, a_ref, wo_ref, mod_ref, g_ref, w13_ref, w2_ref, o_ref, *, row_fn):
    r = row_fn(pl.program_id(0))
    x = x_ref[...] + _mod_row(mod_ref, 5, r) * _dot(a_ref[...], wo_ref[...])
    _ffn_tail(x, r, mod_ref, g_ref, w13_ref, w2_ref, o_ref, 6)


def _resident(shape, index=None):
    index = (0,) * len(shape) if index is None else index
    return pl.BlockSpec(shape, lambda *_: index, pipeline_mode=pl.Buffered(1))


def _ffn(x, mod, g, w13, w2, *, layer, slot, row_fn, tm, mixer=None):
    t = x.shape[0]
    tile = lambda n: pl.BlockSpec((tm, n), lambda i: (i, 0))
    ffn_specs = [
        _resident((N_MOD, MOD_ROWS, D_MODEL)),
        _resident((1, D_MODEL)),
        _resident((1, 1, D_MODEL, 2 * D_FF), (layer, slot, 0, 0)),
        _resident((1, 1, D_FF, D_MODEL), (layer, slot, 0, 0)),
    ]
    ffn_args = [mod, g, w13, w2]
    if mixer is None:
        body = functools.partial(_ffn_kernel, row_fn=row_fn)
        in_specs, args, name = [tile(D_MODEL)], [x], "ffn"
    elif len(mixer) == 2:
        w_out, attn = mixer
        body = functools.partial(_odd_out_ffn_kernel, row_fn=row_fn)
        in_specs = [tile(D_MODEL), tile(attn.shape[1]), _resident(w_out.shape)]
        args, name = [x, attn, w_out], "odd_out_ffn"
    else:
        w_out, attn, bg, u, conv_w, seq = mixer
        nb = tm // SUBLANES
        last = t // SUBLANES - 1
        body = functools.partial(_even_out_ffn_kernel, row_fn=row_fn, seq=seq, tm=tm)
        in_specs = [
            tile(D_MODEL), tile(A_V), tile(CONV_DIM), tile(CONV_DIM),
            pl.BlockSpec((SUBLANES, CONV_DIM), lambda i: (jnp.maximum(i * nb - 1, 0), 0)),
            pl.BlockSpec((SUBLANES, CONV_DIM), lambda i: (jnp.minimum((i + 1) * nb, last), 0)),
            _resident(conv_w.shape), _resident(w_out.shape),
        ]
        args, name = [x, attn, bg, u, u, u, conv_w, w_out], "even_out_ffn"
    return pl.pallas_call(
        body,
        out_shape=jax.ShapeDtypeStruct((t, D_MODEL), F32),
        grid=(t // tm,),
        in_specs=in_specs + ffn_specs,
        out_specs=tile(D_MODEL),
        compiler_params=_params("parallel"),
        name=name,
    )(*args, *ffn_args)


def _even_in_kernel(x_ref, mod_ref, g_ref, w_ref, qkg_ref, *rest, row_fn, rope):
    if rope:
        cos_ref, sin_ref, q_ref, k_ref, v_ref, bg_ref, u_ref = rest
    else:
        q_ref, k_ref, v_ref, bg_ref, u_ref = rest
    r = row_fn(pl.program_id(0))
    h = _mod_norm(x_ref[...], g_ref[...], _mod_row(mod_ref, 3, r), _mod_row(mod_ref, 4, r))
    y = _dot(h.astype(BF16), w_ref[...])
    q_gain = qkg_ref[0:1, :] * Q_PRESCALE
    for hh in range(A_HEADS):
        sl = slice(hh * LANES, (hh + 1) * LANES)
        qs = _seg_rms(y[:, sl], q_gain)
        ks = _seg_rms(y[:, A_QK + hh * LANES:A_QK + (hh + 1) * LANES], qkg_ref[1:2, :])
        if rope:
            qs = _rope(qs, cos_ref[...], sin_ref[...])
            ks = _rope(ks, cos_ref[...], sin_ref[...])
        q_ref[:, sl] = qs.astype(q_ref.dtype)
        k_ref[:, sl] = ks.astype(k_ref.dtype)
    v_ref[...] = y[:, 2 * A_QK:2 * A_QK + A_V].astype(v_ref.dtype)
    o = 2 * A_QK + A_V
    bg_ref[...] = y[:, o:o + CONV_DIM]
    u_ref[...] = y[:, o + CONV_DIM:o + 2 * CONV_DIM] * y[:, o + 2 * CONV_DIM:o + 3 * CONV_DIM]


def _even_in(x, mod, g, w_in, qkg, rope_tabs, *, row_fn, tm, kv_dtype):
    t = x.shape[0]
    rope = rope_tabs is not None
    in_specs = [
        pl.BlockSpec((tm, D_MODEL), lambda i: (i, 0)),
        pl.BlockSpec((N_MOD, MOD_ROWS, D_MODEL), lambda i: (0, 0, 0)),
        pl.BlockSpec((1, D_MODEL), lambda i: (0, 0)),
        pl.BlockSpec((D_MODEL, EVEN_IN), lambda i: (0, 0)),
        pl.BlockSpec((2, LANES), lambda i: (0, 0)),
    ]
    args = [x, mod, g, w_in, qkg]
    if rope:
        nt = rope_tabs[0].shape[0] // tm
        in_specs += [pl.BlockSpec((tm, LANES), lambda i: (i % nt, 0))] * 2
        args += list(rope_tabs)
    wide = lambda dt: jax.ShapeDtypeStruct((t, A_QK), dt)
    spec = pl.BlockSpec((tm, A_QK), lambda i: (i, 0))
    return pl.pallas_call(
        functools.partial(_even_in_kernel, row_fn=row_fn, rope=rope),
        out_shape=(wide(BF16), wide(kv_dtype), wide(kv_dtype), wide(F32), wide(F32)),
        grid=(t // tm,),
        in_specs=in_specs,
        out_specs=(spec,) * 5,
        compiler_params=_params("parallel"),
        name="even_in",
    )(*args)


def _lambda(lam_ref, lam_init):
    lf = lam_ref[...]
    a = jnp.sum(lf[0:1] * lf[1:2], axis=-1, keepdims=True)
    b = jnp.sum(lf[2:3] * lf[3:4], axis=-1, keepdims=True)
    return jnp.exp(a) - jnp.exp(b) + lam_init


def _subln(o, sub, lam_init):
    y = o * lax.rsqrt(jnp.mean(o * o, axis=-1, keepdims=True) + EPS)
    return (y * sub) * (1.0 - lam_init)


def _softmax_pv(q, k, v1):
    s = _dot_t(q, k)
    m = jnp.max(s, axis=-1, keepdims=True)
    r = _dot(jnp.exp2(s - m).astype(BF16), v1)
    return r[:, :LANES] * (1.0 / r[:, LANES:])


def _diff_attn_heads(q_ref, head_kv, lam_ref, sub_ref, o_ref, lam_init):
    lam = _lambda(lam_ref, lam_init)
    for h in range(A_HEADS):
        sl = slice(h * LANES, (h + 1) * LANES)
        q = q_ref[:, sl]
        k, v1 = head_kv(h)
        lo = _lane_lo(q.shape)
        zero = jnp.zeros_like(q)
        o1 = _softmax_pv(jnp.where(lo, q, zero), k, v1)
        o2 = _softmax_pv(jnp.where(lo, zero, q), k, v1)
        o_ref[:, sl] = _subln(o1 - lam * o2, sub_ref[...], lam_init).astype(o_ref.dtype)


def _even_attn_p_kernel(q_ref, k_ref, v_ref, lam_ref, sub_ref, o_ref, *, lam_init):
    ones = jnp.ones((k_ref.shape[0], LANES), BF16)

    def head_kv(h):
        sl = slice(h * LANES, (h + 1) * LANES)
        return k_ref[:, sl].astype(BF16), jnp.concatenate([v_ref[:, sl].astype(BF16), ones], axis=-1)

    _diff_attn_heads(q_ref, head_kv, lam_ref, sub_ref, o_ref, lam_init)


def _even_attn_p(q, k, v, lam_vec, sub, *, seq, lam_init):
    t = q.shape[0]
    blk = pl.BlockSpec((seq, A_QK), lambda b: (b, 0))
    return pl.pallas_call(
        functools.partial(_even_attn_p_kernel, lam_init=lam_init),
        out_shape=jax.ShapeDtypeStruct((t, A_V), BF16),
        grid=(t // seq,),
        in_specs=[blk, blk, blk, _resident((4, HEAD_DIM)), _resident((1, A_VDIM))],
        out_specs=blk,
        compiler_params=_params("parallel"),
        name="even_attn_prompt",
    )(q, k, v, lam_vec, sub)


def _even_attn_s_kernel(q_ref, kl_ref, vl_ref, kc_ref, vc_ref, lam_ref, sub_ref, o_ref, k_scr, v1_scr, *, past, lam_init):
    @pl.when(pl.program_id(1) == 0)
    def _():
        k_scr[:past, :] = kc_ref[...].astype(BF16)
        k_scr[past:, :] = kl_ref[...]
        for h in range(A_HEADS):
            sl = slice(h * LANES, (h + 1) * LANES)
            v1_scr[h, :past, :LANES] = vc_ref[:, sl].astype(BF16)
            v1_scr[h, past:, :LANES] = vl_ref[:, sl]
            v1_scr[h, :, LANES:] = jnp.ones((v1_scr.shape[1], LANES), BF16)

    def head_kv(h):
        return k_scr[:, h * LANES:(h + 1) * LANES], v1_scr[h]

    _diff_attn_heads(q_ref, head_kv, lam_ref, sub_ref, o_ref, lam_init)


def _even_attn_s(q, kl, vl, kc, vc, lam_vec, sub, *, seq, past, tq, lam_init):
    t = q.shape[0]
    nq = seq // tq
    qblk = pl.BlockSpec((tq, A_QK), lambda b, i: (b * nq + i, 0))
    lat = pl.BlockSpec((seq, A_QK), lambda b, i: (b, 0))
    ctx = pl.BlockSpec((past, A_QK), lambda b, i: (b, 0))
    return pl.pallas_call(
        functools.partial(_even_attn_s_kernel, past=past, lam_init=lam_init),
        out_shape=jax.ShapeDtypeStruct((t, A_V), BF16),
        grid=(t // seq, nq),
        in_specs=[qblk, lat, lat, ctx, ctx, _resident((4, HEAD_DIM)), _resident((1, A_VDIM))],
        out_specs=qblk,
        scratch_shapes=[pltpu.VMEM((past + seq, A_QK), BF16),
                        pltpu.VMEM((A_HEADS, past + seq, 2 * LANES), BF16)],
        compiler_params=_params("parallel", "arbitrary"),
        name="even_attn_sample",
    )(q, kl, vl, kc, vc, lam_vec, sub)


def _dup_halves(x):
    lo = _lane_lo(x.shape)
    sw = pltpu.roll(x, HEAD_DIM, 1)
    return jnp.where(lo, x, sw), jnp.where(lo, sw, x)


def _ones_halves(x):
    lo = _lane_lo(x.shape)
    return jnp.where(lo, x, 1.0), jnp.where(lo, pltpu.roll(x, HEAD_DIM, 1), 1.0)


def _odd_in_kernel(x_ref, mod_ref, g_ref, w_ref, qkg_ref, *rest, row_fn, rope, keep_kv):
    rest = list(rest)
    if rope:
        cos_ref, sin_ref = rest[:2]
        rest = rest[2:]
    if keep_kv:
        q_ref, kd_ref, vd_ref, k_ref, v_ref = rest
    else:
        q_ref, kd_ref, vd_ref = rest
    r = row_fn(pl.program_id(0))
    nq = C_HEADS * HEAD_DIM
    nk = C_KV_HEADS * HEAD_DIM
    q_gain = qkg_ref[0:1, :] * Q_PRESCALE
    tm = x_ref.shape[0]
    ones2 = _half_sum_matrix()
    for rows in (slice(0, tm // 2), slice(tm // 2, tm)):
        h = _mod_norm(x_ref[rows, :], g_ref[...], _mod_row(mod_ref, 3, r), _mod_row(mod_ref, 4, r))
        y = _dot(h.astype(BF16), w_ref[...])
        if rope:
            cos, sin = cos_ref[rows, :], sin_ref[rows, :]
        for c in range(nq // LANES):
            sl = slice(c * LANES, (c + 1) * LANES)
            qs = _seg_rms_mxu(y[:, sl], q_gain, ones2)
            if rope:
                qs = _rope(qs, cos, sin)
            q_ref[rows, sl] = qs.astype(q_ref.dtype)
        for c in range(nk // LANES):
            sl = slice(c * LANES, (c + 1) * LANES)
            ks = _seg_rms_mxu(y[:, nq + c * LANES:nq + (c + 1) * LANES], qkg_ref[1:2, :], ones2)
            vs = y[:, nq + nk + c * LANES:nq + nk + (c + 1) * LANES]
            if keep_kv:
                k_ref[rows, sl] = ks
                v_ref[rows, sl] = vs
            if rope:
                ks = _rope(ks, cos, sin)
            for (d0, d1), ref in ((_dup_halves(ks), kd_ref), (_ones_halves(vs), vd_ref)):
                ref[rows, 2 * c * LANES:(2 * c + 1) * LANES] = d0.astype(ref.dtype)
                ref[rows, (2 * c + 1) * LANES:(2 * c + 2) * LANES] = d1.astype(ref.dtype)


def _odd_in(x, mod, g, w_in, qkg, rope_tabs, *, row_fn, tm, keep_kv):
    t = x.shape[0]
    rope = rope_tabs is not None
    in_specs = [
        pl.BlockSpec((tm, D_MODEL), lambda i: (i, 0)),
        pl.BlockSpec((N_MOD, MOD_ROWS, D_MODEL), lambda i: (0, 0, 0)),
        pl.BlockSpec((1, D_MODEL), lambda i: (0, 0)),
        pl.BlockSpec((D_MODEL, ODD_IN), lambda i: (0, 0)),
        pl.BlockSpec((2, LANES), lambda i: (0, 0)),
    ]
    args = [x, mod, g, w_in, qkg]
    if rope:
        nt = rope_tabs[0].shape[0] // tm
        in_specs += [pl.BlockSpec((tm, LANES), lambda i: (i % nt, 0))] * 2
        args += list(rope_tabs)
    nq = C_HEADS * HEAD_DIM
    nd = C_KV_HEADS * LANES
    nk = C_KV_HEADS * HEAD_DIM
    out_shape = [jax.ShapeDtypeStruct((t, nq), BF16), jax.ShapeDtypeStruct((t, nd), BF16),
                 jax.ShapeDtypeStruct((t, nd), BF16)]
    out_specs = [pl.BlockSpec((tm, nq), lambda i: (i, 0)), pl.BlockSpec((tm, nd), lambda i: (i, 0)),
                 pl.BlockSpec((tm, nd), lambda i: (i, 0))]
    if keep_kv:
        out_shape += [jax.ShapeDtypeStruct((t, nk), F32)] * 2
        out_specs += [pl.BlockSpec((tm, nk), lambda i: (i, 0))] * 2
    return pl.pallas_call(
        functools.partial(_odd_in_kernel, row_fn=row_fn, rope=rope, keep_kv=keep_kv),
        out_shape=tuple(out_shape),
        grid=(t // tm,),
        in_specs=in_specs,
        out_specs=tuple(out_specs),
        compiler_params=_params("parallel"),
        name="odd_in",
    )(*args)


def _stack_heads(q_ref, g, rows):
    parts = []
    for rr in range(GROUP):
        c = g * (GROUP // 2) + rr // 2
        qc = q_ref[:, c * LANES:(c + 1) * LANES]
        lo = _lane_lo(qc.shape)
        keep = lo if rr % 2 == 0 else jnp.logical_not(lo)
        parts.append(jnp.where(keep, qc, jnp.zeros_like(qc)))
    return jnp.concatenate(parts, axis=0)


def _gqa_group(q4, scores, vals, sinks, rows):
    es = [[] for _ in scores]
    sink_terms = []
    for rr in range(GROUP):
        rs = slice(rr * rows, (rr + 1) * rows)
        m = functools.reduce(jnp.maximum, [jnp.max(s[rs], axis=-1, keepdims=True) for s in scores])
        m = jnp.maximum(m, sinks[rr])
        for j, s in enumerate(scores):
            es[j].append(jnp.exp2(s[rs] - m).astype(BF16))
        sink_terms.append(jnp.exp2(sinks[rr] - m))
    r = functools.reduce(jnp.add, [_dot(jnp.concatenate(e, axis=0), v) for e, v in zip(es, vals)])
    heads = []
    for rr in range(GROUP):
        rh = r[rr * rows:(rr + 1) * rows]
        heads.append(rh * (1.0 / (pltpu.roll(rh, HEAD_DIM, 1) + sink_terms[rr])))
    lo = _lane_lo((rows, LANES))
    slabs = [jnp.where(lo, heads[2 * c], pltpu.roll(heads[2 * c + 1], HEAD_DIM, 1)) for c in range(GROUP // 2)]
    return jnp.concatenate(slabs, axis=-1)


def _odd_attn_p_kernel(sink_ref, q_ref, kd_ref, vd_ref, o_ref, *, seq):
    g = pl.program_id(1)
    q4 = _stack_heads(q_ref, 0, seq)
    sinks = [sink_ref[g * GROUP + rr] * LOG2E for rr in range(GROUP)]
    o_ref[...] = _gqa_group(q4, [_dot_t(q4, kd_ref[...])], [vd_ref[...]], sinks, seq).astype(o_ref.dtype)


def _odd_attn_p(q, kd, vd, sink, *, seq):
    t = q.shape[0]
    gw = GROUP * HEAD_DIM
    return pl.pallas_call(
        functools.partial(_odd_attn_p_kernel, seq=seq),
        out_shape=jax.ShapeDtypeStruct((t, C_HEADS * HEAD_DIM), BF16),
        grid=(t // seq, C_KV_HEADS),
        in_specs=[
            pl.BlockSpec(memory_space=pltpu.SMEM),
            pl.BlockSpec((seq, gw), lambda b, g: (b, g)),
            pl.BlockSpec((seq, LANES), lambda b, g: (b, g)),
            pl.BlockSpec((seq, LANES), lambda b, g: (b, g)),
        ],
        out_specs=pl.BlockSpec((seq, gw), lambda b, g: (b, g)),
        compiler_params=_params("parallel", "parallel"),
        name="odd_attn_prompt",
    )(sink, q, kd, vd)


def _odd_attn_s_kernel(sink_ref, q_ref, kd_ref, vd_ref, kc_ref, vc_ref, o_ref, kcd_scr, vcd_scr, *, seq, tq):
    i = pl.program_id(1)
    span = tq + 2 * WINDOW

    @pl.when(i == 0)
    def _():
        for c in range(C_KV_HEADS // 2):
            sl = slice(c * LANES, (c + 1) * LANES)
            for (d0, d1), dst in ((_dup_halves(kc_ref[:, sl]), kcd_scr), (_ones_halves(vc_ref[:, sl]), vcd_scr)):
                dst[:, 2 * c * LANES:(2 * c + 1) * LANES] = d0.astype(BF16)
                dst[:, (2 * c + 1) * LANES:(2 * c + 2) * LANES] = d1.astype(BF16)

    start = pl.multiple_of(jnp.clip(i * tq - WINDOW, 0, seq - span), WINDOW)
    rows = GROUP * tq
    qpos = i * tq + lax.broadcasted_iota(jnp.int32, (rows, span), 0) % tq
    kpos = start + lax.broadcasted_iota(jnp.int32, (rows, span), 1)
    valid = jnp.abs(qpos - kpos) <= WINDOW
    for g in range(C_KV_HEADS):
        gl = slice(g * LANES, (g + 1) * LANES)
        q4 = _stack_heads(q_ref, g, tq)
        s_c = _dot_t(q4, kcd_scr[:, gl])
        s_l = jnp.where(valid, _dot_t(q4, kd_ref[pl.ds(start, span), gl]), -jnp.inf)
        sinks = [sink_ref[g * GROUP + rr] * LOG2E for rr in range(GROUP)]
        og = _gqa_group(q4, [s_c, s_l], [vcd_scr[:, gl], vd_ref[pl.ds(start, span), gl]], sinks, tq)
        o_ref[:, g * GROUP * HEAD_DIM:(g + 1) * GROUP * HEAD_DIM] = og.astype(o_ref.dtype)


def _odd_attn_s(q, kd, vd, kc, vc, sink, *, seq, past, tq):
    t = q.shape[0]
    nq = seq // tq
    nqw = C_HEADS * HEAD_DIM
    nd = C_KV_HEADS * LANES
    nk = C_KV_HEADS * HEAD_DIM
    return pl.pallas_call(
        functools.partial(_odd_attn_s_kernel, seq=seq, tq=tq),
        out_shape=jax.ShapeDtypeStruct((t, nqw), BF16),
        grid=(t // seq, nq),
        in_specs=[
            pl.BlockSpec(memory_space=pltpu.SMEM),
            pl.BlockSpec((tq, nqw), lambda b, i: (b * nq + i, 0)),
            pl.BlockSpec((seq, nd), lambda b, i: (b, 0)),
            pl.BlockSpec((seq, nd), lambda b, i: (b, 0)),
            pl.BlockSpec((past, nk), lambda b, i: (b, 0)),
            pl.BlockSpec((past, nk), lambda b, i: (b, 0)),
        ],
        out_specs=pl.BlockSpec((tq, nqw), lambda b, i: (b * nq + i, 0)),
        scratch_shapes=[pltpu.VMEM((past, nd), BF16), pltpu.VMEM((past, nd), BF16)],
        compiler_params=_params("parallel", "arbitrary"),
        name="odd_attn_sample",
    )(sink, q, kd, vd, kc, vc)


def _rope_tables(n):
    rows = n // GRID_W
    row = jnp.repeat(jnp.arange(rows, dtype=F32), GRID_W)
    col = jnp.tile(jnp.arange(GRID_W, dtype=F32), rows)
    inv = ROPE_BASE ** (-jnp.arange(0, AXIS_DIM, 2, dtype=F32) / AXIS_DIM)
    ang_r = row[:, None] * inv[None, :]
    ang_c = col[:, None] * inv[None, :]
    ang = jnp.concatenate([ang_r, ang_r, ang_c, ang_c], axis=-1)
    sign = jnp.where((jnp.arange(HEAD_DIM) & (AXIS_DIM // 2)) == 0, -1.0, 1.0).astype(F32)
    cos = jnp.cos(ang)
    sin = jnp.sin(ang) * sign[None, :]
    return jnp.tile(cos, (1, 2)), jnp.tile(sin, (1, 2))


def kernel(x_prompt, x_sample, cache_even_k, cache_even_v, cache_odd_k, cache_odd_v, c, c_ctx, w_mod, b_mod, norm_g, ffn_w13, ffn_w2, even_w_in, even_w_out, even_qk_norm, even_lambda, even_subln, even_conv_w, odd_w_in, odd_w_out, odd_qk_norm, odd_sink):
    batch, seq, _ = x_prompt.shape
    dec_batch, dec_seq, _ = x_sample.shape
    past = cache_even_k.shape[2]
    depth = w_mod.shape[0]
    assert 1 + dec_batch <= MOD_ROWS and seq % SUBLANES == 0 and dec_seq % TOKEN_TILE == 0

    xp = x_prompt.reshape(batch * seq, D_MODEL)
    xs = x_sample.reshape(dec_batch * dec_seq, D_MODEL)
    tm_p = min(TOKEN_TILE, batch * seq)
    row_p = _row_fn(0, batch * seq, tm_p)
    row_s = _row_fn(1, dec_seq, TOKEN_TILE)

    c_rows = jnp.zeros((MOD_ROWS, D_MODEL), F32).at[0].set(c_ctx).at[1:1 + dec_batch].set(c)
    mods = _adaln(c_rows, w_mod, b_mod)
    rope_tabs = _rope_tables(dec_seq)
    w13 = ffn_w13.astype(BF16)
    w2 = ffn_w2.astype(BF16)

    even_k, even_v, odd_k, odd_v = [], [], [], []
    for l in range(depth):
        mod = mods[l]
        g = [norm_g[l, s].reshape(1, D_MODEL) for s in range(3)]

        ffn_p = functools.partial(_ffn, mod=mod, w13=w13, w2=w2, layer=l, row_fn=row_p, tm=tm_p)
        ffn_s = functools.partial(_ffn, mod=mod, w13=w13, w2=w2, layer=l, row_fn=row_s, tm=TOKEN_TILE)
        xp = ffn_p(xp, g=g[0], slot=0)
        xs = ffn_s(xs, g=g[0], slot=0)

        if l % 2 == 0:
            e = l // 2
            lam_init = 0.8 - 0.6 * math.exp(-0.3 * l)
            w_in = even_w_in[e].astype(BF16)
            w_out = even_w_out[e].astype(BF16)
            qkg = jnp.tile(even_qk_norm[e], (1, 2))
            sub = even_subln[e].reshape(1, A_VDIM)

            q, k, v, bg, u = _even_in(xp, mod, g[1], w_in, qkg, None, row_fn=row_p, tm=tm_p, kv_dtype=F32)
            o = _even_attn_p(q, k, v, even_lambda[e], sub, seq=seq, lam_init=lam_init)
            xp = ffn_p(xp, g=g[2], slot=1, mixer=(w_out, o, bg, u, even_conv_w[e], seq))
            even_k.append(k.reshape(batch, seq, A_HEADS, 2 * HEAD_DIM))
            even_v.append(v.reshape(batch, seq, A_HEADS, A_VDIM))

            q, k, v, bg, u = _even_in(xs, mod, g[1], w_in, qkg, rope_tabs, row_fn=row_s, tm=TOKEN_TILE, kv_dtype=BF16)
            kc = cache_even_k[:, e].reshape(dec_batch * past, A_QK)
            vc = cache_even_v[:, e].reshape(dec_batch * past, A_V)
            o = _even_attn_s(q, k, v, kc, vc, even_lambda[e], sub, seq=dec_seq, past=past, tq=EVEN_Q_TILE,
                             lam_init=lam_init)
            xs = ffn_s(xs, g=g[2], slot=1, mixer=(w_out, o, bg, u, even_conv_w[e], dec_seq))
        else:
            e = l // 2
            w_in = odd_w_in[e].astype(BF16)
            w_out = odd_w_out[e].astype(BF16)
            qkg = jnp.tile(odd_qk_norm[e], (1, 2))
            sink = odd_sink[e]

            q, kd, vd, k, v = _odd_in(xp, mod, g[1], w_in, qkg, None, row_fn=row_p, tm=tm_p, keep_kv=True)
            o = _odd_attn_p(q, kd, vd, sink, seq=seq)
            xp = ffn_p(xp, g=g[2], slot=1, mixer=(w_out, o))
            odd_k.append(k.reshape(batch, seq, C_KV_HEADS, HEAD_DIM))
            odd_v.append(v.reshape(batch, seq, C_KV_HEADS, HEAD_DIM))

            q, kd, vd = _odd_in(xs, mod, g[1], w_in, qkg, rope_tabs, row_fn=row_s, tm=TOKEN_TILE, keep_kv=False)
            kc = cache_odd_k[:, e].reshape(dec_batch * past, C_KV_HEADS * HEAD_DIM)
            vc = cache_odd_v[:, e].reshape(dec_batch * past, C_KV_HEADS * HEAD_DIM)
            o = _odd_attn_s(q, kd, vd, kc, vc, sink, seq=dec_seq, past=past, tq=ODD_Q_TILE)
            xs = ffn_s(xs, g=g[2], slot=1, mixer=(w_out, o))

    return (xp.reshape(batch, seq, D_MODEL), xs.reshape(dec_batch, dec_seq, D_MODEL),
            jnp.stack(even_k, axis=1), jnp.stack(even_v, axis=1),
            jnp.stack(odd_k, axis=1), jnp.stack(odd_v, axis=1))
```

```python
import functools
import math

import jax
import jax.numpy as jnp
from jax import lax
from jax.experimental import pallas as pl
from jax.experimental.pallas import tpu as pltpu

F32 = jnp.float32
BF16 = jnp.bfloat16

D_MODEL = 1024
GRID_W = 64
HEAD_DIM = 64
AXIS_DIM = HEAD_DIM // 2
A_HEADS = 4
A_VDIM = 2 * HEAD_DIM
CONV_DIM = 512
C_HEADS = 16
C_KV_HEADS = 4
GROUP = C_HEADS // C_KV_HEADS
WINDOW = 128
D_FF = 2816
ROPE_BASE = 10000.0
N_MOD = 9
EPS = 1e-6
A_QK = A_HEADS * 2 * HEAD_DIM
A_V = A_HEADS * A_VDIM
EVEN_IN = 2 * A_QK + A_V + 3 * CONV_DIM
ODD_IN = (C_HEADS + 2 * C_KV_HEADS) * HEAD_DIM
LOG2E = math.log2(math.e)
Q_PRESCALE = HEAD_DIM ** -0.5 * LOG2E

LANES = 128
SUBLANES = 8
MOD_ROWS = 16
VMEM_LIMIT_BYTES = 56 * 2 ** 20

TOKEN_TILE = 512
EVEN_Q_TILE = 512
EVEN_Q_SUB = 256
ODD_Q_TILE = 256
ODD_Q_SUB = WINDOW


def _params(*semantics):
    return pltpu.CompilerParams(dimension_semantics=semantics, vmem_limit_bytes=VMEM_LIMIT_BYTES)


def _sigmoid(x):
    return 1.0 / (1.0 + jnp.exp(-x))


def _mod_norm(x, g, shift, scale):
    y = x * lax.rsqrt(jnp.mean(x * x, axis=-1, keepdims=True) + EPS)
    return (y * g) * (1.0 + scale) + shift


def _mod_row(mod_ref, k, r):
    return mod_ref[k, pl.ds(r, 1), :]


def _row_fn(row0, tokens_per_row, tile):
    return lambda i: row0 + (i * tile) // tokens_per_row


def _lane_lo(shape):
    return lax.broadcasted_iota(jnp.int32, shape, len(shape) - 1) < HEAD_DIM


def _seg_rms(xs, gain):
    lo = _lane_lo(xs.shape)
    sq = xs * xs
    s_lo = jnp.sum(jnp.where(lo, sq, 0.0), axis=-1, keepdims=True)
    s_hi = jnp.sum(jnp.where(lo, 0.0, sq), axis=-1, keepdims=True)
    inv = lax.rsqrt(jnp.where(lo, s_lo, s_hi) * (1.0 / HEAD_DIM) + EPS)
    return (xs * inv) * gain


def _half_sum_matrix():
    r = lax.broadcasted_iota(jnp.int32, (2 * LANES, LANES), 0)
    c = lax.broadcasted_iota(jnp.int32, (2 * LANES, LANES), 1)
    return jnp.where((r & HEAD_DIM) == (c & HEAD_DIM), 1.0, 0.0).astype(BF16)


def _seg_rms_mxu(xs, gain, ones2):
    sq = xs * xs
    head = sq.astype(BF16)
    rest = (sq - head.astype(F32)).astype(BF16)
    ssum = _dot(jnp.concatenate([head, rest], axis=-1), ones2)
    return (xs * lax.rsqrt(ssum * (1.0 / HEAD_DIM) + EPS)) * gain


def _rope(xs, cos, sin_signed):
    lane = lax.broadcasted_iota(jnp.int32, xs.shape, 1)
    first = (lane & (AXIS_DIM // 2)) == 0
    partner = jnp.where(first, pltpu.roll(xs, LANES - AXIS_DIM // 2, 1), pltpu.roll(xs, AXIS_DIM // 2, 1))
    return xs * cos + partner * sin_signed


def _dot_t(a, b):
    return lax.dot_general(a, b, (((1,), (1,)), ((), ())), preferred_element_type=F32)


def _dot(a, b):
    return jnp.dot(a, b, preferred_element_type=F32)


def _adaln_kernel(c_ref, w_ref, b_ref, o_ref):
    c = c_ref[...]
    s = c * _sigmoid(c)
    o_ref[0, 0] = _dot(s.astype(BF16), w_ref[0].astype(BF16)) + b_ref[0]


def _adaln(c_rows, w_mod, b_mod):
    depth = w_mod.shape[0]
    b3 = b_mod.reshape(depth * N_MOD, 1, D_MODEL)
    return pl.pallas_call(
        _adaln_kernel,
        out_shape=jax.ShapeDtypeStruct((depth, N_MOD, MOD_ROWS, D_MODEL), F32),
        grid=(depth, N_MOD),
        in_specs=[
            pl.BlockSpec((MOD_ROWS, D_MODEL), lambda l, j: (0, 0)),
            pl.BlockSpec((1, D_MODEL, D_MODEL), lambda l, j: (l, 0, j)),
            pl.BlockSpec((1, 1, D_MODEL), lambda l, j: (l * N_MOD + j, 0, 0)),
        ],
        out_specs=pl.BlockSpec((1, 1, MOD_ROWS, D_MODEL), lambda l, j: (l, j, 0, 0)),
        compiler_params=_params("parallel", "parallel"),
        name="adaln",
    )(c_rows, w_mod, b3)


def _ffn_tail(x, r, mod_ref, g_ref, w13_ref, w2_ref, o_ref, k0):
    h = _mod_norm(x, g_ref[...], _mod_row(mod_ref, k0, r), _mod_row(mod_ref, k0 + 1, r))
    a = _dot(h.astype(BF16), w13_ref[0, 0])
    gate = a[:, :D_FF]
    up = a[:, D_FF:]
    act = (gate * _sigmoid(gate)) * up
    y = _dot(act.astype(BF16), w2_ref[0, 0])
    o_ref[...] = x + (0.5 * _mod_row(mod_ref, k0 + 2, r)) * y


def _ffn_kernel(x_ref, mod_ref, g_ref, w13_ref, w2_ref, o_ref, *, row_fn):
    _ffn_tail(x_ref[...], row_fn(pl.program_id(0)), mod_ref, g_ref, w13_ref, w2_ref, o_ref, 0)


def _conv_gate(bg_ref, u_ref, up_ref, un_ref, cw_ref, i, seq, tm):
    u = u_ref[...]
    row = lax.broadcasted_iota(jnp.int32, (tm, 1), 0)
    pos = (row + i * tm) % seq
    u_dn = jnp.where(row == 0, up_ref[SUBLANES - 1:SUBLANES, :], pltpu.roll(u, 1, 0))
    u_dn = jnp.where(pos == 0, 0.0, u_dn)
    u_up = jnp.where(row == tm - 1, un_ref[0:1, :], pltpu.roll(u, tm - 1, 0))
    u_up = jnp.where(pos == seq - 1, 0.0, u_up)
    return bg_ref[...] * (cw_ref[0:1, :] * u_dn + cw_ref[1:2, :] * u + cw_ref[2:3, :] * u_up)


def _even_out_ffn_kernel(x_ref, a_ref, bg_ref, u_ref, up_ref, un_ref, cw_ref, wo_ref, mod_ref, g_ref, w13_ref, w2_ref,
                         o_ref, *, row_fn, seq, tm):
    i = pl.program_id(0)
    r = row_fn(i)
    y = _conv_gate(bg_ref, u_ref, up_ref, un_ref, cw_ref, i, seq, tm)
    mix = _dot(jnp.concatenate([a_ref[...], y.astype(BF16)], axis=-1), wo_ref[...])
    x = x_ref[...] + _mod_row(mod_ref, 5, r) * mix
    _ffn_tail(x, r, mod_ref, g_ref, w13_ref, w2_ref, o_ref, 6)


def _odd_out_ffn_kernel(x_ref, a_ref, wo_ref, mod_ref, g_ref, w13_ref, w2_ref, o_ref, *, row_fn):
    r = row_fn(pl.program_id(0))
    x = x_ref[...] + _mod_row(mod_ref, 5, r) * _dot(a_ref[...], wo_ref[...])
    _ffn_tail(x, r, mod_ref, g_ref, w13_ref, w2_ref, o_ref, 6)


def _resident(shape, index=None):
    index = (0,) * len(shape) if index is None else index
    return pl.BlockSpec(shape, lambda *_: index, pipeline_mode=pl.Buffered(1))


def _ffn(x, mod, g, w13, w2, *, layer, slot, row_fn, tm, mixer=None):
    t = x.shape[0]
    tile = lambda n: pl.BlockSpec((tm, n), lambda i: (i, 0))
    ffn_specs = [
        _resident((N_MOD, MOD_ROWS, D_MODEL)),
        _resident((1, D_MODEL)),
        _resident((1, 1, D_MODEL, 2 * D_FF), (layer, slot, 0, 0)),
        _resident((1, 1, D_FF, D_MODEL), (layer, slot, 0, 0)),
    ]
    ffn_args = [mod, g, w13, w2]
    if mixer is None:
        body = functools.partial(_ffn_kernel, row_fn=row_fn)
        in_specs, args, name = [tile(D_MODEL)], [x], "ffn"
    elif len(mixer) == 2:
        w_out, attn = mixer
        body = functools.partial(_odd_out_ffn_kernel, row_fn=row_fn)
        in_specs = [tile(D_MODEL), tile(attn.shape[1]), _resident(w_out.shape)]
        args, name = [x, attn, w_out], "odd_out_ffn"
    else:
        w_out, attn, bg, u, conv_w, seq = mixer
        nb = tm // SUBLANES
        last = t // SUBLANES - 1
        body = functools.partial(_even_out_ffn_kernel, row_fn=row_fn, seq=seq, tm=tm)
        in_specs = [
            tile(D_MODEL), tile(A_V), tile(CONV_DIM), tile(CONV_DIM),
            pl.BlockSpec((SUBLANES, CONV_DIM), lambda i: (jnp.maximum(i * nb - 1, 0), 0)),
            pl.BlockSpec((SUBLANES, CONV_DIM), lambda i: (jnp.minimum((i + 1) * nb, last), 0)),
            _resident(conv_w.shape), _resident(w_out.shape),
        ]
        args, name = [x, attn, bg, u, u, u, conv_w, w_out], "even_out_ffn"
    return pl.pallas_call(
        body,
        out_shape=jax.ShapeDtypeStruct((t, D_MODEL), F32),
        grid=(t // tm,),
        in_specs=in_specs + ffn_specs,
        out_specs=tile(D_MODEL),
        compiler_params=_params("parallel"),
        name=name,
    )(*args, *ffn_args)


def _even_in_kernel(x_ref, mod_ref, g_ref, w_ref, qkg_ref, *rest, row_fn, rope):
    if rope:
        cos_ref, sin_ref, q_ref, k_ref, v_ref, bg_ref, u_ref = rest
    else:
        q_ref, k_ref, v_ref, bg_ref, u_ref = rest
    r = row_fn(pl.program_id(0))
    h = _mod_norm(x_ref[...], g_ref[...], _mod_row(mod_ref, 3, r), _mod_row(mod_ref, 4, r))
    y = _dot(h.astype(BF16), w_ref[...])
    q_gain = qkg_ref[0:1, :] * Q_PRESCALE
    for hh in range(A_HEADS):
        sl = slice(hh * LANES, (hh + 1) * LANES)
        qs = _seg_rms(y[:, sl], q_gain)
        ks = _seg_rms(y[:, A_QK + hh * LANES:A_QK + (hh + 1) * LANES], qkg_ref[1:2, :])
        if rope:
            qs = _rope(qs, cos_ref[...], sin_ref[...])
            ks = _rope(ks, cos_ref[...], sin_ref[...])
        q_ref[:, sl] = qs.astype(q_ref.dtype)
        k_ref[:, sl] = ks.astype(k_ref.dtype)
    v_ref[...] = y[:, 2 * A_QK:2 * A_QK + A_V].astype(v_ref.dtype)
    o = 2 * A_QK + A_V
    bg_ref[...] = y[:, o:o + CONV_DIM]
    u_ref[...] = y[:, o + CONV_DIM:o + 2 * CONV_DIM] * y[:, o + 2 * CONV_DIM:o + 3 * CONV_DIM]


def _even_in(x, mod, g, w_in, qkg, rope_tabs, *, row_fn, tm, kv_dtype):
    t = x.shape[0]
    rope = rope_tabs is not None
    in_specs = [
        pl.BlockSpec((tm, D_MODEL), lambda i: (i, 0)),
        pl.BlockSpec((N_MOD, MOD_ROWS, D_MODEL), lambda i: (0, 0, 0)),
        pl.BlockSpec((1, D_MODEL), lambda i: (0, 0)),
        pl.BlockSpec((D_MODEL, EVEN_IN), lambda i: (0, 0)),
        pl.BlockSpec((2, LANES), lambda i: (0, 0)),
    ]
    args = [x, mod, g, w_in, qkg]
    if rope:
        nt = rope_tabs[0].shape[0] // tm
        in_specs += [pl.BlockSpec((tm, LANES), lambda i: (i % nt, 0))] * 2
        args += list(rope_tabs)
    wide = lambda dt: jax.ShapeDtypeStruct((t, A_QK), dt)
    spec = pl.BlockSpec((tm, A_QK), lambda i: (i, 0))
    return pl.pallas_call(
        functools.partial(_even_in_kernel, row_fn=row_fn, rope=rope),
        out_shape=(wide(BF16), wide(kv_dtype), wide(kv_dtype), wide(F32), wide(F32)),
        grid=(t // tm,),
        in_specs=in_specs,
        out_specs=(spec,) * 5,
        compiler_params=_params("parallel"),
        name="even_in",
    )(*args)


def _lambda(lam_ref, lam_init):
    lf = lam_ref[...]
    a = jnp.sum(lf[0:1] * lf[1:2], axis=-1, keepdims=True)
    b = jnp.sum(lf[2:3] * lf[3:4], axis=-1, keepdims=True)
    return jnp.exp(a) - jnp.exp(b) + lam_init


def _subln(o, sub, lam_init):
    y = o * lax.rsqrt(jnp.mean(o * o, axis=-1, keepdims=True) + EPS)
    return (y * sub) * (1.0 - lam_init)


def _softmax_pv(q, k, v1):
    s = _dot_t(q, k)
    m = jnp.max(s, axis=-1, keepdims=True)
    r = _dot(jnp.exp2(s - m).astype(BF16), v1)
    return r[:, :LANES] * (1.0 / r[:, LANES:])


def _diff_attn_heads(q_ref, head_kv, lam_ref, sub_ref, o_ref, lam_init):
    lam = _lambda(lam_ref, lam_init)
    tq = q_ref.shape[0]
    sub_rows = min(tq, EVEN_Q_SUB)
    for rb in range(tq // sub_rows):
        rows = slice(rb * sub_rows, (rb + 1) * sub_rows)
        for h in range(A_HEADS):
            sl = slice(h * LANES, (h + 1) * LANES)
            q = q_ref[rows, sl]
            k, v1 = head_kv(h)
            lo = _lane_lo(q.shape)
            zero = jnp.zeros_like(q)
            o1 = _softmax_pv(jnp.where(lo, q, zero), k, v1)
            o2 = _softmax_pv(jnp.where(lo, zero, q), k, v1)
            o_ref[rows, sl] = _subln(o1 - lam * o2, sub_ref[...], lam_init).astype(o_ref.dtype)


def _even_attn_p_kernel(q_ref, k_ref, v_ref, lam_ref, sub_ref, o_ref, *, lam_init):
    ones = jnp.ones((k_ref.shape[0], LANES), BF16)

    def head_kv(h):
        sl = slice(h * LANES, (h + 1) * LANES)
        return k_ref[:, sl].astype(BF16), jnp.concatenate([v_ref[:, sl].astype(BF16), ones], axis=-1)

    _diff_attn_heads(q_ref, head_kv, lam_ref, sub_ref, o_ref, lam_init)


def _even_attn_p(q, k, v, lam_vec, sub, *, seq, lam_init):
    t = q.shape[0]
    blk = pl.BlockSpec((seq, A_QK), lambda b: (b, 0))
    return pl.pallas_call(
        functools.partial(_even_attn_p_kernel, lam_init=lam_init),
        out_shape=jax.ShapeDtypeStruct((t, A_V), BF16),
        grid=(t // seq,),
        in_specs=[blk, blk, blk, _resident((4, HEAD_DIM)), _resident((1, A_VDIM))],
        out_specs=blk,
        compiler_params=_params("parallel"),
        name="even_attn_prompt",
    )(q, k, v, lam_vec, sub)


def _even_attn_s_kernel(q_ref, kl_ref, vl_ref, kc_ref, vc_ref, lam_ref, sub_ref, o_ref, k_scr, v1_scr, *, past, lam_init):
    @pl.when(pl.program_id(1) == 0)
    def _():
        k_scr[:past, :] = kc_ref[...].astype(BF16)
        k_scr[past:, :] = kl_ref[...]
        for h in range(A_HEADS):
            sl = slice(h * LANES, (h + 1) * LANES)
            v1_scr[h, :past, :LANES] = vc_ref[:, sl].astype(BF16)
            v1_scr[h, past:, :LANES] = vl_ref[:, sl]
            v1_scr[h, :, LANES:] = jnp.ones((v1_scr.shape[1], LANES), BF16)

    def head_kv(h):
        return k_scr[:, h * LANES:(h + 1) * LANES], v1_scr[h]

    _diff_attn_heads(q_ref, head_kv, lam_ref, sub_ref, o_ref, lam_init)


def _even_attn_s(q, kl, vl, kc, vc, lam_vec, sub, *, seq, past, tq, lam_init):
    t = q.shape[0]
    nq = seq // tq
    qblk = pl.BlockSpec((tq, A_QK), lambda b, i: (b * nq + i, 0))
    lat = pl.BlockSpec((seq, A_QK), lambda b, i: (b, 0))
    ctx = pl.BlockSpec((past, A_QK), lambda b, i: (b, 0))
    return pl.pallas_call(
        functools.partial(_even_attn_s_kernel, past=past, lam_init=lam_init),
        out_shape=jax.ShapeDtypeStruct((t, A_V), BF16),
        grid=(t // seq, nq),
        in_specs=[qblk, lat, lat, ctx, ctx, _resident((4, HEAD_DIM)), _resident((1, A_VDIM))],
        out_specs=qblk,
        scratch_shapes=[pltpu.VMEM((past + seq, A_QK), BF16),
                        pltpu.VMEM((A_HEADS, past + seq, 2 * LANES), BF16)],
        compiler_params=_params("parallel", "arbitrary"),
        name="even_attn_sample",
    )(q, kl, vl, kc, vc, lam_vec, sub)


def _dup_halves(x):
    lo = _lane_lo(x.shape)
    sw = pltpu.roll(x, HEAD_DIM, 1)
    return jnp.where(lo, x, sw), jnp.where(lo, sw, x)


def _ones_halves(x):
    lo = _lane_lo(x.shape)
    return jnp.where(lo, x, 1.0), jnp.where(lo, pltpu.roll(x, HEAD_DIM, 1), 1.0)


def _odd_in_kernel(x_ref, mod_ref, g_ref, w_ref, qkg_ref, *rest, row_fn, rope, keep_kv):
    rest = list(rest)
    if rope:
        cos_ref, sin_ref = rest[:2]
        rest = rest[2:]
    if keep_kv:
        q_ref, kd_ref, vd_ref, k_ref, v_ref = rest
    else:
        q_ref, kd_ref, vd_ref = rest
    r = row_fn(pl.program_id(0))
    nq = C_HEADS * HEAD_DIM
    nk = C_KV_HEADS * HEAD_DIM
    q_gain = qkg_ref[0:1, :] * Q_PRESCALE
    tm = x_ref.shape[0]
    ones2 = _half_sum_matrix()
    for rows in (slice(0, tm // 2), slice(tm // 2, tm)):
        h = _mod_norm(x_ref[rows, :], g_ref[...], _mod_row(mod_ref, 3, r), _mod_row(mod_ref, 4, r))
        y = _dot(h.astype(BF16), w_ref[...])
        if rope:
            cos, sin = cos_ref[rows, :], sin_ref[rows, :]
        for c in range(nq // LANES):
            sl = slice(c * LANES, (c + 1) * LANES)
            qs = _seg_rms_mxu(y[:, sl], q_gain, ones2)
            if rope:
                qs = _rope(qs, cos, sin)
            q_ref[rows, sl] = qs.astype(q_ref.dtype)
        for c in range(nk // LANES):
            sl = slice(c * LANES, (c + 1) * LANES)
            ks = _seg_rms_mxu(y[:, nq + c * LANES:nq + (c + 1) * LANES], qkg_ref[1:2, :], ones2)
            vs = y[:, nq + nk + c * LANES:nq + nk + (c + 1) * LANES]
            if keep_kv:
                k_ref[rows, sl] = ks
                v_ref[rows, sl] = vs
            if rope:
                ks = _rope(ks, cos, sin)
            for (d0, d1), ref in ((_dup_halves(ks), kd_ref), (_ones_halves(vs), vd_ref)):
                ref[rows, 2 * c * LANES:(2 * c + 1) * LANES] = d0.astype(ref.dtype)
                ref[rows, (2 * c + 1) * LANES:(2 * c + 2) * LANES] = d1.astype(ref.dtype)


def _odd_in(x, mod, g, w_in, qkg, rope_tabs, *, row_fn, tm, keep_kv):
    t = x.shape[0]
    rope = rope_tabs is not None
    in_specs = [
        pl.BlockSpec((tm, D_MODEL), lambda i: (i, 0)),
        pl.BlockSpec((N_MOD, MOD_ROWS, D_MODEL), lambda i: (0, 0, 0)),
        pl.BlockSpec((1, D_MODEL), lambda i: (0, 0)),
        pl.BlockSpec((D_MODEL, ODD_IN), lambda i: (0, 0)),
        pl.BlockSpec((2, LANES), lambda i: (0, 0)),
    ]
    args = [x, mod, g, w_in, qkg]
    if rope:
        nt = rope_tabs[0].shape[0] // tm
        in_specs += [pl.BlockSpec((tm, LANES), lambda i: (i % nt, 0))] * 2
        args += list(rope_tabs)
    nq = C_HEADS * HEAD_DIM
    nd = C_KV_HEADS * LANES
    nk = C_KV_HEADS * HEAD_DIM
    out_shape = [jax.ShapeDtypeStruct((t, nq), BF16), jax.ShapeDtypeStruct((t, nd), BF16),
                 jax.ShapeDtypeStruct((t, nd), BF16)]
    out_specs = [pl.BlockSpec((tm, nq), lambda i: (i, 0)), pl.BlockSpec((tm, nd), lambda i: (i, 0)),
                 pl.BlockSpec((tm, nd), lambda i: (i, 0))]
    if keep_kv:
        out_shape += [jax.ShapeDtypeStruct((t, nk), F32)] * 2
        out_specs += [pl.BlockSpec((tm, nk), lambda i: (i, 0))] * 2
    return pl.pallas_call(
        functools.partial(_odd_in_kernel, row_fn=row_fn, rope=rope, keep_kv=keep_kv),
        out_shape=tuple(out_shape),
        grid=(t // tm,),
        in_specs=in_specs,
        out_specs=tuple(out_specs),
        compiler_params=_params("parallel"),
        name="odd_in",
    )(*args)


def _stack_heads(q_ref, g, rows):
    parts = []
    for rr in range(GROUP):
        c = g * (GROUP // 2) + rr // 2
        qc = q_ref[rows, c * LANES:(c + 1) * LANES]
        lo = _lane_lo(qc.shape)
        keep = lo if rr % 2 == 0 else jnp.logical_not(lo)
        parts.append(jnp.where(keep, qc, jnp.zeros_like(qc)))
    return jnp.concatenate(parts, axis=0)


def _gqa_group(scores, biases, vals, sinks, rows):
    es = [[] for _ in scores]
    sink_terms = []
    for rr in range(GROUP):
        rs = slice(rr * rows, (rr + 1) * rows)
        parts = [s[rs] if b is None else s[rs] + b for s, b in zip(scores, biases)]
        m = functools.reduce(jnp.maximum, [jnp.max(p, axis=-1, keepdims=True) for p in parts])
        m = jnp.maximum(m, sinks[rr])
        for j, p in enumerate(parts):
            es[j].append(jnp.exp2(p - m).astype(BF16))
        sink_terms.append(jnp.exp2(sinks[rr] - m))
    r = functools.reduce(jnp.add, [_dot(jnp.concatenate(e, axis=0), v) for e, v in zip(es, vals)])
    heads = []
    for rr in range(GROUP):
        rh = r[rr * rows:(rr + 1) * rows]
        heads.append(rh * (1.0 / (pltpu.roll(rh, HEAD_DIM, 1) + sink_terms[rr])))
    lo = _lane_lo((rows, LANES))
    slabs = [jnp.where(lo, heads[2 * c], pltpu.roll(heads[2 * c + 1], HEAD_DIM, 1)) for c in range(GROUP // 2)]
    return jnp.concatenate(slabs, axis=-1)


def _odd_attn_p_kernel(sink_ref, q_ref, kd_ref, vd_ref, o_ref, *, seq):
    g = pl.program_id(1)
    q4 = _stack_heads(q_ref, 0, slice(None))
    sinks = [sink_ref[g * GROUP + rr] * LOG2E for rr in range(GROUP)]
    o_ref[...] = _gqa_group([_dot_t(q4, kd_ref[...])], [None], [vd_ref[...]], sinks, seq).astype(o_ref.dtype)


def _odd_attn_p(q, kd, vd, sink, *, seq):
    t = q.shape[0]
    gw = GROUP * HEAD_DIM
    return pl.pallas_call(
        functools.partial(_odd_attn_p_kernel, seq=seq),
        out_shape=jax.ShapeDtypeStruct((t, C_HEADS * HEAD_DIM), BF16),
        grid=(t // seq, C_KV_HEADS),
        in_specs=[
            pl.BlockSpec(memory_space=pltpu.SMEM),
            pl.BlockSpec((seq, gw), lambda b, g: (b, g)),
            pl.BlockSpec((seq, LANES), lambda b, g: (b, g)),
            pl.BlockSpec((seq, LANES), lambda b, g: (b, g)),
        ],
        out_specs=pl.BlockSpec((seq, gw), lambda b, g: (b, g)),
        compiler_params=_params("parallel", "parallel"),
        name="odd_attn_prompt",
    )(sink, q, kd, vd)


def _odd_attn_s_kernel(sink_ref, q_ref, kd_ref, vd_ref, kc_ref, vc_ref, bias_ref, o_ref, kcd_scr, vcd_scr, *, seq, tq):
    i = pl.program_id(1)
    sub = ODD_Q_SUB
    span = sub + 2 * WINDOW

    @pl.when(i == 0)
    def _():
        for c in range(C_KV_HEADS // 2):
            sl = slice(c * LANES, (c + 1) * LANES)
            for (d0, d1), dst in ((_dup_halves(kc_ref[:, sl]), kcd_scr), (_ones_halves(vc_ref[:, sl]), vcd_scr)):
                dst[:, 2 * c * LANES:(2 * c + 1) * LANES] = d0.astype(BF16)
                dst[:, (2 * c + 1) * LANES:(2 * c + 2) * LANES] = d1.astype(BF16)

    for jb in range(tq // sub):
        q0 = i * tq + jb * sub
        start = pl.multiple_of(jnp.clip(q0 - WINDOW, 0, seq - span), WINDOW)
        bias = bias_ref[(q0 - start) // WINDOW]
        rows = slice(jb * sub, (jb + 1) * sub)
        for g in range(C_KV_HEADS):
            gl = slice(g * LANES, (g + 1) * LANES)
            q4 = _stack_heads(q_ref, g, rows)
            s_c = _dot_t(q4, kcd_scr[:, gl])
            s_l = _dot_t(q4, kd_ref[pl.ds(start, span), gl])
            sinks = [sink_ref[g * GROUP + rr] * LOG2E for rr in range(GROUP)]
            og = _gqa_group([s_c, s_l], [None, bias], [vcd_scr[:, gl], vd_ref[pl.ds(start, span), gl]], sinks, sub)
            o_ref[rows, g * GROUP * HEAD_DIM:(g + 1) * GROUP * HEAD_DIM] = og.astype(o_ref.dtype)


def _odd_attn_s(q, kd, vd, kc, vc, sink, *, seq, past, tq):
    t = q.shape[0]
    nq = seq // tq
    nqw = C_HEADS * HEAD_DIM
    nd = C_KV_HEADS * LANES
    nk = C_KV_HEADS * HEAD_DIM
    span = ODD_Q_SUB + 2 * WINDOW
    rel = (jnp.arange(3)[:, None, None] * WINDOW + jnp.arange(ODD_Q_SUB)[None, :, None]
           - jnp.arange(span)[None, None, :])
    bias = jnp.where(jnp.abs(rel) <= WINDOW, 0.0, -jnp.inf).astype(F32)
    return pl.pallas_call(
        functools.partial(_odd_attn_s_kernel, seq=seq, tq=tq),
        out_shape=jax.ShapeDtypeStruct((t, nqw), BF16),
        grid=(t // seq, nq),
        in_specs=[
            pl.BlockSpec(memory_space=pltpu.SMEM),
            pl.BlockSpec((tq, nqw), lambda b, i: (b * nq + i, 0)),
            pl.BlockSpec((seq, nd), lambda b, i: (b, 0)),
            pl.BlockSpec((seq, nd), lambda b, i: (b, 0)),
            pl.BlockSpec((past, nk), lambda b, i: (b, 0)),
            pl.BlockSpec((past, nk), lambda b, i: (b, 0)),
            _resident(bias.shape),
        ],
        out_specs=pl.BlockSpec((tq, nqw), lambda b, i: (b * nq + i, 0)),
        scratch_shapes=[pltpu.VMEM((past, nd), BF16), pltpu.VMEM((past, nd), BF16)],
        compiler_params=_params("parallel", "arbitrary"),
        name="odd_attn_sample",
    )(sink, q, kd, vd, kc, vc, bias)


def _rope_tables(n):
    rows = n // GRID_W
    row = jnp.repeat(jnp.arange(rows, dtype=F32), GRID_W)
    col = jnp.tile(jnp.arange(GRID_W, dtype=F32), rows)
    inv = ROPE_BASE ** (-jnp.arange(0, AXIS_DIM, 2, dtype=F32) / AXIS_DIM)
    ang_r = row[:, None] * inv[None, :]
    ang_c = col[:, None] * inv[None, :]
    ang = jnp.concatenate([ang_r, ang_r, ang_c, ang_c], axis=-1)
    sign = jnp.where((jnp.arange(HEAD_DIM) & (AXIS_DIM // 2)) == 0, -1.0, 1.0).astype(F32)
    cos = jnp.cos(ang)
    sin = jnp.sin(ang) * sign[None, :]
    return jnp.tile(cos, (1, 2)), jnp.tile(sin, (1, 2))


def kernel(x_prompt, x_sample, cache_even_k, cache_even_v, cache_odd_k, cache_odd_v, c, c_ctx, w_mod, b_mod, norm_g, ffn_w13, ffn_w2, even_w_in, even_w_out, even_qk_norm, even_lambda, even_subln, even_conv_w, odd_w_in, odd_w_out, odd_qk_norm, odd_sink):
    batch, seq, _ = x_prompt.shape
    dec_batch, dec_seq, _ = x_sample.shape
    past = cache_even_k.shape[2]
    depth = w_mod.shape[0]
    assert 1 + dec_batch <= MOD_ROWS and seq % SUBLANES == 0 and dec_seq % TOKEN_TILE == 0

    xp = x_prompt.reshape(batch * seq, D_MODEL)
    xs = x_sample.reshape(dec_batch * dec_seq, D_MODEL)
    tm_p = min(TOKEN_TILE, batch * seq)
    row_p = _row_fn(0, batch * seq, tm_p)
    row_s = _row_fn(1, dec_seq, TOKEN_TILE)

    c_rows = jnp.zeros((MOD_ROWS, D_MODEL), F32).at[0].set(c_ctx).at[1:1 + dec_batch].set(c)
    mods = _adaln(c_rows, w_mod, b_mod)
    rope_tabs = _rope_tables(dec_seq)
    w13 = ffn_w13.astype(BF16)
    w2 = ffn_w2.astype(BF16)

    even_k, even_v, odd_k, odd_v = [], [], [], []
    for l in range(depth):
        mod = mods[l]
        g = [norm_g[l, s].reshape(1, D_MODEL) for s in range(3)]

        ffn_p = functools.partial(_ffn, mod=mod, w13=w13, w2=w2, layer=l, row_fn=row_p, tm=tm_p)
        ffn_s = functools.partial(_ffn, mod=mod, w13=w13, w2=w2, layer=l, row_fn=row_s, tm=TOKEN_TILE)
        xp = ffn_p(xp, g=g[0], slot=0)
        xs = ffn_s(xs, g=g[0], slot=0)

        if l % 2 == 0:
            e = l // 2
            lam_init = 0.8 - 0.6 * math.exp(-0.3 * l)
            w_in = even_w_in[e].astype(BF16)
            w_out = even_w_out[e].astype(BF16)
            qkg = jnp.tile(even_qk_norm[e], (1, 2))
            sub = even_subln[e].reshape(1, A_VDIM)

            q, k, v, bg, u = _even_in(xp, mod, g[1], w_in, qkg, None, row_fn=row_p, tm=tm_p, kv_dtype=F32)
            o = _even_attn_p(q, k, v, even_lambda[e], sub, seq=seq, lam_init=lam_init)
            xp = ffn_p(xp, g=g[2], slot=1, mixer=(w_out, o, bg, u, even_conv_w[e], seq))
            even_k.append(k.reshape(batch, seq, A_HEADS, 2 * HEAD_DIM))
            even_v.append(v.reshape(batch, seq, A_HEADS, A_VDIM))

            q, k, v, bg, u = _even_in(xs, mod, g[1], w_in, qkg, rope_tabs, row_fn=row_s, tm=TOKEN_TILE, kv_dtype=BF16)
            kc = cache_even_k[:, e].reshape(dec_batch * past, A_QK)
            vc = cache_even_v[:, e].reshape(dec_batch * past, A_V)
            o = _even_attn_s(q, k, v, kc, vc, even_lambda[e], sub, seq=dec_seq, past=past, tq=EVEN_Q_TILE,
                             lam_init=lam_init)
            xs = ffn_s(xs, g=g[2], slot=1, mixer=(w_out, o, bg, u, even_conv_w[e], dec_seq))
        else:
            e = l // 2
            w_in = odd_w_in[e].astype(BF16)
            w_out = odd_w_out[e].astype(BF16)
            qkg = jnp.tile(odd_qk_norm[e], (1, 2))
            sink = odd_sink[e]

            q, kd, vd, k, v = _odd_in(xp, mod, g[1], w_in, qkg, None, row_fn=row_p, tm=tm_p, keep_kv=True)
            o = _odd_attn_p(q, kd, vd, sink, seq=seq)
            xp = ffn_p(xp, g=g[2], slot=1, mixer=(w_out, o))
            odd_k.append(k.reshape(batch, seq, C_KV_HEADS, HEAD_DIM))
            odd_v.append(v.reshape(batch, seq, C_KV_HEADS, HEAD_DIM))

            q, kd, vd = _odd_in(xs, mod, g[1], w_in, qkg, rope_tabs, row_fn=row_s, tm=TOKEN_TILE, keep_kv=False)
            kc = cache_odd_k[:, e].reshape(dec_batch * past, C_KV_HEADS * HEAD_DIM)
            vc = cache_odd_v[:, e].reshape(dec_batch * past, C_KV_HEADS * HEAD_DIM)
            o = _odd_attn_s(q, kd, vd, kc, vc, sink, seq=dec_seq, past=past, tq=ODD_Q_TILE)
            xs = ffn_s(xs, g=g[2], slot=1, mixer=(w_out, o))

    return (xp.reshape(batch, seq, D_MODEL), xs.reshape(dec_batch, dec_seq, D_MODEL),
            jnp.stack(even_k, axis=1), jnp.stack(even_v, axis=1),
            jnp.stack(odd_k, axis=1), jnp.stack(odd_v, axis=1))
```

```python
import functools
import math

import jax
import jax.numpy as jnp
from jax import lax
from jax.experimental import pallas as pl
from jax.experimental.pallas import tpu as pltpu

F32 = jnp.float32
BF16 = jnp.bfloat16

D_MODEL = 1024
GRID_W = 64
HEAD_DIM = 64
AXIS_DIM = HEAD_DIM // 2
A_HEADS = 4
A_VDIM = 2 * HEAD_DIM
CONV_DIM = 512
C_HEADS = 16
C_KV_HEADS = 4
GROUP = C_HEADS // C_KV_HEADS
WINDOW = 128
D_FF = 2816
ROPE_BASE = 10000.0
N_MOD = 9
EPS = 1e-6
A_QK = A_HEADS * 2 * HEAD_DIM
A_V = A_HEADS * A_VDIM
EVEN_IN = 2 * A_QK + A_V + 3 * CONV_DIM
ODD_IN = (C_HEADS + 2 * C_KV_HEADS) * HEAD_DIM
LOG2E = math.log2(math.e)
Q_PRESCALE = HEAD_DIM ** -0.5 * LOG2E

LANES = 128
SUBLANES = 8
MOD_ROWS = 16
VMEM_LIMIT_BYTES = 56 * 2 ** 20

TOKEN_TILE = 512
FFN_BLOCK_TILES = 2
IN_BLOCK_TILES = 2
EVEN_Q_TILE = 1024
EVEN_Q_CHUNK = 512
EVEN_Q_SUB = 256
ODD_Q_TILE = 1024
ODD_Q_CHUNK = 256
ODD_Q_SUB = WINDOW


def _params(*semantics):
    return pltpu.CompilerParams(dimension_semantics=semantics, vmem_limit_bytes=VMEM_LIMIT_BYTES)


def _sigmoid(x):
    return 1.0 / (1.0 + jnp.exp(-x))


def _mod_norm(x, g, shift, scale):
    y = x * lax.rsqrt(jnp.mean(x * x, axis=-1, keepdims=True) + EPS)
    return (y * g) * (1.0 + scale) + shift


def _mod_row(mod_ref, k, r):
    return mod_ref[k, pl.ds(r, 1), :]


def _row_fn(row0, tokens_per_row, tile):
    return lambda i: row0 + (i * tile) // tokens_per_row


def _lane_lo(shape):
    return lax.broadcasted_iota(jnp.int32, shape, len(shape) - 1) < HEAD_DIM


def _seg_rms(xs, gain):
    lo = _lane_lo(xs.shape)
    sq = xs * xs
    s_lo = jnp.sum(jnp.where(lo, sq, 0.0), axis=-1, keepdims=True)
    s_hi = jnp.sum(jnp.where(lo, 0.0, sq), axis=-1, keepdims=True)
    inv = lax.rsqrt(jnp.where(lo, s_lo, s_hi) * (1.0 / HEAD_DIM) + EPS)
    return (xs * inv) * gain


def _half_sum_matrix():
    r = lax.broadcasted_iota(jnp.int32, (2 * LANES, LANES), 0)
    c = lax.broadcasted_iota(jnp.int32, (2 * LANES, LANES), 1)
    return jnp.where((r & HEAD_DIM) == (c & HEAD_DIM), 1.0, 0.0).astype(BF16)


def _seg_rms_mxu(xs, gain, ones2):
    sq = xs * xs
    head = sq.astype(BF16)
    rest = (sq - head.astype(F32)).astype(BF16)
    ssum = _dot(jnp.concatenate([head, rest], axis=-1), ones2)
    return (xs * lax.rsqrt(ssum * (1.0 / HEAD_DIM) + EPS)) * gain


def _rope(xs, cos, sin_signed):
    lane = lax.broadcasted_iota(jnp.int32, xs.shape, 1)
    first = (lane & (AXIS_DIM // 2)) == 0
    partner = jnp.where(first, pltpu.roll(xs, LANES - AXIS_DIM // 2, 1), pltpu.roll(xs, AXIS_DIM // 2, 1))
    return xs * cos + partner * sin_signed


def _dot_t(a, b):
    return lax.dot_general(a, b, (((1,), (1,)), ((), ())), preferred_element_type=F32)


def _dot(a, b):
    return jnp.dot(a, b, preferred_element_type=F32)


def _adaln_kernel(c_ref, w_ref, b_ref, o_ref):
    c = c_ref[...]
    s = c * _sigmoid(c)
    o_ref[0, 0] = _dot(s.astype(BF16), w_ref[0].astype(BF16)) + b_ref[0]


def _adaln(c_rows, w_mod, b_mod):
    depth = w_mod.shape[0]
    b3 = b_mod.reshape(depth * N_MOD, 1, D_MODEL)
    return pl.pallas_call(
        _adaln_kernel,
        out_shape=jax.ShapeDtypeStruct((depth, N_MOD, MOD_ROWS, D_MODEL), F32),
        grid=(depth, N_MOD),
        in_specs=[
            pl.BlockSpec((MOD_ROWS, D_MODEL), lambda l, j: (0, 0)),
            pl.BlockSpec((1, D_MODEL, D_MODEL), lambda l, j: (l, 0, j)),
            pl.BlockSpec((1, 1, D_MODEL), lambda l, j: (l * N_MOD + j, 0, 0)),
        ],
        out_specs=pl.BlockSpec((1, 1, MOD_ROWS, D_MODEL), lambda l, j: (l, j, 0, 0)),
        compiler_params=_params("parallel", "parallel"),
        name="adaln",
    )(c_rows, w_mod, b3)


def _ffn_tail(x, r, mod_ref, g_ref, w13_ref, w2_ref, k0):
    h = _mod_norm(x, g_ref[...], _mod_row(mod_ref, k0, r), _mod_row(mod_ref, k0 + 1, r))
    a = _dot(h.astype(BF16), w13_ref[0, 0])
    gate = a[:, :D_FF]
    up = a[:, D_FF:]
    act = (gate * _sigmoid(gate)) * up
    y = _dot(act.astype(BF16), w2_ref[0, 0])
    return x + (0.5 * _mod_row(mod_ref, k0 + 2, r)) * y


def _aligned(row, multiple):
    return row if isinstance(row, int) else pl.multiple_of(row, multiple)


def _for_chunks(block_rows, chunk, body):
    n = block_rows // chunk
    if n == 1:
        body(0, 0)
    else:
        def step(j, carry):
            body(j, pl.multiple_of(j * chunk, chunk))
            return carry
        lax.fori_loop(0, n, step, 0)


def _ffn_kernel(x_ref, mod_ref, g_ref, w13_ref, w2_ref, o_ref, *, row_fn, tm):
    n = x_ref.shape[0] // tm

    def body(j, row0):
        rows = pl.ds(row0, tm)
        r = row_fn(pl.program_id(0) * n + j)
        o_ref[rows, :] = _ffn_tail(x_ref[rows, :], r, mod_ref, g_ref, w13_ref, w2_ref, 0)

    _for_chunks(x_ref.shape[0], tm, body)


def _conv_gate(bg_ref, u_ref, up_ref, un_ref, cw_ref, i, seq, tm):
    u = u_ref[...]
    row = lax.broadcasted_iota(jnp.int32, (tm, 1), 0)
    pos = (row + i * tm) % seq
    u_dn = jnp.where(row == 0, up_ref[SUBLANES - 1:SUBLANES, :], pltpu.roll(u, 1, 0))
    u_dn = jnp.where(pos == 0, 0.0, u_dn)
    u_up = jnp.where(row == tm - 1, un_ref[0:1, :], pltpu.roll(u, tm - 1, 0))
    u_up = jnp.where(pos == seq - 1, 0.0, u_up)
    return bg_ref[...] * (cw_ref[0:1, :] * u_dn + cw_ref[1:2, :] * u + cw_ref[2:3, :] * u_up)


def _even_out_ffn_kernel(x_ref, a_ref, bg_ref, u_ref, up_ref, un_ref, cw_ref, wo_ref, mod_ref, g_ref, w13_ref, w2_ref,
                         o_ref, *, row_fn, seq, tm):
    i = pl.program_id(0)
    r = row_fn(i)
    y = _conv_gate(bg_ref, u_ref, up_ref, un_ref, cw_ref, i, seq, tm)
    mix = _dot(jnp.concatenate([a_ref[...], y.astype(BF16)], axis=-1), wo_ref[...])
    x = x_ref[...] + _mod_row(mod_ref, 5, r) * mix
    o_ref[...] = _ffn_tail(x, r, mod_ref, g_ref, w13_ref, w2_ref, 6)


def _odd_out_ffn_kernel(x_ref, a_ref, wo_ref, mod_ref, g_ref, w13_ref, w2_ref, o_ref, *, row_fn):
    r = row_fn(pl.program_id(0))
    x = x_ref[...] + _mod_row(mod_ref, 5, r) * _dot(a_ref[...], wo_ref[...])
    o_ref[...] = _ffn_tail(x, r, mod_ref, g_ref, w13_ref, w2_ref, 6)


def _resident(shape, index=None):
    index = (0,) * len(shape) if index is None else index
    return pl.BlockSpec(shape, lambda *_: index, pipeline_mode=pl.Buffered(1))


def _ffn(x, mod, g, w13, w2, *, layer, slot, row_fn, tm, mixer=None):
    t = x.shape[0]
    block = tm if mixer is not None else min(t, FFN_BLOCK_TILES * tm)
    tile = lambda n: pl.BlockSpec((block, n), lambda i: (i, 0))
    ffn_specs = [
        _resident((N_MOD, MOD_ROWS, D_MODEL)),
        _resident((1, D_MODEL)),
        _resident((1, 1, D_MODEL, 2 * D_FF), (layer, slot, 0, 0)),
        _resident((1, 1, D_FF, D_MODEL), (layer, slot, 0, 0)),
    ]
    ffn_args = [mod, g, w13, w2]
    if mixer is None:
        body = functools.partial(_ffn_kernel, row_fn=row_fn, tm=tm)
        in_specs, args, name = [tile(D_MODEL)], [x], "ffn"
    elif len(mixer) == 2:
        w_out, attn = mixer
        body = functools.partial(_odd_out_ffn_kernel, row_fn=row_fn)
        in_specs = [tile(D_MODEL), tile(attn.shape[1]), _resident(w_out.shape)]
        args, name = [x, attn, w_out], "odd_out_ffn"
    else:
        w_out, attn, bg, u, conv_w, seq = mixer
        nb = tm // SUBLANES
        last = t // SUBLANES - 1
        body = functools.partial(_even_out_ffn_kernel, row_fn=row_fn, seq=seq, tm=tm)
        in_specs = [
            tile(D_MODEL), tile(A_V), tile(CONV_DIM), tile(CONV_DIM),
            pl.BlockSpec((SUBLANES, CONV_DIM), lambda i: (jnp.maximum(i * nb - 1, 0), 0)),
            pl.BlockSpec((SUBLANES, CONV_DIM), lambda i: (jnp.minimum((i + 1) * nb, last), 0)),
            _resident(conv_w.shape), _resident(w_out.shape),
        ]
        args, name = [x, attn, bg, u, u, u, conv_w, w_out], "even_out_ffn"
    return pl.pallas_call(
        body,
        out_shape=jax.ShapeDtypeStruct((t, D_MODEL), F32),
        grid=(t // block,),
        in_specs=in_specs + ffn_specs,
        out_specs=tile(D_MODEL),
        compiler_params=_params("parallel"),
        name=name,
    )(*args, *ffn_args)


def _even_in_kernel(x_ref, mod_ref, g_ref, w_ref, qkg_ref, *rest, row_fn, rope, tm):
    if rope:
        cos_ref, sin_ref, q_ref, k_ref, v_ref, bg_ref, u_ref = rest
    else:
        q_ref, k_ref, v_ref, bg_ref, u_ref = rest
    n = x_ref.shape[0] // tm
    q_gain = qkg_ref[0:1, :] * Q_PRESCALE

    def body(j, row0):
        rows = pl.ds(row0, tm)
        r = row_fn(pl.program_id(0) * n + j)
        h = _mod_norm(x_ref[rows, :], g_ref[...], _mod_row(mod_ref, 3, r), _mod_row(mod_ref, 4, r))
        y = _dot(h.astype(BF16), w_ref[...])
        for hh in range(A_HEADS):
            sl = slice(hh * LANES, (hh + 1) * LANES)
            qs = _seg_rms(y[:, sl], q_gain)
            ks = _seg_rms(y[:, A_QK + hh * LANES:A_QK + (hh + 1) * LANES], qkg_ref[1:2, :])
            if rope:
                qs = _rope(qs, cos_ref[rows, :], sin_ref[rows, :])
                ks = _rope(ks, cos_ref[rows, :], sin_ref[rows, :])
            q_ref[rows, sl] = qs.astype(q_ref.dtype)
            k_ref[rows, sl] = ks.astype(k_ref.dtype)
        v_ref[rows, :] = y[:, 2 * A_QK:2 * A_QK + A_V].astype(v_ref.dtype)
        o = 2 * A_QK + A_V
        bg_ref[rows, :] = y[:, o:o + CONV_DIM]
        u_ref[rows, :] = y[:, o + CONV_DIM:o + 2 * CONV_DIM] * y[:, o + 2 * CONV_DIM:o + 3 * CONV_DIM]

    _for_chunks(x_ref.shape[0], tm, body)


def _even_in(x, mod, g, w_in, qkg, rope_tabs, *, row_fn, tm, kv_dtype):
    t = x.shape[0]
    rope = rope_tabs is not None
    block = min(t, IN_BLOCK_TILES * tm)
    in_specs = [
        pl.BlockSpec((block, D_MODEL), lambda i: (i, 0)),
        _resident((N_MOD, MOD_ROWS, D_MODEL)),
        _resident((1, D_MODEL)),
        _resident((D_MODEL, EVEN_IN)),
        _resident((2, LANES)),
    ]
    args = [x, mod, g, w_in, qkg]
    if rope:
        nt = rope_tabs[0].shape[0] // block
        in_specs += [pl.BlockSpec((block, LANES), lambda i: (i % nt, 0))] * 2
        args += list(rope_tabs)
    wide = lambda dt: jax.ShapeDtypeStruct((t, A_QK), dt)
    spec = pl.BlockSpec((block, A_QK), lambda i: (i, 0))
    return pl.pallas_call(
        functools.partial(_even_in_kernel, row_fn=row_fn, rope=rope, tm=tm),
        out_shape=(wide(BF16), wide(kv_dtype), wide(kv_dtype), wide(F32), wide(F32)),
        grid=(t // block,),
        in_specs=in_specs,
        out_specs=(spec,) * 5,
        compiler_params=_params("parallel"),
        name="even_in",
    )(*args)


def _lambda(lam_ref, lam_init):
    lf = lam_ref[...]
    a = jnp.sum(lf[0:1] * lf[1:2], axis=-1, keepdims=True)
    b = jnp.sum(lf[2:3] * lf[3:4], axis=-1, keepdims=True)
    return jnp.exp(a) - jnp.exp(b) + lam_init


def _subln(o, sub, lam_init):
    y = o * lax.rsqrt(jnp.mean(o * o, axis=-1, keepdims=True) + EPS)
    return (y * sub) * (1.0 - lam_init)


def _softmax_pv(q, k, v1):
    s = _dot_t(q, k)
    m = jnp.max(s, axis=-1, keepdims=True)
    r = _dot(jnp.exp2(s - m).astype(BF16), v1)
    return r[:, :LANES] * (1.0 / r[:, LANES:])


def _diff_attn_heads(q_ref, head_kv, lam_ref, sub_ref, o_ref, lam_init):
    lam = _lambda(lam_ref, lam_init)
    tq = q_ref.shape[0]
    chunk = min(tq, EVEN_Q_CHUNK)
    sub_rows = min(chunk, EVEN_Q_SUB)

    def body(j, row0):
        for rb in range(chunk // sub_rows):
            rows = pl.ds(_aligned(row0 + rb * sub_rows, sub_rows), sub_rows)
            for h in range(A_HEADS):
                sl = slice(h * LANES, (h + 1) * LANES)
                q = q_ref[rows, sl]
                k, v1 = head_kv(h)
                lo = _lane_lo(q.shape)
                zero = jnp.zeros_like(q)
                o1 = _softmax_pv(jnp.where(lo, q, zero), k, v1)
                o2 = _softmax_pv(jnp.where(lo, zero, q), k, v1)
                o_ref[rows, sl] = _subln(o1 - lam * o2, sub_ref[...], lam_init).astype(o_ref.dtype)

    _for_chunks(tq, chunk, body)


def _even_attn_p_kernel(q_ref, k_ref, v_ref, lam_ref, sub_ref, o_ref, *, lam_init):
    ones = jnp.ones((k_ref.shape[0], LANES), BF16)

    def head_kv(h):
        sl = slice(h * LANES, (h + 1) * LANES)
        return k_ref[:, sl].astype(BF16), jnp.concatenate([v_ref[:, sl].astype(BF16), ones], axis=-1)

    _diff_attn_heads(q_ref, head_kv, lam_ref, sub_ref, o_ref, lam_init)


def _even_attn_p(q, k, v, lam_vec, sub, *, seq, lam_init):
    t = q.shape[0]
    blk = pl.BlockSpec((seq, A_QK), lambda b: (b, 0))
    return pl.pallas_call(
        functools.partial(_even_attn_p_kernel, lam_init=lam_init),
        out_shape=jax.ShapeDtypeStruct((t, A_V), BF16),
        grid=(t // seq,),
        in_specs=[blk, blk, blk, _resident((4, HEAD_DIM)), _resident((1, A_VDIM))],
        out_specs=blk,
        compiler_params=_params("parallel"),
        name="even_attn_prompt",
    )(q, k, v, lam_vec, sub)


def _even_attn_s_kernel(q_ref, kl_ref, vl_ref, kc_ref, vc_ref, lam_ref, sub_ref, o_ref, k_scr, v1_scr, *, past, lam_init):
    @pl.when(pl.program_id(1) == 0)
    def _():
        k_scr[:past, :] = kc_ref[...].astype(BF16)
        k_scr[past:, :] = kl_ref[...]
        for h in range(A_HEADS):
            sl = slice(h * LANES, (h + 1) * LANES)
            v1_scr[h, :past, :LANES] = vc_ref[:, sl].astype(BF16)
            v1_scr[h, past:, :LANES] = vl_ref[:, sl]
            v1_scr[h, :, LANES:] = jnp.ones((v1_scr.shape[1], LANES), BF16)

    def head_kv(h):
        return k_scr[:, h * LANES:(h + 1) * LANES], v1_scr[h]

    _diff_attn_heads(q_ref, head_kv, lam_ref, sub_ref, o_ref, lam_init)


def _even_attn_s(q, kl, vl, kc, vc, lam_vec, sub, *, seq, past, tq, lam_init):
    t = q.shape[0]
    nq = seq // tq
    qblk = pl.BlockSpec((tq, A_QK), lambda b, i: (b * nq + i, 0))
    lat = pl.BlockSpec((seq, A_QK), lambda b, i: (b, 0))
    ctx = pl.BlockSpec((past, A_QK), lambda b, i: (b, 0))
    return pl.pallas_call(
        functools.partial(_even_attn_s_kernel, past=past, lam_init=lam_init),
        out_shape=jax.ShapeDtypeStruct((t, A_V), BF16),
        grid=(t // seq, nq),
        in_specs=[qblk, lat, lat, ctx, ctx, _resident((4, HEAD_DIM)), _resident((1, A_VDIM))],
        out_specs=qblk,
        scratch_shapes=[pltpu.VMEM((past + seq, A_QK), BF16),
                        pltpu.VMEM((A_HEADS, past + seq, 2 * LANES), BF16)],
        compiler_params=_params("parallel", "arbitrary"),
        name="even_attn_sample",
    )(q, kl, vl, kc, vc, lam_vec, sub)


def _dup_halves(x):
    lo = _lane_lo(x.shape)
    sw = pltpu.roll(x, HEAD_DIM, 1)
    return jnp.where(lo, x, sw), jnp.where(lo, sw, x)


def _ones_halves(x):
    lo = _lane_lo(x.shape)
    return jnp.where(lo, x, 1.0), jnp.where(lo, pltpu.roll(x, HEAD_DIM, 1), 1.0)


def _odd_in_kernel(x_ref, mod_ref, g_ref, w_ref, qkg_ref, *rest, row_fn, rope, keep_kv, tm):
    rest = list(rest)
    if rope:
        cos_ref, sin_ref = rest[:2]
        rest = rest[2:]
    if keep_kv:
        q_ref, kd_ref, vd_ref, k_ref, v_ref = rest
    else:
        q_ref, kd_ref, vd_ref = rest
    nq = C_HEADS * HEAD_DIM
    nk = C_KV_HEADS * HEAD_DIM
    q_gain = qkg_ref[0:1, :] * Q_PRESCALE
    n = x_ref.shape[0] // tm
    ones2 = _half_sum_matrix()

    def body(j, row0):
        r = row_fn(pl.program_id(0) * n + j)
        for half in range(2):
            rows = pl.ds(_aligned(row0 + half * (tm // 2), tm // 2), tm // 2)
            h = _mod_norm(x_ref[rows, :], g_ref[...], _mod_row(mod_ref, 3, r), _mod_row(mod_ref, 4, r))
            y = _dot(h.astype(BF16), w_ref[...])
            if rope:
                cos, sin = cos_ref[rows, :], sin_ref[rows, :]
            for c in range(nq // LANES):
                sl = slice(c * LANES, (c + 1) * LANES)
                qs = _seg_rms_mxu(y[:, sl], q_gain, ones2)
                if rope:
                    qs = _rope(qs, cos, sin)
                q_ref[rows, sl] = qs.astype(q_ref.dtype)
            for c in range(nk // LANES):
                sl = slice(c * LANES, (c + 1) * LANES)
                ks = _seg_rms_mxu(y[:, nq + c * LANES:nq + (c + 1) * LANES], qkg_ref[1:2, :], ones2)
                vs = y[:, nq + nk + c * LANES:nq + nk + (c + 1) * LANES]
                if keep_kv:
                    k_ref[rows, sl] = ks
                    v_ref[rows, sl] = vs
                if rope:
                    ks = _rope(ks, cos, sin)
                for (d0, d1), ref in ((_dup_halves(ks), kd_ref), (_ones_halves(vs), vd_ref)):
                    ref[rows, 2 * c * LANES:(2 * c + 1) * LANES] = d0.astype(ref.dtype)
                    ref[rows, (2 * c + 1) * LANES:(2 * c + 2) * LANES] = d1.astype(ref.dtype)

    _for_chunks(x_ref.shape[0], tm, body)


def _odd_in(x, mod, g, w_in, qkg, rope_tabs, *, row_fn, tm, keep_kv):
    t = x.shape[0]
    rope = rope_tabs is not None
    block = min(t, IN_BLOCK_TILES * tm)
    in_specs = [
        pl.BlockSpec((block, D_MODEL), lambda i: (i, 0)),
        _resident((N_MOD, MOD_ROWS, D_MODEL)),
        _resident((1, D_MODEL)),
        _resident((D_MODEL, ODD_IN)),
        _resident((2, LANES)),
    ]
    args = [x, mod, g, w_in, qkg]
    if rope:
        nt = rope_tabs[0].shape[0] // block
        in_specs += [pl.BlockSpec((block, LANES), lambda i: (i % nt, 0))] * 2
        args += list(rope_tabs)
    nq = C_HEADS * HEAD_DIM
    nd = C_KV_HEADS * LANES
    nk = C_KV_HEADS * HEAD_DIM
    out_shape = [jax.ShapeDtypeStruct((t, nq), BF16), jax.ShapeDtypeStruct((t, nd), BF16),
                 jax.ShapeDtypeStruct((t, nd), BF16)]
    out_specs = [pl.BlockSpec((block, nq), lambda i: (i, 0)), pl.BlockSpec((block, nd), lambda i: (i, 0)),
                 pl.BlockSpec((block, nd), lambda i: (i, 0))]
    if keep_kv:
        out_shape += [jax.ShapeDtypeStruct((t, nk), F32)] * 2
        out_specs += [pl.BlockSpec((block, nk), lambda i: (i, 0))] * 2
    return pl.pallas_call(
        functools.partial(_odd_in_kernel, row_fn=row_fn, rope=rope, keep_kv=keep_kv, tm=tm),
        out_shape=tuple(out_shape),
        grid=(t // block,),
        in_specs=in_specs,
        out_specs=tuple(out_specs),
        compiler_params=_params("parallel"),
        name="odd_in",
    )(*args)


def _stack_heads(q_ref, g, rows):
    parts = []
    for rr in range(GROUP):
        c = g * (GROUP // 2) + rr // 2
        qc = q_ref[rows, c * LANES:(c + 1) * LANES]
        lo = _lane_lo(qc.shape)
        keep = lo if rr % 2 == 0 else jnp.logical_not(lo)
        parts.append(jnp.where(keep, qc, jnp.zeros_like(qc)))
    return jnp.concatenate(parts, axis=0)


def _gqa_group(scores, biases, vals, sinks, rows):
    es = [[] for _ in scores]
    sink_terms = []
    for rr in range(GROUP):
        rs = slice(rr * rows, (rr + 1) * rows)
        parts = [s[rs] if b is None else s[rs] + b for s, b in zip(scores, biases)]
        m = functools.reduce(jnp.maximum, [jnp.max(p, axis=-1, keepdims=True) for p in parts])
        m = jnp.maximum(m, sinks[rr])
        for j, p in enumerate(parts):
            es[j].append(jnp.exp2(p - m).astype(BF16))
        sink_terms.append(jnp.exp2(sinks[rr] - m))
    r = functools.reduce(jnp.add, [_dot(jnp.concatenate(e, axis=0), v) for e, v in zip(es, vals)])
    heads = []
    for rr in range(GROUP):
        rh = r[rr * rows:(rr + 1) * rows]
        heads.append(rh * (1.0 / (pltpu.roll(rh, HEAD_DIM, 1) + sink_terms[rr])))
    lo = _lane_lo((rows, LANES))
    slabs = [jnp.where(lo, heads[2 * c], pltpu.roll(heads[2 * c + 1], HEAD_DIM, 1)) for c in range(GROUP // 2)]
    return jnp.concatenate(slabs, axis=-1)


def _odd_attn_p_kernel(sink_ref, q_ref, kd_ref, vd_ref, o_ref, *, seq):
    g = pl.program_id(1)
    q4 = _stack_heads(q_ref, 0, slice(None))
    sinks = [sink_ref[g * GROUP + rr] * LOG2E for rr in range(GROUP)]
    o_ref[...] = _gqa_group([_dot_t(q4, kd_ref[...])], [None], [vd_ref[...]], sinks, seq).astype(o_ref.dtype)


def _odd_attn_p(q, kd, vd, sink, *, seq):
    t = q.shape[0]
    gw = GROUP * HEAD_DIM
    return pl.pallas_call(
        functools.partial(_odd_attn_p_kernel, seq=seq),
        out_shape=jax.ShapeDtypeStruct((t, C_HEADS * HEAD_DIM), BF16),
        grid=(t // seq, C_KV_HEADS),
        in_specs=[
            pl.BlockSpec(memory_space=pltpu.SMEM),
            pl.BlockSpec((seq, gw), lambda b, g: (b, g)),
            pl.BlockSpec((seq, LANES), lambda b, g: (b, g)),
            pl.BlockSpec((seq, LANES), lambda b, g: (b, g)),
        ],
        out_specs=pl.BlockSpec((seq, gw), lambda b, g: (b, g)),
        compiler_params=_params("parallel", "parallel"),
        name="odd_attn_prompt",
    )(sink, q, kd, vd)


def _odd_attn_s_kernel(sink_ref, q_ref, kd_ref, vd_ref, kc_ref, vc_ref, bias_ref, o_ref, kcd_scr, vcd_scr, *, seq, tq):
    i = pl.program_id(1)
    sub = ODD_Q_SUB
    span = sub + 2 * WINDOW

    @pl.when(i == 0)
    def _():
        for c in range(C_KV_HEADS // 2):
            sl = slice(c * LANES, (c + 1) * LANES)
            for (d0, d1), dst in ((_dup_halves(kc_ref[:, sl]), kcd_scr), (_ones_halves(vc_ref[:, sl]), vcd_scr)):
                dst[:, 2 * c * LANES:(2 * c + 1) * LANES] = d0.astype(BF16)
                dst[:, (2 * c + 1) * LANES:(2 * c + 2) * LANES] = d1.astype(BF16)

    chunk = min(tq, ODD_Q_CHUNK)

    def body(j, row0):
        for jb in range(chunk // sub):
            q0 = i * tq + row0 + jb * sub
            start = pl.multiple_of(jnp.clip(q0 - WINDOW, 0, seq - span), WINDOW)
            bias = bias_ref[(q0 - start) // WINDOW]
            rows = pl.ds(_aligned(row0 + jb * sub, sub), sub)
            for g in range(C_KV_HEADS):
                gl = slice(g * LANES, (g + 1) * LANES)
                q4 = _stack_heads(q_ref, g, rows)
                s_c = _dot_t(q4, kcd_scr[:, gl])
                s_l = _dot_t(q4, kd_ref[pl.ds(start, span), gl])
                sinks = [sink_ref[g * GROUP + rr] * LOG2E for rr in range(GROUP)]
                og = _gqa_group([s_c, s_l], [None, bias], [vcd_scr[:, gl], vd_ref[pl.ds(start, span), gl]],
                                sinks, sub)
                o_ref[rows, g * GROUP * HEAD_DIM:(g + 1) * GROUP * HEAD_DIM] = og.astype(o_ref.dtype)

    _for_chunks(tq, chunk, body)


def _odd_attn_s(q, kd, vd, kc, vc, sink, *, seq, past, tq):
    t = q.shape[0]
    nq = seq // tq
    nqw = C_HEADS * HEAD_DIM
    nd = C_KV_HEADS * LANES
    nk = C_KV_HEADS * HEAD_DIM
    span = ODD_Q_SUB + 2 * WINDOW
    rel = (jnp.arange(3)[:, None, None] * WINDOW + jnp.arange(ODD_Q_SUB)[None, :, None]
           - jnp.arange(span)[None, None, :])
    bias = jnp.where(jnp.abs(rel) <= WINDOW, 0.0, -jnp.inf).astype(F32)
    return pl.pallas_call(
        functools.partial(_odd_attn_s_kernel, seq=seq, tq=tq),
        out_shape=jax.ShapeDtypeStruct((t, nqw), BF16),
        grid=(t // seq, nq),
        in_specs=[
            pl.BlockSpec(memory_space=pltpu.SMEM),
            pl.BlockSpec((tq, nqw), lambda b, i: (b * nq + i, 0)),
            pl.BlockSpec((seq, nd), lambda b, i: (b, 0)),
            pl.BlockSpec((seq, nd), lambda b, i: (b, 0)),
            pl.BlockSpec((past, nk), lambda b, i: (b, 0)),
            pl.BlockSpec((past, nk), lambda b, i: (b, 0)),
            _resident(bias.shape),
        ],
        out_specs=pl.BlockSpec((tq, nqw), lambda b, i: (b * nq + i, 0)),
        scratch_shapes=[pltpu.VMEM((past, nd), BF16), pltpu.VMEM((past, nd), BF16)],
        compiler_params=_params("parallel", "arbitrary"),
        name="odd_attn_sample",
    )(sink, q, kd, vd, kc, vc, bias)


def _rope_tables(n):
    rows = n // GRID_W
    row = jnp.repeat(jnp.arange(rows, dtype=F32), GRID_W)
    col = jnp.tile(jnp.arange(GRID_W, dtype=F32), rows)
    inv = ROPE_BASE ** (-jnp.arange(0, AXIS_DIM, 2, dtype=F32) / AXIS_DIM)
    ang_r = row[:, None] * inv[None, :]
    ang_c = col[:, None] * inv[None, :]
    ang = jnp.concatenate([ang_r, ang_r, ang_c, ang_c], axis=-1)
    sign = jnp.where((jnp.arange(HEAD_DIM) & (AXIS_DIM // 2)) == 0, -1.0, 1.0).astype(F32)
    cos = jnp.cos(ang)
    sin = jnp.sin(ang) * sign[None, :]
    return jnp.tile(cos, (1, 2)), jnp.tile(sin, (1, 2))


def kernel(x_prompt, x_sample, cache_even_k, cache_even_v, cache_odd_k, cache_odd_v, c, c_ctx, w_mod, b_mod, norm_g, ffn_w13, ffn_w2, even_w_in, even_w_out, even_qk_norm, even_lambda, even_subln, even_conv_w, odd_w_in, odd_w_out, odd_qk_norm, odd_sink):
    batch, seq, _ = x_prompt.shape
    dec_batch, dec_seq, _ = x_sample.shape
    past = cache_even_k.shape[2]
    depth = w_mod.shape[0]
    assert 1 + dec_batch <= MOD_ROWS and seq % SUBLANES == 0 and dec_seq % TOKEN_TILE == 0

    xp = x_prompt.reshape(batch * seq, D_MODEL)
    xs = x_sample.reshape(dec_batch * dec_seq, D_MODEL)
    tm_p = min(TOKEN_TILE, batch * seq)
    row_p = _row_fn(0, batch * seq, tm_p)
    row_s = _row_fn(1, dec_seq, TOKEN_TILE)

    c_rows = jnp.zeros((MOD_ROWS, D_MODEL), F32).at[0].set(c_ctx).at[1:1 + dec_batch].set(c)
    mods = _adaln(c_rows, w_mod, b_mod)
    rope_tabs = _rope_tables(dec_seq)
    w13 = ffn_w13.astype(BF16)
    w2 = ffn_w2.astype(BF16)

    even_k, even_v, odd_k, odd_v = [], [], [], []
    for l in range(depth):
        mod = mods[l]
        g = [norm_g[l, s].reshape(1, D_MODEL) for s in range(3)]

        ffn_p = functools.partial(_ffn, mod=mod, w13=w13, w2=w2, layer=l, row_fn=row_p, tm=tm_p)
        ffn_s = functools.partial(_ffn, mod=mod, w13=w13, w2=w2, layer=l, row_fn=row_s, tm=TOKEN_TILE)
        xp = ffn_p(xp, g=g[0], slot=0)
        xs = ffn_s(xs, g=g[0], slot=0)

        if l % 2 == 0:
            e = l // 2
            lam_init = 0.8 - 0.6 * math.exp(-0.3 * l)
            w_in = even_w_in[e].astype(BF16)
            w_out = even_w_out[e].astype(BF16)
            qkg = jnp.tile(even_qk_norm[e], (1, 2))
            sub = even_subln[e].reshape(1, A_VDIM)

            q, k, v, bg, u = _even_in(xp, mod, g[1], w_in, qkg, None, row_fn=row_p, tm=tm_p, kv_dtype=F32)
            o = _even_attn_p(q, k, v, even_lambda[e], sub, seq=seq, lam_init=lam_init)
            xp = ffn_p(xp, g=g[2], slot=1, mixer=(w_out, o, bg, u, even_conv_w[e], seq))
            even_k.append(k.reshape(batch, seq, A_HEADS, 2 * HEAD_DIM))
            even_v.append(v.reshape(batch, seq, A_HEADS, A_VDIM))

            q, k, v, bg, u = _even_in(xs, mod, g[1], w_in, qkg, rope_tabs, row_fn=row_s, tm=TOKEN_TILE, kv_dtype=BF16)
            kc = cache_even_k[:, e].reshape(dec_batch * past, A_QK)
            vc = cache_even_v[:, e].reshape(dec_batch * past, A_V)
            o = _even_attn_s(q, k, v, kc, vc, even_lambda[e], sub, seq=dec_seq, past=past, tq=EVEN_Q_TILE,
                             lam_init=lam_init)
            xs = ffn_s(xs, g=g[2], slot=1, mixer=(w_out, o, bg, u, even_conv_w[e], dec_seq))
        else:
            e = l // 2
            w_in = odd_w_in[e].astype(BF16)
            w_out = odd_w_out[e].astype(BF16)
            qkg = jnp.tile(odd_qk_norm[e], (1, 2))
            sink = odd_sink[e]

            q, kd, vd, k, v = _odd_in(xp, mod, g[1], w_in, qkg, None, row_fn=row_p, tm=tm_p, keep_kv=True)
            o = _odd_attn_p(q, kd, vd, sink, seq=seq)
            xp = ffn_p(xp, g=g[2], slot=1, mixer=(w_out, o))
            odd_k.append(k.reshape(batch, seq, C_KV_HEADS, HEAD_DIM))
            odd_v.append(v.reshape(batch, seq, C_KV_HEADS, HEAD_DIM))

            q, kd, vd = _odd_in(xs, mod, g[1], w_in, qkg, rope_tabs, row_fn=row_s, tm=TOKEN_TILE, keep_kv=False)
            kc = cache_odd_k[:, e].reshape(dec_batch * past, C_KV_HEADS * HEAD_DIM)
            vc = cache_odd_v[:, e].reshape(dec_batch * past, C_KV_HEADS * HEAD_DIM)
            o = _odd_attn_s(q, kd, vd, kc, vc, sink, seq=dec_seq, past=past, tq=ODD_Q_TILE)
            xs = ffn_s(xs, g=g[2], slot=1, mixer=(w_out, o))

    return (xp.reshape(batch, seq, D_MODEL), xs.reshape(dec_batch, dec_seq, D_MODEL),
            jnp.stack(even_k, axis=1), jnp.stack(even_v, axis=1),
            jnp.stack(odd_k, axis=1), jnp.stack(odd_v, axis=1))
```

```python
import functools
import math

import jax
import jax.numpy as jnp
from jax import lax
from jax.experimental import pallas as pl
from jax.experimental.pallas import tpu as pltpu

F32 = jnp.float32
BF16 = jnp.bfloat16

D_MODEL = 1024
GRID_W = 64
HEAD_DIM = 64
AXIS_DIM = HEAD_DIM // 2
A_HEADS = 4
A_VDIM = 2 * HEAD_DIM
CONV_DIM = 512
C_HEADS = 16
C_KV_HEADS = 4
GROUP = C_HEADS // C_KV_HEADS
WINDOW = 128
D_FF = 2816
ROPE_BASE = 10000.0
N_MOD = 9
EPS = 1e-6
A_QK = A_HEADS * 2 * HEAD_DIM
A_V = A_HEADS * A_VDIM
EVEN_IN = 2 * A_QK + A_V + 3 * CONV_DIM
ODD_IN = (C_HEADS + 2 * C_KV_HEADS) * HEAD_DIM
LOG2E = math.log2(math.e)
Q_PRESCALE = HEAD_DIM ** -0.5 * LOG2E

LANES = 128
SUBLANES = 8
MOD_ROWS = 16
VMEM_LIMIT_BYTES = 56 * 2 ** 20

TOKEN_TILE = 512
FFN_BLOCK_TILES = 1
IN_BLOCK_TILES = 1
EVEN_Q_TILE = 512
EVEN_Q_CHUNK = 512
EVEN_Q_SUB = 256
ODD_Q_TILE = 256
ODD_Q_CHUNK = 256
ODD_Q_SUB = WINDOW


def _params(*semantics):
    return pltpu.CompilerParams(dimension_semantics=semantics, vmem_limit_bytes=VMEM_LIMIT_BYTES)


def _sigmoid(x):
    return 1.0 / (1.0 + jnp.exp(-x))


def _mod_norm(x, g, shift, scale):
    y = x * lax.rsqrt(jnp.mean(x * x, axis=-1, keepdims=True) + EPS)
    return (y * g) * (1.0 + scale) + shift


def _mod_row(mod_ref, k, r):
    return mod_ref[k, pl.ds(r, 1), :]


def _row_fn(row0, tokens_per_row, tile):
    return lambda i: row0 + (i * tile) // tokens_per_row


def _lane_lo(shape):
    return lax.broadcasted_iota(jnp.int32, shape, len(shape) - 1) < HEAD_DIM


def _seg_rms(xs, gain):
    lo = _lane_lo(xs.shape)
    sq = xs * xs
    s_lo = jnp.sum(jnp.where(lo, sq, 0.0), axis=-1, keepdims=True)
    s_hi = jnp.sum(jnp.where(lo, 0.0, sq), axis=-1, keepdims=True)
    inv = lax.rsqrt(jnp.where(lo, s_lo, s_hi) * (1.0 / HEAD_DIM) + EPS)
    return (xs * inv) * gain


def _half_sum_matrix():
    r = lax.broadcasted_iota(jnp.int32, (2 * LANES, LANES), 0)
    c = lax.broadcasted_iota(jnp.int32, (2 * LANES, LANES), 1)
    return jnp.where((r & HEAD_DIM) == (c & HEAD_DIM), 1.0, 0.0).astype(BF16)


def _seg_rms_mxu(xs, gain, ones2):
    sq = xs * xs
    head = sq.astype(BF16)
    rest = (sq - head.astype(F32)).astype(BF16)
    ssum = _dot(jnp.concatenate([head, rest], axis=-1), ones2)
    return (xs * lax.rsqrt(ssum * (1.0 / HEAD_DIM) + EPS)) * gain


def _rope(xs, cos, sin_signed):
    lane = lax.broadcasted_iota(jnp.int32, xs.shape, 1)
    first = (lane & (AXIS_DIM // 2)) == 0
    partner = jnp.where(first, pltpu.roll(xs, LANES - AXIS_DIM // 2, 1), pltpu.roll(xs, AXIS_DIM // 2, 1))
    return xs * cos + partner * sin_signed


def _dot_t(a, b):
    return lax.dot_general(a, b, (((1,), (1,)), ((), ())), preferred_element_type=F32)


def _dot(a, b):
    return jnp.dot(a, b, preferred_element_type=F32)


def _adaln_kernel(c_ref, w_ref, b_ref, o_ref):
    c = c_ref[...]
    s = c * _sigmoid(c)
    o_ref[0, 0] = _dot(s.astype(BF16), w_ref[0].astype(BF16)) + b_ref[0]


def _adaln(c_rows, w_mod, b_mod):
    depth = w_mod.shape[0]
    b3 = b_mod.reshape(depth * N_MOD, 1, D_MODEL)
    return pl.pallas_call(
        _adaln_kernel,
        out_shape=jax.ShapeDtypeStruct((depth, N_MOD, MOD_ROWS, D_MODEL), F32),
        grid=(depth, N_MOD),
        in_specs=[
            pl.BlockSpec((MOD_ROWS, D_MODEL), lambda l, j: (0, 0)),
            pl.BlockSpec((1, D_MODEL, D_MODEL), lambda l, j: (l, 0, j)),
            pl.BlockSpec((1, 1, D_MODEL), lambda l, j: (l * N_MOD + j, 0, 0)),
        ],
        out_specs=pl.BlockSpec((1, 1, MOD_ROWS, D_MODEL), lambda l, j: (l, j, 0, 0)),
        compiler_params=_params("parallel", "parallel"),
        name="adaln",
    )(c_rows, w_mod, b3)


def _ffn_tail(x, r, mod_ref, g_ref, w13_ref, w2_ref, k0):
    h = _mod_norm(x, g_ref[...], _mod_row(mod_ref, k0, r), _mod_row(mod_ref, k0 + 1, r))
    a = _dot(h.astype(BF16), w13_ref[0, 0])
    gate = a[:, :D_FF]
    up = a[:, D_FF:]
    act = (gate * _sigmoid(gate)) * up
    y = _dot(act.astype(BF16), w2_ref[0, 0])
    return x + (0.5 * _mod_row(mod_ref, k0 + 2, r)) * y


def _aligned(row, multiple):
    return row if isinstance(row, int) else pl.multiple_of(row, multiple)


def _for_chunks(block_rows, chunk, body):
    n = block_rows // chunk
    if n == 1:
        body(0, 0)
    else:
        def step(j, carry):
            body(j, pl.multiple_of(j * chunk, chunk))
            return carry
        lax.fori_loop(0, n, step, 0)


def _ffn_kernel(x_ref, mod_ref, g_ref, w13_ref, w2_ref, o_ref, *, row_fn, tm):
    n = x_ref.shape[0] // tm

    def body(j, row0):
        rows = pl.ds(row0, tm)
        r = row_fn(pl.program_id(0) * n + j)
        o_ref[rows, :] = _ffn_tail(x_ref[rows, :], r, mod_ref, g_ref, w13_ref, w2_ref, 0)

    _for_chunks(x_ref.shape[0], tm, body)


def _conv_gate(bg_ref, u_ref, up_ref, un_ref, cw_ref, i, seq, tm):
    u = u_ref[...]
    row = lax.broadcasted_iota(jnp.int32, (tm, 1), 0)
    pos = (row + i * tm) % seq
    u_dn = jnp.where(row == 0, up_ref[SUBLANES - 1:SUBLANES, :], pltpu.roll(u, 1, 0))
    u_dn = jnp.where(pos == 0, 0.0, u_dn)
    u_up = jnp.where(row == tm - 1, un_ref[0:1, :], pltpu.roll(u, tm - 1, 0))
    u_up = jnp.where(pos == seq - 1, 0.0, u_up)
    return bg_ref[...] * (cw_ref[0:1, :] * u_dn + cw_ref[1:2, :] * u + cw_ref[2:3, :] * u_up)


def _even_out_ffn_kernel(x_ref, a_ref, bg_ref, u_ref, up_ref, un_ref, cw_ref, wo_ref, mod_ref, g_ref, w13_ref, w2_ref,
                         o_ref, *, row_fn, seq, tm):
    i = pl.program_id(0)
    r = row_fn(i)
    y = _conv_gate(bg_ref, u_ref, up_ref, un_ref, cw_ref, i, seq, tm)
    mix = _dot(jnp.concatenate([a_ref[...], y.astype(BF16)], axis=-1), wo_ref[...])
    x = x_ref[...] + _mod_row(mod_ref, 5, r) * mix
    o_ref[...] = _ffn_tail(x, r, mod_ref, g_ref, w13_ref, w2_ref, 6)


def _odd_out_ffn_kernel(x_ref, a_ref, wo_ref, mod_ref, g_ref, w13_ref, w2_ref, o_ref, *, row_fn):
    r = row_fn(pl.program_id(0))
    x = x_ref[...] + _mod_row(mod_ref, 5, r) * _dot(a_ref[...], wo_ref[...])
    o_ref[...] = _ffn_tail(x, r, mod_ref, g_ref, w13_ref, w2_ref, 6)


def _resident(shape, index=None):
    index = (0,) * len(shape) if index is None else index
    return pl.BlockSpec(shape, lambda *_: index, pipeline_mode=pl.Buffered(1))


def _ffn(x, mod, g, w13, w2, *, layer, slot, row_fn, tm, mixer=None):
    t = x.shape[0]
    block = tm if mixer is not None else min(t, FFN_BLOCK_TILES * tm)
    tile = lambda n: pl.BlockSpec((block, n), lambda i: (i, 0))
    ffn_specs = [
        _resident((N_MOD, MOD_ROWS, D_MODEL)),
        _resident((1, D_MODEL)),
        _resident((1, 1, D_MODEL, 2 * D_FF), (layer, slot, 0, 0)),
        _resident((1, 1, D_FF, D_MODEL), (layer, slot, 0, 0)),
    ]
    ffn_args = [mod, g, w13, w2]
    if mixer is None:
        body = functools.partial(_ffn_kernel, row_fn=row_fn, tm=tm)
        in_specs, args, name = [tile(D_MODEL)], [x], "ffn"
    elif len(mixer) == 2:
        w_out, attn = mixer
        body = functools.partial(_odd_out_ffn_kernel, row_fn=row_fn)
        in_specs = [tile(D_MODEL), tile(attn.shape[1]), _resident(w_out.shape)]
        args, name = [x, attn, w_out], "odd_out_ffn"
    else:
        w_out, attn, bg, u, conv_w, seq = mixer
        nb = tm // SUBLANES
        last = t // SUBLANES - 1
        body = functools.partial(_even_out_ffn_kernel, row_fn=row_fn, seq=seq, tm=tm)
        in_specs = [
            tile(D_MODEL), tile(A_V), tile(CONV_DIM), tile(CONV_DIM),
            pl.BlockSpec((SUBLANES, CONV_DIM), lambda i: (jnp.maximum(i * nb - 1, 0), 0)),
            pl.BlockSpec((SUBLANES, CONV_DIM), lambda i: (jnp.minimum((i + 1) * nb, last), 0)),
            _resident(conv_w.shape), _resident(w_out.shape),
        ]
        args, name = [x, attn, bg, u, u, u, conv_w, w_out], "even_out_ffn"
    return pl.pallas_call(
        body,
        out_shape=jax.ShapeDtypeStruct((t, D_MODEL), F32),
        grid=(t // block,),
        in_specs=in_specs + ffn_specs,
        out_specs=tile(D_MODEL),
        compiler_params=_params("parallel"),
        name=name,
    )(*args, *ffn_args)


def _even_in_kernel(x_ref, mod_ref, g_ref, w_ref, qkg_ref, *rest, row_fn, rope, tm):
    if rope:
        cos_ref, sin_ref, q_ref, k_ref, v_ref, bg_ref, u_ref = rest
    else:
        q_ref, k_ref, v_ref, bg_ref, u_ref = rest
    n = x_ref.shape[0] // tm
    q_gain = qkg_ref[0:1, :] * Q_PRESCALE

    def body(j, row0):
        rows = pl.ds(row0, tm)
        r = row_fn(pl.program_id(0) * n + j)
        h = _mod_norm(x_ref[rows, :], g_ref[...], _mod_row(mod_ref, 3, r), _mod_row(mod_ref, 4, r))
        y = _dot(h.astype(BF16), w_ref[...])
        for hh in range(A_HEADS):
            sl = slice(hh * LANES, (hh + 1) * LANES)
            qs = _seg_rms(y[:, sl], q_gain)
            ks = _seg_rms(y[:, A_QK + hh * LANES:A_QK + (hh + 1) * LANES], qkg_ref[1:2, :])
            if rope:
                qs = _rope(qs, cos_ref[rows, :], sin_ref[rows, :])
                ks = _rope(ks, cos_ref[rows, :], sin_ref[rows, :])
            q_ref[rows, sl] = qs.astype(q_ref.dtype)
            k_ref[rows, sl] = ks.astype(k_ref.dtype)
        v_ref[rows, :] = y[:, 2 * A_QK:2 * A_QK + A_V].astype(v_ref.dtype)
        o = 2 * A_QK + A_V
        bg_ref[rows, :] = y[:, o:o + CONV_DIM]
        u_ref[rows, :] = y[:, o + CONV_DIM:o + 2 * CONV_DIM] * y[:, o + 2 * CONV_DIM:o + 3 * CONV_DIM]

    _for_chunks(x_ref.shape[0], tm, body)


def _even_in(x, mod, g, w_in, qkg, rope_tabs, *, row_fn, tm, kv_dtype):
    t = x.shape[0]
    rope = rope_tabs is not None
    block = min(t, IN_BLOCK_TILES * tm)
    in_specs = [
        pl.BlockSpec((block, D_MODEL), lambda i: (i, 0)),
        _resident((N_MOD, MOD_ROWS, D_MODEL)),
        _resident((1, D_MODEL)),
        _resident((D_MODEL, EVEN_IN)),
        _resident((2, LANES)),
    ]
    args = [x, mod, g, w_in, qkg]
    if rope:
        nt = rope_tabs[0].shape[0] // block
        in_specs += [pl.BlockSpec((block, LANES), lambda i: (i % nt, 0))] * 2
        args += list(rope_tabs)
    wide = lambda dt: jax.ShapeDtypeStruct((t, A_QK), dt)
    spec = pl.BlockSpec((block, A_QK), lambda i: (i, 0))
    return pl.pallas_call(
        functools.partial(_even_in_kernel, row_fn=row_fn, rope=rope, tm=tm),
        out_shape=(wide(BF16), wide(kv_dtype), wide(kv_dtype), wide(F32), wide(F32)),
        grid=(t // block,),
        in_specs=in_specs,
        out_specs=(spec,) * 5,
        compiler_params=_params("parallel"),
        name="even_in",
    )(*args)


def _lambda(lam_ref, lam_init):
    lf = lam_ref[...]
    a = jnp.sum(lf[0:1] * lf[1:2], axis=-1, keepdims=True)
    b = jnp.sum(lf[2:3] * lf[3:4], axis=-1, keepdims=True)
    return jnp.exp(a) - jnp.exp(b) + lam_init


def _subln(o, sub, lam_init):
    y = o * lax.rsqrt(jnp.mean(o * o, axis=-1, keepdims=True) + EPS)
    return (y * sub) * (1.0 - lam_init)


def _softmax_pv(q, k, v1):
    s = _dot_t(q, k)
    m = jnp.max(s, axis=-1, keepdims=True)
    r = _dot(jnp.exp2(s - m).astype(BF16), v1)
    return r[:, :LANES] * (1.0 / r[:, LANES:])


def _diff_attn_heads(q_ref, head_kv, lam_ref, sub_ref, o_ref, lam_init):
    lam = _lambda(lam_ref, lam_init)
    tq = q_ref.shape[0]
    chunk = min(tq, EVEN_Q_CHUNK)
    sub_rows = min(chunk, EVEN_Q_SUB)

    def body(j, row0):
        for rb in range(chunk // sub_rows):
            rows = pl.ds(_aligned(row0 + rb * sub_rows, sub_rows), sub_rows)
            for h in range(A_HEADS):
                sl = slice(h * LANES, (h + 1) * LANES)
                q = q_ref[rows, sl]
                k, v1 = head_kv(h)
                lo = _lane_lo(q.shape)
                zero = jnp.zeros_like(q)
                o1 = _softmax_pv(jnp.where(lo, q, zero), k, v1)
                o2 = _softmax_pv(jnp.where(lo, zero, q), k, v1)
                o_ref[rows, sl] = _subln(o1 - lam * o2, sub_ref[...], lam_init).astype(o_ref.dtype)

    _for_chunks(tq, chunk, body)


def _even_attn_p_kernel(q_ref, k_ref, v_ref, lam_ref, sub_ref, o_ref, *, lam_init):
    ones = jnp.ones((k_ref.shape[0], LANES), BF16)

    def head_kv(h):
        sl = slice(h * LANES, (h + 1) * LANES)
        return k_ref[:, sl].astype(BF16), jnp.concatenate([v_ref[:, sl].astype(BF16), ones], axis=-1)

    _diff_attn_heads(q_ref, head_kv, lam_ref, sub_ref, o_ref, lam_init)


def _even_attn_p(q, k, v, lam_vec, sub, *, seq, lam_init):
    t = q.shape[0]
    blk = pl.BlockSpec((seq, A_QK), lambda b: (b, 0))
    return pl.pallas_call(
        functools.partial(_even_attn_p_kernel, lam_init=lam_init),
        out_shape=jax.ShapeDtypeStruct((t, A_V), BF16),
        grid=(t // seq,),
        in_specs=[blk, blk, blk, _resident((4, HEAD_DIM)), _resident((1, A_VDIM))],
        out_specs=blk,
        compiler_params=_params("parallel"),
        name="even_attn_prompt",
    )(q, k, v, lam_vec, sub)


def _even_attn_s_kernel(q_ref, kl_ref, vl_ref, kc_ref, vc_ref, lam_ref, sub_ref, o_ref, k_scr, v1_scr, *, past, lam_init):
    @pl.when(pl.program_id(1) == 0)
    def _():
        k_scr[:past, :] = kc_ref[...].astype(BF16)
        k_scr[past:, :] = kl_ref[...]
        for h in range(A_HEADS):
            sl = slice(h * LANES, (h + 1) * LANES)
            v1_scr[h, :past, :LANES] = vc_ref[:, sl].astype(BF16)
            v1_scr[h, past:, :LANES] = vl_ref[:, sl]
            v1_scr[h, :, LANES:] = jnp.ones((v1_scr.shape[1], LANES), BF16)

    def head_kv(h):
        return k_scr[:, h * LANES:(h + 1) * LANES], v1_scr[h]

    _diff_attn_heads(q_ref, head_kv, lam_ref, sub_ref, o_ref, lam_init)


def _even_attn_s(q, kl, vl, kc, vc, lam_vec, sub, *, seq, past, tq, lam_init):
    t = q.shape[0]
    nq = seq // tq
    qblk = pl.BlockSpec((tq, A_QK), lambda b, i: (b * nq + i, 0))
    lat = pl.BlockSpec((seq, A_QK), lambda b, i: (b, 0))
    ctx = pl.BlockSpec((past, A_QK), lambda b, i: (b, 0))
    return pl.pallas_call(
        functools.partial(_even_attn_s_kernel, past=past, lam_init=lam_init),
        out_shape=jax.ShapeDtypeStruct((t, A_V), BF16),
        grid=(t // seq, nq),
        in_specs=[qblk, lat, lat, ctx, ctx, _resident((4, HEAD_DIM)), _resident((1, A_VDIM))],
        out_specs=qblk,
        scratch_shapes=[pltpu.VMEM((past + seq, A_QK), BF16),
                        pltpu.VMEM((A_HEADS, past + seq, 2 * LANES), BF16)],
        compiler_params=_params("parallel", "arbitrary"),
        name="even_attn_sample",
    )(q, kl, vl, kc, vc, lam_vec, sub)


def _dup_halves(x):
    lo = _lane_lo(x.shape)
    sw = pltpu.roll(x, HEAD_DIM, 1)
    return jnp.where(lo, x, sw), jnp.where(lo, sw, x)


def _ones_halves(x):
    lo = _lane_lo(x.shape)
    return jnp.where(lo, x, 1.0), jnp.where(lo, pltpu.roll(x, HEAD_DIM, 1), 1.0)


def _odd_in_kernel(x_ref, mod_ref, g_ref, w_ref, qkg_ref, *rest, row_fn, rope, keep_kv, tm):
    rest = list(rest)
    if rope:
        cos_ref, sin_ref = rest[:2]
        rest = rest[2:]
    if keep_kv:
        q_ref, kd_ref, vd_ref, k_ref, v_ref = rest
    else:
        q_ref, kd_ref, vd_ref = rest
    nq = C_HEADS * HEAD_DIM
    nk = C_KV_HEADS * HEAD_DIM
    q_gain = qkg_ref[0:1, :] * Q_PRESCALE
    n = x_ref.shape[0] // tm
    ones2 = _half_sum_matrix()

    def body(j, row0):
        r = row_fn(pl.program_id(0) * n + j)
        for half in range(2):
            rows = pl.ds(_aligned(row0 + half * (tm // 2), tm // 2), tm // 2)
            h = _mod_norm(x_ref[rows, :], g_ref[...], _mod_row(mod_ref, 3, r), _mod_row(mod_ref, 4, r))
            y = _dot(h.astype(BF16), w_ref[...])
            if rope:
                cos, sin = cos_ref[rows, :], sin_ref[rows, :]
            for c in range(nq // LANES):
                sl = slice(c * LANES, (c + 1) * LANES)
                qs = _seg_rms_mxu(y[:, sl], q_gain, ones2)
                if rope:
                    qs = _rope(qs, cos, sin)
                q_ref[rows, sl] = qs.astype(q_ref.dtype)
            for c in range(nk // LANES):
                sl = slice(c * LANES, (c + 1) * LANES)
                ks = _seg_rms_mxu(y[:, nq + c * LANES:nq + (c + 1) * LANES], qkg_ref[1:2, :], ones2)
                vs = y[:, nq + nk + c * LANES:nq + nk + (c + 1) * LANES]
                if keep_kv:
                    k_ref[rows, sl] = ks
                    v_ref[rows, sl] = vs
                if rope:
                    ks = _rope(ks, cos, sin)
                for (d0, d1), ref in ((_dup_halves(ks), kd_ref), (_ones_halves(vs), vd_ref)):
                    ref[rows, 2 * c * LANES:(2 * c + 1) * LANES] = d0.astype(ref.dtype)
                    ref[rows, (2 * c + 1) * LANES:(2 * c + 2) * LANES] = d1.astype(ref.dtype)

    _for_chunks(x_ref.shape[0], tm, body)


def _odd_in(x, mod, g, w_in, qkg, rope_tabs, *, row_fn, tm, keep_kv):
    t = x.shape[0]
    rope = rope_tabs is not None
    block = min(t, IN_BLOCK_TILES * tm)
    in_specs = [
        pl.BlockSpec((block, D_MODEL), lambda i: (i, 0)),
        _resident((N_MOD, MOD_ROWS, D_MODEL)),
        _resident((1, D_MODEL)),
        _resident((D_MODEL, ODD_IN)),
        _resident((2, LANES)),
    ]
    args = [x, mod, g, w_in, qkg]
    if rope:
        nt = rope_tabs[0].shape[0] // block
        in_specs += [pl.BlockSpec((block, LANES), lambda i: (i % nt, 0))] * 2
        args += list(rope_tabs)
    nq = C_HEADS * HEAD_DIM
    nd = C_KV_HEADS * LANES
    nk = C_KV_HEADS * HEAD_DIM
    out_shape = [jax.ShapeDtypeStruct((t, nq), BF16), jax.ShapeDtypeStruct((t, nd), BF16),
                 jax.ShapeDtypeStruct((t, nd), BF16)]
    out_specs = [pl.BlockSpec((block, nq), lambda i: (i, 0)), pl.BlockSpec((block, nd), lambda i: (i, 0)),
                 pl.BlockSpec((block, nd), lambda i: (i, 0))]
    if keep_kv:
        out_shape += [jax.ShapeDtypeStruct((t, nk), F32)] * 2
        out_specs += [pl.BlockSpec((block, nk), lambda i: (i, 0))] * 2
    return pl.pallas_call(
        functools.partial(_odd_in_kernel, row_fn=row_fn, rope=rope, keep_kv=keep_kv, tm=tm),
        out_shape=tuple(out_shape),
        grid=(t // block,),
        in_specs=in_specs,
        out_specs=tuple(out_specs),
        compiler_params=_params("parallel"),
        name="odd_in",
    )(*args)


def _stack_heads(q_ref, g, rows):
    parts = []
    for rr in range(GROUP):
        c = g * (GROUP // 2) + rr // 2
        qc = q_ref[rows, c * LANES:(c + 1) * LANES]
        lo = _lane_lo(qc.shape)
        keep = lo if rr % 2 == 0 else jnp.logical_not(lo)
        parts.append(jnp.where(keep, qc, jnp.zeros_like(qc)))
    return jnp.concatenate(parts, axis=0)


def _gqa_group(scores, biases, vals, sinks, rows):
    es = [[] for _ in scores]
    sink_terms = []
    for rr in range(GROUP):
        rs = slice(rr * rows, (rr + 1) * rows)
        parts = [s[rs] if b is None else s[rs] + b for s, b in zip(scores, biases)]
        m = functools.reduce(jnp.maximum, [jnp.max(p, axis=-1, keepdims=True) for p in parts])
        m = jnp.maximum(m, sinks[rr])
        for j, p in enumerate(parts):
            es[j].append(jnp.exp2(p - m).astype(BF16))
        sink_terms.append(jnp.exp2(sinks[rr] - m))
    r = functools.reduce(jnp.add, [_dot(jnp.concatenate(e, axis=0), v) for e, v in zip(es, vals)])
    heads = []
    for rr in range(GROUP):
        rh = r[rr * rows:(rr + 1) * rows]
        heads.append(rh * (1.0 / (pltpu.roll(rh, HEAD_DIM, 1) + sink_terms[rr])))
    lo = _lane_lo((rows, LANES))
    slabs = [jnp.where(lo, heads[2 * c], pltpu.roll(heads[2 * c + 1], HEAD_DIM, 1)) for c in range(GROUP // 2)]
    return jnp.concatenate(slabs, axis=-1)


def _odd_attn_p_kernel(sink_ref, q_ref, kd_ref, vd_ref, o_ref, *, seq):
    g = pl.program_id(1)
    q4 = _stack_heads(q_ref, 0, slice(None))
    sinks = [sink_ref[g * GROUP + rr] * LOG2E for rr in range(GROUP)]
    o_ref[...] = _gqa_group([_dot_t(q4, kd_ref[...])], [None], [vd_ref[...]], sinks, seq).astype(o_ref.dtype)


def _odd_attn_p(q, kd, vd, sink, *, seq):
    t = q.shape[0]
    gw = GROUP * HEAD_DIM
    return pl.pallas_call(
        functools.partial(_odd_attn_p_kernel, seq=seq),
        out_shape=jax.ShapeDtypeStruct((t, C_HEADS * HEAD_DIM), BF16),
        grid=(t // seq, C_KV_HEADS),
        in_specs=[
            pl.BlockSpec(memory_space=pltpu.SMEM),
            pl.BlockSpec((seq, gw), lambda b, g: (b, g)),
            pl.BlockSpec((seq, LANES), lambda b, g: (b, g)),
            pl.BlockSpec((seq, LANES), lambda b, g: (b, g)),
        ],
        out_specs=pl.BlockSpec((seq, gw), lambda b, g: (b, g)),
        compiler_params=_params("parallel", "parallel"),
        name="odd_attn_prompt",
    )(sink, q, kd, vd)


def _odd_attn_s_kernel(sink_ref, q_ref, kd_ref, vd_ref, kc_ref, vc_ref, bias_ref, o_ref, kcd_scr, vcd_scr, *, seq, tq):
    i = pl.program_id(1)
    sub = ODD_Q_SUB
    span = sub + 2 * WINDOW

    @pl.when(i == 0)
    def _():
        for c in range(C_KV_HEADS // 2):
            sl = slice(c * LANES, (c + 1) * LANES)
            for (d0, d1), dst in ((_dup_halves(kc_ref[:, sl]), kcd_scr), (_ones_halves(vc_ref[:, sl]), vcd_scr)):
                dst[:, 2 * c * LANES:(2 * c + 1) * LANES] = d0.astype(BF16)
                dst[:, (2 * c + 1) * LANES:(2 * c + 2) * LANES] = d1.astype(BF16)

    chunk = min(tq, ODD_Q_CHUNK)

    def body(j, row0):
        units = []
        for jb in range(chunk // sub):
            q0 = i * tq + row0 + jb * sub
            start = pl.multiple_of(jnp.clip(q0 - WINDOW, 0, seq - span), WINDOW)
            bias = bias_ref[(q0 - start) // WINDOW]
            rows = pl.ds(_aligned(row0 + jb * sub, sub), sub)
            units += [(rows, start, bias, g) for g in range(C_KV_HEADS)]

        def scores(unit):
            rows, start, _, g = unit
            gl = slice(g * LANES, (g + 1) * LANES)
            q4 = _stack_heads(q_ref, g, rows)
            return [_dot_t(q4, kcd_scr[:, gl]), _dot_t(q4, kd_ref[pl.ds(start, span), gl])]

        s_next = scores(units[0])
        for n, (rows, start, bias, g) in enumerate(units):
            s_cur = s_next
            if n + 1 < len(units):
                s_next = scores(units[n + 1])
            gl = slice(g * LANES, (g + 1) * LANES)
            sinks = [sink_ref[g * GROUP + rr] * LOG2E for rr in range(GROUP)]
            og = _gqa_group(s_cur, [None, bias], [vcd_scr[:, gl], vd_ref[pl.ds(start, span), gl]], sinks, sub)
            o_ref[rows, g * GROUP * HEAD_DIM:(g + 1) * GROUP * HEAD_DIM] = og.astype(o_ref.dtype)

    _for_chunks(tq, chunk, body)


def _odd_attn_s(q, kd, vd, kc, vc, sink, *, seq, past, tq):
    t = q.shape[0]
    nq = seq // tq
    nqw = C_HEADS * HEAD_DIM
    nd = C_KV_HEADS * LANES
    nk = C_KV_HEADS * HEAD_DIM
    span = ODD_Q_SUB + 2 * WINDOW
    rel = (jnp.arange(3)[:, None, None] * WINDOW + jnp.arange(ODD_Q_SUB)[None, :, None]
           - jnp.arange(span)[None, None, :])
    bias = jnp.where(jnp.abs(rel) <= WINDOW, 0.0, -jnp.inf).astype(F32)
    return pl.pallas_call(
        functools.partial(_odd_attn_s_kernel, seq=seq, tq=tq),
        out_shape=jax.ShapeDtypeStruct((t, nqw), BF16),
        grid=(t // seq, nq),
        in_specs=[
            pl.BlockSpec(memory_space=pltpu.SMEM),
            pl.BlockSpec((tq, nqw), lambda b, i: (b * nq + i, 0)),
            pl.BlockSpec((seq, nd), lambda b, i: (b, 0)),
            pl.BlockSpec((seq, nd), lambda b, i: (b, 0)),
            pl.BlockSpec((past, nk), lambda b, i: (b, 0)),
            pl.BlockSpec((past, nk), lambda b, i: (b, 0)),
            _resident(bias.shape),
        ],
        out_specs=pl.BlockSpec((tq, nqw), lambda b, i: (b * nq + i, 0)),
        scratch_shapes=[pltpu.VMEM((past, nd), BF16), pltpu.VMEM((past, nd), BF16)],
        compiler_params=_params("parallel", "arbitrary"),
        name="odd_attn_sample",
    )(sink, q, kd, vd, kc, vc, bias)


def _rope_tables(n):
    rows = n // GRID_W
    row = jnp.repeat(jnp.arange(rows, dtype=F32), GRID_W)
    col = jnp.tile(jnp.arange(GRID_W, dtype=F32), rows)
    inv = ROPE_BASE ** (-jnp.arange(0, AXIS_DIM, 2, dtype=F32) / AXIS_DIM)
    ang_r = row[:, None] * inv[None, :]
    ang_c = col[:, None] * inv[None, :]
    ang = jnp.concatenate([ang_r, ang_r, ang_c, ang_c], axis=-1)
    sign = jnp.where((jnp.arange(HEAD_DIM) & (AXIS_DIM // 2)) == 0, -1.0, 1.0).astype(F32)
    cos = jnp.cos(ang)
    sin = jnp.sin(ang) * sign[None, :]
    return jnp.tile(cos, (1, 2)), jnp.tile(sin, (1, 2))


def kernel(x_prompt, x_sample, cache_even_k, cache_even_v, cache_odd_k, cache_odd_v, c, c_ctx, w_mod, b_mod, norm_g, ffn_w13, ffn_w2, even_w_in, even_w_out, even_qk_norm, even_lambda, even_subln, even_conv_w, odd_w_in, odd_w_out, odd_qk_norm, odd_sink):
    batch, seq, _ = x_prompt.shape
    dec_batch, dec_seq, _ = x_sample.shape
    past = cache_even_k.shape[2]
    depth = w_mod.shape[0]
    assert 1 + dec_batch <= MOD_ROWS and seq % SUBLANES == 0 and dec_seq % TOKEN_TILE == 0

    xp = x_prompt.reshape(batch * seq, D_MODEL)
    xs = x_sample.reshape(dec_batch * dec_seq, D_MODEL)
    tm_p = min(TOKEN_TILE, batch * seq)
    row_p = _row_fn(0, batch * seq, tm_p)
    row_s = _row_fn(1, dec_seq, TOKEN_TILE)

    c_rows = jnp.zeros((MOD_ROWS, D_MODEL), F32).at[0].set(c_ctx).at[1:1 + dec_batch].set(c)
    mods = _adaln(c_rows, w_mod, b_mod)
    rope_tabs = _rope_tables(dec_seq)
    w13 = ffn_w13.astype(BF16)
    w2 = ffn_w2.astype(BF16)

    even_k, even_v, odd_k, odd_v = [], [], [], []
    for l in range(depth):
        mod = mods[l]
        g = [norm_g[l, s].reshape(1, D_MODEL) for s in range(3)]

        ffn_p = functools.partial(_ffn, mod=mod, w13=w13, w2=w2, layer=l, row_fn=row_p, tm=tm_p)
        ffn_s = functools.partial(_ffn, mod=mod, w13=w13, w2=w2, layer=l, row_fn=row_s, tm=TOKEN_TILE)
        xp = ffn_p(xp, g=g[0], slot=0)
        xs = ffn_s(xs, g=g[0], slot=0)

        if l % 2 == 0:
            e = l // 2
            lam_init = 0.8 - 0.6 * math.exp(-0.3 * l)
            w_in = even_w_in[e].astype(BF16)
            w_out = even_w_out[e].astype(BF16)
            qkg = jnp.tile(even_qk_norm[e], (1, 2))
            sub = even_subln[e].reshape(1, A_VDIM)

            q, k, v, bg, u = _even_in(xp, mod, g[1], w_in, qkg, None, row_fn=row_p, tm=tm_p, kv_dtype=F32)
            o = _even_attn_p(q, k, v, even_lambda[e], sub, seq=seq, lam_init=lam_init)
            xp = ffn_p(xp, g=g[2], slot=1, mixer=(w_out, o, bg, u, even_conv_w[e], seq))
            even_k.append(k.reshape(batch, seq, A_HEADS, 2 * HEAD_DIM))
            even_v.append(v.reshape(batch, seq, A_HEADS, A_VDIM))

            q, k, v, bg, u = _even_in(xs, mod, g[1], w_in, qkg, rope_tabs, row_fn=row_s, tm=TOKEN_TILE, kv_dtype=BF16)
            kc = cache_even_k[:, e].reshape(dec_batch * past, A_QK)
            vc = cache_even_v[:, e].reshape(dec_batch * past, A_V)
            o = _even_attn_s(q, k, v, kc, vc, even_lambda[e], sub, seq=dec_seq, past=past, tq=EVEN_Q_TILE,
                             lam_init=lam_init)
            xs = ffn_s(xs, g=g[2], slot=1, mixer=(w_out, o, bg, u, even_conv_w[e], dec_seq))
        else:
            e = l // 2
            w_in = odd_w_in[e].astype(BF16)
            w_out = odd_w_out[e].astype(BF16)
            qkg = jnp.tile(odd_qk_norm[e], (1, 2))
            sink = odd_sink[e]

            q, kd, vd, k, v = _odd_in(xp, mod, g[1], w_in, qkg, None, row_fn=row_p, tm=tm_p, keep_kv=True)
            o = _odd_attn_p(q, kd, vd, sink, seq=seq)
            xp = ffn_p(xp, g=g[2], slot=1, mixer=(w_out, o))
            odd_k.append(k.reshape(batch, seq, C_KV_HEADS, HEAD_DIM))
            odd_v.append(v.reshape(batch, seq, C_KV_HEADS, HEAD_DIM))

            q, kd, vd = _odd_in(xs, mod, g[1], w_in, qkg, rope_tabs, row_fn=row_s, tm=TOKEN_TILE, keep_kv=False)
            kc = cache_odd_k[:, e].reshape(dec_batch * past, C_KV_HEADS * HEAD_DIM)
            vc = cache_odd_v[:, e].reshape(dec_batch * past, C_KV_HEADS * HEAD_DIM)
            o = _odd_attn_s(q, kd, vd, kc, vc, sink, seq=dec_seq, past=past, tq=ODD_Q_TILE)
            xs = ffn_s(xs, g=g[2], slot=1, mixer=(w_out, o))

    return (xp.reshape(batch, seq, D_MODEL), xs.reshape(dec_batch, dec_seq, D_MODEL),
            jnp.stack(even_k, axis=1), jnp.stack(even_v, axis=1),
            jnp.stack(odd_k, axis=1), jnp.stack(odd_v, axis=1))
```

```python
import functools
import math

import jax
import jax.numpy as jnp
from jax import lax
from jax.experimental import pallas as pl
from jax.experimental.pallas import tpu as pltpu

F32 = jnp.float32
BF16 = jnp.bfloat16

D_MODEL = 1024
GRID_W = 64
HEAD_DIM = 64
AXIS_DIM = HEAD_DIM // 2
A_HEADS = 4
A_VDIM = 2 * HEAD_DIM
CONV_DIM = 512
C_HEADS = 16
C_KV_HEADS = 4
GROUP = C_HEADS // C_KV_HEADS
WINDOW = 128
D_FF = 2816
ROPE_BASE = 10000.0
N_MOD = 9
EPS = 1e-6
A_QK = A_HEADS * 2 * HEAD_DIM
A_V = A_HEADS * A_VDIM
EVEN_IN = 2 * A_QK + A_V + 3 * CONV_DIM
ODD_IN = (C_HEADS + 2 * C_KV_HEADS) * HEAD_DIM
LOG2E = math.log2(math.e)
Q_PRESCALE = HEAD_DIM ** -0.5 * LOG2E

LANES = 128
SUBLANES = 8
MOD_ROWS = 16
VMEM_LIMIT_BYTES = 56 * 2 ** 20

TOKEN_TILE = 512
FFN_BLOCK_TILES = 1
IN_BLOCK_TILES = 1
EVEN_Q_TILE = 512
EVEN_Q_CHUNK = 512
EVEN_Q_SUB = 256
ODD_Q_TILE = 512
ODD_Q_CHUNK = 512
ODD_Q_SUB = WINDOW


def _params(*semantics):
    return pltpu.CompilerParams(dimension_semantics=semantics, vmem_limit_bytes=VMEM_LIMIT_BYTES)


def _sigmoid(x):
    return 1.0 / (1.0 + jnp.exp(-x))


def _mod_norm(x, g, shift, scale):
    y = x * lax.rsqrt(jnp.mean(x * x, axis=-1, keepdims=True) + EPS)
    return (y * g) * (1.0 + scale) + shift


def _mod_row(mod_ref, k, r):
    return mod_ref[k, pl.ds(r, 1), :]


def _row_fn(row0, tokens_per_row, tile):
    return lambda i: row0 + (i * tile) // tokens_per_row


def _lane_lo(shape):
    return lax.broadcasted_iota(jnp.int32, shape, len(shape) - 1) < HEAD_DIM


def _seg_rms(xs, gain):
    lo = _lane_lo(xs.shape)
    sq = xs * xs
    s_lo = jnp.sum(jnp.where(lo, sq, 0.0), axis=-1, keepdims=True)
    s_hi = jnp.sum(jnp.where(lo, 0.0, sq), axis=-1, keepdims=True)
    inv = lax.rsqrt(jnp.where(lo, s_lo, s_hi) * (1.0 / HEAD_DIM) + EPS)
    return (xs * inv) * gain


def _half_sum_matrix():
    r = lax.broadcasted_iota(jnp.int32, (2 * LANES, LANES), 0)
    c = lax.broadcasted_iota(jnp.int32, (2 * LANES, LANES), 1)
    return jnp.where((r & HEAD_DIM) == (c & HEAD_DIM), 1.0, 0.0).astype(BF16)


def _seg_rms_mxu(xs, gain, ones2):
    sq = xs * xs
    head = sq.astype(BF16)
    rest = (sq - head.astype(F32)).astype(BF16)
    ssum = _dot(jnp.concatenate([head, rest], axis=-1), ones2)
    return (xs * lax.rsqrt(ssum * (1.0 / HEAD_DIM) + EPS)) * gain


def _rope(xs, cos, sin_signed):
    lane = lax.broadcasted_iota(jnp.int32, xs.shape, 1)
    first = (lane & (AXIS_DIM // 2)) == 0
    partner = jnp.where(first, pltpu.roll(xs, LANES - AXIS_DIM // 2, 1), pltpu.roll(xs, AXIS_DIM // 2, 1))
    return xs * cos + partner * sin_signed


def _dot_t(a, b):
    return lax.dot_general(a, b, (((1,), (1,)), ((), ())), preferred_element_type=F32)


def _dot(a, b):
    return jnp.dot(a, b, preferred_element_type=F32)


def _adaln_kernel(c_ref, w_ref, b_ref, o_ref):
    c = c_ref[...]
    s = c * _sigmoid(c)
    o_ref[0, 0] = _dot(s.astype(BF16), w_ref[0].astype(BF16)) + b_ref[0]


def _adaln(c_rows, w_mod, b_mod):
    depth = w_mod.shape[0]
    b3 = b_mod.reshape(depth * N_MOD, 1, D_MODEL)
    return pl.pallas_call(
        _adaln_kernel,
        out_shape=jax.ShapeDtypeStruct((depth, N_MOD, MOD_ROWS, D_MODEL), F32),
        grid=(depth, N_MOD),
        in_specs=[
            pl.BlockSpec((MOD_ROWS, D_MODEL), lambda l, j: (0, 0)),
            pl.BlockSpec((1, D_MODEL, D_MODEL), lambda l, j: (l, 0, j)),
            pl.BlockSpec((1, 1, D_MODEL), lambda l, j: (l * N_MOD + j, 0, 0)),
        ],
        out_specs=pl.BlockSpec((1, 1, MOD_ROWS, D_MODEL), lambda l, j: (l, j, 0, 0)),
        compiler_params=_params("parallel", "parallel"),
        name="adaln",
    )(c_rows, w_mod, b3)


def _ffn_tail(x, r, mod_ref, g_ref, w13_ref, w2_ref, k0):
    h = _mod_norm(x, g_ref[...], _mod_row(mod_ref, k0, r), _mod_row(mod_ref, k0 + 1, r))
    a = _dot(h.astype(BF16), w13_ref[0, 0])
    gate = a[:, :D_FF]
    up = a[:, D_FF:]
    act = (gate * _sigmoid(gate)) * up
    y = _dot(act.astype(BF16), w2_ref[0, 0])
    return x + (0.5 * _mod_row(mod_ref, k0 + 2, r)) * y


def _aligned(row, multiple):
    return row if isinstance(row, int) else pl.multiple_of(row, multiple)


def _for_chunks(block_rows, chunk, body):
    n = block_rows // chunk
    if n == 1:
        body(0, 0)
    else:
        def step(j, carry):
            body(j, pl.multiple_of(j * chunk, chunk))
            return carry
        lax.fori_loop(0, n, step, 0)


def _ffn_kernel(x_ref, mod_ref, g_ref, w13_ref, w2_ref, o_ref, *, row_fn, tm):
    n = x_ref.shape[0] // tm

    def body(j, row0):
        rows = pl.ds(row0, tm)
        r = row_fn(pl.program_id(0) * n + j)
        o_ref[rows, :] = _ffn_tail(x_ref[rows, :], r, mod_ref, g_ref, w13_ref, w2_ref, 0)

    _for_chunks(x_ref.shape[0], tm, body)


def _conv_gate(bg_ref, u_ref, up_ref, un_ref, cw_ref, i, seq, tm):
    u = u_ref[...]
    row = lax.broadcasted_iota(jnp.int32, (tm, 1), 0)
    pos = (row + i * tm) % seq
    u_dn = jnp.where(row == 0, up_ref[SUBLANES - 1:SUBLANES, :], pltpu.roll(u, 1, 0))
    u_dn = jnp.where(pos == 0, 0.0, u_dn)
    u_up = jnp.where(row == tm - 1, un_ref[0:1, :], pltpu.roll(u, tm - 1, 0))
    u_up = jnp.where(pos == seq - 1, 0.0, u_up)
    return bg_ref[...] * (cw_ref[0:1, :] * u_dn + cw_ref[1:2, :] * u + cw_ref[2:3, :] * u_up)


def _even_out_ffn_kernel(x_ref, a_ref, bg_ref, u_ref, up_ref, un_ref, cw_ref, wo_ref, mod_ref, g_ref, w13_ref, w2_ref,
                         o_ref, *, row_fn, seq, tm):
    i = pl.program_id(0)
    r = row_fn(i)
    y = _conv_gate(bg_ref, u_ref, up_ref, un_ref, cw_ref, i, seq, tm)
    mix = _dot(jnp.concatenate([a_ref[...], y.astype(BF16)], axis=-1), wo_ref[...])
    x = x_ref[...] + _mod_row(mod_ref, 5, r) * mix
    o_ref[...] = _ffn_tail(x, r, mod_ref, g_ref, w13_ref, w2_ref, 6)


def _odd_out_ffn_kernel(x_ref, a_ref, wo_ref, mod_ref, g_ref, w13_ref, w2_ref, o_ref, *, row_fn):
    r = row_fn(pl.program_id(0))
    x = x_ref[...] + _mod_row(mod_ref, 5, r) * _dot(a_ref[...], wo_ref[...])
    o_ref[...] = _ffn_tail(x, r, mod_ref, g_ref, w13_ref, w2_ref, 6)


def _resident(shape, index=None):
    index = (0,) * len(shape) if index is None else index
    return pl.BlockSpec(shape, lambda *_: index, pipeline_mode=pl.Buffered(1))


def _ffn(x, mod, g, w13, w2, *, layer, slot, row_fn, tm, mixer=None):
    t = x.shape[0]
    block = tm if mixer is not None else min(t, FFN_BLOCK_TILES * tm)
    tile = lambda n: pl.BlockSpec((block, n), lambda i: (i, 0))
    ffn_specs = [
        _resident((N_MOD, MOD_ROWS, D_MODEL)),
        _resident((1, D_MODEL)),
        _resident((1, 1, D_MODEL, 2 * D_FF), (layer, slot, 0, 0)),
        _resident((1, 1, D_FF, D_MODEL), (layer, slot, 0, 0)),
    ]
    ffn_args = [mod, g, w13, w2]
    if mixer is None:
        body = functools.partial(_ffn_kernel, row_fn=row_fn, tm=tm)
        in_specs, args, name = [tile(D_MODEL)], [x], "ffn"
    elif len(mixer) == 2:
        w_out, attn = mixer
        body = functools.partial(_odd_out_ffn_kernel, row_fn=row_fn)
        in_specs = [tile(D_MODEL), tile(attn.shape[1]), _resident(w_out.shape)]
        args, name = [x, attn, w_out], "odd_out_ffn"
    else:
        w_out, attn, bg, u, conv_w, seq = mixer
        nb = tm // SUBLANES
        last = t // SUBLANES - 1
        body = functools.partial(_even_out_ffn_kernel, row_fn=row_fn, seq=seq, tm=tm)
        in_specs = [
            tile(D_MODEL), tile(A_V), tile(CONV_DIM), tile(CONV_DIM),
            pl.BlockSpec((SUBLANES, CONV_DIM), lambda i: (jnp.maximum(i * nb - 1, 0), 0)),
            pl.BlockSpec((SUBLANES, CONV_DIM), lambda i: (jnp.minimum((i + 1) * nb, last), 0)),
            _resident(conv_w.shape), _resident(w_out.shape),
        ]
        args, name = [x, attn, bg, u, u, u, conv_w, w_out], "even_out_ffn"
    return pl.pallas_call(
        body,
        out_shape=jax.ShapeDtypeStruct((t, D_MODEL), F32),
        grid=(t // block,),
        in_specs=in_specs + ffn_specs,
        out_specs=tile(D_MODEL),
        compiler_params=_params("parallel"),
        name=name,
    )(*args, *ffn_args)


def _even_in_kernel(x_ref, mod_ref, g_ref, w_ref, qkg_ref, *rest, row_fn, rope, tm):
    if rope:
        cos_ref, sin_ref, q_ref, k_ref, v_ref, bg_ref, u_ref = rest
    else:
        q_ref, k_ref, v_ref, bg_ref, u_ref = rest
    n = x_ref.shape[0] // tm
    q_gain = qkg_ref[0:1, :] * Q_PRESCALE

    def body(j, row0):
        rows = pl.ds(row0, tm)
        r = row_fn(pl.program_id(0) * n + j)
        h = _mod_norm(x_ref[rows, :], g_ref[...], _mod_row(mod_ref, 3, r), _mod_row(mod_ref, 4, r))
        y = _dot(h.astype(BF16), w_ref[...])
        for hh in range(A_HEADS):
            sl = slice(hh * LANES, (hh + 1) * LANES)
            qs = _seg_rms(y[:, sl], q_gain)
            ks = _seg_rms(y[:, A_QK + hh * LANES:A_QK + (hh + 1) * LANES], qkg_ref[1:2, :])
            if rope:
                qs = _rope(qs, cos_ref[rows, :], sin_ref[rows, :])
                ks = _rope(ks, cos_ref[rows, :], sin_ref[rows, :])
            q_ref[rows, sl] = qs.astype(q_ref.dtype)
            k_ref[rows, sl] = ks.astype(k_ref.dtype)
        v_ref[rows, :] = y[:, 2 * A_QK:2 * A_QK + A_V].astype(v_ref.dtype)
        o = 2 * A_QK + A_V
        bg_ref[rows, :] = y[:, o:o + CONV_DIM]
        u_ref[rows, :] = y[:, o + CONV_DIM:o + 2 * CONV_DIM] * y[:, o + 2 * CONV_DIM:o + 3 * CONV_DIM]

    _for_chunks(x_ref.shape[0], tm, body)


def _even_in(x, mod, g, w_in, qkg, rope_tabs, *, row_fn, tm, kv_dtype):
    t = x.shape[0]
    rope = rope_tabs is not None
    block = min(t, IN_BLOCK_TILES * tm)
    in_specs = [
        pl.BlockSpec((block, D_MODEL), lambda i: (i, 0)),
        _resident((N_MOD, MOD_ROWS, D_MODEL)),
        _resident((1, D_MODEL)),
        _resident((D_MODEL, EVEN_IN)),
        _resident((2, LANES)),
    ]
    args = [x, mod, g, w_in, qkg]
    if rope:
        nt = rope_tabs[0].shape[0] // block
        in_specs += [pl.BlockSpec((block, LANES), lambda i: (i % nt, 0))] * 2
        args += list(rope_tabs)
    wide = lambda dt: jax.ShapeDtypeStruct((t, A_QK), dt)
    spec = pl.BlockSpec((block, A_QK), lambda i: (i, 0))
    return pl.pallas_call(
        functools.partial(_even_in_kernel, row_fn=row_fn, rope=rope, tm=tm),
        out_shape=(wide(BF16), wide(kv_dtype), wide(kv_dtype), wide(F32), wide(F32)),
        grid=(t // block,),
        in_specs=in_specs,
        out_specs=(spec,) * 5,
        compiler_params=_params("parallel"),
        name="even_in",
    )(*args)


def _lambda(lam_ref, lam_init):
    lf = lam_ref[...]
    a = jnp.sum(lf[0:1] * lf[1:2], axis=-1, keepdims=True)
    b = jnp.sum(lf[2:3] * lf[3:4], axis=-1, keepdims=True)
    return jnp.exp(a) - jnp.exp(b) + lam_init


def _subln(o, sub, lam_init):
    y = o * lax.rsqrt(jnp.mean(o * o, axis=-1, keepdims=True) + EPS)
    return (y * sub) * (1.0 - lam_init)


def _softmax_rows_pv(s, v1):
    m = jnp.max(s, axis=-1, keepdims=True)
    r = _dot(jnp.exp2(s - m).astype(BF16), v1)
    return r[:, :LANES] * (1.0 / r[:, LANES:])


def _diff_attn_heads(q_ref, head_kv, lam_ref, sub_ref, o_ref, lam_init):
    lam = _lambda(lam_ref, lam_init)
    tq = q_ref.shape[0]
    chunk = min(tq, EVEN_Q_CHUNK)
    sub_rows = min(chunk, EVEN_Q_SUB)

    def body(j, row0):
        units = [(pl.ds(_aligned(row0 + rb * sub_rows, sub_rows), sub_rows), h)
                 for rb in range(chunk // sub_rows) for h in range(A_HEADS)]

        def scores(unit):
            rows, h = unit
            q = q_ref[rows, h * LANES:(h + 1) * LANES]
            k, _ = head_kv(h)
            lo = _lane_lo(q.shape)
            zero = jnp.zeros_like(q)
            return _dot_t(jnp.where(lo, q, zero), k), _dot_t(jnp.where(lo, zero, q), k)

        s_next = scores(units[0])
        for n, (rows, h) in enumerate(units):
            s1, s2 = s_next
            if n + 1 < len(units):
                s_next = scores(units[n + 1])
            _, v1 = head_kv(h)
            o = _softmax_rows_pv(s1, v1) - lam * _softmax_rows_pv(s2, v1)
            o_ref[rows, h * LANES:(h + 1) * LANES] = _subln(o, sub_ref[...], lam_init).astype(o_ref.dtype)

    _for_chunks(tq, chunk, body)


def _even_attn_p_kernel(q_ref, k_ref, v_ref, lam_ref, sub_ref, o_ref, *, lam_init):
    ones = jnp.ones((k_ref.shape[0], LANES), BF16)

    def head_kv(h):
        sl = slice(h * LANES, (h + 1) * LANES)
        return k_ref[:, sl].astype(BF16), jnp.concatenate([v_ref[:, sl].astype(BF16), ones], axis=-1)

    _diff_attn_heads(q_ref, head_kv, lam_ref, sub_ref, o_ref, lam_init)


def _even_attn_p(q, k, v, lam_vec, sub, *, seq, lam_init):
    t = q.shape[0]
    blk = pl.BlockSpec((seq, A_QK), lambda b: (b, 0))
    return pl.pallas_call(
        functools.partial(_even_attn_p_kernel, lam_init=lam_init),
        out_shape=jax.ShapeDtypeStruct((t, A_V), BF16),
        grid=(t // seq,),
        in_specs=[blk, blk, blk, _resident((4, HEAD_DIM)), _resident((1, A_VDIM))],
        out_specs=blk,
        compiler_params=_params("parallel"),
        name="even_attn_prompt",
    )(q, k, v, lam_vec, sub)


def _even_attn_s_kernel(q_ref, kl_ref, vl_ref, kc_ref, vc_ref, lam_ref, sub_ref, o_ref, k_scr, v1_scr, *, past, lam_init):
    @pl.when(pl.program_id(1) == 0)
    def _():
        k_scr[:past, :] = kc_ref[...].astype(BF16)
        k_scr[past:, :] = kl_ref[...]
        for h in range(A_HEADS):
            sl = slice(h * LANES, (h + 1) * LANES)
            v1_scr[h, :past, :LANES] = vc_ref[:, sl].astype(BF16)
            v1_scr[h, past:, :LANES] = vl_ref[:, sl]
            v1_scr[h, :, LANES:] = jnp.ones((v1_scr.shape[1], LANES), BF16)

    def head_kv(h):
        return k_scr[:, h * LANES:(h + 1) * LANES], v1_scr[h]

    _diff_attn_heads(q_ref, head_kv, lam_ref, sub_ref, o_ref, lam_init)


def _even_attn_s(q, kl, vl, kc, vc, lam_vec, sub, *, seq, past, tq, lam_init):
    t = q.shape[0]
    nq = seq // tq
    qblk = pl.BlockSpec((tq, A_QK), lambda b, i: (b * nq + i, 0))
    lat = pl.BlockSpec((seq, A_QK), lambda b, i: (b, 0))
    ctx = pl.BlockSpec((past, A_QK), lambda b, i: (b, 0))
    return pl.pallas_call(
        functools.partial(_even_attn_s_kernel, past=past, lam_init=lam_init),
        out_shape=jax.ShapeDtypeStruct((t, A_V), BF16),
        grid=(t // seq, nq),
        in_specs=[qblk, lat, lat, ctx, ctx, _resident((4, HEAD_DIM)), _resident((1, A_VDIM))],
        out_specs=qblk,
        scratch_shapes=[pltpu.VMEM((past + seq, A_QK), BF16),
                        pltpu.VMEM((A_HEADS, past + seq, 2 * LANES), BF16)],
        compiler_params=_params("parallel", "arbitrary"),
        name="even_attn_sample",
    )(q, kl, vl, kc, vc, lam_vec, sub)


def _dup_halves(x):
    lo = _lane_lo(x.shape)
    sw = pltpu.roll(x, HEAD_DIM, 1)
    return jnp.where(lo, x, sw), jnp.where(lo, sw, x)


def _ones_halves(x):
    lo = _lane_lo(x.shape)
    return jnp.where(lo, x, 1.0), jnp.where(lo, pltpu.roll(x, HEAD_DIM, 1), 1.0)


def _odd_in_kernel(x_ref, mod_ref, g_ref, w_ref, qkg_ref, *rest, row_fn, rope, keep_kv, tm):
    rest = list(rest)
    if rope:
        cos_ref, sin_ref = rest[:2]
        rest = rest[2:]
    if keep_kv:
        q_ref, kd_ref, vd_ref, k_ref, v_ref = rest
    else:
        q_ref, kd_ref, vd_ref = rest
    nq = C_HEADS * HEAD_DIM
    nk = C_KV_HEADS * HEAD_DIM
    q_gain = qkg_ref[0:1, :] * Q_PRESCALE
    n = x_ref.shape[0] // tm
    ones2 = _half_sum_matrix()

    def body(j, row0):
        r = row_fn(pl.program_id(0) * n + j)
        for half in range(2):
            rows = pl.ds(_aligned(row0 + half * (tm // 2), tm // 2), tm // 2)
            h = _mod_norm(x_ref[rows, :], g_ref[...], _mod_row(mod_ref, 3, r), _mod_row(mod_ref, 4, r))
            y = _dot(h.astype(BF16), w_ref[...])
            if rope:
                cos, sin = cos_ref[rows, :], sin_ref[rows, :]
            for c in range(nq // LANES):
                sl = slice(c * LANES, (c + 1) * LANES)
                qs = _seg_rms_mxu(y[:, sl], q_gain, ones2)
                if rope:
                    qs = _rope(qs, cos, sin)
                q_ref[rows, sl] = qs.astype(q_ref.dtype)
            for c in range(nk // LANES):
                sl = slice(c * LANES, (c + 1) * LANES)
                ks = _seg_rms_mxu(y[:, nq + c * LANES:nq + (c + 1) * LANES], qkg_ref[1:2, :], ones2)
                vs = y[:, nq + nk + c * LANES:nq + nk + (c + 1) * LANES]
                if keep_kv:
                    k_ref[rows, sl] = ks
                    v_ref[rows, sl] = vs
                if rope:
                    ks = _rope(ks, cos, sin)
                for (d0, d1), ref in ((_dup_halves(ks), kd_ref), (_ones_halves(vs), vd_ref)):
                    ref[rows, 2 * c * LANES:(2 * c + 1) * LANES] = d0.astype(ref.dtype)
                    ref[rows, (2 * c + 1) * LANES:(2 * c + 2) * LANES] = d1.astype(ref.dtype)

    _for_chunks(x_ref.shape[0], tm, body)


def _odd_in(x, mod, g, w_in, qkg, rope_tabs, *, row_fn, tm, keep_kv):
    t = x.shape[0]
    rope = rope_tabs is not None
    block = min(t, IN_BLOCK_TILES * tm)
    in_specs = [
        pl.BlockSpec((block, D_MODEL), lambda i: (i, 0)),
        _resident((N_MOD, MOD_ROWS, D_MODEL)),
        _resident((1, D_MODEL)),
        _resident((D_MODEL, ODD_IN)),
        _resident((2, LANES)),
    ]
    args = [x, mod, g, w_in, qkg]
    if rope:
        nt = rope_tabs[0].shape[0] // block
        in_specs += [pl.BlockSpec((block, LANES), lambda i: (i % nt, 0))] * 2
        args += list(rope_tabs)
    nq = C_HEADS * HEAD_DIM
    nd = C_KV_HEADS * LANES
    nk = C_KV_HEADS * HEAD_DIM
    out_shape = [jax.ShapeDtypeStruct((t, nq), BF16), jax.ShapeDtypeStruct((t, nd), BF16),
                 jax.ShapeDtypeStruct((t, nd), BF16)]
    out_specs = [pl.BlockSpec((block, nq), lambda i: (i, 0)), pl.BlockSpec((block, nd), lambda i: (i, 0)),
                 pl.BlockSpec((block, nd), lambda i: (i, 0))]
    if keep_kv:
        out_shape += [jax.ShapeDtypeStruct((t, nk), F32)] * 2
        out_specs += [pl.BlockSpec((block, nk), lambda i: (i, 0))] * 2
    return pl.pallas_call(
        functools.partial(_odd_in_kernel, row_fn=row_fn, rope=rope, keep_kv=keep_kv, tm=tm),
        out_shape=tuple(out_shape),
        grid=(t // block,),
        in_specs=in_specs,
        out_specs=tuple(out_specs),
        compiler_params=_params("parallel"),
        name="odd_in",
    )(*args)


def _stack_heads(q_ref, g, rows):
    parts = []
    for rr in range(GROUP):
        c = g * (GROUP // 2) + rr // 2
        qc = q_ref[rows, c * LANES:(c + 1) * LANES]
        lo = _lane_lo(qc.shape)
        keep = lo if rr % 2 == 0 else jnp.logical_not(lo)
        parts.append(jnp.where(keep, qc, jnp.zeros_like(qc)))
    return jnp.concatenate(parts, axis=0)


def _gqa_group(scores, biases, vals, sinks, rows):
    es = [[] for _ in scores]
    sink_terms = []
    for rr in range(GROUP):
        rs = slice(rr * rows, (rr + 1) * rows)
        parts = [s[rs] if b is None else s[rs] + b for s, b in zip(scores, biases)]
        m = functools.reduce(jnp.maximum, [jnp.max(p, axis=-1, keepdims=True) for p in parts])
        m = jnp.maximum(m, sinks[rr])
        for j, p in enumerate(parts):
            es[j].append(jnp.exp2(p - m).astype(BF16))
        sink_terms.append(jnp.exp2(sinks[rr] - m))
    r = functools.reduce(jnp.add, [_dot(jnp.concatenate(e, axis=0), v) for e, v in zip(es, vals)])
    heads = []
    for rr in range(GROUP):
        rh = r[rr * rows:(rr + 1) * rows]
        heads.append(rh * (1.0 / (pltpu.roll(rh, HEAD_DIM, 1) + sink_terms[rr])))
    lo = _lane_lo((rows, LANES))
    slabs = [jnp.where(lo, heads[2 * c], pltpu.roll(heads[2 * c + 1], HEAD_DIM, 1)) for c in range(GROUP // 2)]
    return jnp.concatenate(slabs, axis=-1)


def _odd_attn_p_kernel(sink_ref, q_ref, kd_ref, vd_ref, o_ref, *, seq):
    g = pl.program_id(1)
    q4 = _stack_heads(q_ref, 0, slice(None))
    sinks = [sink_ref[g * GROUP + rr] * LOG2E for rr in range(GROUP)]
    o_ref[...] = _gqa_group([_dot_t(q4, kd_ref[...])], [None], [vd_ref[...]], sinks, seq).astype(o_ref.dtype)


def _odd_attn_p(q, kd, vd, sink, *, seq):
    t = q.shape[0]
    gw = GROUP * HEAD_DIM
    return pl.pallas_call(
        functools.partial(_odd_attn_p_kernel, seq=seq),
        out_shape=jax.ShapeDtypeStruct((t, C_HEADS * HEAD_DIM), BF16),
        grid=(t // seq, C_KV_HEADS),
        in_specs=[
            pl.BlockSpec(memory_space=pltpu.SMEM),
            pl.BlockSpec((seq, gw), lambda b, g: (b, g)),
            pl.BlockSpec((seq, LANES), lambda b, g: (b, g)),
            pl.BlockSpec((seq, LANES), lambda b, g: (b, g)),
        ],
        out_specs=pl.BlockSpec((seq, gw), lambda b, g: (b, g)),
        compiler_params=_params("parallel", "parallel"),
        name="odd_attn_prompt",
    )(sink, q, kd, vd)


def _odd_attn_s_kernel(sink_ref, q_ref, kd_ref, vd_ref, kc_ref, vc_ref, bias_ref, o_ref, kcd_scr, vcd_scr, *, seq, tq):
    i = pl.program_id(1)
    sub = ODD_Q_SUB
    span = sub + 2 * WINDOW

    @pl.when(i == 0)
    def _():
        for c in range(C_KV_HEADS // 2):
            sl = slice(c * LANES, (c + 1) * LANES)
            for (d0, d1), dst in ((_dup_halves(kc_ref[:, sl]), kcd_scr), (_ones_halves(vc_ref[:, sl]), vcd_scr)):
                dst[:, 2 * c * LANES:(2 * c + 1) * LANES] = d0.astype(BF16)
                dst[:, (2 * c + 1) * LANES:(2 * c + 2) * LANES] = d1.astype(BF16)

    chunk = min(tq, ODD_Q_CHUNK)

    def body(j, row0):
        units = []
        for jb in range(chunk // sub):
            q0 = i * tq + row0 + jb * sub
            start = pl.multiple_of(jnp.clip(q0 - WINDOW, 0, seq - span), WINDOW)
            bias = bias_ref[(q0 - start) // WINDOW]
            rows = pl.ds(_aligned(row0 + jb * sub, sub), sub)
            units += [(rows, start, bias, g) for g in range(C_KV_HEADS)]

        def scores(unit):
            rows, start, _, g = unit
            gl = slice(g * LANES, (g + 1) * LANES)
            q4 = _stack_heads(q_ref, g, rows)
            return [_dot_t(q4, kcd_scr[:, gl]), _dot_t(q4, kd_ref[pl.ds(start, span), gl])]

        s_next = scores(units[0])
        for n, (rows, start, bias, g) in enumerate(units):
            s_cur = s_next
            if n + 1 < len(units):
                s_next = scores(units[n + 1])
            gl = slice(g * LANES, (g + 1) * LANES)
            sinks = [sink_ref[g * GROUP + rr] * LOG2E for rr in range(GROUP)]
            og = _gqa_group(s_cur, [None, bias], [vcd_scr[:, gl], vd_ref[pl.ds(start, span), gl]], sinks, sub)
            o_ref[rows, g * GROUP * HEAD_DIM:(g + 1) * GROUP * HEAD_DIM] = og.astype(o_ref.dtype)

    _for_chunks(tq, chunk, body)


def _odd_attn_s(q, kd, vd, kc, vc, sink, *, seq, past, tq):
    t = q.shape[0]
    nq = seq // tq
    nqw = C_HEADS * HEAD_DIM
    nd = C_KV_HEADS * LANES
    nk = C_KV_HEADS * HEAD_DIM
    span = ODD_Q_SUB + 2 * WINDOW
    rel = (jnp.arange(3)[:, None, None] * WINDOW + jnp.arange(ODD_Q_SUB)[None, :, None]
           - jnp.arange(span)[None, None, :])
    bias = jnp.where(jnp.abs(rel) <= WINDOW, 0.0, -jnp.inf).astype(F32)
    return pl.pallas_call(
        functools.partial(_odd_attn_s_kernel, seq=seq, tq=tq),
        out_shape=jax.ShapeDtypeStruct((t, nqw), BF16),
        grid=(t // seq, nq),
        in_specs=[
            pl.BlockSpec(memory_space=pltpu.SMEM),
            pl.BlockSpec((tq, nqw), lambda b, i: (b * nq + i, 0)),
            pl.BlockSpec((seq, nd), lambda b, i: (b, 0)),
            pl.BlockSpec((seq, nd), lambda b, i: (b, 0)),
            pl.BlockSpec((past, nk), lambda b, i: (b, 0)),
            pl.BlockSpec((past, nk), lambda b, i: (b, 0)),
            _resident(bias.shape),
        ],
        out_specs=pl.BlockSpec((tq, nqw), lambda b, i: (b * nq + i, 0)),
        scratch_shapes=[pltpu.VMEM((past, nd), BF16), pltpu.VMEM((past, nd), BF16)],
        compiler_params=_params("parallel", "arbitrary"),
        name="odd_attn_sample",
    )(sink, q, kd, vd, kc, vc, bias)


def _rope_tables(n):
    rows = n // GRID_W
    row = jnp.repeat(jnp.arange(rows, dtype=F32), GRID_W)
    col = jnp.tile(jnp.arange(GRID_W, dtype=F32), rows)
    inv = ROPE_BASE ** (-jnp.arange(0, AXIS_DIM, 2, dtype=F32) / AXIS_DIM)
    ang_r = row[:, None] * inv[None, :]
    ang_c = col[:, None] * inv[None, :]
    ang = jnp.concatenate([ang_r, ang_r, ang_c, ang_c], axis=-1)
    sign = jnp.where((jnp.arange(HEAD_DIM) & (AXIS_DIM // 2)) == 0, -1.0, 1.0).astype(F32)
    cos = jnp.cos(ang)
    sin = jnp.sin(ang) * sign[None, :]
    return jnp.tile(cos, (1, 2)), jnp.tile(sin, (1, 2))


def kernel(x_prompt, x_sample, cache_even_k, cache_even_v, cache_odd_k, cache_odd_v, c, c_ctx, w_mod, b_mod, norm_g, ffn_w13, ffn_w2, even_w_in, even_w_out, even_qk_norm, even_lambda, even_subln, even_conv_w, odd_w_in, odd_w_out, odd_qk_norm, odd_sink):
    batch, seq, _ = x_prompt.shape
    dec_batch, dec_seq, _ = x_sample.shape
    past = cache_even_k.shape[2]
    depth = w_mod.shape[0]
    assert 1 + dec_batch <= MOD_ROWS and seq % SUBLANES == 0 and dec_seq % TOKEN_TILE == 0

    xp = x_prompt.reshape(batch * seq, D_MODEL)
    xs = x_sample.reshape(dec_batch * dec_seq, D_MODEL)
    tm_p = min(TOKEN_TILE, batch * seq)
    row_p = _row_fn(0, batch * seq, tm_p)
    row_s = _row_fn(1, dec_seq, TOKEN_TILE)

    c_rows = jnp.zeros((MOD_ROWS, D_MODEL), F32).at[0].set(c_ctx).at[1:1 + dec_batch].set(c)
    mods = _adaln(c_rows, w_mod, b_mod)
    rope_tabs = _rope_tables(dec_seq)
    w13 = ffn_w13.astype(BF16)
    w2 = ffn_w2.astype(BF16)

    even_k, even_v, odd_k, odd_v = [], [], [], []
    for l in range(depth):
        mod = mods[l]
        g = [norm_g[l, s].reshape(1, D_MODEL) for s in range(3)]

        ffn_p = functools.partial(_ffn, mod=mod, w13=w13, w2=w2, layer=l, row_fn=row_p, tm=tm_p)
        ffn_s = functools.partial(_ffn, mod=mod, w13=w13, w2=w2, layer=l, row_fn=row_s, tm=TOKEN_TILE)
        xp = ffn_p(xp, g=g[0], slot=0)
        xs = ffn_s(xs, g=g[0], slot=0)

        if l % 2 == 0:
            e = l // 2
            lam_init = 0.8 - 0.6 * math.exp(-0.3 * l)
            w_in = even_w_in[e].astype(BF16)
            w_out = even_w_out[e].astype(BF16)
            qkg = jnp.tile(even_qk_norm[e], (1, 2))
            sub = even_subln[e].reshape(1, A_VDIM)

            q, k, v, bg, u = _even_in(xp, mod, g[1], w_in, qkg, None, row_fn=row_p, tm=tm_p, kv_dtype=F32)
            o = _even_attn_p(q, k, v, even_lambda[e], sub, seq=seq, lam_init=lam_init)
            xp = ffn_p(xp, g=g[2], slot=1, mixer=(w_out, o, bg, u, even_conv_w[e], seq))
            even_k.append(k.reshape(batch, seq, A_HEADS, 2 * HEAD_DIM))
            even_v.append(v.reshape(batch, seq, A_HEADS, A_VDIM))

            q, k, v, bg, u = _even_in(xs, mod, g[1], w_in, qkg, rope_tabs, row_fn=row_s, tm=TOKEN_TILE, kv_dtype=BF16)
            kc = cache_even_k[:, e].reshape(dec_batch * past, A_QK)
            vc = cache_even_v[:, e].reshape(dec_batch * past, A_V)
            o = _even_attn_s(q, k, v, kc, vc, even_lambda[e], sub, seq=dec_seq, past=past, tq=EVEN_Q_TILE,
                             lam_init=lam_init)
            xs = ffn_s(xs, g=g[2], slot=1, mixer=(w_out, o, bg, u, even_conv_w[e], dec_seq))
        else:
            e = l // 2
            w_in = odd_w_in[e].astype(BF16)
            w_out = odd_w_out[e].astype(BF16)
            qkg = jnp.tile(odd_qk_norm[e], (1, 2))
            sink = odd_sink[e]

            q, kd, vd, k, v = _odd_in(xp, mod, g[1], w_in, qkg, None, row_fn=row_p, tm=tm_p, keep_kv=True)
            o = _odd_attn_p(q, kd, vd, sink, seq=seq)
            xp = ffn_p(xp, g=g[2], slot=1, mixer=(w_out, o))
            odd_k.append(k.reshape(batch, seq, C_KV_HEADS, HEAD_DIM))
            odd_v.append(v.reshape(batch, seq, C_KV_HEADS, HEAD_DIM))

            q, kd, vd = _odd_in(xs, mod, g[1], w_in, qkg, rope_tabs, row_fn=row_s, tm=TOKEN_TILE, keep_kv=False)
            kc = cache_odd_k[:, e].reshape(dec_batch * past, C_KV_HEADS * HEAD_DIM)
            vc = cache_odd_v[:, e].reshape(dec_batch * past, C_KV_HEADS * HEAD_DIM)
            o = _odd_attn_s(q, kd, vd, kc, vc, sink, seq=dec_seq, past=past, tq=ODD_Q_TILE)
            xs = ffn_s(xs, g=g[2], slot=1, mixer=(w_out, o))

    return (xp.reshape(batch, seq, D_MODEL), xs.reshape(dec_batch, dec_seq, D_MODEL),
            jnp.stack(even_k, axis=1), jnp.stack(even_v, axis=1),
            jnp.stack(odd_k, axis=1), jnp.stack(odd_v, axis=1))
```

```python
import functools
import math

import jax
import jax.numpy as jnp
from jax import lax
from jax.experimental import pallas as pl
from jax.experimental.pallas import tpu as pltpu

F32 = jnp.float32
BF16 = jnp.bfloat16

D_MODEL = 1024
GRID_W = 64
HEAD_DIM = 64
AXIS_DIM = HEAD_DIM // 2
A_HEADS = 4
A_VDIM = 2 * HEAD_DIM
CONV_DIM = 512
C_HEADS = 16
C_KV_HEADS = 4
GROUP = C_HEADS // C_KV_HEADS
WINDOW = 128
D_FF = 2816
ROPE_BASE = 10000.0
N_MOD = 9
EPS = 1e-6
A_QK = A_HEADS * 2 * HEAD_DIM
A_V = A_HEADS * A_VDIM
EVEN_IN = 2 * A_QK + A_V + 3 * CONV_DIM
ODD_IN = (C_HEADS + 2 * C_KV_HEADS) * HEAD_DIM
LOG2E = math.log2(math.e)
Q_PRESCALE = HEAD_DIM ** -0.5 * LOG2E

LANES = 128
SUBLANES = 8
BF16_SUBLANES = 16
MOD_ROWS = 16
VMEM_LIMIT_BYTES = 56 * 2 ** 20

TOKEN_TILE = 512
FFN_BLOCK_TILES = 1
IN_BLOCK_TILES = 1
EVEN_Q_TILE = 512
EVEN_Q_CHUNK = 512
EVEN_Q_SUB = 256
ODD_Q_TILE = 512
ODD_Q_CHUNK = 512
ODD_Q_SUB = WINDOW


def _params(*semantics):
    return pltpu.CompilerParams(dimension_semantics=semantics, vmem_limit_bytes=VMEM_LIMIT_BYTES)


def _sigmoid(x):
    return 1.0 / (1.0 + jnp.exp(-x))


def _mod_norm(x, g, shift, scale):
    y = x * lax.rsqrt(jnp.mean(x * x, axis=-1, keepdims=True) + EPS)
    return (y * g) * (1.0 + scale) + shift


def _mod_row(mod_ref, k, r):
    return mod_ref[k, pl.ds(r, 1), :]


def _row_fn(row0, tokens_per_row, tile):
    return lambda i: row0 + (i * tile) // tokens_per_row


def _lane_lo(shape):
    return lax.broadcasted_iota(jnp.int32, shape, len(shape) - 1) < HEAD_DIM


def _seg_rms(xs, gain):
    lo = _lane_lo(xs.shape)
    sq = xs * xs
    s_lo = jnp.sum(jnp.where(lo, sq, 0.0), axis=-1, keepdims=True)
    s_hi = jnp.sum(jnp.where(lo, 0.0, sq), axis=-1, keepdims=True)
    inv = lax.rsqrt(jnp.where(lo, s_lo, s_hi) * (1.0 / HEAD_DIM) + EPS)
    return (xs * inv) * gain


def _half_sum_matrix():
    r = lax.broadcasted_iota(jnp.int32, (2 * LANES, LANES), 0)
    c = lax.broadcasted_iota(jnp.int32, (2 * LANES, LANES), 1)
    return jnp.where((r & HEAD_DIM) == (c & HEAD_DIM), 1.0, 0.0).astype(BF16)


def _seg_rms_mxu(xs, gain, ones2):
    sq = xs * xs
    head = sq.astype(BF16)
    rest = (sq - head.astype(F32)).astype(BF16)
    ssum = _dot(jnp.concatenate([head, rest], axis=-1), ones2)
    return (xs * lax.rsqrt(ssum * (1.0 / HEAD_DIM) + EPS)) * gain


def _rope(xs, cos, sin_signed):
    lane = lax.broadcasted_iota(jnp.int32, xs.shape, 1)
    first = (lane & (AXIS_DIM // 2)) == 0
    partner = jnp.where(first, pltpu.roll(xs, LANES - AXIS_DIM // 2, 1), pltpu.roll(xs, AXIS_DIM // 2, 1))
    return xs * cos + partner * sin_signed


def _dot_t(a, b):
    return lax.dot_general(a, b, (((1,), (1,)), ((), ())), preferred_element_type=F32)


def _dot(a, b):
    return jnp.dot(a, b, preferred_element_type=F32)


def _adaln_kernel(c_ref, w_ref, b_ref, o_ref):
    c = c_ref[...]
    s = c * _sigmoid(c)
    o_ref[0, 0] = _dot(s.astype(BF16), w_ref[0].astype(BF16)) + b_ref[0]


def _adaln(c_rows, w_mod, b_mod):
    depth = w_mod.shape[0]
    b3 = b_mod.reshape(depth * N_MOD, 1, D_MODEL)
    return pl.pallas_call(
        _adaln_kernel,
        out_shape=jax.ShapeDtypeStruct((depth, N_MOD, MOD_ROWS, D_MODEL), F32),
        grid=(depth, N_MOD),
        in_specs=[
            pl.BlockSpec((MOD_ROWS, D_MODEL), lambda l, j: (0, 0)),
            pl.BlockSpec((1, D_MODEL, D_MODEL), lambda l, j: (l, 0, j)),
            pl.BlockSpec((1, 1, D_MODEL), lambda l, j: (l * N_MOD + j, 0, 0)),
        ],
        out_specs=pl.BlockSpec((1, 1, MOD_ROWS, D_MODEL), lambda l, j: (l, j, 0, 0)),
        compiler_params=_params("parallel", "parallel"),
        name="adaln",
    )(c_rows, w_mod, b3)


def _ffn_tail(x, r, mod_ref, g_ref, w13_ref, w2_ref, k0):
    h = _mod_norm(x, g_ref[...], _mod_row(mod_ref, k0, r), _mod_row(mod_ref, k0 + 1, r))
    a = _dot(h.astype(BF16), w13_ref[...])
    gate = a[:, :D_FF]
    up = a[:, D_FF:]
    act = (gate * _sigmoid(gate)) * up
    y = _dot(act.astype(BF16), w2_ref[...])
    return x + (0.5 * _mod_row(mod_ref, k0 + 2, r)) * y


def _aligned(row, multiple):
    return row if isinstance(row, int) else pl.multiple_of(row, multiple)


def _for_chunks(block_rows, chunk, body):
    n = block_rows // chunk
    if n == 1:
        body(0, 0)
    else:
        def step(j, carry):
            body(j, pl.multiple_of(j * chunk, chunk))
            return carry
        lax.fori_loop(0, n, step, 0)


def _ffn_kernel(x_ref, mod_ref, g_ref, w13_ref, w2_ref, o_ref, *, row_fn, tm):
    n = x_ref.shape[0] // tm

    def body(j, row0):
        rows = pl.ds(row0, tm)
        r = row_fn(pl.program_id(0) * n + j)
        o_ref[rows, :] = _ffn_tail(x_ref[rows, :], r, mod_ref, g_ref, w13_ref, w2_ref, 0)

    _for_chunks(x_ref.shape[0], tm, body)


def _conv_gate(bg_ref, u_ref, up_ref, un_ref, cw_ref, i, seq, tm):
    u = u_ref[...]
    row = lax.broadcasted_iota(jnp.int32, (tm, 1), 0)
    pos = (row + i * tm) % seq
    u_dn = jnp.where(row == 0, up_ref[SUBLANES - 1:SUBLANES, :], pltpu.roll(u, 1, 0))
    u_dn = jnp.where(pos == 0, 0.0, u_dn)
    u_up = jnp.where(row == tm - 1, un_ref[0:1, :], pltpu.roll(u, tm - 1, 0))
    u_up = jnp.where(pos == seq - 1, 0.0, u_up)
    return bg_ref[...] * (cw_ref[0:1, :] * u_dn + cw_ref[1:2, :] * u + cw_ref[2:3, :] * u_up)


def _even_out_ffn_kernel(x_ref, a_ref, bg_ref, u_ref, up_ref, un_ref, cw_ref, wo_ref, mod_ref, g_ref, w13_ref, w2_ref,
                         o_ref, *, row_fn, seq, tm):
    i = pl.program_id(0)
    r = row_fn(i)
    y = _conv_gate(bg_ref, u_ref, up_ref, un_ref, cw_ref, i, seq, tm)
    mix = _dot(jnp.concatenate([a_ref[...], y.astype(BF16)], axis=-1), wo_ref[...])
    x = x_ref[...] + _mod_row(mod_ref, 5, r) * mix
    o_ref[...] = _ffn_tail(x, r, mod_ref, g_ref, w13_ref, w2_ref, 6)


def _odd_out_ffn_kernel(x_ref, a_ref, wo_ref, mod_ref, g_ref, w13_ref, w2_ref, o_ref, *, row_fn):
    r = row_fn(pl.program_id(0))
    x = x_ref[...] + _mod_row(mod_ref, 5, r) * _dot(a_ref[...], wo_ref[...])
    o_ref[...] = _ffn_tail(x, r, mod_ref, g_ref, w13_ref, w2_ref, 6)


def _resident(shape, index=None):
    index = (0,) * len(shape) if index is None else index
    return pl.BlockSpec(shape, lambda *_: index, pipeline_mode=pl.Buffered(1))


def _cast_specs(casts, steps):
    in_specs, out_specs, out_shapes = [], [], []
    for arr, lead in casts:
        rows_total, cols = arr.shape[-2:]
        nblk = next(d for d in range(steps, 0, -1)
                    if steps % d == 0 and rows_total % d == 0 and (rows_total // d) % BF16_SUBLANES == 0)
        rows, rep = rows_total // nblk, steps // nblk
        in_specs.append(pl.BlockSpec((1,) * len(lead) + (rows, cols), lambda i, lead=lead, rep=rep: lead + (i // rep, 0)))
        out_specs.append(pl.BlockSpec((rows, cols), lambda i, rep=rep: (i // rep, 0)))
        out_shapes.append(jax.ShapeDtypeStruct((rows_total, cols), BF16))
    return in_specs, out_specs, out_shapes


def _with_casts(body, n_in, n_out, n_cast):
    if n_cast == 0:
        return body

    def kernel(*refs):
        body(*refs[:n_in], *refs[n_in + n_cast:n_in + n_cast + n_out])
        for src, dst in zip(refs[n_in:n_in + n_cast], refs[n_in + n_cast + n_out:]):
            dst[...] = src[(0,) * (len(src.shape) - 2)].astype(dst.dtype)

    return kernel


def _ffn(x, mod, g, w13, w2, *, row_fn, tm, mixer=None, casts=()):
    t = x.shape[0]
    block = tm if mixer is not None else min(t, FFN_BLOCK_TILES * tm)
    tile = lambda n: pl.BlockSpec((block, n), lambda i: (i, 0))
    ffn_specs = [
        _resident((N_MOD, MOD_ROWS, D_MODEL)),
        _resident((1, D_MODEL)),
        _resident((D_MODEL, 2 * D_FF)),
        _resident((D_FF, D_MODEL)),
    ]
    ffn_args = [mod, g, w13, w2]
    if mixer is None:
        body = functools.partial(_ffn_kernel, row_fn=row_fn, tm=tm)
        in_specs, args, name = [tile(D_MODEL)], [x], "ffn"
    elif len(mixer) == 2:
        w_out, attn = mixer
        body = functools.partial(_odd_out_ffn_kernel, row_fn=row_fn)
        in_specs = [tile(D_MODEL), tile(attn.shape[1]), _resident(w_out.shape)]
        args, name = [x, attn, w_out], "odd_out_ffn"
    else:
        w_out, attn, bg, u, conv_w, seq = mixer
        nb = tm // SUBLANES
        last = t // SUBLANES - 1
        body = functools.partial(_even_out_ffn_kernel, row_fn=row_fn, seq=seq, tm=tm)
        in_specs = [
            tile(D_MODEL), tile(A_V), tile(CONV_DIM), tile(CONV_DIM),
            pl.BlockSpec((SUBLANES, CONV_DIM), lambda i: (jnp.maximum(i * nb - 1, 0), 0)),
            pl.BlockSpec((SUBLANES, CONV_DIM), lambda i: (jnp.minimum((i + 1) * nb, last), 0)),
            _resident(conv_w.shape), _resident(w_out.shape),
        ]
        args, name = [x, attn, bg, u, u, u, conv_w, w_out], "even_out_ffn"
    steps = t // block
    c_in, c_out, c_shapes = _cast_specs(casts, steps)
    in_specs = in_specs + ffn_specs
    out = pl.pallas_call(
        _with_casts(body, len(in_specs), 1, len(casts)),
        out_shape=(jax.ShapeDtypeStruct((t, D_MODEL), F32), *c_shapes),
        grid=(steps,),
        in_specs=in_specs + c_in,
        out_specs=(tile(D_MODEL), *c_out),
        compiler_params=_params("parallel"),
        name=name,
    )(*args, *ffn_args, *[arr for arr, _ in casts])
    return (out[0], list(out[1:])) if casts else out[0]


def _even_in_kernel(x_ref, mod_ref, g_ref, w_ref, qkg_ref, *rest, row_fn, rope, tm):
    if rope:
        cos_ref, sin_ref, q_ref, k_ref, v_ref, bg_ref, u_ref = rest
    else:
        q_ref, k_ref, v_ref, bg_ref, u_ref = rest
    n = x_ref.shape[0] // tm
    q_gain = qkg_ref[0:1, :] * Q_PRESCALE

    def body(j, row0):
        rows = pl.ds(row0, tm)
        r = row_fn(pl.program_id(0) * n + j)
        h = _mod_norm(x_ref[rows, :], g_ref[...], _mod_row(mod_ref, 3, r), _mod_row(mod_ref, 4, r))
        y = _dot(h.astype(BF16), w_ref[...])
        for hh in range(A_HEADS):
            sl = slice(hh * LANES, (hh + 1) * LANES)
            qs = _seg_rms(y[:, sl], q_gain)
            ks = _seg_rms(y[:, A_QK + hh * LANES:A_QK + (hh + 1) * LANES], qkg_ref[1:2, :])
            if rope:
                qs = _rope(qs, cos_ref[rows, :], sin_ref[rows, :])
                ks = _rope(ks, cos_ref[rows, :], sin_ref[rows, :])
            q_ref[rows, sl] = qs.astype(q_ref.dtype)
            k_ref[rows, sl] = ks.astype(k_ref.dtype)
        v_ref[rows, :] = y[:, 2 * A_QK:2 * A_QK + A_V].astype(v_ref.dtype)
        o = 2 * A_QK + A_V
        bg_ref[rows, :] = y[:, o:o + CONV_DIM]
        u_ref[rows, :] = y[:, o + CONV_DIM:o + 2 * CONV_DIM] * y[:, o + 2 * CONV_DIM:o + 3 * CONV_DIM]

    _for_chunks(x_ref.shape[0], tm, body)


def _even_in(x, mod, g, w_in, qkg, rope_tabs, *, row_fn, tm, kv_dtype, casts=()):
    t = x.shape[0]
    rope = rope_tabs is not None
    block = min(t, IN_BLOCK_TILES * tm)
    in_specs = [
        pl.BlockSpec((block, D_MODEL), lambda i: (i, 0)),
        _resident((N_MOD, MOD_ROWS, D_MODEL)),
        _resident((1, D_MODEL)),
        _resident((D_MODEL, EVEN_IN)),
        _resident((2, LANES)),
    ]
    args = [x, mod, g, w_in, qkg]
    if rope:
        nt = rope_tabs[0].shape[0] // block
        in_specs += [pl.BlockSpec((block, LANES), lambda i: (i % nt, 0))] * 2
        args += list(rope_tabs)
    wide = lambda dt: jax.ShapeDtypeStruct((t, A_QK), dt)
    spec = pl.BlockSpec((block, A_QK), lambda i: (i, 0))
    steps = t // block
    c_in, c_out, c_shapes = _cast_specs(casts, steps)
    body = functools.partial(_even_in_kernel, row_fn=row_fn, rope=rope, tm=tm)
    out = pl.pallas_call(
        _with_casts(body, len(in_specs), 5, len(casts)),
        out_shape=(wide(BF16), wide(kv_dtype), wide(kv_dtype), wide(F32), wide(F32), *c_shapes),
        grid=(steps,),
        in_specs=in_specs + c_in,
        out_specs=(spec,) * 5 + tuple(c_out),
        compiler_params=_params("parallel"),
        name="even_in",
    )(*args, *[arr for arr, _ in casts])
    return (out[:5], list(out[5:])) if casts else out


def _lambda(lam_ref, lam_init):
    lf = lam_ref[...]
    a = jnp.sum(lf[0:1] * lf[1:2], axis=-1, keepdims=True)
    b = jnp.sum(lf[2:3] * lf[3:4], axis=-1, keepdims=True)
    return jnp.exp(a) - jnp.exp(b) + lam_init


def _subln(o, sub, lam_init):
    y = o * lax.rsqrt(jnp.mean(o * o, axis=-1, keepdims=True) + EPS)
    return (y * sub) * (1.0 - lam_init)


def _softmax_rows_pv(s, v1):
    m = jnp.max(s, axis=-1, keepdims=True)
    r = _dot(jnp.exp2(s - m).astype(BF16), v1)
    return r[:, :LANES] * (1.0 / r[:, LANES:])


def _diff_attn_heads(q_ref, head_kv, lam_ref, sub_ref, o_ref, lam_init):
    lam = _lambda(lam_ref, lam_init)
    tq = q_ref.shape[0]
    chunk = min(tq, EVEN_Q_CHUNK)
    sub_rows = min(chunk, EVEN_Q_SUB)

    def body(j, row0):
        units = [(pl.ds(_aligned(row0 + rb * sub_rows, sub_rows), sub_rows), h)
                 for rb in range(chunk // sub_rows) for h in range(A_HEADS)]

        def scores(unit):
            rows, h = unit
            q = q_ref[rows, h * LANES:(h + 1) * LANES]
            k, _ = head_kv(h)
            lo = _lane_lo(q.shape)
            zero = jnp.zeros_like(q)
            return _dot_t(jnp.where(lo, q, zero), k), _dot_t(jnp.where(lo, zero, q), k)

        s_next = scores(units[0])
        for n, (rows, h) in enumerate(units):
            s1, s2 = s_next
            if n + 1 < len(units):
                s_next = scores(units[n + 1])
            _, v1 = head_kv(h)
            o = _softmax_rows_pv(s1, v1) - lam * _softmax_rows_pv(s2, v1)
            o_ref[rows, h * LANES:(h + 1) * LANES] = _subln(o, sub_ref[...], lam_init).astype(o_ref.dtype)

    _for_chunks(tq, chunk, body)


def _even_attn_p_kernel(q_ref, k_ref, v_ref, lam_ref, sub_ref, o_ref, *, lam_init):
    ones = jnp.ones((k_ref.shape[0], LANES), BF16)

    def head_kv(h):
        sl = slice(h * LANES, (h + 1) * LANES)
        return k_ref[:, sl].astype(BF16), jnp.concatenate([v_ref[:, sl].astype(BF16), ones], axis=-1)

    _diff_attn_heads(q_ref, head_kv, lam_ref, sub_ref, o_ref, lam_init)


def _even_attn_p(q, k, v, lam_vec, sub, *, seq, lam_init):
    t = q.shape[0]
    blk = pl.BlockSpec((seq, A_QK), lambda b: (b, 0))
    return pl.pallas_call(
        functools.partial(_even_attn_p_kernel, lam_init=lam_init),
        out_shape=jax.ShapeDtypeStruct((t, A_V), BF16),
        grid=(t // seq,),
        in_specs=[blk, blk, blk, _resident((4, HEAD_DIM)), _resident((1, A_VDIM))],
        out_specs=blk,
        compiler_params=_params("parallel"),
        name="even_attn_prompt",
    )(q, k, v, lam_vec, sub)


def _even_attn_s_kernel(q_ref, kl_ref, vl_ref, kc_ref, vc_ref, lam_ref, sub_ref, o_ref, k_scr, v1_scr, *, past, lam_init):
    @pl.when(pl.program_id(1) == 0)
    def _():
        k_scr[:past, :] = kc_ref[...].astype(BF16)
        k_scr[past:, :] = kl_ref[...]
        for h in range(A_HEADS):
            sl = slice(h * LANES, (h + 1) * LANES)
            v1_scr[h, :past, :LANES] = vc_ref[:, sl].astype(BF16)
            v1_scr[h, past:, :LANES] = vl_ref[:, sl]
            v1_scr[h, :, LANES:] = jnp.ones((v1_scr.shape[1], LANES), BF16)

    def head_kv(h):
        return k_scr[:, h * LANES:(h + 1) * LANES], v1_scr[h]

    _diff_attn_heads(q_ref, head_kv, lam_ref, sub_ref, o_ref, lam_init)


def _even_attn_s(q, kl, vl, kc, vc, lam_vec, sub, *, seq, past, tq, lam_init):
    t = q.shape[0]
    nq = seq // tq
    qblk = pl.BlockSpec((tq, A_QK), lambda b, i: (b * nq + i, 0))
    lat = pl.BlockSpec((seq, A_QK), lambda b, i: (b, 0))
    ctx = pl.BlockSpec((past, A_QK), lambda b, i: (b, 0))
    return pl.pallas_call(
        functools.partial(_even_attn_s_kernel, past=past, lam_init=lam_init),
        out_shape=jax.ShapeDtypeStruct((t, A_V), BF16),
        grid=(t // seq, nq),
        in_specs=[qblk, lat, lat, ctx, ctx, _resident((4, HEAD_DIM)), _resident((1, A_VDIM))],
        out_specs=qblk,
        scratch_shapes=[pltpu.VMEM((past + seq, A_QK), BF16),
                        pltpu.VMEM((A_HEADS, past + seq, 2 * LANES), BF16)],
        compiler_params=_params("parallel", "arbitrary"),
        name="even_attn_sample",
    )(q, kl, vl, kc, vc, lam_vec, sub)


def _dup_halves(x):
    lo = _lane_lo(x.shape)
    sw = pltpu.roll(x, HEAD_DIM, 1)
    return jnp.where(lo, x, sw), jnp.where(lo, sw, x)


def _ones_halves(x):
    lo = _lane_lo(x.shape)
    return jnp.where(lo, x, 1.0), jnp.where(lo, pltpu.roll(x, HEAD_DIM, 1), 1.0)


def _odd_in_kernel(x_ref, mod_ref, g_ref, w_ref, qkg_ref, *rest, row_fn, rope, keep_kv, tm):
    rest = list(rest)
    if rope:
        cos_ref, sin_ref = rest[:2]
        rest = rest[2:]
    if keep_kv:
        q_ref, kd_ref, vd_ref, k_ref, v_ref = rest
    else:
        q_ref, kd_ref, vd_ref = rest
    nq = C_HEADS * HEAD_DIM
    nk = C_KV_HEADS * HEAD_DIM
    q_gain = qkg_ref[0:1, :] * Q_PRESCALE
    n = x_ref.shape[0] // tm
    ones2 = _half_sum_matrix()

    def body(j, row0):
        r = row_fn(pl.program_id(0) * n + j)
        for half in range(2):
            rows = pl.ds(_aligned(row0 + half * (tm // 2), tm // 2), tm // 2)
            h = _mod_norm(x_ref[rows, :], g_ref[...], _mod_row(mod_ref, 3, r), _mod_row(mod_ref, 4, r))
            y = _dot(h.astype(BF16), w_ref[...])
            if rope:
                cos, sin = cos_ref[rows, :], sin_ref[rows, :]
            for c in range(nq // LANES):
                sl = slice(c * LANES, (c + 1) * LANES)
                qs = _seg_rms_mxu(y[:, sl], q_gain, ones2)
                if rope:
                    qs = _rope(qs, cos, sin)
                q_ref[rows, sl] = qs.astype(q_ref.dtype)
            for c in range(nk // LANES):
                sl = slice(c * LANES, (c + 1) * LANES)
                ks = _seg_rms_mxu(y[:, nq + c * LANES:nq + (c + 1) * LANES], qkg_ref[1:2, :], ones2)
                vs = y[:, nq + nk + c * LANES:nq + nk + (c + 1) * LANES]
                if keep_kv:
                    k_ref[rows, sl] = ks
                    v_ref[rows, sl] = vs
                if rope:
                    ks = _rope(ks, cos, sin)
                for (d0, d1), ref in ((_dup_halves(ks), kd_ref), (_ones_halves(vs), vd_ref)):
                    ref[rows, 2 * c * LANES:(2 * c + 1) * LANES] = d0.astype(ref.dtype)
                    ref[rows, (2 * c + 1) * LANES:(2 * c + 2) * LANES] = d1.astype(ref.dtype)

    _for_chunks(x_ref.shape[0], tm, body)


def _odd_in(x, mod, g, w_in, qkg, rope_tabs, *, row_fn, tm, keep_kv):
    t = x.shape[0]
    rope = rope_tabs is not None
    block = min(t, IN_BLOCK_TILES * tm)
    in_specs = [
        pl.BlockSpec((block, D_MODEL), lambda i: (i, 0)),
        _resident((N_MOD, MOD_ROWS, D_MODEL)),
        _resident((1, D_MODEL)),
        _resident((D_MODEL, ODD_IN)),
        _resident((2, LANES)),
    ]
    args = [x, mod, g, w_in, qkg]
    if rope:
        nt = rope_tabs[0].shape[0] // block
        in_specs += [pl.BlockSpec((block, LANES), lambda i: (i % nt, 0))] * 2
        args += list(rope_tabs)
    nq = C_HEADS * HEAD_DIM
    nd = C_KV_HEADS * LANES
    nk = C_KV_HEADS * HEAD_DIM
    out_shape = [jax.ShapeDtypeStruct((t, nq), BF16), jax.ShapeDtypeStruct((t, nd), BF16),
                 jax.ShapeDtypeStruct((t, nd), BF16)]
    out_specs = [pl.BlockSpec((block, nq), lambda i: (i, 0)), pl.BlockSpec((block, nd), lambda i: (i, 0)),
                 pl.BlockSpec((block, nd), lambda i: (i, 0))]
    if keep_kv:
        out_shape += [jax.ShapeDtypeStruct((t, nk), F32)] * 2
        out_specs += [pl.BlockSpec((block, nk), lambda i: (i, 0))] * 2
    return pl.pallas_call(
        functools.partial(_odd_in_kernel, row_fn=row_fn, rope=rope, keep_kv=keep_kv, tm=tm),
        out_shape=tuple(out_shape),
        grid=(t // block,),
        in_specs=in_specs,
        out_specs=tuple(out_specs),
        compiler_params=_params("parallel"),
        name="odd_in",
    )(*args)


def _stack_heads(q_ref, g, rows):
    parts = []
    for rr in range(GROUP):
        c = g * (GROUP // 2) + rr // 2
        qc = q_ref[rows, c * LANES:(c + 1) * LANES]
        lo = _lane_lo(qc.shape)
        keep = lo if rr % 2 == 0 else jnp.logical_not(lo)
        parts.append(jnp.where(keep, qc, jnp.zeros_like(qc)))
    return jnp.concatenate(parts, axis=0)


def _gqa_group(scores, biases, vals, sinks, rows):
    es = [[] for _ in scores]
    sink_terms = []
    for rr in range(GROUP):
        rs = slice(rr * rows, (rr + 1) * rows)
        parts = [s[rs] if b is None else s[rs] + b for s, b in zip(scores, biases)]
        m = functools.reduce(jnp.maximum, [jnp.max(p, axis=-1, keepdims=True) for p in parts])
        m = jnp.maximum(m, sinks[rr])
        for j, p in enumerate(parts):
            es[j].append(jnp.exp2(p - m).astype(BF16))
        sink_terms.append(jnp.exp2(sinks[rr] - m))
    r = functools.reduce(jnp.add, [_dot(jnp.concatenate(e, axis=0), v) for e, v in zip(es, vals)])
    heads = []
    for rr in range(GROUP):
        rh = r[rr * rows:(rr + 1) * rows]
        heads.append(rh * (1.0 / (pltpu.roll(rh, HEAD_DIM, 1) + sink_terms[rr])))
    lo = _lane_lo((rows, LANES))
    slabs = [jnp.where(lo, heads[2 * c], pltpu.roll(heads[2 * c + 1], HEAD_DIM, 1)) for c in range(GROUP // 2)]
    return jnp.concatenate(slabs, axis=-1)


def _odd_attn_p_kernel(sink_ref, q_ref, kd_ref, vd_ref, o_ref, *, seq):
    g = pl.program_id(1)
    q4 = _stack_heads(q_ref, 0, slice(None))
    sinks = [sink_ref[g * GROUP + rr] * LOG2E for rr in range(GROUP)]
    o_ref[...] = _gqa_group([_dot_t(q4, kd_ref[...])], [None], [vd_ref[...]], sinks, seq).astype(o_ref.dtype)


def _odd_attn_p(q, kd, vd, sink, *, seq):
    t = q.shape[0]
    gw = GROUP * HEAD_DIM
    return pl.pallas_call(
        functools.partial(_odd_attn_p_kernel, seq=seq),
        out_shape=jax.ShapeDtypeStruct((t, C_HEADS * HEAD_DIM), BF16),
        grid=(t // seq, C_KV_HEADS),
        in_specs=[
            pl.BlockSpec(memory_space=pltpu.SMEM),
            pl.BlockSpec((seq, gw), lambda b, g: (b, g)),
            pl.BlockSpec((seq, LANES), lambda b, g: (b, g)),
            pl.BlockSpec((seq, LANES), lambda b, g: (b, g)),
        ],
        out_specs=pl.BlockSpec((seq, gw), lambda b, g: (b, g)),
        compiler_params=_params("parallel", "parallel"),
        name="odd_attn_prompt",
    )(sink, q, kd, vd)


def _odd_attn_s_kernel(sink_ref, q_ref, kd_ref, vd_ref, kc_ref, vc_ref, bias_ref, o_ref, kcd_scr, vcd_scr, *, seq, tq):
    i = pl.program_id(1)
    sub = ODD_Q_SUB
    span = sub + 2 * WINDOW

    @pl.when(i == 0)
    def _():
        for c in range(C_KV_HEADS // 2):
            sl = slice(c * LANES, (c + 1) * LANES)
            for (d0, d1), dst in ((_dup_halves(kc_ref[:, sl]), kcd_scr), (_ones_halves(vc_ref[:, sl]), vcd_scr)):
                dst[:, 2 * c * LANES:(2 * c + 1) * LANES] = d0.astype(BF16)
                dst[:, (2 * c + 1) * LANES:(2 * c + 2) * LANES] = d1.astype(BF16)

    chunk = min(tq, ODD_Q_CHUNK)

    def body(j, row0):
        units = []
        for jb in range(chunk // sub):
            q0 = i * tq + row0 + jb * sub
            start = pl.multiple_of(jnp.clip(q0 - WINDOW, 0, seq - span), WINDOW)
            bias = bias_ref[(q0 - start) // WINDOW]
            rows = pl.ds(_aligned(row0 + jb * sub, sub), sub)
            units += [(rows, start, bias, g) for g in range(C_KV_HEADS)]

        def scores(unit):
            rows, start, _, g = unit
            gl = slice(g * LANES, (g + 1) * LANES)
            q4 = _stack_heads(q_ref, g, rows)
            return [_dot_t(q4, kcd_scr[:, gl]), _dot_t(q4, kd_ref[pl.ds(start, span), gl])]

        s_next = scores(units[0])
        for n, (rows, start, bias, g) in enumerate(units):
            s_cur = s_next
            if n + 1 < len(units):
                s_next = scores(units[n + 1])
            gl = slice(g * LANES, (g + 1) * LANES)
            sinks = [sink_ref[g * GROUP + rr] * LOG2E for rr in range(GROUP)]
            og = _gqa_group(s_cur, [None, bias], [vcd_scr[:, gl], vd_ref[pl.ds(start, span), gl]], sinks, sub)
            o_ref[rows, g * GROUP * HEAD_DIM:(g + 1) * GROUP * HEAD_DIM] = og.astype(o_ref.dtype)

    _for_chunks(tq, chunk, body)


def _odd_attn_s(q, kd, vd, kc, vc, sink, *, seq, past, tq):
    t = q.shape[0]
    nq = seq // tq
    nqw = C_HEADS * HEAD_DIM
    nd = C_KV_HEADS * LANES
    nk = C_KV_HEADS * HEAD_DIM
    span = ODD_Q_SUB + 2 * WINDOW
    rel = (jnp.arange(3)[:, None, None] * WINDOW + jnp.arange(ODD_Q_SUB)[None, :, None]
           - jnp.arange(span)[None, None, :])
    bias = jnp.where(jnp.abs(rel) <= WINDOW, 0.0, -jnp.inf).astype(F32)
    return pl.pallas_call(
        functools.partial(_odd_attn_s_kernel, seq=seq, tq=tq),
        out_shape=jax.ShapeDtypeStruct((t, nqw), BF16),
        grid=(t // seq, nq),
        in_specs=[
            pl.BlockSpec(memory_space=pltpu.SMEM),
            pl.BlockSpec((tq, nqw), lambda b, i: (b * nq + i, 0)),
            pl.BlockSpec((seq, nd), lambda b, i: (b, 0)),
            pl.BlockSpec((seq, nd), lambda b, i: (b, 0)),
            pl.BlockSpec((past, nk), lambda b, i: (b, 0)),
            pl.BlockSpec((past, nk), lambda b, i: (b, 0)),
            _resident(bias.shape),
        ],
        out_specs=pl.BlockSpec((tq, nqw), lambda b, i: (b * nq + i, 0)),
        scratch_shapes=[pltpu.VMEM((past, nd), BF16), pltpu.VMEM((past, nd), BF16)],
        compiler_params=_params("parallel", "arbitrary"),
        name="odd_attn_sample",
    )(sink, q, kd, vd, kc, vc, bias)


def _rope_tables(n):
    rows = n // GRID_W
    row = jnp.repeat(jnp.arange(rows, dtype=F32), GRID_W)
    col = jnp.tile(jnp.arange(GRID_W, dtype=F32), rows)
    inv = ROPE_BASE ** (-jnp.arange(0, AXIS_DIM, 2, dtype=F32) / AXIS_DIM)
    ang_r = row[:, None] * inv[None, :]
    ang_c = col[:, None] * inv[None, :]
    ang = jnp.concatenate([ang_r, ang_r, ang_c, ang_c], axis=-1)
    sign = jnp.where((jnp.arange(HEAD_DIM) & (AXIS_DIM // 2)) == 0, -1.0, 1.0).astype(F32)
    cos = jnp.cos(ang)
    sin = jnp.sin(ang) * sign[None, :]
    return jnp.tile(cos, (1, 2)), jnp.tile(sin, (1, 2))


def kernel(x_prompt, x_sample, cache_even_k, cache_even_v, cache_odd_k, cache_odd_v, c, c_ctx, w_mod, b_mod, norm_g, ffn_w13, ffn_w2, even_w_in, even_w_out, even_qk_norm, even_lambda, even_subln, even_conv_w, odd_w_in, odd_w_out, odd_qk_norm, odd_sink):
    batch, seq, _ = x_prompt.shape
    dec_batch, dec_seq, _ = x_sample.shape
    past = cache_even_k.shape[2]
    depth = w_mod.shape[0]
    assert 1 + dec_batch <= MOD_ROWS and seq % SUBLANES == 0 and dec_seq % TOKEN_TILE == 0

    xp = x_prompt.reshape(batch * seq, D_MODEL)
    xs = x_sample.reshape(dec_batch * dec_seq, D_MODEL)
    tm_p = min(TOKEN_TILE, batch * seq)
    row_p = _row_fn(0, batch * seq, tm_p)
    row_s = _row_fn(1, dec_seq, TOKEN_TILE)

    c_rows = jnp.zeros((MOD_ROWS, D_MODEL), F32).at[0].set(c_ctx).at[1:1 + dec_batch].set(c)
    mods = _adaln(c_rows, w_mod, b_mod)
    rope_tabs = _rope_tables(dec_seq)

    src = {}
    for l in range(depth):
        for s in range(2):
            src["w13", l, s] = (ffn_w13, (l, s))
            src["w2", l, s] = (ffn_w2, (l, s))
        names, w_in_all, w_out_all = (("ein", "eout"), even_w_in, even_w_out) if l % 2 == 0 else (
            ("oin", "oout"), odd_w_in, odd_w_out)
        src[names[0], l // 2] = (w_in_all, (l // 2,))
        src[names[1], l // 2] = (w_out_all, (l // 2,))
    wb = {("w13", 0, 0): ffn_w13[0, 0].astype(BF16), ("w2", 0, 0): ffn_w2[0, 0].astype(BF16)}
    first = [("w13", 0, 1), ("w2", 0, 1), ("ein", 0), ("eout", 0)]
    rest = [key for key in src if key not in wb and key not in first]

    even_k, even_v, odd_k, odd_v = [], [], [], []
    for l in range(depth):
        mod = mods[l]
        g = [norm_g[l, s].reshape(1, D_MODEL) for s in range(3)]

        ffn_p = functools.partial(_ffn, mod=mod, row_fn=row_p, tm=tm_p)
        ffn_s = functools.partial(_ffn, mod=mod, row_fn=row_s, tm=TOKEN_TILE)
        xp = ffn_p(xp, g=g[0], w13=wb["w13", l, 0], w2=wb["w2", l, 0])
        if l == 0:
            xs, copies = ffn_s(xs, g=g[0], w13=wb["w13", l, 0], w2=wb["w2", l, 0], casts=[src[key] for key in first])
            wb.update(zip(first, copies))
        else:
            xs = ffn_s(xs, g=g[0], w13=wb["w13", l, 0], w2=wb["w2", l, 0])
        w13b, w2b = wb["w13", l, 1], wb["w2", l, 1]

        if l % 2 == 0:
            e = l // 2
            lam_init = 0.8 - 0.6 * math.exp(-0.3 * l)
            w_in, w_out = wb["ein", e], wb["eout", e]
            qkg = jnp.tile(even_qk_norm[e], (1, 2))
            sub = even_subln[e].reshape(1, A_VDIM)

            q, k, v, bg, u = _even_in(xp, mod, g[1], w_in, qkg, None, row_fn=row_p, tm=tm_p, kv_dtype=F32)
            o = _even_attn_p(q, k, v, even_lambda[e], sub, seq=seq, lam_init=lam_init)
            xp = ffn_p(xp, g=g[2], w13=w13b, w2=w2b, mixer=(w_out, o, bg, u, even_conv_w[e], seq))
            even_k.append(k.reshape(batch, seq, A_HEADS, 2 * HEAD_DIM))
            even_v.append(v.reshape(batch, seq, A_HEADS, A_VDIM))

            casts = [src[key] for key in rest] if l == 0 else []
            outs = _even_in(xs, mod, g[1], w_in, qkg, rope_tabs, row_fn=row_s, tm=TOKEN_TILE, kv_dtype=BF16,
                            casts=casts)
            if casts:
                outs, copies = outs
                wb.update(zip(rest, copies))
            q, k, v, bg, u = outs
            kc = cache_even_k[:, e].reshape(dec_batch * past, A_QK)
            vc = cache_even_v[:, e].reshape(dec_batch * past, A_V)
            o = _even_attn_s(q, k, v, kc, vc, even_lambda[e], sub, seq=dec_seq, past=past, tq=EVEN_Q_TILE,
                             lam_init=lam_init)
            xs = ffn_s(xs, g=g[2], w13=w13b, w2=w2b, mixer=(w_out, o, bg, u, even_conv_w[e], dec_seq))
        else:
            e = l // 2
            w_in, w_out = wb["oin", e], wb["oout", e]
            qkg = jnp.tile(odd_qk_norm[e], (1, 2))
            sink = odd_sink[e]

            q, kd, vd, k, v = _odd_in(xp, mod, g[1], w_in, qkg, None, row_fn=row_p, tm=tm_p, keep_kv=True)
            o = _odd_attn_p(q, kd, vd, sink, seq=seq)
            xp = ffn_p(xp, g=g[2], w13=w13b, w2=w2b, mixer=(w_out, o))
            odd_k.append(k.reshape(batch, seq, C_KV_HEADS, HEAD_DIM))
            odd_v.append(v.reshape(batch, seq, C_KV_HEADS, HEAD_DIM))

            q, kd, vd = _odd_in(xs, mod, g[1], w_in, qkg, rope_tabs, row_fn=row_s, tm=TOKEN_TILE, keep_kv=False)
            kc = cache_odd_k[:, e].reshape(dec_batch * past, C_KV_HEADS * HEAD_DIM)
            vc = cache_odd_v[:, e].reshape(dec_batch * past, C_KV_HEADS * HEAD_DIM)
            o = _odd_attn_s(q, kd, vd, kc, vc, sink, seq=dec_seq, past=past, tq=ODD_Q_TILE)
            xs = ffn_s(xs, g=g[2], w13=w13b, w2=w2b, mixer=(w_out, o))

    return (xp.reshape(batch, seq, D_MODEL), xs.reshape(dec_batch, dec_seq, D_MODEL),
            jnp.stack(even_k, axis=1), jnp.stack(even_v, axis=1),
            jnp.stack(odd_k, axis=1), jnp.stack(odd_v, axis=1))
```

```python
import functools
import math

import jax
import jax.numpy as jnp
from jax import lax
from jax.experimental import pallas as pl
from jax.experimental.pallas import tpu as pltpu

F32 = jnp.float32
BF16 = jnp.bfloat16

D_MODEL = 1024
GRID_W = 64
HEAD_DIM = 64
AXIS_DIM = HEAD_DIM // 2
A_HEADS = 4
A_VDIM = 2 * HEAD_DIM
CONV_DIM = 512
C_HEADS = 16
C_KV_HEADS = 4
GROUP = C_HEADS // C_KV_HEADS
WINDOW = 128
D_FF = 2816
ROPE_BASE = 10000.0
N_MOD = 9
EPS = 1e-6
A_QK = A_HEADS * 2 * HEAD_DIM
A_V = A_HEADS * A_VDIM
EVEN_IN = 2 * A_QK + A_V + 3 * CONV_DIM
ODD_IN = (C_HEADS + 2 * C_KV_HEADS) * HEAD_DIM
LOG2E = math.log2(math.e)
Q_PRESCALE = HEAD_DIM ** -0.5 * LOG2E

LANES = 128
SUBLANES = 8
BF16_SUBLANES = 16
MOD_ROWS = 16
VMEM_LIMIT_BYTES = 56 * 2 ** 20

TOKEN_TILE = 512
FFN_BLOCK_TILES = 1
IN_BLOCK_TILES = 1
EVEN_Q_TILE = 512
EVEN_Q_CHUNK = 512
EVEN_Q_SUB = 256
ODD_Q_TILE = 512
ODD_Q_CHUNK = 512
ODD_Q_SUB = WINDOW


def _params(*semantics):
    return pltpu.CompilerParams(dimension_semantics=semantics, vmem_limit_bytes=VMEM_LIMIT_BYTES)


def _sigmoid(x):
    return 1.0 / (1.0 + jnp.exp(-x))


def _mod_norm(x, g, shift, scale):
    y = x * lax.rsqrt(jnp.mean(x * x, axis=-1, keepdims=True) + EPS)
    return (y * g) * (1.0 + scale) + shift


def _mod_row(mod_ref, k, r):
    return mod_ref[k, pl.ds(r, 1), :]


def _row_fn(row0, tokens_per_row, tile):
    return lambda i: row0 + (i * tile) // tokens_per_row


def _lane_lo(shape):
    return lax.broadcasted_iota(jnp.int32, shape, len(shape) - 1) < HEAD_DIM


def _seg_rms(xs, gain):
    lo = _lane_lo(xs.shape)
    sq = xs * xs
    s_lo = jnp.sum(jnp.where(lo, sq, 0.0), axis=-1, keepdims=True)
    s_hi = jnp.sum(jnp.where(lo, 0.0, sq), axis=-1, keepdims=True)
    inv = lax.rsqrt(jnp.where(lo, s_lo, s_hi) * (1.0 / HEAD_DIM) + EPS)
    return (xs * inv) * gain


def _half_sum_matrix():
    r = lax.broadcasted_iota(jnp.int32, (2 * LANES, LANES), 0)
    c = lax.broadcasted_iota(jnp.int32, (2 * LANES, LANES), 1)
    return jnp.where((r & HEAD_DIM) == (c & HEAD_DIM), 1.0, 0.0).astype(BF16)


def _seg_rms_mxu(xs, gain, ones2):
    sq = xs * xs
    head = sq.astype(BF16)
    rest = (sq - head.astype(F32)).astype(BF16)
    ssum = _dot(jnp.concatenate([head, rest], axis=-1), ones2)
    return (xs * lax.rsqrt(ssum * (1.0 / HEAD_DIM) + EPS)) * gain


def _rope(xs, cos, sin_signed):
    lane = lax.broadcasted_iota(jnp.int32, xs.shape, 1)
    first = (lane & (AXIS_DIM // 2)) == 0
    partner = jnp.where(first, pltpu.roll(xs, LANES - AXIS_DIM // 2, 1), pltpu.roll(xs, AXIS_DIM // 2, 1))
    return xs * cos + partner * sin_signed


def _dot_t(a, b):
    return lax.dot_general(a, b, (((1,), (1,)), ((), ())), preferred_element_type=F32)


def _dot(a, b):
    return jnp.dot(a, b, preferred_element_type=F32)


def _adaln_kernel(c_ref, w_ref, b_ref, o_ref):
    c = c_ref[...]
    s = c * _sigmoid(c)
    o_ref[0, 0] = _dot(s.astype(BF16), w_ref[0].astype(BF16)) + b_ref[0]


def _adaln(c_rows, w_mod, b_mod):
    depth = w_mod.shape[0]
    b3 = b_mod.reshape(depth * N_MOD, 1, D_MODEL)
    return pl.pallas_call(
        _adaln_kernel,
        out_shape=jax.ShapeDtypeStruct((depth, N_MOD, MOD_ROWS, D_MODEL), F32),
        grid=(depth, N_MOD),
        in_specs=[
            pl.BlockSpec((MOD_ROWS, D_MODEL), lambda l, j: (0, 0)),
            pl.BlockSpec((1, D_MODEL, D_MODEL), lambda l, j: (l, 0, j)),
            pl.BlockSpec((1, 1, D_MODEL), lambda l, j: (l * N_MOD + j, 0, 0)),
        ],
        out_specs=pl.BlockSpec((1, 1, MOD_ROWS, D_MODEL), lambda l, j: (l, j, 0, 0)),
        compiler_params=_params("parallel", "parallel"),
        name="adaln",
    )(c_rows, w_mod, b3)


def _ffn_tail(x, r, mod_ref, g_ref, w13_ref, w2_ref, k0):
    h = _mod_norm(x, g_ref[...], _mod_row(mod_ref, k0, r), _mod_row(mod_ref, k0 + 1, r))
    a = _dot(h.astype(BF16), w13_ref[...])
    gate = a[:, :D_FF]
    up = a[:, D_FF:]
    act = (gate * _sigmoid(gate)) * up
    y = _dot(act.astype(BF16), w2_ref[...])
    return x + (0.5 * _mod_row(mod_ref, k0 + 2, r)) * y


def _aligned(row, multiple):
    return row if isinstance(row, int) else pl.multiple_of(row, multiple)


def _for_chunks(block_rows, chunk, body):
    n = block_rows // chunk
    if n == 1:
        body(0, 0)
    else:
        def step(j, carry):
            body(j, pl.multiple_of(j * chunk, chunk))
            return carry
        lax.fori_loop(0, n, step, 0)


def _ffn_kernel(x_ref, mod_ref, g_ref, w13_ref, w2_ref, o_ref, *, row_fn, tm):
    n = x_ref.shape[0] // tm

    def body(j, row0):
        rows = pl.ds(row0, tm)
        r = row_fn(pl.program_id(0) * n + j)
        o_ref[rows, :] = _ffn_tail(x_ref[rows, :], r, mod_ref, g_ref, w13_ref, w2_ref, 0)

    _for_chunks(x_ref.shape[0], tm, body)


def _conv_gate(bg_ref, u_ref, up_ref, un_ref, cw_ref, i, seq, tm):
    u = u_ref[...]
    row = lax.broadcasted_iota(jnp.int32, (tm, 1), 0)
    pos = (row + i * tm) % seq
    u_dn = jnp.where(row == 0, up_ref[SUBLANES - 1:SUBLANES, :], pltpu.roll(u, 1, 0))
    u_dn = jnp.where(pos == 0, 0.0, u_dn)
    u_up = jnp.where(row == tm - 1, un_ref[0:1, :], pltpu.roll(u, tm - 1, 0))
    u_up = jnp.where(pos == seq - 1, 0.0, u_up)
    return bg_ref[...] * (cw_ref[0:1, :] * u_dn + cw_ref[1:2, :] * u + cw_ref[2:3, :] * u_up)


def _even_out_ffn_kernel(x_ref, a_ref, bg_ref, u_ref, up_ref, un_ref, cw_ref, wo_ref, mod_ref, g_ref, w13_ref, w2_ref,
                         o_ref, *, row_fn, seq, tm):
    i = pl.program_id(0)
    r = row_fn(i)
    y = _conv_gate(bg_ref, u_ref, up_ref, un_ref, cw_ref, i, seq, tm)
    mix = _dot(jnp.concatenate([a_ref[...], y.astype(BF16)], axis=-1), wo_ref[...])
    x = x_ref[...] + _mod_row(mod_ref, 5, r) * mix
    o_ref[...] = _ffn_tail(x, r, mod_ref, g_ref, w13_ref, w2_ref, 6)


def _odd_out_ffn_kernel(x_ref, a_ref, wo_ref, mod_ref, g_ref, w13_ref, w2_ref, o_ref, *, row_fn):
    r = row_fn(pl.program_id(0))
    x = x_ref[...] + _mod_row(mod_ref, 5, r) * _dot(a_ref[...], wo_ref[...])
    o_ref[...] = _ffn_tail(x, r, mod_ref, g_ref, w13_ref, w2_ref, 6)


def _resident(shape, index=None):
    index = (0,) * len(shape) if index is None else index
    return pl.BlockSpec(shape, lambda *_: index, pipeline_mode=pl.Buffered(1))


def _cast_specs(casts, steps, flat_step=lambda i: i):
    in_specs, out_specs, out_shapes = [], [], []
    for arr, lead in casts:
        rows_total, cols = arr.shape[-2:]
        nblk = next(d for d in range(steps, 0, -1)
                    if steps % d == 0 and rows_total % d == 0 and (rows_total // d) % BF16_SUBLANES == 0)
        rows, rep = rows_total // nblk, steps // nblk
        in_specs.append(pl.BlockSpec((1,) * len(lead) + (rows, cols),
                                     lambda *idx, lead=lead, rep=rep: lead + (flat_step(*idx) // rep, 0)))
        out_specs.append(pl.BlockSpec((rows, cols), lambda *idx, rep=rep: (flat_step(*idx) // rep, 0)))
        out_shapes.append(jax.ShapeDtypeStruct((rows_total, cols), BF16))
    return in_specs, out_specs, out_shapes


def _with_casts(body, n_in, n_out, n_cast):
    if n_cast == 0:
        return body

    def kernel(*refs):
        outs = n_in + n_cast
        body(*refs[:n_in], *refs[outs:outs + n_out], *refs[outs + n_out + n_cast:])
        for src, dst in zip(refs[n_in:outs], refs[outs + n_out:outs + n_out + n_cast]):
            dst[...] = src[(0,) * (len(src.shape) - 2)].astype(dst.dtype)

    return kernel


def _ffn(x, mod, g, w13, w2, *, row_fn, tm, mixer=None, casts=()):
    t = x.shape[0]
    block = tm if mixer is not None else min(t, FFN_BLOCK_TILES * tm)
    tile = lambda n: pl.BlockSpec((block, n), lambda i: (i, 0))
    ffn_specs = [
        _resident((N_MOD, MOD_ROWS, D_MODEL)),
        _resident((1, D_MODEL)),
        _resident((D_MODEL, 2 * D_FF)),
        _resident((D_FF, D_MODEL)),
    ]
    ffn_args = [mod, g, w13, w2]
    if mixer is None:
        body = functools.partial(_ffn_kernel, row_fn=row_fn, tm=tm)
        in_specs, args, name = [tile(D_MODEL)], [x], "ffn"
    elif len(mixer) == 2:
        w_out, attn = mixer
        body = functools.partial(_odd_out_ffn_kernel, row_fn=row_fn)
        in_specs = [tile(D_MODEL), tile(attn.shape[1]), _resident(w_out.shape)]
        args, name = [x, attn, w_out], "odd_out_ffn"
    else:
        w_out, attn, bg, u, conv_w, seq = mixer
        nb = tm // SUBLANES
        last = t // SUBLANES - 1
        body = functools.partial(_even_out_ffn_kernel, row_fn=row_fn, seq=seq, tm=tm)
        in_specs = [
            tile(D_MODEL), tile(A_V), tile(CONV_DIM), tile(CONV_DIM),
            pl.BlockSpec((SUBLANES, CONV_DIM), lambda i: (jnp.maximum(i * nb - 1, 0), 0)),
            pl.BlockSpec((SUBLANES, CONV_DIM), lambda i: (jnp.minimum((i + 1) * nb, last), 0)),
            _resident(conv_w.shape), _resident(w_out.shape),
        ]
        args, name = [x, attn, bg, u, u, u, conv_w, w_out], "even_out_ffn"
    steps = t // block
    c_in, c_out, c_shapes = _cast_specs(casts, steps)
    in_specs = in_specs + ffn_specs
    out = pl.pallas_call(
        _with_casts(body, len(in_specs), 1, len(casts)),
        out_shape=(jax.ShapeDtypeStruct((t, D_MODEL), F32), *c_shapes),
        grid=(steps,),
        in_specs=in_specs + c_in,
        out_specs=(tile(D_MODEL), *c_out),
        compiler_params=_params("parallel"),
        name=name,
    )(*args, *ffn_args, *[arr for arr, _ in casts])
    return (out[0], list(out[1:])) if casts else out[0]


def _even_in_kernel(x_ref, mod_ref, g_ref, w_ref, qkg_ref, *rest, row_fn, rope, tm):
    if rope:
        cos_ref, sin_ref, q_ref, k_ref, v_ref, bg_ref, u_ref = rest
    else:
        q_ref, k_ref, v_ref, bg_ref, u_ref = rest
    n = x_ref.shape[0] // tm
    q_gain = qkg_ref[0:1, :] * Q_PRESCALE

    def body(j, row0):
        rows = pl.ds(row0, tm)
        r = row_fn(pl.program_id(0) * n + j)
        h = _mod_norm(x_ref[rows, :], g_ref[...], _mod_row(mod_ref, 3, r), _mod_row(mod_ref, 4, r))
        y = _dot(h.astype(BF16), w_ref[...])
        for hh in range(A_HEADS):
            sl = slice(hh * LANES, (hh + 1) * LANES)
            qs = _seg_rms(y[:, sl], q_gain)
            ks = _seg_rms(y[:, A_QK + hh * LANES:A_QK + (hh + 1) * LANES], qkg_ref[1:2, :])
            if rope:
                qs = _rope(qs, cos_ref[rows, :], sin_ref[rows, :])
                ks = _rope(ks, cos_ref[rows, :], sin_ref[rows, :])
            q_ref[rows, sl] = qs.astype(q_ref.dtype)
            k_ref[rows, sl] = ks.astype(k_ref.dtype)
        v_ref[rows, :] = y[:, 2 * A_QK:2 * A_QK + A_V].astype(v_ref.dtype)
        o = 2 * A_QK + A_V
        bg_ref[rows, :] = y[:, o:o + CONV_DIM]
        u_ref[rows, :] = y[:, o + CONV_DIM:o + 2 * CONV_DIM] * y[:, o + 2 * CONV_DIM:o + 3 * CONV_DIM]

    _for_chunks(x_ref.shape[0], tm, body)


def _even_in(x, mod, g, w_in, qkg, rope_tabs, *, row_fn, tm, kv_dtype):
    t = x.shape[0]
    rope = rope_tabs is not None
    block = min(t, IN_BLOCK_TILES * tm)
    in_specs = [
        pl.BlockSpec((block, D_MODEL), lambda i: (i, 0)),
        _resident((N_MOD, MOD_ROWS, D_MODEL)),
        _resident((1, D_MODEL)),
        _resident((D_MODEL, EVEN_IN)),
        _resident((2, LANES)),
    ]
    args = [x, mod, g, w_in, qkg]
    if rope:
        nt = rope_tabs[0].shape[0] // block
        in_specs += [pl.BlockSpec((block, LANES), lambda i: (i % nt, 0))] * 2
        args += list(rope_tabs)
    wide = lambda dt: jax.ShapeDtypeStruct((t, A_QK), dt)
    spec = pl.BlockSpec((block, A_QK), lambda i: (i, 0))
    return pl.pallas_call(
        functools.partial(_even_in_kernel, row_fn=row_fn, rope=rope, tm=tm),
        out_shape=(wide(BF16), wide(kv_dtype), wide(kv_dtype), wide(F32), wide(F32)),
        grid=(t // block,),
        in_specs=in_specs,
        out_specs=(spec,) * 5,
        compiler_params=_params("parallel"),
        name="even_in",
    )(*args)


def _lambda(lam_ref, lam_init):
    lf = lam_ref[...]
    a = jnp.sum(lf[0:1] * lf[1:2], axis=-1, keepdims=True)
    b = jnp.sum(lf[2:3] * lf[3:4], axis=-1, keepdims=True)
    return jnp.exp(a) - jnp.exp(b) + lam_init


def _subln(o, sub, lam_init):
    y = o * lax.rsqrt(jnp.mean(o * o, axis=-1, keepdims=True) + EPS)
    return (y * sub) * (1.0 - lam_init)


def _softmax_rows_pv(s, v1):
    m = jnp.max(s, axis=-1, keepdims=True)
    r = _dot(jnp.exp2(s - m).astype(BF16), v1)
    return r[:, :LANES] * (1.0 / r[:, LANES:])


def _diff_attn_heads(q_ref, head_kv, lam_ref, sub_ref, o_ref, lam_init):
    lam = _lambda(lam_ref, lam_init)
    tq = q_ref.shape[0]
    chunk = min(tq, EVEN_Q_CHUNK)
    sub_rows = min(chunk, EVEN_Q_SUB)

    def body(j, row0):
        units = [(pl.ds(_aligned(row0 + rb * sub_rows, sub_rows), sub_rows), h)
                 for rb in range(chunk // sub_rows) for h in range(A_HEADS)]

        def scores(unit):
            rows, h = unit
            q = q_ref[rows, h * LANES:(h + 1) * LANES]
            k, _ = head_kv(h)
            lo = _lane_lo(q.shape)
            zero = jnp.zeros_like(q)
            return _dot_t(jnp.where(lo, q, zero), k), _dot_t(jnp.where(lo, zero, q), k)

        s_next = scores(units[0])
        for n, (rows, h) in enumerate(units):
            s1, s2 = s_next
            if n + 1 < len(units):
                s_next = scores(units[n + 1])
            _, v1 = head_kv(h)
            o = _softmax_rows_pv(s1, v1) - lam * _softmax_rows_pv(s2, v1)
            o_ref[rows, h * LANES:(h + 1) * LANES] = _subln(o, sub_ref[...], lam_init).astype(o_ref.dtype)

    _for_chunks(tq, chunk, body)


def _even_attn_p_kernel(q_ref, k_ref, v_ref, lam_ref, sub_ref, o_ref, *, lam_init):
    ones = jnp.ones((k_ref.shape[0], LANES), BF16)

    def head_kv(h):
        sl = slice(h * LANES, (h + 1) * LANES)
        return k_ref[:, sl].astype(BF16), jnp.concatenate([v_ref[:, sl].astype(BF16), ones], axis=-1)

    _diff_attn_heads(q_ref, head_kv, lam_ref, sub_ref, o_ref, lam_init)


def _even_attn_p(q, k, v, lam_vec, sub, *, seq, lam_init):
    t = q.shape[0]
    blk = pl.BlockSpec((seq, A_QK), lambda b: (b, 0))
    return pl.pallas_call(
        functools.partial(_even_attn_p_kernel, lam_init=lam_init),
        out_shape=jax.ShapeDtypeStruct((t, A_V), BF16),
        grid=(t // seq,),
        in_specs=[blk, blk, blk, _resident((4, HEAD_DIM)), _resident((1, A_VDIM))],
        out_specs=blk,
        compiler_params=_params("parallel"),
        name="even_attn_prompt",
    )(q, k, v, lam_vec, sub)


def _even_attn_s_kernel(q_ref, kl_ref, vl_ref, kc_ref, vc_ref, lam_ref, sub_ref, o_ref, k_scr, v1_scr, *, past, lam_init):
    @pl.when(pl.program_id(1) == 0)
    def _():
        k_scr[:past, :] = kc_ref[...].astype(BF16)
        k_scr[past:, :] = kl_ref[...]
        for h in range(A_HEADS):
            sl = slice(h * LANES, (h + 1) * LANES)
            v1_scr[h, :past, :LANES] = vc_ref[:, sl].astype(BF16)
            v1_scr[h, past:, :LANES] = vl_ref[:, sl]
            v1_scr[h, :, LANES:] = jnp.ones((v1_scr.shape[1], LANES), BF16)

    def head_kv(h):
        return k_scr[:, h * LANES:(h + 1) * LANES], v1_scr[h]

    _diff_attn_heads(q_ref, head_kv, lam_ref, sub_ref, o_ref, lam_init)


def _even_attn_s(q, kl, vl, kc, vc, lam_vec, sub, *, seq, past, tq, lam_init, casts=()):
    t = q.shape[0]
    nq = seq // tq
    qblk = pl.BlockSpec((tq, A_QK), lambda b, i: (b * nq + i, 0))
    lat = pl.BlockSpec((seq, A_QK), lambda b, i: (b, 0))
    ctx = pl.BlockSpec((past, A_QK), lambda b, i: (b, 0))
    in_specs = [qblk, lat, lat, ctx, ctx, _resident((4, HEAD_DIM)), _resident((1, A_VDIM))]
    c_in, c_out, c_shapes = _cast_specs(casts, (t // seq) * nq, lambda b, i: b * nq + i)
    body = functools.partial(_even_attn_s_kernel, past=past, lam_init=lam_init)
    out = pl.pallas_call(
        _with_casts(body, len(in_specs), 1, len(casts)),
        out_shape=(jax.ShapeDtypeStruct((t, A_V), BF16), *c_shapes),
        grid=(t // seq, nq),
        in_specs=in_specs + c_in,
        out_specs=(qblk, *c_out),
        scratch_shapes=[pltpu.VMEM((past + seq, A_QK), BF16),
                        pltpu.VMEM((A_HEADS, past + seq, 2 * LANES), BF16)],
        compiler_params=_params("parallel", "arbitrary"),
        name="even_attn_sample",
    )(q, kl, vl, kc, vc, lam_vec, sub, *[arr for arr, _ in casts])
    return (out[0], list(out[1:])) if casts else out[0]


def _dup_halves(x):
    lo = _lane_lo(x.shape)
    sw = pltpu.roll(x, HEAD_DIM, 1)
    return jnp.where(lo, x, sw), jnp.where(lo, sw, x)


def _ones_halves(x):
    lo = _lane_lo(x.shape)
    return jnp.where(lo, x, 1.0), jnp.where(lo, pltpu.roll(x, HEAD_DIM, 1), 1.0)


def _odd_in_kernel(x_ref, mod_ref, g_ref, w_ref, qkg_ref, *rest, row_fn, rope, keep_kv, tm):
    rest = list(rest)
    if rope:
        cos_ref, sin_ref = rest[:2]
        rest = rest[2:]
    if keep_kv:
        q_ref, kd_ref, vd_ref, k_ref, v_ref = rest
    else:
        q_ref, kd_ref, vd_ref = rest
    nq = C_HEADS * HEAD_DIM
    nk = C_KV_HEADS * HEAD_DIM
    q_gain = qkg_ref[0:1, :] * Q_PRESCALE
    n = x_ref.shape[0] // tm
    ones2 = _half_sum_matrix()

    def body(j, row0):
        r = row_fn(pl.program_id(0) * n + j)
        for half in range(2):
            rows = pl.ds(_aligned(row0 + half * (tm // 2), tm // 2), tm // 2)
            h = _mod_norm(x_ref[rows, :], g_ref[...], _mod_row(mod_ref, 3, r), _mod_row(mod_ref, 4, r))
            y = _dot(h.astype(BF16), w_ref[...])
            if rope:
                cos, sin = cos_ref[rows, :], sin_ref[rows, :]
            for c in range(nq // LANES):
                sl = slice(c * LANES, (c + 1) * LANES)
                qs = _seg_rms_mxu(y[:, sl], q_gain, ones2)
                if rope:
                    qs = _rope(qs, cos, sin)
                q_ref[rows, sl] = qs.astype(q_ref.dtype)
            for c in range(nk // LANES):
                sl = slice(c * LANES, (c + 1) * LANES)
                ks = _seg_rms_mxu(y[:, nq + c * LANES:nq + (c + 1) * LANES], qkg_ref[1:2, :], ones2)
                vs = y[:, nq + nk + c * LANES:nq + nk + (c + 1) * LANES]
                if keep_kv:
                    k_ref[rows, sl] = ks
                    v_ref[rows, sl] = vs
                if rope:
                    ks = _rope(ks, cos, sin)
                for (d0, d1), ref in ((_dup_halves(ks), kd_ref), (_ones_halves(vs), vd_ref)):
                    ref[rows, 2 * c * LANES:(2 * c + 1) * LANES] = d0.astype(ref.dtype)
                    ref[rows, (2 * c + 1) * LANES:(2 * c + 2) * LANES] = d1.astype(ref.dtype)

    _for_chunks(x_ref.shape[0], tm, body)


def _odd_in(x, mod, g, w_in, qkg, rope_tabs, *, row_fn, tm, keep_kv):
    t = x.shape[0]
    rope = rope_tabs is not None
    block = min(t, IN_BLOCK_TILES * tm)
    in_specs = [
        pl.BlockSpec((block, D_MODEL), lambda i: (i, 0)),
        _resident((N_MOD, MOD_ROWS, D_MODEL)),
        _resident((1, D_MODEL)),
        _resident((D_MODEL, ODD_IN)),
        _resident((2, LANES)),
    ]
    args = [x, mod, g, w_in, qkg]
    if rope:
        nt = rope_tabs[0].shape[0] // block
        in_specs += [pl.BlockSpec((block, LANES), lambda i: (i % nt, 0))] * 2
        args += list(rope_tabs)
    nq = C_HEADS * HEAD_DIM
    nd = C_KV_HEADS * LANES
    nk = C_KV_HEADS * HEAD_DIM
    out_shape = [jax.ShapeDtypeStruct((t, nq), BF16), jax.ShapeDtypeStruct((t, nd), BF16),
                 jax.ShapeDtypeStruct((t, nd), BF16)]
    out_specs = [pl.BlockSpec((block, nq), lambda i: (i, 0)), pl.BlockSpec((block, nd), lambda i: (i, 0)),
                 pl.BlockSpec((block, nd), lambda i: (i, 0))]
    if keep_kv:
        out_shape += [jax.ShapeDtypeStruct((t, nk), F32)] * 2
        out_specs += [pl.BlockSpec((block, nk), lambda i: (i, 0))] * 2
    return pl.pallas_call(
        functools.partial(_odd_in_kernel, row_fn=row_fn, rope=rope, keep_kv=keep_kv, tm=tm),
        out_shape=tuple(out_shape),
        grid=(t // block,),
        in_specs=in_specs,
        out_specs=tuple(out_specs),
        compiler_params=_params("parallel"),
        name="odd_in",
    )(*args)


def _stack_heads(q_ref, g, rows):
    parts = []
    for rr in range(GROUP):
        c = g * (GROUP // 2) + rr // 2
        qc = q_ref[rows, c * LANES:(c + 1) * LANES]
        lo = _lane_lo(qc.shape)
        keep = lo if rr % 2 == 0 else jnp.logical_not(lo)
        parts.append(jnp.where(keep, qc, jnp.zeros_like(qc)))
    return jnp.concatenate(parts, axis=0)


def _gqa_group(scores, biases, vals, sinks, rows):
    es = [[] for _ in scores]
    sink_terms = []
    for rr in range(GROUP):
        rs = slice(rr * rows, (rr + 1) * rows)
        parts = [s[rs] if b is None else s[rs] + b for s, b in zip(scores, biases)]
        m = functools.reduce(jnp.maximum, [jnp.max(p, axis=-1, keepdims=True) for p in parts])
        m = jnp.maximum(m, sinks[rr])
        for j, p in enumerate(parts):
            es[j].append(jnp.exp2(p - m).astype(BF16))
        sink_terms.append(jnp.exp2(sinks[rr] - m))
    r = functools.reduce(jnp.add, [_dot(jnp.concatenate(e, axis=0), v) for e, v in zip(es, vals)])
    heads = []
    for rr in range(GROUP):
        rh = r[rr * rows:(rr + 1) * rows]
        heads.append(rh * (1.0 / (pltpu.roll(rh, HEAD_DIM, 1) + sink_terms[rr])))
    lo = _lane_lo((rows, LANES))
    slabs = [jnp.where(lo, heads[2 * c], pltpu.roll(heads[2 * c + 1], HEAD_DIM, 1)) for c in range(GROUP // 2)]
    return jnp.concatenate(slabs, axis=-1)


def _odd_attn_p_kernel(sink_ref, q_ref, kd_ref, vd_ref, o_ref, *, seq):
    g = pl.program_id(1)
    q4 = _stack_heads(q_ref, 0, slice(None))
    sinks = [sink_ref[g * GROUP + rr] * LOG2E for rr in range(GROUP)]
    o_ref[...] = _gqa_group([_dot_t(q4, kd_ref[...])], [None], [vd_ref[...]], sinks, seq).astype(o_ref.dtype)


def _odd_attn_p(q, kd, vd, sink, *, seq):
    t = q.shape[0]
    gw = GROUP * HEAD_DIM
    return pl.pallas_call(
        functools.partial(_odd_attn_p_kernel, seq=seq),
        out_shape=jax.ShapeDtypeStruct((t, C_HEADS * HEAD_DIM), BF16),
        grid=(t // seq, C_KV_HEADS),
        in_specs=[
            pl.BlockSpec(memory_space=pltpu.SMEM),
            pl.BlockSpec((seq, gw), lambda b, g: (b, g)),
            pl.BlockSpec((seq, LANES), lambda b, g: (b, g)),
            pl.BlockSpec((seq, LANES), lambda b, g: (b, g)),
        ],
        out_specs=pl.BlockSpec((seq, gw), lambda b, g: (b, g)),
        compiler_params=_params("parallel", "parallel"),
        name="odd_attn_prompt",
    )(sink, q, kd, vd)


def _odd_attn_s_kernel(sink_ref, q_ref, kd_ref, vd_ref, kc_ref, vc_ref, bias_ref, o_ref, kcd_scr, vcd_scr, *, seq, tq):
    i = pl.program_id(1)
    sub = ODD_Q_SUB
    span = sub + 2 * WINDOW

    @pl.when(i == 0)
    def _():
        for c in range(C_KV_HEADS // 2):
            sl = slice(c * LANES, (c + 1) * LANES)
            for (d0, d1), dst in ((_dup_halves(kc_ref[:, sl]), kcd_scr), (_ones_halves(vc_ref[:, sl]), vcd_scr)):
                dst[:, 2 * c * LANES:(2 * c + 1) * LANES] = d0.astype(BF16)
                dst[:, (2 * c + 1) * LANES:(2 * c + 2) * LANES] = d1.astype(BF16)

    chunk = min(tq, ODD_Q_CHUNK)

    def body(j, row0):
        units = []
        for jb in range(chunk // sub):
            q0 = i * tq + row0 + jb * sub
            start = pl.multiple_of(jnp.clip(q0 - WINDOW, 0, seq - span), WINDOW)
            bias = bias_ref[(q0 - start) // WINDOW]
            rows = pl.ds(_aligned(row0 + jb * sub, sub), sub)
            units += [(rows, start, bias, g) for g in range(C_KV_HEADS)]

        def scores(unit):
            rows, start, _, g = unit
            gl = slice(g * LANES, (g + 1) * LANES)
            q4 = _stack_heads(q_ref, g, rows)
            return [_dot_t(q4, kcd_scr[:, gl]), _dot_t(q4, kd_ref[pl.ds(start, span), gl])]

        s_next = scores(units[0])
        for n, (rows, start, bias, g) in enumerate(units):
            s_cur = s_next
            if n + 1 < len(units):
                s_next = scores(units[n + 1])
            gl = slice(g * LANES, (g + 1) * LANES)
            sinks = [sink_ref[g * GROUP + rr] * LOG2E for rr in range(GROUP)]
            og = _gqa_group(s_cur, [None, bias], [vcd_scr[:, gl], vd_ref[pl.ds(start, span), gl]], sinks, sub)
            o_ref[rows, g * GROUP * HEAD_DIM:(g + 1) * GROUP * HEAD_DIM] = og.astype(o_ref.dtype)

    _for_chunks(tq, chunk, body)


def _odd_attn_s(q, kd, vd, kc, vc, sink, *, seq, past, tq):
    t = q.shape[0]
    nq = seq // tq
    nqw = C_HEADS * HEAD_DIM
    nd = C_KV_HEADS * LANES
    nk = C_KV_HEADS * HEAD_DIM
    span = ODD_Q_SUB + 2 * WINDOW
    rel = (jnp.arange(3)[:, None, None] * WINDOW + jnp.arange(ODD_Q_SUB)[None, :, None]
           - jnp.arange(span)[None, None, :])
    bias = jnp.where(jnp.abs(rel) <= WINDOW, 0.0, -jnp.inf).astype(F32)
    return pl.pallas_call(
        functools.partial(_odd_attn_s_kernel, seq=seq, tq=tq),
        out_shape=jax.ShapeDtypeStruct((t, nqw), BF16),
        grid=(t // seq, nq),
        in_specs=[
            pl.BlockSpec(memory_space=pltpu.SMEM),
            pl.BlockSpec((tq, nqw), lambda b, i: (b * nq + i, 0)),
            pl.BlockSpec((seq, nd), lambda b, i: (b, 0)),
            pl.BlockSpec((seq, nd), lambda b, i: (b, 0)),
            pl.BlockSpec((past, nk), lambda b, i: (b, 0)),
            pl.BlockSpec((past, nk), lambda b, i: (b, 0)),
            _resident(bias.shape),
        ],
        out_specs=pl.BlockSpec((tq, nqw), lambda b, i: (b * nq + i, 0)),
        scratch_shapes=[pltpu.VMEM((past, nd), BF16), pltpu.VMEM((past, nd), BF16)],
        compiler_params=_params("parallel", "arbitrary"),
        name="odd_attn_sample",
    )(sink, q, kd, vd, kc, vc, bias)


def _rope_tables(n):
    rows = n // GRID_W
    row = jnp.repeat(jnp.arange(rows, dtype=F32), GRID_W)
    col = jnp.tile(jnp.arange(GRID_W, dtype=F32), rows)
    inv = ROPE_BASE ** (-jnp.arange(0, AXIS_DIM, 2, dtype=F32) / AXIS_DIM)
    ang_r = row[:, None] * inv[None, :]
    ang_c = col[:, None] * inv[None, :]
    ang = jnp.concatenate([ang_r, ang_r, ang_c, ang_c], axis=-1)
    sign = jnp.where((jnp.arange(HEAD_DIM) & (AXIS_DIM // 2)) == 0, -1.0, 1.0).astype(F32)
    cos = jnp.cos(ang)
    sin = jnp.sin(ang) * sign[None, :]
    return jnp.tile(cos, (1, 2)), jnp.tile(sin, (1, 2))


def kernel(x_prompt, x_sample, cache_even_k, cache_even_v, cache_odd_k, cache_odd_v, c, c_ctx, w_mod, b_mod, norm_g, ffn_w13, ffn_w2, even_w_in, even_w_out, even_qk_norm, even_lambda, even_subln, even_conv_w, odd_w_in, odd_w_out, odd_qk_norm, odd_sink):
    batch, seq, _ = x_prompt.shape
    dec_batch, dec_seq, _ = x_sample.shape
    past = cache_even_k.shape[2]
    depth = w_mod.shape[0]
    assert 1 + dec_batch <= MOD_ROWS and seq % SUBLANES == 0 and dec_seq % TOKEN_TILE == 0

    xp = x_prompt.reshape(batch * seq, D_MODEL)
    xs = x_sample.reshape(dec_batch * dec_seq, D_MODEL)
    tm_p = min(TOKEN_TILE, batch * seq)
    row_p = _row_fn(0, batch * seq, tm_p)
    row_s = _row_fn(1, dec_seq, TOKEN_TILE)

    c_rows = jnp.zeros((MOD_ROWS, D_MODEL), F32).at[0].set(c_ctx).at[1:1 + dec_batch].set(c)
    mods = _adaln(c_rows, w_mod, b_mod)
    rope_tabs = _rope_tables(dec_seq)

    src = {}
    for l in range(depth):
        for s in range(2):
            src["w13", l, s] = (ffn_w13, (l, s))
            src["w2", l, s] = (ffn_w2, (l, s))
        names, w_in_all, w_out_all = (("ein", "eout"), even_w_in, even_w_out) if l % 2 == 0 else (
            ("oin", "oout"), odd_w_in, odd_w_out)
        src[names[0], l // 2] = (w_in_all, (l // 2,))
        src[names[1], l // 2] = (w_out_all, (l // 2,))
    wb = {("w13", 0, 0): ffn_w13[0, 0].astype(BF16), ("w2", 0, 0): ffn_w2[0, 0].astype(BF16)}
    first = [("w13", 0, 1), ("w2", 0, 1), ("ein", 0), ("eout", 0)]
    rest = [key for key in src if key not in wb and key not in first]

    even_k, even_v, odd_k, odd_v = [], [], [], []
    for l in range(depth):
        mod = mods[l]
        g = [norm_g[l, s].reshape(1, D_MODEL) for s in range(3)]

        ffn_p = functools.partial(_ffn, mod=mod, row_fn=row_p, tm=tm_p)
        ffn_s = functools.partial(_ffn, mod=mod, row_fn=row_s, tm=TOKEN_TILE)
        xp = ffn_p(xp, g=g[0], w13=wb["w13", l, 0], w2=wb["w2", l, 0])
        if l == 0:
            xs, copies = ffn_s(xs, g=g[0], w13=wb["w13", l, 0], w2=wb["w2", l, 0], casts=[src[key] for key in first])
            wb.update(zip(first, copies))
        else:
            xs = ffn_s(xs, g=g[0], w13=wb["w13", l, 0], w2=wb["w2", l, 0])
        w13b, w2b = wb["w13", l, 1], wb["w2", l, 1]

        if l % 2 == 0:
            e = l // 2
            lam_init = 0.8 - 0.6 * math.exp(-0.3 * l)
            w_in, w_out = wb["ein", e], wb["eout", e]
            qkg = jnp.tile(even_qk_norm[e], (1, 2))
            sub = even_subln[e].reshape(1, A_VDIM)

            q, k, v, bg, u = _even_in(xp, mod, g[1], w_in, qkg, None, row_fn=row_p, tm=tm_p, kv_dtype=F32)
            o = _even_attn_p(q, k, v, even_lambda[e], sub, seq=seq, lam_init=lam_init)
            xp = ffn_p(xp, g=g[2], w13=w13b, w2=w2b, mixer=(w_out, o, bg, u, even_conv_w[e], seq))
            even_k.append(k.reshape(batch, seq, A_HEADS, 2 * HEAD_DIM))
            even_v.append(v.reshape(batch, seq, A_HEADS, A_VDIM))

            q, k, v, bg, u = _even_in(xs, mod, g[1], w_in, qkg, rope_tabs, row_fn=row_s, tm=TOKEN_TILE,
                                      kv_dtype=BF16)
            kc = cache_even_k[:, e].reshape(dec_batch * past, A_QK)
            vc = cache_even_v[:, e].reshape(dec_batch * past, A_V)
            casts = [src[key] for key in rest] if l == 0 else []
            o = _even_attn_s(q, k, v, kc, vc, even_lambda[e], sub, seq=dec_seq, past=past, tq=EVEN_Q_TILE,
                             lam_init=lam_init, casts=casts)
            if casts:
                o, copies = o
                wb.update(zip(rest, copies))
            xs = ffn_s(xs, g=g[2], w13=w13b, w2=w2b, mixer=(w_out, o, bg, u, even_conv_w[e], dec_seq))
        else:
            e = l // 2
            w_in, w_out = wb["oin", e], wb["oout", e]
            qkg = jnp.tile(odd_qk_norm[e], (1, 2))
            sink = odd_sink[e]

            q, kd, vd, k, v = _odd_in(xp, mod, g[1], w_in, qkg, None, row_fn=row_p, tm=tm_p, keep_kv=True)
            o = _odd_attn_p(q, kd, vd, sink, seq=seq)
            xp = ffn_p(xp, g=g[2], w13=w13b, w2=w2b, mixer=(w_out, o))
            odd_k.append(k.reshape(batch, seq, C_KV_HEADS, HEAD_DIM))
            odd_v.append(v.reshape(batch, seq, C_KV_HEADS, HEAD_DIM))

            q, kd, vd = _odd_in(xs, mod, g[1], w_in, qkg, rope_tabs, row_fn=row_s, tm=TOKEN_TILE, keep_kv=False)
            kc = cache_odd_k[:, e].reshape(dec_batch * past, C_KV_HEADS * HEAD_DIM)
            vc = cache_odd_v[:, e].reshape(dec_batch * past, C_KV_HEADS * HEAD_DIM)
            o = _odd_attn_s(q, kd, vd, kc, vc, sink, seq=dec_seq, past=past, tq=ODD_Q_TILE)
            xs = ffn_s(xs, g=g[2], w13=w13b, w2=w2b, mixer=(w_out, o))

    return (xp.reshape(batch, seq, D_MODEL), xs.reshape(dec_batch, dec_seq, D_MODEL),
            jnp.stack(even_k, axis=1), jnp.stack(even_v, axis=1),
            jnp.stack(odd_k, axis=1), jnp.stack(odd_v, axis=1))
```

```python
import functools
import math

import jax
import jax.numpy as jnp
import numpy as np
from jax import lax
from jax.experimental import pallas as pl
from jax.experimental.pallas import tpu as pltpu

F32 = jnp.float32
BF16 = jnp.bfloat16

D_MODEL = 1024
GRID_W = 64
HEAD_DIM = 64
AXIS_DIM = HEAD_DIM // 2
A_HEADS = 4
A_VDIM = 2 * HEAD_DIM
CONV_DIM = 512
C_HEADS = 16
C_KV_HEADS = 4
GROUP = C_HEADS // C_KV_HEADS
WINDOW = 128
D_FF = 2816
ROPE_BASE = 10000.0
N_MOD = 9
EPS = 1e-6
A_QK = A_HEADS * 2 * HEAD_DIM
A_V = A_HEADS * A_VDIM
EVEN_IN = 2 * A_QK + A_V + 3 * CONV_DIM
ODD_IN = (C_HEADS + 2 * C_KV_HEADS) * HEAD_DIM
LOG2E = math.log2(math.e)
Q_PRESCALE = HEAD_DIM ** -0.5 * LOG2E

LANES = 128
SUBLANES = 8
BF16_SUBLANES = 16
MOD_ROWS = 16
VMEM_LIMIT_BYTES = 56 * 2 ** 20

TOKEN_TILE = 512
FFN_BLOCK_TILES = 1
IN_BLOCK_TILES = 1
EVEN_Q_TILE = 512
EVEN_Q_CHUNK = 512
EVEN_Q_SUB = 256
ODD_Q_TILE = 512
ODD_Q_CHUNK = 512
ODD_Q_SUB = WINDOW
ODD_P_SEQS = 2


def _params(*semantics):
    return pltpu.CompilerParams(dimension_semantics=semantics, vmem_limit_bytes=VMEM_LIMIT_BYTES)


def _sigmoid(x):
    return 1.0 / (1.0 + jnp.exp(-x))


def _mod_norm(x, g, shift, scale):
    y = x * lax.rsqrt(jnp.mean(x * x, axis=-1, keepdims=True) + EPS)
    return (y * g) * (1.0 + scale) + shift


def _mod_row(mod_ref, k, r):
    return mod_ref[k, pl.ds(r, 1), :]


def _row_fn(row0, tokens_per_row, tile):
    return lambda i: row0 + (i * tile) // tokens_per_row


def _lane_lo(shape):
    return lax.broadcasted_iota(jnp.int32, shape, len(shape) - 1) < HEAD_DIM


def _seg_rms(xs, gain):
    lo = _lane_lo(xs.shape)
    sq = xs * xs
    s_lo = jnp.sum(jnp.where(lo, sq, 0.0), axis=-1, keepdims=True)
    s_hi = jnp.sum(jnp.where(lo, 0.0, sq), axis=-1, keepdims=True)
    inv = lax.rsqrt(jnp.where(lo, s_lo, s_hi) * (1.0 / HEAD_DIM) + EPS)
    return (xs * inv) * gain


def _half_sum_matrix():
    r = lax.broadcasted_iota(jnp.int32, (2 * LANES, LANES), 0)
    c = lax.broadcasted_iota(jnp.int32, (2 * LANES, LANES), 1)
    return jnp.where((r & HEAD_DIM) == (c & HEAD_DIM), 1.0, 0.0).astype(BF16)


def _seg_rms_mxu(xs, gain, ones2):
    sq = xs * xs
    head = sq.astype(BF16)
    rest = (sq - head.astype(F32)).astype(BF16)
    ssum = _dot(jnp.concatenate([head, rest], axis=-1), ones2)
    return (xs * lax.rsqrt(ssum * (1.0 / HEAD_DIM) + EPS)) * gain


def _rope(xs, cos, sin_signed):
    lane = lax.broadcasted_iota(jnp.int32, xs.shape, 1)
    first = (lane & (AXIS_DIM // 2)) == 0
    partner = jnp.where(first, pltpu.roll(xs, LANES - AXIS_DIM // 2, 1), pltpu.roll(xs, AXIS_DIM // 2, 1))
    return xs * cos + partner * sin_signed


def _dot_t(a, b):
    return lax.dot_general(a, b, (((1,), (1,)), ((), ())), preferred_element_type=F32)


def _dot(a, b):
    return jnp.dot(a, b, preferred_element_type=F32)


def _adaln_kernel(c_ref, w_ref, b_ref, o_ref):
    c = c_ref[...]
    s = c * _sigmoid(c)
    o_ref[0, 0] = _dot(s.astype(BF16), w_ref[0].astype(BF16)) + b_ref[0]


def _adaln(c_rows, w_mod, b_mod):
    depth = w_mod.shape[0]
    b3 = b_mod.reshape(depth * N_MOD, 1, D_MODEL)
    return pl.pallas_call(
        _adaln_kernel,
        out_shape=jax.ShapeDtypeStruct((depth, N_MOD, MOD_ROWS, D_MODEL), F32),
        grid=(depth, N_MOD),
        in_specs=[
            pl.BlockSpec((MOD_ROWS, D_MODEL), lambda l, j: (0, 0)),
            pl.BlockSpec((1, D_MODEL, D_MODEL), lambda l, j: (l, 0, j)),
            pl.BlockSpec((1, 1, D_MODEL), lambda l, j: (l * N_MOD + j, 0, 0)),
        ],
        out_specs=pl.BlockSpec((1, 1, MOD_ROWS, D_MODEL), lambda l, j: (l, j, 0, 0)),
        compiler_params=_params("parallel", "parallel"),
        name="adaln",
    )(c_rows, w_mod, b3)


def _ffn_tail(x, r, mod_ref, g_ref, w13_ref, w2_ref, k0):
    h = _mod_norm(x, g_ref[...], _mod_row(mod_ref, k0, r), _mod_row(mod_ref, k0 + 1, r))
    a = _dot(h.astype(BF16), w13_ref[...])
    gate = a[:, :D_FF]
    up = a[:, D_FF:]
    act = (gate * _sigmoid(gate)) * up
    y = _dot(act.astype(BF16), w2_ref[...])
    return x + (0.5 * _mod_row(mod_ref, k0 + 2, r)) * y


def _aligned(row, multiple):
    return row if isinstance(row, int) else pl.multiple_of(row, multiple)


def _for_chunks(block_rows, chunk, body):
    n = block_rows // chunk
    if n == 1:
        body(0, 0)
    else:
        def step(j, carry):
            body(j, pl.multiple_of(j * chunk, chunk))
            return carry
        lax.fori_loop(0, n, step, 0)


def _ffn_kernel(x_ref, mod_ref, g_ref, w13_ref, w2_ref, o_ref, *, row_fn, tm):
    n = x_ref.shape[0] // tm

    def body(j, row0):
        rows = pl.ds(row0, tm)
        r = row_fn(pl.program_id(0) * n + j)
        o_ref[rows, :] = _ffn_tail(x_ref[rows, :], r, mod_ref, g_ref, w13_ref, w2_ref, 0)

    _for_chunks(x_ref.shape[0], tm, body)


def _conv_gate(bg_ref, u_ref, up_ref, un_ref, cw_ref, i, seq, tm):
    u = u_ref[...]
    row = lax.broadcasted_iota(jnp.int32, (tm, 1), 0)
    pos = (row + i * tm) % seq
    u_dn = jnp.where(row == 0, up_ref[SUBLANES - 1:SUBLANES, :], pltpu.roll(u, 1, 0))
    u_dn = jnp.where(pos == 0, 0.0, u_dn)
    u_up = jnp.where(row == tm - 1, un_ref[0:1, :], pltpu.roll(u, tm - 1, 0))
    u_up = jnp.where(pos == seq - 1, 0.0, u_up)
    return bg_ref[...] * (cw_ref[0:1, :] * u_dn + cw_ref[1:2, :] * u + cw_ref[2:3, :] * u_up)


def _even_out_ffn_kernel(x_ref, a_ref, bg_ref, u_ref, up_ref, un_ref, cw_ref, wo_ref, mod_ref, g_ref, w13_ref, w2_ref,
                         o_ref, *, row_fn, seq, tm):
    i = pl.program_id(0)
    r = row_fn(i)
    y = _conv_gate(bg_ref, u_ref, up_ref, un_ref, cw_ref, i, seq, tm)
    mix = _dot(jnp.concatenate([a_ref[...], y.astype(BF16)], axis=-1), wo_ref[...])
    x = x_ref[...] + _mod_row(mod_ref, 5, r) * mix
    o_ref[...] = _ffn_tail(x, r, mod_ref, g_ref, w13_ref, w2_ref, 6)


def _odd_out_ffn_kernel(x_ref, a_ref, wo_ref, mod_ref, g_ref, w13_ref, w2_ref, o_ref, *, row_fn):
    r = row_fn(pl.program_id(0))
    x = x_ref[...] + _mod_row(mod_ref, 5, r) * _dot(a_ref[...], wo_ref[...])
    o_ref[...] = _ffn_tail(x, r, mod_ref, g_ref, w13_ref, w2_ref, 6)


def _resident(shape, index=None):
    index = (0,) * len(shape) if index is None else index
    return pl.BlockSpec(shape, lambda *_: index, pipeline_mode=pl.Buffered(1))


def _cast_specs(casts, steps, flat_step=lambda i: i):
    in_specs, out_specs, out_shapes = [], [], []
    for arr, lead in casts:
        rows_total, cols = arr.shape[-2:]
        nblk = next(d for d in range(steps, 0, -1)
                    if steps % d == 0 and rows_total % d == 0 and (rows_total // d) % BF16_SUBLANES == 0)
        rows, rep = rows_total // nblk, steps // nblk
        in_specs.append(pl.BlockSpec((1,) * len(lead) + (rows, cols),
                                     lambda *idx, lead=lead, rep=rep: lead + (flat_step(*idx) // rep, 0)))
        out_specs.append(pl.BlockSpec((rows, cols), lambda *idx, rep=rep: (flat_step(*idx) // rep, 0)))
        out_shapes.append(jax.ShapeDtypeStruct((rows_total, cols), BF16))
    return in_specs, out_specs, out_shapes


def _with_casts(body, n_in, n_out, n_cast):
    if n_cast == 0:
        return body

    def kernel(*refs):
        outs = n_in + n_cast
        body(*refs[:n_in], *refs[outs:outs + n_out], *refs[outs + n_out + n_cast:])
        for src, dst in zip(refs[n_in:outs], refs[outs + n_out:outs + n_out + n_cast]):
            dst[...] = src[(0,) * (len(src.shape) - 2)].astype(dst.dtype)

    return kernel


def _ffn(x, mod, g, w13, w2, *, row_fn, tm, mixer=None, casts=()):
    t = x.shape[0]
    block = tm if mixer is not None else min(t, FFN_BLOCK_TILES * tm)
    tile = lambda n: pl.BlockSpec((block, n), lambda i: (i, 0))
    ffn_specs = [
        _resident((N_MOD, MOD_ROWS, D_MODEL)),
        _resident((1, D_MODEL)),
        _resident((D_MODEL, 2 * D_FF)),
        _resident((D_FF, D_MODEL)),
    ]
    ffn_args = [mod, g, w13, w2]
    if mixer is None:
        body = functools.partial(_ffn_kernel, row_fn=row_fn, tm=tm)
        in_specs, args, name = [tile(D_MODEL)], [x], "ffn"
    elif len(mixer) == 2:
        w_out, attn = mixer
        body = functools.partial(_odd_out_ffn_kernel, row_fn=row_fn)
        in_specs = [tile(D_MODEL), tile(attn.shape[1]), _resident(w_out.shape)]
        args, name = [x, attn, w_out], "odd_out_ffn"
    else:
        w_out, attn, bg, u, conv_w, seq = mixer
        nb = tm // SUBLANES
        last = t // SUBLANES - 1
        body = functools.partial(_even_out_ffn_kernel, row_fn=row_fn, seq=seq, tm=tm)
        in_specs = [
            tile(D_MODEL), tile(A_V), tile(CONV_DIM), tile(CONV_DIM),
            pl.BlockSpec((SUBLANES, CONV_DIM), lambda i: (jnp.maximum(i * nb - 1, 0), 0)),
            pl.BlockSpec((SUBLANES, CONV_DIM), lambda i: (jnp.minimum((i + 1) * nb, last), 0)),
            _resident(conv_w.shape), _resident(w_out.shape),
        ]
        args, name = [x, attn, bg, u, u, u, conv_w, w_out], "even_out_ffn"
    steps = t // block
    c_in, c_out, c_shapes = _cast_specs(casts, steps)
    in_specs = in_specs + ffn_specs
    out = pl.pallas_call(
        _with_casts(body, len(in_specs), 1, len(casts)),
        out_shape=(jax.ShapeDtypeStruct((t, D_MODEL), F32), *c_shapes),
        grid=(steps,),
        in_specs=in_specs + c_in,
        out_specs=(tile(D_MODEL), *c_out),
        compiler_params=_params("parallel"),
        name=name,
    )(*args, *ffn_args, *[arr for arr, _ in casts])
    return (out[0], list(out[1:])) if casts else out[0]


def _even_in_kernel(x_ref, mod_ref, g_ref, w_ref, qkg_ref, *rest, row_fn, rope, tm):
    if rope:
        cos_ref, sin_ref, q_ref, k_ref, v_ref, bg_ref, u_ref = rest
    else:
        q_ref, k_ref, v_ref, bg_ref, u_ref = rest
    n = x_ref.shape[0] // tm
    q_gain = qkg_ref[0:1, :] * Q_PRESCALE

    def body(j, row0):
        rows = pl.ds(row0, tm)
        r = row_fn(pl.program_id(0) * n + j)
        h = _mod_norm(x_ref[rows, :], g_ref[...], _mod_row(mod_ref, 3, r), _mod_row(mod_ref, 4, r))
        y = _dot(h.astype(BF16), w_ref[...])
        for hh in range(A_HEADS):
            sl = slice(hh * LANES, (hh + 1) * LANES)
            qs = _seg_rms(y[:, sl], q_gain)
            ks = _seg_rms(y[:, A_QK + hh * LANES:A_QK + (hh + 1) * LANES], qkg_ref[1:2, :])
            if rope:
                qs = _rope(qs, cos_ref[rows, :], sin_ref[rows, :])
                ks = _rope(ks, cos_ref[rows, :], sin_ref[rows, :])
            q_ref[rows, sl] = qs.astype(q_ref.dtype)
            k_ref[rows, sl] = ks.astype(k_ref.dtype)
        v_ref[rows, :] = y[:, 2 * A_QK:2 * A_QK + A_V].astype(v_ref.dtype)
        o = 2 * A_QK + A_V
        bg_ref[rows, :] = y[:, o:o + CONV_DIM]
        u_ref[rows, :] = y[:, o + CONV_DIM:o + 2 * CONV_DIM] * y[:, o + 2 * CONV_DIM:o + 3 * CONV_DIM]

    _for_chunks(x_ref.shape[0], tm, body)


def _even_in(x, mod, g, w_in, qkg, rope_tabs, *, row_fn, tm, kv_dtype):
    t = x.shape[0]
    rope = rope_tabs is not None
    block = min(t, IN_BLOCK_TILES * tm)
    in_specs = [
        pl.BlockSpec((block, D_MODEL), lambda i: (i, 0)),
        _resident((N_MOD, MOD_ROWS, D_MODEL)),
        _resident((1, D_MODEL)),
        _resident((D_MODEL, EVEN_IN)),
        _resident((2, LANES)),
    ]
    args = [x, mod, g, w_in, qkg]
    if rope:
        nt = rope_tabs[0].shape[0] // block
        in_specs += [pl.BlockSpec((block, LANES), lambda i: (i % nt, 0))] * 2
        args += list(rope_tabs)
    wide = lambda dt: jax.ShapeDtypeStruct((t, A_QK), dt)
    spec = pl.BlockSpec((block, A_QK), lambda i: (i, 0))
    return pl.pallas_call(
        functools.partial(_even_in_kernel, row_fn=row_fn, rope=rope, tm=tm),
        out_shape=(wide(BF16), wide(kv_dtype), wide(kv_dtype), wide(F32), wide(F32)),
        grid=(t // block,),
        in_specs=in_specs,
        out_specs=(spec,) * 5,
        compiler_params=_params("parallel"),
        name="even_in",
    )(*args)


def _lambda(lam_ref, lam_init):
    lf = lam_ref[...]
    a = jnp.sum(lf[0:1] * lf[1:2], axis=-1, keepdims=True)
    b = jnp.sum(lf[2:3] * lf[3:4], axis=-1, keepdims=True)
    return jnp.exp(a) - jnp.exp(b) + lam_init


def _subln(o, sub, lam_init):
    y = o * lax.rsqrt(jnp.mean(o * o, axis=-1, keepdims=True) + EPS)
    return (y * sub) * (1.0 - lam_init)


def _softmax_rows_pv(s, v1):
    m = jnp.max(s, axis=-1, keepdims=True)
    r = _dot(jnp.exp2(s - m).astype(BF16), v1)
    return r[:, :LANES] * (1.0 / r[:, LANES:])


def _diff_attn_heads(q_ref, head_kv, lam_ref, sub_ref, o_ref, lam_init):
    lam = _lambda(lam_ref, lam_init)
    tq = q_ref.shape[0]
    chunk = min(tq, EVEN_Q_CHUNK)
    sub_rows = min(chunk, EVEN_Q_SUB)

    def body(j, row0):
        units = [(pl.ds(_aligned(row0 + rb * sub_rows, sub_rows), sub_rows), h)
                 for rb in range(chunk // sub_rows) for h in range(A_HEADS)]

        def scores(unit):
            rows, h = unit
            q = q_ref[rows, h * LANES:(h + 1) * LANES]
            k, _ = head_kv(h)
            lo = _lane_lo(q.shape)
            zero = jnp.zeros_like(q)
            return _dot_t(jnp.where(lo, q, zero), k), _dot_t(jnp.where(lo, zero, q), k)

        s_next = scores(units[0])
        for n, (rows, h) in enumerate(units):
            s1, s2 = s_next
            if n + 1 < len(units):
                s_next = scores(units[n + 1])
            _, v1 = head_kv(h)
            o = _softmax_rows_pv(s1, v1) - lam * _softmax_rows_pv(s2, v1)
            o_ref[rows, h * LANES:(h + 1) * LANES] = _subln(o, sub_ref[...], lam_init).astype(o_ref.dtype)

    _for_chunks(tq, chunk, body)


def _even_attn_p_kernel(q_ref, k_ref, v_ref, lam_ref, sub_ref, o_ref, *, lam_init):
    ones = jnp.ones((k_ref.shape[0], LANES), BF16)

    def head_kv(h):
        sl = slice(h * LANES, (h + 1) * LANES)
        return k_ref[:, sl].astype(BF16), jnp.concatenate([v_ref[:, sl].astype(BF16), ones], axis=-1)

    _diff_attn_heads(q_ref, head_kv, lam_ref, sub_ref, o_ref, lam_init)


def _even_attn_p(q, k, v, lam_vec, sub, *, seq, lam_init):
    t = q.shape[0]
    blk = pl.BlockSpec((seq, A_QK), lambda b: (b, 0))
    return pl.pallas_call(
        functools.partial(_even_attn_p_kernel, lam_init=lam_init),
        out_shape=jax.ShapeDtypeStruct((t, A_V), BF16),
        grid=(t // seq,),
        in_specs=[blk, blk, blk, _resident((4, HEAD_DIM)), _resident((1, A_VDIM))],
        out_specs=blk,
        compiler_params=_params("parallel"),
        name="even_attn_prompt",
    )(q, k, v, lam_vec, sub)


def _even_attn_s_kernel(q_ref, kl_ref, vl_ref, kc_ref, vc_ref, lam_ref, sub_ref, o_ref, k_scr, v1_scr, *, past, lam_init):
    @pl.when(pl.program_id(1) == 0)
    def _():
        k_scr[:past, :] = kc_ref[...].astype(BF16)
        k_scr[past:, :] = kl_ref[...]
        for h in range(A_HEADS):
            sl = slice(h * LANES, (h + 1) * LANES)
            v1_scr[h, :past, :LANES] = vc_ref[:, sl].astype(BF16)
            v1_scr[h, past:, :LANES] = vl_ref[:, sl]
            v1_scr[h, :, LANES:] = jnp.ones((v1_scr.shape[1], LANES), BF16)

    def head_kv(h):
        return k_scr[:, h * LANES:(h + 1) * LANES], v1_scr[h]

    _diff_attn_heads(q_ref, head_kv, lam_ref, sub_ref, o_ref, lam_init)


def _even_attn_s(q, kl, vl, kc, vc, lam_vec, sub, *, seq, past, tq, lam_init, casts=()):
    t = q.shape[0]
    nq = seq // tq
    qblk = pl.BlockSpec((tq, A_QK), lambda b, i: (b * nq + i, 0))
    lat = pl.BlockSpec((seq, A_QK), lambda b, i: (b, 0))
    ctx = pl.BlockSpec((past, A_QK), lambda b, i: (b, 0))
    in_specs = [qblk, lat, lat, ctx, ctx, _resident((4, HEAD_DIM)), _resident((1, A_VDIM))]
    c_in, c_out, c_shapes = _cast_specs(casts, (t // seq) * nq, lambda b, i: b * nq + i)
    body = functools.partial(_even_attn_s_kernel, past=past, lam_init=lam_init)
    out = pl.pallas_call(
        _with_casts(body, len(in_specs), 1, len(casts)),
        out_shape=(jax.ShapeDtypeStruct((t, A_V), BF16), *c_shapes),
        grid=(t // seq, nq),
        in_specs=in_specs + c_in,
        out_specs=(qblk, *c_out),
        scratch_shapes=[pltpu.VMEM((past + seq, A_QK), BF16),
                        pltpu.VMEM((A_HEADS, past + seq, 2 * LANES), BF16)],
        compiler_params=_params("parallel", "arbitrary"),
        name="even_attn_sample",
    )(q, kl, vl, kc, vc, lam_vec, sub, *[arr for arr, _ in casts])
    return (out[0], list(out[1:])) if casts else out[0]


def _dup_halves(x):
    lo = _lane_lo(x.shape)
    sw = pltpu.roll(x, HEAD_DIM, 1)
    return jnp.where(lo, x, sw), jnp.where(lo, sw, x)


def _ones_halves(x):
    lo = _lane_lo(x.shape)
    return jnp.where(lo, x, 1.0), jnp.where(lo, pltpu.roll(x, HEAD_DIM, 1), 1.0)


def _odd_in_kernel(x_ref, mod_ref, g_ref, w_ref, qkg_ref, *rest, row_fn, rope, keep_kv, tm):
    rest = list(rest)
    if rope:
        cos_ref, sin_ref = rest[:2]
        rest = rest[2:]
    if keep_kv:
        q_ref, kd_ref, vd_ref, k_ref, v_ref = rest
    else:
        q_ref, kd_ref, vd_ref = rest
    nq = C_HEADS * HEAD_DIM
    nk = C_KV_HEADS * HEAD_DIM
    q_gain = qkg_ref[0:1, :] * Q_PRESCALE
    n = x_ref.shape[0] // tm
    ones2 = _half_sum_matrix()

    def body(j, row0):
        r = row_fn(pl.program_id(0) * n + j)
        for half in range(2):
            rows = pl.ds(_aligned(row0 + half * (tm // 2), tm // 2), tm // 2)
            h = _mod_norm(x_ref[rows, :], g_ref[...], _mod_row(mod_ref, 3, r), _mod_row(mod_ref, 4, r))
            y = _dot(h.astype(BF16), w_ref[...])
            if rope:
                cos, sin = cos_ref[rows, :], sin_ref[rows, :]
            for c in range(nq // LANES):
                sl = slice(c * LANES, (c + 1) * LANES)
                qs = _seg_rms_mxu(y[:, sl], q_gain, ones2)
                if rope:
                    qs = _rope(qs, cos, sin)
                q_ref[rows, sl] = qs.astype(q_ref.dtype)
            for c in range(nk // LANES):
                sl = slice(c * LANES, (c + 1) * LANES)
                ks = _seg_rms_mxu(y[:, nq + c * LANES:nq + (c + 1) * LANES], qkg_ref[1:2, :], ones2)
                vs = y[:, nq + nk + c * LANES:nq + nk + (c + 1) * LANES]
                if keep_kv:
                    k_ref[rows, sl] = ks
                    v_ref[rows, sl] = vs
                if rope:
                    ks = _rope(ks, cos, sin)
                for (d0, d1), ref in ((_dup_halves(ks), kd_ref), (_ones_halves(vs), vd_ref)):
                    ref[rows, 2 * c * LANES:(2 * c + 1) * LANES] = d0.astype(ref.dtype)
                    ref[rows, (2 * c + 1) * LANES:(2 * c + 2) * LANES] = d1.astype(ref.dtype)

    _for_chunks(x_ref.shape[0], tm, body)


def _odd_in(x, mod, g, w_in, qkg, rope_tabs, *, row_fn, tm, keep_kv):
    t = x.shape[0]
    rope = rope_tabs is not None
    block = min(t, IN_BLOCK_TILES * tm)
    in_specs = [
        pl.BlockSpec((block, D_MODEL), lambda i: (i, 0)),
        _resident((N_MOD, MOD_ROWS, D_MODEL)),
        _resident((1, D_MODEL)),
        _resident((D_MODEL, ODD_IN)),
        _resident((2, LANES)),
    ]
    args = [x, mod, g, w_in, qkg]
    if rope:
        nt = rope_tabs[0].shape[0] // block
        in_specs += [pl.BlockSpec((block, LANES), lambda i: (i % nt, 0))] * 2
        args += list(rope_tabs)
    nq = C_HEADS * HEAD_DIM
    nd = C_KV_HEADS * LANES
    nk = C_KV_HEADS * HEAD_DIM
    out_shape = [jax.ShapeDtypeStruct((t, nq), BF16), jax.ShapeDtypeStruct((t, nd), BF16),
                 jax.ShapeDtypeStruct((t, nd), BF16)]
    out_specs = [pl.BlockSpec((block, nq), lambda i: (i, 0)), pl.BlockSpec((block, nd), lambda i: (i, 0)),
                 pl.BlockSpec((block, nd), lambda i: (i, 0))]
    if keep_kv:
        out_shape += [jax.ShapeDtypeStruct((t, nk), F32)] * 2
        out_specs += [pl.BlockSpec((block, nk), lambda i: (i, 0))] * 2
    return pl.pallas_call(
        functools.partial(_odd_in_kernel, row_fn=row_fn, rope=rope, keep_kv=keep_kv, tm=tm),
        out_shape=tuple(out_shape),
        grid=(t // block,),
        in_specs=in_specs,
        out_specs=tuple(out_specs),
        compiler_params=_params("parallel"),
        name="odd_in",
    )(*args)


def _stack_heads(q_ref, g, rows):
    parts = []
    for rr in range(GROUP):
        c = g * (GROUP // 2) + rr // 2
        qc = q_ref[rows, c * LANES:(c + 1) * LANES]
        lo = _lane_lo(qc.shape)
        keep = lo if rr % 2 == 0 else jnp.logical_not(lo)
        parts.append(jnp.where(keep, qc, jnp.zeros_like(qc)))
    return jnp.concatenate(parts, axis=0)


def _gqa_group(scores, biases, vals, sinks, rows):
    es = [[] for _ in scores]
    sink_terms = []
    for rr in range(GROUP):
        rs = slice(rr * rows, (rr + 1) * rows)
        parts = [s[rs] if b is None else s[rs] + b for s, b in zip(scores, biases)]
        m = functools.reduce(jnp.maximum, [jnp.max(p, axis=-1, keepdims=True) for p in parts])
        m = jnp.maximum(m, sinks[rr])
        for j, p in enumerate(parts):
            es[j].append(jnp.exp2(p - m).astype(BF16))
        sink_terms.append(jnp.exp2(sinks[rr] - m))
    r = functools.reduce(jnp.add, [_dot(jnp.concatenate(e, axis=0), v) for e, v in zip(es, vals)])
    heads = []
    for rr in range(GROUP):
        rh = r[rr * rows:(rr + 1) * rows]
        heads.append(rh * (1.0 / (pltpu.roll(rh, HEAD_DIM, 1) + sink_terms[rr])))
    lo = _lane_lo((rows, LANES))
    slabs = [jnp.where(lo, heads[2 * c], pltpu.roll(heads[2 * c + 1], HEAD_DIM, 1)) for c in range(GROUP // 2)]
    return jnp.concatenate(slabs, axis=-1)


def _odd_attn_p_kernel(sink_ref, q_ref, kd_ref, vd_ref, o_ref, *, seq):
    g = pl.program_id(1)
    sinks = [sink_ref[g * GROUP + rr] * LOG2E for rr in range(GROUP)]
    units = [slice(u * seq, (u + 1) * seq) for u in range(q_ref.shape[0] // seq)]

    def scores(rows):
        return [_dot_t(_stack_heads(q_ref, 0, rows), kd_ref[rows, :])]

    s_next = scores(units[0])
    for n, rows in enumerate(units):
        s_cur = s_next
        if n + 1 < len(units):
            s_next = scores(units[n + 1])
        o_ref[rows, :] = _gqa_group(s_cur, [None], [vd_ref[rows, :]], sinks, seq).astype(o_ref.dtype)


def _odd_attn_p(q, kd, vd, sink, *, seq):
    t = q.shape[0]
    gw = GROUP * HEAD_DIM
    rows = seq * (ODD_P_SEQS if (t // seq) % ODD_P_SEQS == 0 else 1)
    return pl.pallas_call(
        functools.partial(_odd_attn_p_kernel, seq=seq),
        out_shape=jax.ShapeDtypeStruct((t, C_HEADS * HEAD_DIM), BF16),
        grid=(t // rows, C_KV_HEADS),
        in_specs=[
            pl.BlockSpec(memory_space=pltpu.SMEM),
            pl.BlockSpec((rows, gw), lambda b, g: (b, g)),
            pl.BlockSpec((rows, LANES), lambda b, g: (b, g)),
            pl.BlockSpec((rows, LANES), lambda b, g: (b, g)),
        ],
        out_specs=pl.BlockSpec((rows, gw), lambda b, g: (b, g)),
        compiler_params=_params("parallel", "parallel"),
        name="odd_attn_prompt",
    )(sink, q, kd, vd)


def _odd_attn_s_kernel(sink_ref, q_ref, kd_ref, vd_ref, kc_ref, vc_ref, bias_ref, o_ref, kcd_scr, vcd_scr, *, seq, tq):
    i = pl.program_id(1)
    sub = ODD_Q_SUB
    span = sub + 2 * WINDOW

    @pl.when(i == 0)
    def _():
        for c in range(C_KV_HEADS // 2):
            sl = slice(c * LANES, (c + 1) * LANES)
            for (d0, d1), dst in ((_dup_halves(kc_ref[:, sl]), kcd_scr), (_ones_halves(vc_ref[:, sl]), vcd_scr)):
                dst[:, 2 * c * LANES:(2 * c + 1) * LANES] = d0.astype(BF16)
                dst[:, (2 * c + 1) * LANES:(2 * c + 2) * LANES] = d1.astype(BF16)

    chunk = min(tq, ODD_Q_CHUNK)

    def body(j, row0):
        units = []
        for jb in range(chunk // sub):
            q0 = i * tq + row0 + jb * sub
            start = pl.multiple_of(jnp.clip(q0 - WINDOW, 0, seq - span), WINDOW)
            bias = bias_ref[(q0 - start) // WINDOW]
            rows = pl.ds(_aligned(row0 + jb * sub, sub), sub)
            units += [(rows, start, bias, g) for g in range(C_KV_HEADS)]

        def scores(unit):
            rows, start, _, g = unit
            gl = slice(g * LANES, (g + 1) * LANES)
            q4 = _stack_heads(q_ref, g, rows)
            return [_dot_t(q4, kcd_scr[:, gl]), _dot_t(q4, kd_ref[pl.ds(start, span), gl])]

        s_next = scores(units[0])
        for n, (rows, start, bias, g) in enumerate(units):
            s_cur = s_next
            if n + 1 < len(units):
                s_next = scores(units[n + 1])
            gl = slice(g * LANES, (g + 1) * LANES)
            sinks = [sink_ref[g * GROUP + rr] * LOG2E for rr in range(GROUP)]
            og = _gqa_group(s_cur, [None, bias], [vcd_scr[:, gl], vd_ref[pl.ds(start, span), gl]], sinks, sub)
            o_ref[rows, g * GROUP * HEAD_DIM:(g + 1) * GROUP * HEAD_DIM] = og.astype(o_ref.dtype)

    _for_chunks(tq, chunk, body)


def _odd_attn_s(q, kd, vd, kc, vc, sink, *, seq, past, tq):
    t = q.shape[0]
    nq = seq // tq
    nqw = C_HEADS * HEAD_DIM
    nd = C_KV_HEADS * LANES
    nk = C_KV_HEADS * HEAD_DIM
    span = ODD_Q_SUB + 2 * WINDOW
    rel = (np.arange(3)[:, None, None] * WINDOW + np.arange(ODD_Q_SUB)[None, :, None]
           - np.arange(span)[None, None, :])
    bias = jnp.asarray(np.where(np.abs(rel) <= WINDOW, 0.0, -np.inf).astype(np.float32))
    return pl.pallas_call(
        functools.partial(_odd_attn_s_kernel, seq=seq, tq=tq),
        out_shape=jax.ShapeDtypeStruct((t, nqw), BF16),
        grid=(t // seq, nq),
        in_specs=[
            pl.BlockSpec(memory_space=pltpu.SMEM),
            pl.BlockSpec((tq, nqw), lambda b, i: (b * nq + i, 0)),
            pl.BlockSpec((seq, nd), lambda b, i: (b, 0)),
            pl.BlockSpec((seq, nd), lambda b, i: (b, 0)),
            pl.BlockSpec((past, nk), lambda b, i: (b, 0)),
            pl.BlockSpec((past, nk), lambda b, i: (b, 0)),
            _resident(bias.shape),
        ],
        out_specs=pl.BlockSpec((tq, nqw), lambda b, i: (b * nq + i, 0)),
        scratch_shapes=[pltpu.VMEM((past, nd), BF16), pltpu.VMEM((past, nd), BF16)],
        compiler_params=_params("parallel", "arbitrary"),
        name="odd_attn_sample",
    )(sink, q, kd, vd, kc, vc, bias)


def _rope_tables(n):
    rows = n // GRID_W
    row = np.repeat(np.arange(rows, dtype=np.float64), GRID_W)
    col = np.tile(np.arange(GRID_W, dtype=np.float64), rows)
    inv = ROPE_BASE ** (-np.arange(0, AXIS_DIM, 2, dtype=np.float64) / AXIS_DIM)
    ang_r = row[:, None] * inv[None, :]
    ang_c = col[:, None] * inv[None, :]
    ang = np.concatenate([ang_r, ang_r, ang_c, ang_c], axis=-1)
    sign = np.where((np.arange(HEAD_DIM) & (AXIS_DIM // 2)) == 0, -1.0, 1.0)
    cos = np.tile(np.cos(ang), (1, 2)).astype(np.float32)
    sin = np.tile(np.sin(ang) * sign[None, :], (1, 2)).astype(np.float32)
    return jnp.asarray(cos), jnp.asarray(sin)


def kernel(x_prompt, x_sample, cache_even_k, cache_even_v, cache_odd_k, cache_odd_v, c, c_ctx, w_mod, b_mod, norm_g, ffn_w13, ffn_w2, even_w_in, even_w_out, even_qk_norm, even_lambda, even_subln, even_conv_w, odd_w_in, odd_w_out, odd_qk_norm, odd_sink):
    batch, seq, _ = x_prompt.shape
    dec_batch, dec_seq, _ = x_sample.shape
    past = cache_even_k.shape[2]
    depth = w_mod.shape[0]
    assert 1 + dec_batch <= MOD_ROWS and seq % SUBLANES == 0 and dec_seq % TOKEN_TILE == 0

    xp = x_prompt.reshape(batch * seq, D_MODEL)
    xs = x_sample.reshape(dec_batch * dec_seq, D_MODEL)
    tm_p = min(TOKEN_TILE, batch * seq)
    row_p = _row_fn(0, batch * seq, tm_p)
    row_s = _row_fn(1, dec_seq, TOKEN_TILE)

    c_rows = jnp.zeros((MOD_ROWS, D_MODEL), F32).at[0].set(c_ctx).at[1:1 + dec_batch].set(c)
    mods = _adaln(c_rows, w_mod, b_mod)
    rope_tabs = _rope_tables(dec_seq)

    src = {}
    for l in range(depth):
        for s in range(2):
            src["w13", l, s] = (ffn_w13, (l, s))
            src["w2", l, s] = (ffn_w2, (l, s))
        names, w_in_all, w_out_all = (("ein", "eout"), even_w_in, even_w_out) if l % 2 == 0 else (
            ("oin", "oout"), odd_w_in, odd_w_out)
        src[names[0], l // 2] = (w_in_all, (l // 2,))
        src[names[1], l // 2] = (w_out_all, (l // 2,))
    wb = {("w13", 0, 0): ffn_w13[0, 0].astype(BF16), ("w2", 0, 0): ffn_w2[0, 0].astype(BF16)}
    first = [("w13", 0, 1), ("w2", 0, 1), ("ein", 0), ("eout", 0)]
    rest = [key for key in src if key not in wb and key not in first]

    even_k, even_v, odd_k, odd_v = [], [], [], []
    for l in range(depth):
        mod = mods[l]
        g = [norm_g[l, s].reshape(1, D_MODEL) for s in range(3)]

        ffn_p = functools.partial(_ffn, mod=mod, row_fn=row_p, tm=tm_p)
        ffn_s = functools.partial(_ffn, mod=mod, row_fn=row_s, tm=TOKEN_TILE)
        xp = ffn_p(xp, g=g[0], w13=wb["w13", l, 0], w2=wb["w2", l, 0])
        if l == 0:
            xs, copies = ffn_s(xs, g=g[0], w13=wb["w13", l, 0], w2=wb["w2", l, 0], casts=[src[key] for key in first])
            wb.update(zip(first, copies))
        else:
            xs = ffn_s(xs, g=g[0], w13=wb["w13", l, 0], w2=wb["w2", l, 0])
        w13b, w2b = wb["w13", l, 1], wb["w2", l, 1]

        if l % 2 == 0:
            e = l // 2
            lam_init = 0.8 - 0.6 * math.exp(-0.3 * l)
            w_in, w_out = wb["ein", e], wb["eout", e]
            qkg = jnp.tile(even_qk_norm[e], (1, 2))
            sub = even_subln[e].reshape(1, A_VDIM)

            q, k, v, bg, u = _even_in(xp, mod, g[1], w_in, qkg, None, row_fn=row_p, tm=tm_p, kv_dtype=F32)
            o = _even_attn_p(q, k, v, even_lambda[e], sub, seq=seq, lam_init=lam_init)
            xp = ffn_p(xp, g=g[2], w13=w13b, w2=w2b, mixer=(w_out, o, bg, u, even_conv_w[e], seq))
            even_k.append(k.reshape(batch, seq, A_HEADS, 2 * HEAD_DIM))
            even_v.append(v.reshape(batch, seq, A_HEADS, A_VDIM))

            q, k, v, bg, u = _even_in(xs, mod, g[1], w_in, qkg, rope_tabs, row_fn=row_s, tm=TOKEN_TILE,
                                      kv_dtype=BF16)
            kc = cache_even_k[:, e].reshape(dec_batch * past, A_QK)
            vc = cache_even_v[:, e].reshape(dec_batch * past, A_V)
            casts = [src[key] for key in rest] if l == 0 else []
            o = _even_attn_s(q, k, v, kc, vc, even_lambda[e], sub, seq=dec_seq, past=past, tq=EVEN_Q_TILE,
                             lam_init=lam_init, casts=casts)
            if casts:
                o, copies = o
                wb.update(zip(rest, copies))
            xs = ffn_s(xs, g=g[2], w13=w13b, w2=w2b, mixer=(w_out, o, bg, u, even_conv_w[e], dec_seq))
        else:
            e = l // 2
            w_in, w_out = wb["oin", e], wb["oout", e]
            qkg = jnp.tile(odd_qk_norm[e], (1, 2))
            sink = odd_sink[e]

            q, kd, vd, k, v = _odd_in(xp, mod, g[1], w_in, qkg, None, row_fn=row_p, tm=tm_p, keep_kv=True)
            o = _odd_attn_p(q, kd, vd, sink, seq=seq)
            xp = ffn_p(xp, g=g[2], w13=w13b, w2=w2b, mixer=(w_out, o))
            odd_k.append(k.reshape(batch, seq, C_KV_HEADS, HEAD_DIM))
            odd_v.append(v.reshape(batch, seq, C_KV_HEADS, HEAD_DIM))

            q, kd, vd = _odd_in(xs, mod, g[1], w_in, qkg, rope_tabs, row_fn=row_s, tm=TOKEN_TILE, keep_kv=False)
            kc = cache_odd_k[:, e].reshape(dec_batch * past, C_KV_HEADS * HEAD_DIM)
            vc = cache_odd_v[:, e].reshape(dec_batch * past, C_KV_HEADS * HEAD_DIM)
            o = _odd_attn_s(q, kd, vd, kc, vc, sink, seq=dec_seq, past=past, tq=ODD_Q_TILE)
            xs = ffn_s(xs, g=g[2], w13=w13b, w2=w2b, mixer=(w_out, o))

    return (xp.reshape(batch, seq, D_MODEL), xs.reshape(dec_batch, dec_seq, D_MODEL),
            jnp.stack(even_k, axis=1), jnp.stack(even_v, axis=1),
            jnp.stack(odd_k, axis=1), jnp.stack(odd_v, axis=1))
```

```python
import functools
import math

import jax
import jax.numpy as jnp
import numpy as np
from jax import lax
from jax.experimental import pallas as pl
from jax.experimental.pallas import tpu as pltpu

F32 = jnp.float32
BF16 = jnp.bfloat16

D_MODEL = 1024
GRID_W = 64
HEAD_DIM = 64
AXIS_DIM = HEAD_DIM // 2
A_HEADS = 4
A_VDIM = 2 * HEAD_DIM
CONV_DIM = 512
C_HEADS = 16
C_KV_HEADS = 4
GROUP = C_HEADS // C_KV_HEADS
WINDOW = 128
D_FF = 2816
ROPE_BASE = 10000.0
N_MOD = 9
EPS = 1e-6
A_QK = A_HEADS * 2 * HEAD_DIM
A_V = A_HEADS * A_VDIM
EVEN_IN = 2 * A_QK + A_V + 3 * CONV_DIM
ODD_IN = (C_HEADS + 2 * C_KV_HEADS) * HEAD_DIM
LOG2E = math.log2(math.e)
Q_PRESCALE = HEAD_DIM ** -0.5 * LOG2E

LANES = 128
SUBLANES = 8
BF16_SUBLANES = 16
MOD_ROWS = 16
VMEM_LIMIT_BYTES = 56 * 2 ** 20

TOKEN_TILE = 512
EVEN_Q_TILE = 512
EVEN_Q_SUB = 256
ODD_Q_TILE = 512
ODD_Q_SUB = WINDOW
ODD_P_SEQS = 2


def _params(*semantics):
    return pltpu.CompilerParams(dimension_semantics=semantics, vmem_limit_bytes=VMEM_LIMIT_BYTES)


def _sigmoid(x):
    return 1.0 / (1.0 + jnp.exp(-x))


def _mod_norm(x, g, shift, scale):
    y = x * lax.rsqrt(jnp.mean(x * x, axis=-1, keepdims=True) + EPS)
    return (y * g) * (1.0 + scale) + shift


def _mod_row(mod_ref, k, r):
    return mod_ref[k, pl.ds(r, 1), :]


def _row_fn(row0, tokens_per_row, tile):
    return lambda i: row0 + (i * tile) // tokens_per_row


def _lane_lo(shape):
    return lax.broadcasted_iota(jnp.int32, shape, len(shape) - 1) < HEAD_DIM


def _seg_rms(xs, gain):
    lo = _lane_lo(xs.shape)
    sq = xs * xs
    s_lo = jnp.sum(jnp.where(lo, sq, 0.0), axis=-1, keepdims=True)
    s_hi = jnp.sum(jnp.where(lo, 0.0, sq), axis=-1, keepdims=True)
    inv = lax.rsqrt(jnp.where(lo, s_lo, s_hi) * (1.0 / HEAD_DIM) + EPS)
    return (xs * inv) * gain


def _half_sum_matrix():
    r = lax.broadcasted_iota(jnp.int32, (2 * LANES, LANES), 0)
    c = lax.broadcasted_iota(jnp.int32, (2 * LANES, LANES), 1)
    return jnp.where((r & HEAD_DIM) == (c & HEAD_DIM), 1.0, 0.0).astype(BF16)


def _seg_rms_mxu(xs, gain, ones2):
    sq = xs * xs
    head = sq.astype(BF16)
    rest = (sq - head.astype(F32)).astype(BF16)
    ssum = _dot(jnp.concatenate([head, rest], axis=-1), ones2)
    return (xs * lax.rsqrt(ssum * (1.0 / HEAD_DIM) + EPS)) * gain


def _rope(xs, cos, sin_signed):
    lane = lax.broadcasted_iota(jnp.int32, xs.shape, 1)
    first = (lane & (AXIS_DIM // 2)) == 0
    partner = jnp.where(first, pltpu.roll(xs, LANES - AXIS_DIM // 2, 1), pltpu.roll(xs, AXIS_DIM // 2, 1))
    return xs * cos + partner * sin_signed


def _dot_t(a, b):
    return lax.dot_general(a, b, (((1,), (1,)), ((), ())), preferred_element_type=F32)


def _dot(a, b):
    return jnp.dot(a, b, preferred_element_type=F32)


def _adaln_kernel(c_ref, w_ref, b_ref, o_ref):
    c = c_ref[...]
    s = c * _sigmoid(c)
    o_ref[0, 0] = _dot(s.astype(BF16), w_ref[0].astype(BF16)) + b_ref[0]


def _adaln(c_rows, w_mod, b_mod):
    depth = w_mod.shape[0]
    b3 = b_mod.reshape(depth * N_MOD, 1, D_MODEL)
    return pl.pallas_call(
        _adaln_kernel,
        out_shape=jax.ShapeDtypeStruct((depth, N_MOD, MOD_ROWS, D_MODEL), F32),
        grid=(depth, N_MOD),
        in_specs=[
            pl.BlockSpec((MOD_ROWS, D_MODEL), lambda l, j: (0, 0)),
            pl.BlockSpec((1, D_MODEL, D_MODEL), lambda l, j: (l, 0, j)),
            pl.BlockSpec((1, 1, D_MODEL), lambda l, j: (l * N_MOD + j, 0, 0)),
        ],
        out_specs=pl.BlockSpec((1, 1, MOD_ROWS, D_MODEL), lambda l, j: (l, j, 0, 0)),
        compiler_params=_params("parallel", "parallel"),
        name="adaln",
    )(c_rows, w_mod, b3)


def _ffn_tail(x, r, mod_ref, g_ref, w13_ref, w2_ref, k0):
    h = _mod_norm(x, g_ref[...], _mod_row(mod_ref, k0, r), _mod_row(mod_ref, k0 + 1, r))
    a = _dot(h.astype(BF16), w13_ref[...])
    gate = a[:, :D_FF]
    up = a[:, D_FF:]
    act = (gate * _sigmoid(gate)) * up
    y = _dot(act.astype(BF16), w2_ref[...])
    return x + (0.5 * _mod_row(mod_ref, k0 + 2, r)) * y


def _ffn_kernel(x_ref, mod_ref, g_ref, w13_ref, w2_ref, o_ref, *, row_fn):
    o_ref[...] = _ffn_tail(x_ref[...], row_fn(pl.program_id(0)), mod_ref, g_ref, w13_ref, w2_ref, 0)


def _conv_gate(bg_ref, u_ref, up_ref, un_ref, cw_ref, i, seq, tm):
    u = u_ref[...]
    row = lax.broadcasted_iota(jnp.int32, (tm, 1), 0)
    pos = (row + i * tm) % seq
    u_dn = jnp.where(row == 0, up_ref[SUBLANES - 1:SUBLANES, :], pltpu.roll(u, 1, 0))
    u_dn = jnp.where(pos == 0, 0.0, u_dn)
    u_up = jnp.where(row == tm - 1, un_ref[0:1, :], pltpu.roll(u, tm - 1, 0))
    u_up = jnp.where(pos == seq - 1, 0.0, u_up)
    return bg_ref[...] * (cw_ref[0:1, :] * u_dn + cw_ref[1:2, :] * u + cw_ref[2:3, :] * u_up)


def _even_out_ffn_kernel(x_ref, a_ref, bg_ref, u_ref, up_ref, un_ref, cw_ref, wo_ref, mod_ref, g_ref, w13_ref, w2_ref,
                         o_ref, *, row_fn, seq, tm):
    i = pl.program_id(0)
    r = row_fn(i)
    y = _conv_gate(bg_ref, u_ref, up_ref, un_ref, cw_ref, i, seq, tm)
    mix = _dot(jnp.concatenate([a_ref[...], y.astype(BF16)], axis=-1), wo_ref[...])
    x = x_ref[...] + _mod_row(mod_ref, 5, r) * mix
    o_ref[...] = _ffn_tail(x, r, mod_ref, g_ref, w13_ref, w2_ref, 6)


def _odd_out_ffn_kernel(x_ref, a_ref, wo_ref, mod_ref, g_ref, w13_ref, w2_ref, o_ref, *, row_fn):
    r = row_fn(pl.program_id(0))
    x = x_ref[...] + _mod_row(mod_ref, 5, r) * _dot(a_ref[...], wo_ref[...])
    o_ref[...] = _ffn_tail(x, r, mod_ref, g_ref, w13_ref, w2_ref, 6)


def _resident(shape, index=None):
    index = (0,) * len(shape) if index is None else index
    return pl.BlockSpec(shape, lambda *_: index, pipeline_mode=pl.Buffered(1))


def _cast_specs(casts, steps, flat_step=lambda i: i):
    in_specs, out_specs, out_shapes = [], [], []
    for arr, lead in casts:
        rows_total, cols = arr.shape[-2:]
        nblk = next(d for d in range(steps, 0, -1)
                    if steps % d == 0 and rows_total % d == 0 and (rows_total // d) % BF16_SUBLANES == 0)
        rows, rep = rows_total // nblk, steps // nblk
        in_specs.append(pl.BlockSpec((1,) * len(lead) + (rows, cols),
                                     lambda *idx, lead=lead, rep=rep: lead + (flat_step(*idx) // rep, 0)))
        out_specs.append(pl.BlockSpec((rows, cols), lambda *idx, rep=rep: (flat_step(*idx) // rep, 0)))
        out_shapes.append(jax.ShapeDtypeStruct((rows_total, cols), BF16))
    return in_specs, out_specs, out_shapes


def _with_casts(body, n_in, n_out, n_cast):
    if n_cast == 0:
        return body

    def kernel(*refs):
        outs = n_in + n_cast
        body(*refs[:n_in], *refs[outs:outs + n_out], *refs[outs + n_out + n_cast:])
        for src, dst in zip(refs[n_in:outs], refs[outs + n_out:outs + n_out + n_cast]):
            dst[...] = src[(0,) * (len(src.shape) - 2)].astype(dst.dtype)

    return kernel


def _ffn(x, mod, g, w13, w2, *, row_fn, tm, mixer=None, casts=()):
    t = x.shape[0]
    tile = lambda n: pl.BlockSpec((tm, n), lambda i: (i, 0))
    ffn_specs = [
        _resident((N_MOD, MOD_ROWS, D_MODEL)),
        _resident((1, D_MODEL)),
        _resident((D_MODEL, 2 * D_FF)),
        _resident((D_FF, D_MODEL)),
    ]
    ffn_args = [mod, g, w13, w2]
    if mixer is None:
        body = functools.partial(_ffn_kernel, row_fn=row_fn)
        in_specs, args, name = [tile(D_MODEL)], [x], "ffn"
    elif len(mixer) == 2:
        w_out, attn = mixer
        body = functools.partial(_odd_out_ffn_kernel, row_fn=row_fn)
        in_specs = [tile(D_MODEL), tile(attn.shape[1]), _resident(w_out.shape)]
        args, name = [x, attn, w_out], "odd_out_ffn"
    else:
        w_out, attn, bg, u, conv_w, seq = mixer
        nb = tm // SUBLANES
        last = t // SUBLANES - 1
        body = functools.partial(_even_out_ffn_kernel, row_fn=row_fn, seq=seq, tm=tm)
        in_specs = [
            tile(D_MODEL), tile(A_V), tile(CONV_DIM), tile(CONV_DIM),
            pl.BlockSpec((SUBLANES, CONV_DIM), lambda i: (jnp.maximum(i * nb - 1, 0), 0)),
            pl.BlockSpec((SUBLANES, CONV_DIM), lambda i: (jnp.minimum((i + 1) * nb, last), 0)),
            _resident(conv_w.shape), _resident(w_out.shape),
        ]
        args, name = [x, attn, bg, u, u, u, conv_w, w_out], "even_out_ffn"
    steps = t // tm
    c_in, c_out, c_shapes = _cast_specs(casts, steps)
    in_specs = in_specs + ffn_specs
    out = pl.pallas_call(
        _with_casts(body, len(in_specs), 1, len(casts)),
        out_shape=(jax.ShapeDtypeStruct((t, D_MODEL), F32), *c_shapes),
        grid=(steps,),
        in_specs=in_specs + c_in,
        out_specs=(tile(D_MODEL), *c_out),
        compiler_params=_params("parallel"),
        name=name,
    )(*args, *ffn_args, *[arr for arr, _ in casts])
    return (out[0], list(out[1:])) if casts else out[0]


def _even_in_kernel(x_ref, mod_ref, g_ref, w_ref, qkg_ref, *rest, row_fn, rope):
    if rope:
        cos_ref, sin_ref, q_ref, k_ref, v_ref, bg_ref, u_ref = rest
    else:
        q_ref, k_ref, v_ref, bg_ref, u_ref = rest
    r = row_fn(pl.program_id(0))
    h = _mod_norm(x_ref[...], g_ref[...], _mod_row(mod_ref, 3, r), _mod_row(mod_ref, 4, r))
    y = _dot(h.astype(BF16), w_ref[...])
    q_gain = qkg_ref[0:1, :] * Q_PRESCALE
    for hh in range(A_HEADS):
        sl = slice(hh * LANES, (hh + 1) * LANES)
        qs = _seg_rms(y[:, sl], q_gain)
        ks = _seg_rms(y[:, A_QK + hh * LANES:A_QK + (hh + 1) * LANES], qkg_ref[1:2, :])
        if rope:
            qs = _rope(qs, cos_ref[...], sin_ref[...])
            ks = _rope(ks, cos_ref[...], sin_ref[...])
        q_ref[:, sl] = qs.astype(q_ref.dtype)
        k_ref[:, sl] = ks.astype(k_ref.dtype)
    v_ref[...] = y[:, 2 * A_QK:2 * A_QK + A_V].astype(v_ref.dtype)
    o = 2 * A_QK + A_V
    bg_ref[...] = y[:, o:o + CONV_DIM]
    u_ref[...] = y[:, o + CONV_DIM:o + 2 * CONV_DIM] * y[:, o + 2 * CONV_DIM:o + 3 * CONV_DIM]


def _even_in(x, mod, g, w_in, qkg, rope_tabs, *, row_fn, tm, kv_dtype):
    t = x.shape[0]
    rope = rope_tabs is not None
    block = tm
    in_specs = [
        pl.BlockSpec((block, D_MODEL), lambda i: (i, 0)),
        _resident((N_MOD, MOD_ROWS, D_MODEL)),
        _resident((1, D_MODEL)),
        _resident((D_MODEL, EVEN_IN)),
        _resident((2, LANES)),
    ]
    args = [x, mod, g, w_in, qkg]
    if rope:
        nt = rope_tabs[0].shape[0] // block
        in_specs += [pl.BlockSpec((block, LANES), lambda i: (i % nt, 0))] * 2
        args += list(rope_tabs)
    wide = lambda dt: jax.ShapeDtypeStruct((t, A_QK), dt)
    spec = pl.BlockSpec((block, A_QK), lambda i: (i, 0))
    return pl.pallas_call(
        functools.partial(_even_in_kernel, row_fn=row_fn, rope=rope),
        out_shape=(wide(BF16), wide(kv_dtype), wide(kv_dtype), wide(F32), wide(F32)),
        grid=(t // block,),
        in_specs=in_specs,
        out_specs=(spec,) * 5,
        compiler_params=_params("parallel"),
        name="even_in",
    )(*args)


def _lambda(lam_ref, lam_init):
    lf = lam_ref[...]
    a = jnp.sum(lf[0:1] * lf[1:2], axis=-1, keepdims=True)
    b = jnp.sum(lf[2:3] * lf[3:4], axis=-1, keepdims=True)
    return jnp.exp(a) - jnp.exp(b) + lam_init


def _subln(o, sub, lam_init):
    y = o * lax.rsqrt(jnp.mean(o * o, axis=-1, keepdims=True) + EPS)
    return (y * sub) * (1.0 - lam_init)


def _softmax_rows_pv(s, v1):
    m = jnp.max(s, axis=-1, keepdims=True)
    r = _dot(jnp.exp2(s - m).astype(BF16), v1)
    return r[:, :LANES] * (1.0 / r[:, LANES:])


def _diff_attn_heads(q_ref, head_kv, lam_ref, sub_ref, o_ref, lam_init):
    lam = _lambda(lam_ref, lam_init)
    tq = q_ref.shape[0]
    sub_rows = min(tq, EVEN_Q_SUB)
    units = [(slice(rb * sub_rows, (rb + 1) * sub_rows), h)
             for rb in range(tq // sub_rows) for h in range(A_HEADS)]

    def scores(unit):
        rows, h = unit
        q = q_ref[rows, h * LANES:(h + 1) * LANES]
        k, _ = head_kv(h)
        lo = _lane_lo(q.shape)
        zero = jnp.zeros_like(q)
        return _dot_t(jnp.where(lo, q, zero), k), _dot_t(jnp.where(lo, zero, q), k)

    s_next = scores(units[0])
    for n, (rows, h) in enumerate(units):
        s1, s2 = s_next
        if n + 1 < len(units):
            s_next = scores(units[n + 1])
        _, v1 = head_kv(h)
        o = _softmax_rows_pv(s1, v1) - lam * _softmax_rows_pv(s2, v1)
        o_ref[rows, h * LANES:(h + 1) * LANES] = _subln(o, sub_ref[...], lam_init).astype(o_ref.dtype)


def _even_attn_p_kernel(q_ref, k_ref, v_ref, lam_ref, sub_ref, o_ref, *, lam_init):
    ones = jnp.ones((k_ref.shape[0], LANES), BF16)

    def head_kv(h):
        sl = slice(h * LANES, (h + 1) * LANES)
        return k_ref[:, sl].astype(BF16), jnp.concatenate([v_ref[:, sl].astype(BF16), ones], axis=-1)

    _diff_attn_heads(q_ref, head_kv, lam_ref, sub_ref, o_ref, lam_init)


def _even_attn_p(q, k, v, lam_vec, sub, *, seq, lam_init):
    t = q.shape[0]
    blk = pl.BlockSpec((seq, A_QK), lambda b: (b, 0))
    return pl.pallas_call(
        functools.partial(_even_attn_p_kernel, lam_init=lam_init),
        out_shape=jax.ShapeDtypeStruct((t, A_V), BF16),
        grid=(t // seq,),
        in_specs=[blk, blk, blk, _resident((4, HEAD_DIM)), _resident((1, A_VDIM))],
        out_specs=blk,
        compiler_params=_params("parallel"),
        name="even_attn_prompt",
    )(q, k, v, lam_vec, sub)


def _even_attn_s_kernel(q_ref, kl_ref, vl_ref, kc_ref, vc_ref, lam_ref, sub_ref, o_ref, k_scr, v1_scr, *, past, lam_init):
    @pl.when(pl.program_id(1) == 0)
    def _():
        k_scr[:past, :] = kc_ref[...].astype(BF16)
        k_scr[past:, :] = kl_ref[...]
        for h in range(A_HEADS):
            sl = slice(h * LANES, (h + 1) * LANES)
            v1_scr[h, :past, :LANES] = vc_ref[:, sl].astype(BF16)
            v1_scr[h, past:, :LANES] = vl_ref[:, sl]
            v1_scr[h, :, LANES:] = jnp.ones((v1_scr.shape[1], LANES), BF16)

    def head_kv(h):
        return k_scr[:, h * LANES:(h + 1) * LANES], v1_scr[h]

    _diff_attn_heads(q_ref, head_kv, lam_ref, sub_ref, o_ref, lam_init)


def _even_attn_s(q, kl, vl, kc, vc, lam_vec, sub, *, seq, past, tq, lam_init, casts=()):
    t = q.shape[0]
    nq = seq // tq
    qblk = pl.BlockSpec((tq, A_QK), lambda b, i: (b * nq + i, 0))
    lat = pl.BlockSpec((seq, A_QK), lambda b, i: (b, 0))
    ctx = pl.BlockSpec((past, A_QK), lambda b, i: (b, 0))
    in_specs = [qblk, lat, lat, ctx, ctx, _resident((4, HEAD_DIM)), _resident((1, A_VDIM))]
    c_in, c_out, c_shapes = _cast_specs(casts, (t // seq) * nq, lambda b, i: b * nq + i)
    body = functools.partial(_even_attn_s_kernel, past=past, lam_init=lam_init)
    out = pl.pallas_call(
        _with_casts(body, len(in_specs), 1, len(casts)),
        out_shape=(jax.ShapeDtypeStruct((t, A_V), BF16), *c_shapes),
        grid=(t // seq, nq),
        in_specs=in_specs + c_in,
        out_specs=(qblk, *c_out),
        scratch_shapes=[pltpu.VMEM((past + seq, A_QK), BF16),
                        pltpu.VMEM((A_HEADS, past + seq, 2 * LANES), BF16)],
        compiler_params=_params("parallel", "arbitrary"),
        name="even_attn_sample",
    )(q, kl, vl, kc, vc, lam_vec, sub, *[arr for arr, _ in casts])
    return (out[0], list(out[1:])) if casts else out[0]


def _dup_halves(x):
    lo = _lane_lo(x.shape)
    sw = pltpu.roll(x, HEAD_DIM, 1)
    return jnp.where(lo, x, sw), jnp.where(lo, sw, x)


def _ones_halves(x):
    lo = _lane_lo(x.shape)
    return jnp.where(lo, x, 1.0), jnp.where(lo, pltpu.roll(x, HEAD_DIM, 1), 1.0)


def _odd_in_kernel(x_ref, mod_ref, g_ref, w_ref, qkg_ref, *rest, row_fn, rope, keep_kv):
    rest = list(rest)
    if rope:
        cos_ref, sin_ref = rest[:2]
        rest = rest[2:]
    if keep_kv:
        q_ref, kd_ref, vd_ref, k_ref, v_ref = rest
    else:
        q_ref, kd_ref, vd_ref = rest
    nq = C_HEADS * HEAD_DIM
    nk = C_KV_HEADS * HEAD_DIM
    q_gain = qkg_ref[0:1, :] * Q_PRESCALE
    tm = x_ref.shape[0]
    ones2 = _half_sum_matrix()
    r = row_fn(pl.program_id(0))
    for rows in (slice(0, tm // 2), slice(tm // 2, tm)):
        h = _mod_norm(x_ref[rows, :], g_ref[...], _mod_row(mod_ref, 3, r), _mod_row(mod_ref, 4, r))
        y = _dot(h.astype(BF16), w_ref[...])
        if rope:
            cos, sin = cos_ref[rows, :], sin_ref[rows, :]
        for c in range(nq // LANES):
            sl = slice(c * LANES, (c + 1) * LANES)
            qs = _seg_rms_mxu(y[:, sl], q_gain, ones2)
            if rope:
                qs = _rope(qs, cos, sin)
            q_ref[rows, sl] = qs.astype(q_ref.dtype)
        for c in range(nk // LANES):
            sl = slice(c * LANES, (c + 1) * LANES)
            ks = _seg_rms_mxu(y[:, nq + c * LANES:nq + (c + 1) * LANES], qkg_ref[1:2, :], ones2)
            vs = y[:, nq + nk + c * LANES:nq + nk + (c + 1) * LANES]
            if keep_kv:
                k_ref[rows, sl] = ks
                v_ref[rows, sl] = vs
            if rope:
                ks = _rope(ks, cos, sin)
            for (d0, d1), ref in ((_dup_halves(ks), kd_ref), (_ones_halves(vs), vd_ref)):
                ref[rows, 2 * c * LANES:(2 * c + 1) * LANES] = d0.astype(ref.dtype)
                ref[rows, (2 * c + 1) * LANES:(2 * c + 2) * LANES] = d1.astype(ref.dtype)


def _odd_in(x, mod, g, w_in, qkg, rope_tabs, *, row_fn, tm, keep_kv):
    t = x.shape[0]
    rope = rope_tabs is not None
    block = tm
    in_specs = [
        pl.BlockSpec((block, D_MODEL), lambda i: (i, 0)),
        _resident((N_MOD, MOD_ROWS, D_MODEL)),
        _resident((1, D_MODEL)),
        _resident((D_MODEL, ODD_IN)),
        _resident((2, LANES)),
    ]
    args = [x, mod, g, w_in, qkg]
    if rope:
        nt = rope_tabs[0].shape[0] // block
        in_specs += [pl.BlockSpec((block, LANES), lambda i: (i % nt, 0))] * 2
        args += list(rope_tabs)
    nq = C_HEADS * HEAD_DIM
    nd = C_KV_HEADS * LANES
    nk = C_KV_HEADS * HEAD_DIM
    out_shape = [jax.ShapeDtypeStruct((t, nq), BF16), jax.ShapeDtypeStruct((t, nd), BF16),
                 jax.ShapeDtypeStruct((t, nd), BF16)]
    out_specs = [pl.BlockSpec((block, nq), lambda i: (i, 0)), pl.BlockSpec((block, nd), lambda i: (i, 0)),
                 pl.BlockSpec((block, nd), lambda i: (i, 0))]
    if keep_kv:
        out_shape += [jax.ShapeDtypeStruct((t, nk), F32)] * 2
        out_specs += [pl.BlockSpec((block, nk), lambda i: (i, 0))] * 2
    return pl.pallas_call(
        functools.partial(_odd_in_kernel, row_fn=row_fn, rope=rope, keep_kv=keep_kv),
        out_shape=tuple(out_shape),
        grid=(t // block,),
        in_specs=in_specs,
        out_specs=tuple(out_specs),
        compiler_params=_params("parallel"),
        name="odd_in",
    )(*args)


def _stack_heads(q_ref, g, rows):
    parts = []
    for rr in range(GROUP):
        c = g * (GROUP // 2) + rr // 2
        qc = q_ref[rows, c * LANES:(c + 1) * LANES]
        lo = _lane_lo(qc.shape)
        keep = lo if rr % 2 == 0 else jnp.logical_not(lo)
        parts.append(jnp.where(keep, qc, jnp.zeros_like(qc)))
    return jnp.concatenate(parts, axis=0)


def _gqa_group(scores, biases, vals, sinks, rows):
    es = [[] for _ in scores]
    sink_terms = []
    for rr in range(GROUP):
        rs = slice(rr * rows, (rr + 1) * rows)
        parts = [s[rs] if b is None else s[rs] + b for s, b in zip(scores, biases)]
        m = functools.reduce(jnp.maximum, [jnp.max(p, axis=-1, keepdims=True) for p in parts])
        m = jnp.maximum(m, sinks[rr])
        for j, p in enumerate(parts):
            es[j].append(jnp.exp2(p - m).astype(BF16))
        sink_terms.append(jnp.exp2(sinks[rr] - m))
    r = functools.reduce(jnp.add, [_dot(jnp.concatenate(e, axis=0), v) for e, v in zip(es, vals)])
    heads = []
    for rr in range(GROUP):
        rh = r[rr * rows:(rr + 1) * rows]
        heads.append(rh * (1.0 / (pltpu.roll(rh, HEAD_DIM, 1) + sink_terms[rr])))
    lo = _lane_lo((rows, LANES))
    slabs = [jnp.where(lo, heads[2 * c], pltpu.roll(heads[2 * c + 1], HEAD_DIM, 1)) for c in range(GROUP // 2)]
    return jnp.concatenate(slabs, axis=-1)


def _odd_attn_p_kernel(sink_ref, q_ref, kd_ref, vd_ref, o_ref, *, seq):
    g = pl.program_id(1)
    sinks = [sink_ref[g * GROUP + rr] * LOG2E for rr in range(GROUP)]
    units = [slice(u * seq, (u + 1) * seq) for u in range(q_ref.shape[0] // seq)]

    def scores(rows):
        return [_dot_t(_stack_heads(q_ref, 0, rows), kd_ref[rows, :])]

    s_next = scores(units[0])
    for n, rows in enumerate(units):
        s_cur = s_next
        if n + 1 < len(units):
            s_next = scores(units[n + 1])
        o_ref[rows, :] = _gqa_group(s_cur, [None], [vd_ref[rows, :]], sinks, seq).astype(o_ref.dtype)


def _odd_attn_p(q, kd, vd, sink, *, seq):
    t = q.shape[0]
    gw = GROUP * HEAD_DIM
    rows = seq * (ODD_P_SEQS if (t // seq) % ODD_P_SEQS == 0 else 1)
    return pl.pallas_call(
        functools.partial(_odd_attn_p_kernel, seq=seq),
        out_shape=jax.ShapeDtypeStruct((t, C_HEADS * HEAD_DIM), BF16),
        grid=(t // rows, C_KV_HEADS),
        in_specs=[
            pl.BlockSpec(memory_space=pltpu.SMEM),
            pl.BlockSpec((rows, gw), lambda b, g: (b, g)),
            pl.BlockSpec((rows, LANES), lambda b, g: (b, g)),
            pl.BlockSpec((rows, LANES), lambda b, g: (b, g)),
        ],
        out_specs=pl.BlockSpec((rows, gw), lambda b, g: (b, g)),
        compiler_params=_params("parallel", "parallel"),
        name="odd_attn_prompt",
    )(sink, q, kd, vd)


def _odd_attn_s_kernel(sink_ref, q_ref, kd_ref, vd_ref, kc_ref, vc_ref, bias_ref, o_ref, kcd_scr, vcd_scr, *, seq, tq):
    i = pl.program_id(1)
    sub = ODD_Q_SUB
    span = sub + 2 * WINDOW

    @pl.when(i == 0)
    def _():
        ones = jnp.ones((kc_ref.shape[1], HEAD_DIM), F32)
        for g in range(C_KV_HEADS):
            gl = slice(g * LANES, (g + 1) * LANES)
            k_g = kc_ref[0, :, g, :]
            kcd_scr[:, gl] = jnp.concatenate([k_g, k_g], axis=-1).astype(BF16)
            vcd_scr[:, gl] = jnp.concatenate([vc_ref[0, :, g, :], ones], axis=-1).astype(BF16)

    units = []
    for jb in range(tq // sub):
        q0 = i * tq + jb * sub
        start = pl.multiple_of(jnp.clip(q0 - WINDOW, 0, seq - span), WINDOW)
        bias = bias_ref[(q0 - start) // WINDOW]
        units += [(slice(jb * sub, (jb + 1) * sub), start, bias, g) for g in range(C_KV_HEADS)]

    def scores(unit):
        rows, start, _, g = unit
        gl = slice(g * LANES, (g + 1) * LANES)
        q4 = _stack_heads(q_ref, g, rows)
        return [_dot_t(q4, kcd_scr[:, gl]), _dot_t(q4, kd_ref[pl.ds(start, span), gl])]

    s_next = scores(units[0])
    for n, (rows, start, bias, g) in enumerate(units):
        s_cur = s_next
        if n + 1 < len(units):
            s_next = scores(units[n + 1])
        gl = slice(g * LANES, (g + 1) * LANES)
        sinks = [sink_ref[g * GROUP + rr] * LOG2E for rr in range(GROUP)]
        og = _gqa_group(s_cur, [None, bias], [vcd_scr[:, gl], vd_ref[pl.ds(start, span), gl]], sinks, sub)
        o_ref[rows, g * GROUP * HEAD_DIM:(g + 1) * GROUP * HEAD_DIM] = og.astype(o_ref.dtype)


def _odd_attn_s(q, kd, vd, kc, vc, sink, *, seq, past, tq):
    t = q.shape[0]
    nq = seq // tq
    nqw = C_HEADS * HEAD_DIM
    nd = C_KV_HEADS * LANES
    nk = C_KV_HEADS * HEAD_DIM
    span = ODD_Q_SUB + 2 * WINDOW
    rel = (np.arange(3)[:, None, None] * WINDOW + np.arange(ODD_Q_SUB)[None, :, None]
           - np.arange(span)[None, None, :])
    bias = jnp.asarray(np.where(np.abs(rel) <= WINDOW, 0.0, -np.inf).astype(np.float32))
    return pl.pallas_call(
        functools.partial(_odd_attn_s_kernel, seq=seq, tq=tq),
        out_shape=jax.ShapeDtypeStruct((t, nqw), BF16),
        grid=(t // seq, nq),
        in_specs=[
            pl.BlockSpec(memory_space=pltpu.SMEM),
            pl.BlockSpec((tq, nqw), lambda b, i: (b * nq + i, 0)),
            pl.BlockSpec((seq, nd), lambda b, i: (b, 0)),
            pl.BlockSpec((seq, nd), lambda b, i: (b, 0)),
            pl.BlockSpec((1, past, C_KV_HEADS, HEAD_DIM), lambda b, i: (b, 0, 0, 0)),
            pl.BlockSpec((1, past, C_KV_HEADS, HEAD_DIM), lambda b, i: (b, 0, 0, 0)),
            _resident(bias.shape),
        ],
        out_specs=pl.BlockSpec((tq, nqw), lambda b, i: (b * nq + i, 0)),
        scratch_shapes=[pltpu.VMEM((past, nd), BF16), pltpu.VMEM((past, nd), BF16)],
        compiler_params=_params("parallel", "arbitrary"),
        name="odd_attn_sample",
    )(sink, q, kd, vd, kc, vc, bias)


def _rope_tables(n):
    rows = n // GRID_W
    row = np.repeat(np.arange(rows, dtype=np.float64), GRID_W)
    col = np.tile(np.arange(GRID_W, dtype=np.float64), rows)
    inv = ROPE_BASE ** (-np.arange(0, AXIS_DIM, 2, dtype=np.float64) / AXIS_DIM)
    ang_r = row[:, None] * inv[None, :]
    ang_c = col[:, None] * inv[None, :]
    ang = np.concatenate([ang_r, ang_r, ang_c, ang_c], axis=-1)
    sign = np.where((np.arange(HEAD_DIM) & (AXIS_DIM // 2)) == 0, -1.0, 1.0)
    cos = np.tile(np.cos(ang), (1, 2)).astype(np.float32)
    sin = np.tile(np.sin(ang) * sign[None, :], (1, 2)).astype(np.float32)
    return jnp.asarray(cos), jnp.asarray(sin)


def kernel(x_prompt, x_sample, cache_even_k, cache_even_v, cache_odd_k, cache_odd_v, c, c_ctx, w_mod, b_mod, norm_g, ffn_w13, ffn_w2, even_w_in, even_w_out, even_qk_norm, even_lambda, even_subln, even_conv_w, odd_w_in, odd_w_out, odd_qk_norm, odd_sink):
    batch, seq, _ = x_prompt.shape
    dec_batch, dec_seq, _ = x_sample.shape
    past = cache_even_k.shape[2]
    depth = w_mod.shape[0]
    assert 1 + dec_batch <= MOD_ROWS and seq % SUBLANES == 0 and dec_seq % TOKEN_TILE == 0

    xp = x_prompt.reshape(batch * seq, D_MODEL)
    xs = x_sample.reshape(dec_batch * dec_seq, D_MODEL)
    tm_p = min(TOKEN_TILE, batch * seq)
    row_p = _row_fn(0, batch * seq, tm_p)
    row_s = _row_fn(1, dec_seq, TOKEN_TILE)

    c_rows = jnp.zeros((MOD_ROWS, D_MODEL), F32).at[0].set(c_ctx).at[1:1 + dec_batch].set(c)
    mods = _adaln(c_rows, w_mod, b_mod)
    rope_tabs = _rope_tables(dec_seq)

    src = {}
    for l in range(depth):
        for s in range(2):
            src["w13", l, s] = (ffn_w13, (l, s))
            src["w2", l, s] = (ffn_w2, (l, s))
        names, w_in_all, w_out_all = (("ein", "eout"), even_w_in, even_w_out) if l % 2 == 0 else (
            ("oin", "oout"), odd_w_in, odd_w_out)
        src[names[0], l // 2] = (w_in_all, (l // 2,))
        src[names[1], l // 2] = (w_out_all, (l // 2,))
    wb = {("w13", 0, 0): ffn_w13[0, 0].astype(BF16), ("w2", 0, 0): ffn_w2[0, 0].astype(BF16)}
    first = [("w13", 0, 1), ("w2", 0, 1), ("ein", 0), ("eout", 0)]
    rest = [key for key in src if key not in wb and key not in first]

    even_k, even_v, odd_k, odd_v = [], [], [], []
    for l in range(depth):
        mod = mods[l]
        g = [norm_g[l, s].reshape(1, D_MODEL) for s in range(3)]

        ffn_p = functools.partial(_ffn, mod=mod, row_fn=row_p, tm=tm_p)
        ffn_s = functools.partial(_ffn, mod=mod, row_fn=row_s, tm=TOKEN_TILE)
        xp = ffn_p(xp, g=g[0], w13=wb["w13", l, 0], w2=wb["w2", l, 0])
        if l == 0:
            xs, copies = ffn_s(xs, g=g[0], w13=wb["w13", l, 0], w2=wb["w2", l, 0], casts=[src[key] for key in first])
            wb.update(zip(first, copies))
        else:
            xs = ffn_s(xs, g=g[0], w13=wb["w13", l, 0], w2=wb["w2", l, 0])
        w13b, w2b = wb["w13", l, 1], wb["w2", l, 1]

        if l % 2 == 0:
            e = l // 2
            lam_init = 0.8 - 0.6 * math.exp(-0.3 * l)
            w_in, w_out = wb["ein", e], wb["eout", e]
            qkg = jnp.tile(even_qk_norm[e], (1, 2))
            sub = even_subln[e].reshape(1, A_VDIM)

            q, k, v, bg, u = _even_in(xp, mod, g[1], w_in, qkg, None, row_fn=row_p, tm=tm_p, kv_dtype=F32)
            o = _even_attn_p(q, k, v, even_lambda[e], sub, seq=seq, lam_init=lam_init)
            xp = ffn_p(xp, g=g[2], w13=w13b, w2=w2b, mixer=(w_out, o, bg, u, even_conv_w[e], seq))
            even_k.append(k.reshape(batch, seq, A_HEADS, 2 * HEAD_DIM))
            even_v.append(v.reshape(batch, seq, A_HEADS, A_VDIM))

            q, k, v, bg, u = _even_in(xs, mod, g[1], w_in, qkg, rope_tabs, row_fn=row_s, tm=TOKEN_TILE,
                                      kv_dtype=BF16)
            kc = cache_even_k[:, e].reshape(dec_batch * past, A_QK)
            vc = cache_even_v[:, e].reshape(dec_batch * past, A_V)
            casts = [src[key] for key in rest] if l == 0 else []
            o = _even_attn_s(q, k, v, kc, vc, even_lambda[e], sub, seq=dec_seq, past=past, tq=EVEN_Q_TILE,
                             lam_init=lam_init, casts=casts)
            if casts:
                o, copies = o
                wb.update(zip(rest, copies))
            xs = ffn_s(xs, g=g[2], w13=w13b, w2=w2b, mixer=(w_out, o, bg, u, even_conv_w[e], dec_seq))
        else:
            e = l // 2
            w_in, w_out = wb["oin", e], wb["oout", e]
            qkg = jnp.tile(odd_qk_norm[e], (1, 2))
            sink = odd_sink[e]

            q, kd, vd, k, v = _odd_in(xp, mod, g[1], w_in, qkg, None, row_fn=row_p, tm=tm_p, keep_kv=True)
            o = _odd_attn_p(q, kd, vd, sink, seq=seq)
            xp = ffn_p(xp, g=g[2], w13=w13b, w2=w2b, mixer=(w_out, o))
            odd_k.append(k.reshape(batch, seq, C_KV_HEADS, HEAD_DIM))
            odd_v.append(v.reshape(batch, seq, C_KV_HEADS, HEAD_DIM))

            q, kd, vd = _odd_in(xs, mod, g[1], w_in, qkg, rope_tabs, row_fn=row_s, tm=TOKEN_TILE, keep_kv=False)
            kc = cache_odd_k[:, e]
            vc = cache_odd_v[:, e]
            o = _odd_attn_s(q, kd, vd, kc, vc, sink, seq=dec_seq, past=past, tq=ODD_Q_TILE)
            xs = ffn_s(xs, g=g[2], w13=w13b, w2=w2b, mixer=(w_out, o))

    return (xp.reshape(batch, seq, D_MODEL), xs.reshape(dec_batch, dec_seq, D_MODEL),
            jnp.stack(even_k, axis=1), jnp.stack(even_v, axis=1),
            jnp.stack(odd_k, axis=1), jnp.stack(odd_v, axis=1))
```

```python
import functools
import math

import jax
import jax.numpy as jnp
import numpy as np
from jax import lax
from jax.experimental import pallas as pl
from jax.experimental.pallas import tpu as pltpu

F32 = jnp.float32
BF16 = jnp.bfloat16

D_MODEL = 1024
GRID_W = 64
HEAD_DIM = 64
AXIS_DIM = HEAD_DIM // 2
A_HEADS = 4
A_VDIM = 2 * HEAD_DIM
CONV_DIM = 512
C_HEADS = 16
C_KV_HEADS = 4
GROUP = C_HEADS // C_KV_HEADS
WINDOW = 128
D_FF = 2816
ROPE_BASE = 10000.0
N_MOD = 9
EPS = 1e-6
A_QK = A_HEADS * 2 * HEAD_DIM
A_V = A_HEADS * A_VDIM
EVEN_IN = 2 * A_QK + A_V + 3 * CONV_DIM
ODD_IN = (C_HEADS + 2 * C_KV_HEADS) * HEAD_DIM
LOG2E = math.log2(math.e)
Q_PRESCALE = HEAD_DIM ** -0.5 * LOG2E

LANES = 128
SUBLANES = 8
BF16_SUBLANES = 16
MOD_ROWS = 16
VMEM_LIMIT_BYTES = 56 * 2 ** 20

TOKEN_TILE = 512
EVEN_Q_TILE = 512
EVEN_Q_SUB = 256
ODD_Q_TILE = 512
ODD_Q_SUB = WINDOW
ODD_P_SEQS = 2


def _params(*semantics):
    return pltpu.CompilerParams(dimension_semantics=semantics, vmem_limit_bytes=VMEM_LIMIT_BYTES)


def _sigmoid(x):
    return 1.0 / (1.0 + jnp.exp(-x))


def _mod_norm(x, g, shift, scale):
    y = x * lax.rsqrt(jnp.mean(x * x, axis=-1, keepdims=True) + EPS)
    return (y * g) * (1.0 + scale) + shift


def _mod_row(mod_ref, k, r):
    return mod_ref[k, pl.ds(r, 1), :]


def _row_fn(row0, tokens_per_row, tile):
    return lambda i: row0 + (i * tile) // tokens_per_row


def _lane_lo(shape):
    return lax.broadcasted_iota(jnp.int32, shape, len(shape) - 1) < HEAD_DIM


def _seg_rms(xs, gain):
    lo = _lane_lo(xs.shape)
    sq = xs * xs
    s_lo = jnp.sum(jnp.where(lo, sq, 0.0), axis=-1, keepdims=True)
    s_hi = jnp.sum(jnp.where(lo, 0.0, sq), axis=-1, keepdims=True)
    inv = lax.rsqrt(jnp.where(lo, s_lo, s_hi) * (1.0 / HEAD_DIM) + EPS)
    return (xs * inv) * gain


def _half_sum_matrix():
    r = lax.broadcasted_iota(jnp.int32, (2 * LANES, LANES), 0)
    c = lax.broadcasted_iota(jnp.int32, (2 * LANES, LANES), 1)
    return jnp.where((r & HEAD_DIM) == (c & HEAD_DIM), 1.0, 0.0).astype(BF16)


def _seg_rms_mxu(xs, gain, ones2):
    sq = xs * xs
    head = sq.astype(BF16)
    rest = (sq - head.astype(F32)).astype(BF16)
    ssum = _dot(jnp.concatenate([head, rest], axis=-1), ones2)
    return (xs * lax.rsqrt(ssum * (1.0 / HEAD_DIM) + EPS)) * gain


def _rope(xs, cos, sin_signed):
    lane = lax.broadcasted_iota(jnp.int32, xs.shape, 1)
    first = (lane & (AXIS_DIM // 2)) == 0
    partner = jnp.where(first, pltpu.roll(xs, LANES - AXIS_DIM // 2, 1), pltpu.roll(xs, AXIS_DIM // 2, 1))
    return xs * cos + partner * sin_signed


def _dot_t(a, b):
    return lax.dot_general(a, b, (((1,), (1,)), ((), ())), preferred_element_type=F32)


def _dot(a, b):
    return jnp.dot(a, b, preferred_element_type=F32)


def _adaln_kernel(c_ref, w_ref, b_ref, o_ref):
    c = c_ref[...]
    s = c * _sigmoid(c)
    o_ref[0, 0] = _dot(s.astype(BF16), w_ref[0].astype(BF16)) + b_ref[0]


def _adaln(c_rows, w_mod, b_mod):
    depth = w_mod.shape[0]
    b3 = b_mod.reshape(depth * N_MOD, 1, D_MODEL)
    return pl.pallas_call(
        _adaln_kernel,
        out_shape=jax.ShapeDtypeStruct((depth, N_MOD, MOD_ROWS, D_MODEL), F32),
        grid=(depth, N_MOD),
        in_specs=[
            pl.BlockSpec((MOD_ROWS, D_MODEL), lambda l, j: (0, 0)),
            pl.BlockSpec((1, D_MODEL, D_MODEL), lambda l, j: (l, 0, j)),
            pl.BlockSpec((1, 1, D_MODEL), lambda l, j: (l * N_MOD + j, 0, 0)),
        ],
        out_specs=pl.BlockSpec((1, 1, MOD_ROWS, D_MODEL), lambda l, j: (l, j, 0, 0)),
        compiler_params=_params("parallel", "parallel"),
        name="adaln",
    )(c_rows, w_mod, b3)


def _ffn_tail(x, r, mod_ref, g_ref, w13_ref, w2_ref, k0):
    half = x.shape[0] // 2
    xs = [x[:half], x[half:]]
    hs = [_mod_norm(xr, g_ref[...], _mod_row(mod_ref, k0, r), _mod_row(mod_ref, k0 + 1, r)) for xr in xs]
    a_s = [_dot(h.astype(BF16), w13_ref[...]) for h in hs]
    acts = [(a[:, :D_FF] * _sigmoid(a[:, :D_FF])) * a[:, D_FF:] for a in a_s]
    ys = [_dot(act.astype(BF16), w2_ref[...]) for act in acts]
    gate = 0.5 * _mod_row(mod_ref, k0 + 2, r)
    return jnp.concatenate([xr + gate * y for xr, y in zip(xs, ys)], axis=0)


def _ffn_kernel(x_ref, mod_ref, g_ref, w13_ref, w2_ref, o_ref, *, row_fn):
    o_ref[...] = _ffn_tail(x_ref[...], row_fn(pl.program_id(0)), mod_ref, g_ref, w13_ref, w2_ref, 0)


def _conv_gate(bg_ref, u_ref, up_ref, un_ref, cw_ref, i, seq, tm):
    u = u_ref[...]
    row = lax.broadcasted_iota(jnp.int32, (tm, 1), 0)
    pos = (row + i * tm) % seq
    u_dn = jnp.where(row == 0, up_ref[SUBLANES - 1:SUBLANES, :], pltpu.roll(u, 1, 0))
    u_dn = jnp.where(pos == 0, 0.0, u_dn)
    u_up = jnp.where(row == tm - 1, un_ref[0:1, :], pltpu.roll(u, tm - 1, 0))
    u_up = jnp.where(pos == seq - 1, 0.0, u_up)
    return bg_ref[...] * (cw_ref[0:1, :] * u_dn + cw_ref[1:2, :] * u + cw_ref[2:3, :] * u_up)


def _even_out_ffn_kernel(x_ref, a_ref, bg_ref, u_ref, up_ref, un_ref, cw_ref, wo_ref, mod_ref, g_ref, w13_ref, w2_ref,
                         o_ref, *, row_fn, seq, tm):
    i = pl.program_id(0)
    r = row_fn(i)
    y = _conv_gate(bg_ref, u_ref, up_ref, un_ref, cw_ref, i, seq, tm)
    mix = _dot(jnp.concatenate([a_ref[...], y.astype(BF16)], axis=-1), wo_ref[...])
    x = x_ref[...] + _mod_row(mod_ref, 5, r) * mix
    o_ref[...] = _ffn_tail(x, r, mod_ref, g_ref, w13_ref, w2_ref, 6)


def _odd_out_ffn_kernel(x_ref, a_ref, wo_ref, mod_ref, g_ref, w13_ref, w2_ref, o_ref, *, row_fn):
    r = row_fn(pl.program_id(0))
    x = x_ref[...] + _mod_row(mod_ref, 5, r) * _dot(a_ref[...], wo_ref[...])
    o_ref[...] = _ffn_tail(x, r, mod_ref, g_ref, w13_ref, w2_ref, 6)


def _resident(shape, index=None):
    index = (0,) * len(shape) if index is None else index
    return pl.BlockSpec(shape, lambda *_: index, pipeline_mode=pl.Buffered(1))


def _cast_specs(casts, steps, flat_step=lambda i: i):
    in_specs, out_specs, out_shapes = [], [], []
    for arr, lead in casts:
        rows_total, cols = arr.shape[-2:]
        nblk = next(d for d in range(steps, 0, -1)
                    if steps % d == 0 and rows_total % d == 0 and (rows_total // d) % BF16_SUBLANES == 0)
        rows, rep = rows_total // nblk, steps // nblk
        in_specs.append(pl.BlockSpec((1,) * len(lead) + (rows, cols),
                                     lambda *idx, lead=lead, rep=rep: lead + (flat_step(*idx) // rep, 0)))
        out_specs.append(pl.BlockSpec((rows, cols), lambda *idx, rep=rep: (flat_step(*idx) // rep, 0)))
        out_shapes.append(jax.ShapeDtypeStruct((rows_total, cols), BF16))
    return in_specs, out_specs, out_shapes


def _with_casts(body, n_in, n_out, n_cast):
    if n_cast == 0:
        return body

    def kernel(*refs):
        outs = n_in + n_cast
        body(*refs[:n_in], *refs[outs:outs + n_out], *refs[outs + n_out + n_cast:])
        for src, dst in zip(refs[n_in:outs], refs[outs + n_out:outs + n_out + n_cast]):
            dst[...] = src[(0,) * (len(src.shape) - 2)].astype(dst.dtype)

    return kernel


def _ffn(x, mod, g, w13, w2, *, row_fn, tm, mixer=None, casts=()):
    t = x.shape[0]
    tile = lambda n: pl.BlockSpec((tm, n), lambda i: (i, 0))
    ffn_specs = [
        _resident((N_MOD, MOD_ROWS, D_MODEL)),
        _resident((1, D_MODEL)),
        _resident((D_MODEL, 2 * D_FF)),
        _resident((D_FF, D_MODEL)),
    ]
    ffn_args = [mod, g, w13, w2]
    if mixer is None:
        body = functools.partial(_ffn_kernel, row_fn=row_fn)
        in_specs, args, name = [tile(D_MODEL)], [x], "ffn"
    elif len(mixer) == 2:
        w_out, attn = mixer
        body = functools.partial(_odd_out_ffn_kernel, row_fn=row_fn)
        in_specs = [tile(D_MODEL), tile(attn.shape[1]), _resident(w_out.shape)]
        args, name = [x, attn, w_out], "odd_out_ffn"
    else:
        w_out, attn, bg, u, conv_w, seq = mixer
        nb = tm // SUBLANES
        last = t // SUBLANES - 1
        body = functools.partial(_even_out_ffn_kernel, row_fn=row_fn, seq=seq, tm=tm)
        in_specs = [
            tile(D_MODEL), tile(A_V), tile(CONV_DIM), tile(CONV_DIM),
            pl.BlockSpec((SUBLANES, CONV_DIM), lambda i: (jnp.maximum(i * nb - 1, 0), 0)),
            pl.BlockSpec((SUBLANES, CONV_DIM), lambda i: (jnp.minimum((i + 1) * nb, last), 0)),
            _resident(conv_w.shape), _resident(w_out.shape),
        ]
        args, name = [x, attn, bg, u, u, u, conv_w, w_out], "even_out_ffn"
    steps = t // tm
    c_in, c_out, c_shapes = _cast_specs(casts, steps)
    in_specs = in_specs + ffn_specs
    out = pl.pallas_call(
        _with_casts(body, len(in_specs), 1, len(casts)),
        out_shape=(jax.ShapeDtypeStruct((t, D_MODEL), F32), *c_shapes),
        grid=(steps,),
        in_specs=in_specs + c_in,
        out_specs=(tile(D_MODEL), *c_out),
        compiler_params=_params("parallel"),
        name=name,
    )(*args, *ffn_args, *[arr for arr, _ in casts])
    return (out[0], list(out[1:])) if casts else out[0]


def _even_in_kernel(x_ref, mod_ref, g_ref, w_ref, qkg_ref, *rest, row_fn, rope):
    if rope:
        cos_ref, sin_ref, q_ref, k_ref, v_ref, bg_ref, u_ref = rest
    else:
        q_ref, k_ref, v_ref, bg_ref, u_ref = rest
    r = row_fn(pl.program_id(0))
    h = _mod_norm(x_ref[...], g_ref[...], _mod_row(mod_ref, 3, r), _mod_row(mod_ref, 4, r))
    y = _dot(h.astype(BF16), w_ref[...])
    q_gain = qkg_ref[0:1, :] * Q_PRESCALE
    for hh in range(A_HEADS):
        sl = slice(hh * LANES, (hh + 1) * LANES)
        qs = _seg_rms(y[:, sl], q_gain)
        ks = _seg_rms(y[:, A_QK + hh * LANES:A_QK + (hh + 1) * LANES], qkg_ref[1:2, :])
        if rope:
            qs = _rope(qs, cos_ref[...], sin_ref[...])
            ks = _rope(ks, cos_ref[...], sin_ref[...])
        q_ref[:, sl] = qs.astype(q_ref.dtype)
        k_ref[:, sl] = ks.astype(k_ref.dtype)
    v_ref[...] = y[:, 2 * A_QK:2 * A_QK + A_V].astype(v_ref.dtype)
    o = 2 * A_QK + A_V
    bg_ref[...] = y[:, o:o + CONV_DIM]
    u_ref[...] = y[:, o + CONV_DIM:o + 2 * CONV_DIM] * y[:, o + 2 * CONV_DIM:o + 3 * CONV_DIM]


def _even_in(x, mod, g, w_in, qkg, rope_tabs, *, row_fn, tm, kv_dtype):
    t = x.shape[0]
    rope = rope_tabs is not None
    block = tm
    in_specs = [
        pl.BlockSpec((block, D_MODEL), lambda i: (i, 0)),
        _resident((N_MOD, MOD_ROWS, D_MODEL)),
        _resident((1, D_MODEL)),
        _resident((D_MODEL, EVEN_IN)),
        _resident((2, LANES)),
    ]
    args = [x, mod, g, w_in, qkg]
    if rope:
        nt = rope_tabs[0].shape[0] // block
        in_specs += [pl.BlockSpec((block, LANES), lambda i: (i % nt, 0))] * 2
        args += list(rope_tabs)
    wide = lambda dt: jax.ShapeDtypeStruct((t, A_QK), dt)
    spec = pl.BlockSpec((block, A_QK), lambda i: (i, 0))
    return pl.pallas_call(
        functools.partial(_even_in_kernel, row_fn=row_fn, rope=rope),
        out_shape=(wide(BF16), wide(kv_dtype), wide(kv_dtype), wide(F32), wide(F32)),
        grid=(t // block,),
        in_specs=in_specs,
        out_specs=(spec,) * 5,
        compiler_params=_params("parallel"),
        name="even_in",
    )(*args)


def _lambda(lam_ref, lam_init):
    lf = lam_ref[...]
    a = jnp.sum(lf[0:1] * lf[1:2], axis=-1, keepdims=True)
    b = jnp.sum(lf[2:3] * lf[3:4], axis=-1, keepdims=True)
    return jnp.exp(a) - jnp.exp(b) + lam_init


def _subln(o, sub, lam_init):
    y = o * lax.rsqrt(jnp.mean(o * o, axis=-1, keepdims=True) + EPS)
    return (y * sub) * (1.0 - lam_init)


def _softmax_rows_pv(s, v1):
    m = jnp.max(s, axis=-1, keepdims=True)
    r = _dot(jnp.exp2(s - m).astype(BF16), v1)
    return r[:, :LANES] * (1.0 / r[:, LANES:])


def _diff_attn_heads(q_ref, head_kv, lam_ref, sub_ref, o_ref, lam_init):
    lam = _lambda(lam_ref, lam_init)
    tq = q_ref.shape[0]
    sub_rows = min(tq, EVEN_Q_SUB)
    units = [(slice(rb * sub_rows, (rb + 1) * sub_rows), h)
             for rb in range(tq // sub_rows) for h in range(A_HEADS)]

    def scores(unit):
        rows, h = unit
        q = q_ref[rows, h * LANES:(h + 1) * LANES]
        k, _ = head_kv(h)
        lo = _lane_lo(q.shape)
        zero = jnp.zeros_like(q)
        return _dot_t(jnp.where(lo, q, zero), k), _dot_t(jnp.where(lo, zero, q), k)

    s_next = scores(units[0])
    for n, (rows, h) in enumerate(units):
        s1, s2 = s_next
        if n + 1 < len(units):
            s_next = scores(units[n + 1])
        _, v1 = head_kv(h)
        o = _softmax_rows_pv(s1, v1) - lam * _softmax_rows_pv(s2, v1)
        o_ref[rows, h * LANES:(h + 1) * LANES] = _subln(o, sub_ref[...], lam_init).astype(o_ref.dtype)


def _even_attn_p_kernel(q_ref, k_ref, v_ref, lam_ref, sub_ref, o_ref, *, lam_init):
    ones = jnp.ones((k_ref.shape[0], LANES), BF16)

    def head_kv(h):
        sl = slice(h * LANES, (h + 1) * LANES)
        return k_ref[:, sl].astype(BF16), jnp.concatenate([v_ref[:, sl].astype(BF16), ones], axis=-1)

    _diff_attn_heads(q_ref, head_kv, lam_ref, sub_ref, o_ref, lam_init)


def _even_attn_p(q, k, v, lam_vec, sub, *, seq, lam_init):
    t = q.shape[0]
    blk = pl.BlockSpec((seq, A_QK), lambda b: (b, 0))
    return pl.pallas_call(
        functools.partial(_even_attn_p_kernel, lam_init=lam_init),
        out_shape=jax.ShapeDtypeStruct((t, A_V), BF16),
        grid=(t // seq,),
        in_specs=[blk, blk, blk, _resident((4, HEAD_DIM)), _resident((1, A_VDIM))],
        out_specs=blk,
        compiler_params=_params("parallel"),
        name="even_attn_prompt",
    )(q, k, v, lam_vec, sub)


def _even_attn_s_kernel(q_ref, kl_ref, vl_ref, kc_ref, vc_ref, lam_ref, sub_ref, o_ref, k_scr, v1_scr, *, past, lam_init):
    @pl.when(pl.program_id(1) == 0)
    def _():
        k_scr[:past, :] = kc_ref[...].astype(BF16)
        k_scr[past:, :] = kl_ref[...]
        for h in range(A_HEADS):
            sl = slice(h * LANES, (h + 1) * LANES)
            v1_scr[h, :past, :LANES] = vc_ref[:, sl].astype(BF16)
            v1_scr[h, past:, :LANES] = vl_ref[:, sl]
            v1_scr[h, :, LANES:] = jnp.ones((v1_scr.shape[1], LANES), BF16)

    def head_kv(h):
        return k_scr[:, h * LANES:(h + 1) * LANES], v1_scr[h]

    _diff_attn_heads(q_ref, head_kv, lam_ref, sub_ref, o_ref, lam_init)


def _even_attn_s(q, kl, vl, kc, vc, lam_vec, sub, *, seq, past, tq, lam_init, casts=()):
    t = q.shape[0]
    nq = seq // tq
    qblk = pl.BlockSpec((tq, A_QK), lambda b, i: (b * nq + i, 0))
    lat = pl.BlockSpec((seq, A_QK), lambda b, i: (b, 0))
    ctx = pl.BlockSpec((past, A_QK), lambda b, i: (b, 0))
    in_specs = [qblk, lat, lat, ctx, ctx, _resident((4, HEAD_DIM)), _resident((1, A_VDIM))]
    c_in, c_out, c_shapes = _cast_specs(casts, (t // seq) * nq, lambda b, i: b * nq + i)
    body = functools.partial(_even_attn_s_kernel, past=past, lam_init=lam_init)
    out = pl.pallas_call(
        _with_casts(body, len(in_specs), 1, len(casts)),
        out_shape=(jax.ShapeDtypeStruct((t, A_V), BF16), *c_shapes),
        grid=(t // seq, nq),
        in_specs=in_specs + c_in,
        out_specs=(qblk, *c_out),
        scratch_shapes=[pltpu.VMEM((past + seq, A_QK), BF16),
                        pltpu.VMEM((A_HEADS, past + seq, 2 * LANES), BF16)],
        compiler_params=_params("parallel", "arbitrary"),
        name="even_attn_sample",
    )(q, kl, vl, kc, vc, lam_vec, sub, *[arr for arr, _ in casts])
    return (out[0], list(out[1:])) if casts else out[0]


def _dup_halves(x):
    lo = _lane_lo(x.shape)
    sw = pltpu.roll(x, HEAD_DIM, 1)
    return jnp.where(lo, x, sw), jnp.where(lo, sw, x)


def _ones_halves(x):
    lo = _lane_lo(x.shape)
    return jnp.where(lo, x, 1.0), jnp.where(lo, pltpu.roll(x, HEAD_DIM, 1), 1.0)


def _odd_in_kernel(x_ref, mod_ref, g_ref, w_ref, qkg_ref, *rest, row_fn, rope, keep_kv):
    rest = list(rest)
    if rope:
        cos_ref, sin_ref = rest[:2]
        rest = rest[2:]
    if keep_kv:
        q_ref, kd_ref, vd_ref, k_ref, v_ref = rest
    else:
        q_ref, kd_ref, vd_ref = rest
    nq = C_HEADS * HEAD_DIM
    nk = C_KV_HEADS * HEAD_DIM
    q_gain = qkg_ref[0:1, :] * Q_PRESCALE
    tm = x_ref.shape[0]
    ones2 = _half_sum_matrix()
    r = row_fn(pl.program_id(0))
    for rows in (slice(0, tm // 2), slice(tm // 2, tm)):
        h = _mod_norm(x_ref[rows, :], g_ref[...], _mod_row(mod_ref, 3, r), _mod_row(mod_ref, 4, r))
        y = _dot(h.astype(BF16), w_ref[...])
        if rope:
            cos, sin = cos_ref[rows, :], sin_ref[rows, :]
        for c in range(nq // LANES):
            sl = slice(c * LANES, (c + 1) * LANES)
            qs = _seg_rms_mxu(y[:, sl], q_gain, ones2)
            if rope:
                qs = _rope(qs, cos, sin)
            q_ref[rows, sl] = qs.astype(q_ref.dtype)
        for c in range(nk // LANES):
            sl = slice(c * LANES, (c + 1) * LANES)
            ks = _seg_rms_mxu(y[:, nq + c * LANES:nq + (c + 1) * LANES], qkg_ref[1:2, :], ones2)
            vs = y[:, nq + nk + c * LANES:nq + nk + (c + 1) * LANES]
            if keep_kv:
                k_ref[rows, sl] = ks
                v_ref[rows, sl] = vs
            if rope:
                ks = _rope(ks, cos, sin)
            for (d0, d1), ref in ((_dup_halves(ks), kd_ref), (_ones_halves(vs), vd_ref)):
                ref[rows, 2 * c * LANES:(2 * c + 1) * LANES] = d0.astype(ref.dtype)
                ref[rows, (2 * c + 1) * LANES:(2 * c + 2) * LANES] = d1.astype(ref.dtype)


def _odd_in(x, mod, g, w_in, qkg, rope_tabs, *, row_fn, tm, keep_kv):
    t = x.shape[0]
    rope = rope_tabs is not None
    block = tm
    in_specs = [
        pl.BlockSpec((block, D_MODEL), lambda i: (i, 0)),
        _resident((N_MOD, MOD_ROWS, D_MODEL)),
        _resident((1, D_MODEL)),
        _resident((D_MODEL, ODD_IN)),
        _resident((2, LANES)),
    ]
    args = [x, mod, g, w_in, qkg]
    if rope:
        nt = rope_tabs[0].shape[0] // block
        in_specs += [pl.BlockSpec((block, LANES), lambda i: (i % nt, 0))] * 2
        args += list(rope_tabs)
    nq = C_HEADS * HEAD_DIM
    nd = C_KV_HEADS * LANES
    nk = C_KV_HEADS * HEAD_DIM
    out_shape = [jax.ShapeDtypeStruct((t, nq), BF16), jax.ShapeDtypeStruct((t, nd), BF16),
                 jax.ShapeDtypeStruct((t, nd), BF16)]
    out_specs = [pl.BlockSpec((block, nq), lambda i: (i, 0)), pl.BlockSpec((block, nd), lambda i: (i, 0)),
                 pl.BlockSpec((block, nd), lambda i: (i, 0))]
    if keep_kv:
        out_shape += [jax.ShapeDtypeStruct((t, nk), F32)] * 2
        out_specs += [pl.BlockSpec((block, nk), lambda i: (i, 0))] * 2
    return pl.pallas_call(
        functools.partial(_odd_in_kernel, row_fn=row_fn, rope=rope, keep_kv=keep_kv),
        out_shape=tuple(out_shape),
        grid=(t // block,),
        in_specs=in_specs,
        out_specs=tuple(out_specs),
        compiler_params=_params("parallel"),
        name="odd_in",
    )(*args)


def _stack_heads(q_ref, g, rows):
    parts = []
    for rr in range(GROUP):
        c = g * (GROUP // 2) + rr // 2
        qc = q_ref[rows, c * LANES:(c + 1) * LANES]
        lo = _lane_lo(qc.shape)
        keep = lo if rr % 2 == 0 else jnp.logical_not(lo)
        parts.append(jnp.where(keep, qc, jnp.zeros_like(qc)))
    return jnp.concatenate(parts, axis=0)


def _gqa_group(scores, biases, vals, sinks, rows):
    es = [[] for _ in scores]
    sink_terms = []
    for rr in range(GROUP):
        rs = slice(rr * rows, (rr + 1) * rows)
        parts = [s[rs] if b is None else s[rs] + b for s, b in zip(scores, biases)]
        m = functools.reduce(jnp.maximum, [jnp.max(p, axis=-1, keepdims=True) for p in parts])
        m = jnp.maximum(m, sinks[rr])
        for j, p in enumerate(parts):
            es[j].append(jnp.exp2(p - m).astype(BF16))
        sink_terms.append(jnp.exp2(sinks[rr] - m))
    r = functools.reduce(jnp.add, [_dot(jnp.concatenate(e, axis=0), v) for e, v in zip(es, vals)])
    heads = []
    for rr in range(GROUP):
        rh = r[rr * rows:(rr + 1) * rows]
        heads.append(rh * (1.0 / (pltpu.roll(rh, HEAD_DIM, 1) + sink_terms[rr])))
    lo = _lane_lo((rows, LANES))
    slabs = [jnp.where(lo, heads[2 * c], pltpu.roll(heads[2 * c + 1], HEAD_DIM, 1)) for c in range(GROUP // 2)]
    return jnp.concatenate(slabs, axis=-1)


def _odd_attn_p_kernel(sink_ref, q_ref, kd_ref, vd_ref, o_ref, *, seq):
    g = pl.program_id(1)
    sinks = [sink_ref[g * GROUP + rr] * LOG2E for rr in range(GROUP)]
    units = [slice(u * seq, (u + 1) * seq) for u in range(q_ref.shape[0] // seq)]

    def scores(rows):
        return [_dot_t(_stack_heads(q_ref, 0, rows), kd_ref[rows, :])]

    s_next = scores(units[0])
    for n, rows in enumerate(units):
        s_cur = s_next
        if n + 1 < len(units):
            s_next = scores(units[n + 1])
        o_ref[rows, :] = _gqa_group(s_cur, [None], [vd_ref[rows, :]], sinks, seq).astype(o_ref.dtype)


def _odd_attn_p(q, kd, vd, sink, *, seq):
    t = q.shape[0]
    gw = GROUP * HEAD_DIM
    rows = seq * (ODD_P_SEQS if (t // seq) % ODD_P_SEQS == 0 else 1)
    return pl.pallas_call(
        functools.partial(_odd_attn_p_kernel, seq=seq),
        out_shape=jax.ShapeDtypeStruct((t, C_HEADS * HEAD_DIM), BF16),
        grid=(t // rows, C_KV_HEADS),
        in_specs=[
            pl.BlockSpec(memory_space=pltpu.SMEM),
            pl.BlockSpec((rows, gw), lambda b, g: (b, g)),
            pl.BlockSpec((rows, LANES), lambda b, g: (b, g)),
            pl.BlockSpec((rows, LANES), lambda b, g: (b, g)),
        ],
        out_specs=pl.BlockSpec((rows, gw), lambda b, g: (b, g)),
        compiler_params=_params("parallel", "parallel"),
        name="odd_attn_prompt",
    )(sink, q, kd, vd)


def _odd_attn_s_kernel(sink_ref, q_ref, kd_ref, vd_ref, kc_ref, vc_ref, bias_ref, o_ref, kcd_scr, vcd_scr, *, seq, tq):
    i = pl.program_id(1)
    sub = ODD_Q_SUB
    span = sub + 2 * WINDOW

    @pl.when(i == 0)
    def _():
        ones = jnp.ones((kc_ref.shape[1], HEAD_DIM), F32)
        for g in range(C_KV_HEADS):
            gl = slice(g * LANES, (g + 1) * LANES)
            k_g = kc_ref[0, :, g, :]
            kcd_scr[:, gl] = jnp.concatenate([k_g, k_g], axis=-1).astype(BF16)
            vcd_scr[:, gl] = jnp.concatenate([vc_ref[0, :, g, :], ones], axis=-1).astype(BF16)

    units = []
    for jb in range(tq // sub):
        q0 = i * tq + jb * sub
        start = pl.multiple_of(jnp.clip(q0 - WINDOW, 0, seq - span), WINDOW)
        bias = bias_ref[(q0 - start) // WINDOW]
        units += [(slice(jb * sub, (jb + 1) * sub), start, bias, g) for g in range(C_KV_HEADS)]

    def scores(unit):
        rows, start, _, g = unit
        gl = slice(g * LANES, (g + 1) * LANES)
        q4 = _stack_heads(q_ref, g, rows)
        return [_dot_t(q4, kcd_scr[:, gl]), _dot_t(q4, kd_ref[pl.ds(start, span), gl])]

    s_next = scores(units[0])
    for n, (rows, start, bias, g) in enumerate(units):
        s_cur = s_next
        if n + 1 < len(units):
            s_next = scores(units[n + 1])
        gl = slice(g * LANES, (g + 1) * LANES)
        sinks = [sink_ref[g * GROUP + rr] * LOG2E for rr in range(GROUP)]
        og = _gqa_group(s_cur, [None, bias], [vcd_scr[:, gl], vd_ref[pl.ds(start, span), gl]], sinks, sub)
        o_ref[rows, g * GROUP * HEAD_DIM:(g + 1) * GROUP * HEAD_DIM] = og.astype(o_ref.dtype)


def _odd_attn_s(q, kd, vd, kc, vc, sink, *, seq, past, tq):
    t = q.shape[0]
    nq = seq // tq
    nqw = C_HEADS * HEAD_DIM
    nd = C_KV_HEADS * LANES
    nk = C_KV_HEADS * HEAD_DIM
    span = ODD_Q_SUB + 2 * WINDOW
    rel = (np.arange(3)[:, None, None] * WINDOW + np.arange(ODD_Q_SUB)[None, :, None]
           - np.arange(span)[None, None, :])
    bias = jnp.asarray(np.where(np.abs(rel) <= WINDOW, 0.0, -np.inf).astype(np.float32))
    return pl.pallas_call(
        functools.partial(_odd_attn_s_kernel, seq=seq, tq=tq),
        out_shape=jax.ShapeDtypeStruct((t, nqw), BF16),
        grid=(t // seq, nq),
        in_specs=[
            pl.BlockSpec(memory_space=pltpu.SMEM),
            pl.BlockSpec((tq, nqw), lambda b, i: (b * nq + i, 0)),
            pl.BlockSpec((seq, nd), lambda b, i: (b, 0)),
            pl.BlockSpec((seq, nd), lambda b, i: (b, 0)),
            pl.BlockSpec((1, past, C_KV_HEADS, HEAD_DIM), lambda b, i: (b, 0, 0, 0)),
            pl.BlockSpec((1, past, C_KV_HEADS, HEAD_DIM), lambda b, i: (b, 0, 0, 0)),
            _resident(bias.shape),
        ],
        out_specs=pl.BlockSpec((tq, nqw), lambda b, i: (b * nq + i, 0)),
        scratch_shapes=[pltpu.VMEM((past, nd), BF16), pltpu.VMEM((past, nd), BF16)],
        compiler_params=_params("parallel", "arbitrary"),
        name="odd_attn_sample",
    )(sink, q, kd, vd, kc, vc, bias)


def _rope_tables(n):
    rows = n // GRID_W
    row = np.repeat(np.arange(rows, dtype=np.float64), GRID_W)
    col = np.tile(np.arange(GRID_W, dtype=np.float64), rows)
    inv = ROPE_BASE ** (-np.arange(0, AXIS_DIM, 2, dtype=np.float64) / AXIS_DIM)
    ang_r = row[:, None] * inv[None, :]
    ang_c = col[:, None] * inv[None, :]
    ang = np.concatenate([ang_r, ang_r, ang_c, ang_c], axis=-1)
    sign = np.where((np.arange(HEAD_DIM) & (AXIS_DIM // 2)) == 0, -1.0, 1.0)
    cos = np.tile(np.cos(ang), (1, 2)).astype(np.float32)
    sin = np.tile(np.sin(ang) * sign[None, :], (1, 2)).astype(np.float32)
    return jnp.asarray(cos), jnp.asarray(sin)


def kernel(x_prompt, x_sample, cache_even_k, cache_even_v, cache_odd_k, cache_odd_v, c, c_ctx, w_mod, b_mod, norm_g, ffn_w13, ffn_w2, even_w_in, even_w_out, even_qk_norm, even_lambda, even_subln, even_conv_w, odd_w_in, odd_w_out, odd_qk_norm, odd_sink):
    batch, seq, _ = x_prompt.shape
    dec_batch, dec_seq, _ = x_sample.shape
    past = cache_even_k.shape[2]
    depth = w_mod.shape[0]
    assert 1 + dec_batch <= MOD_ROWS and seq % SUBLANES == 0 and dec_seq % TOKEN_TILE == 0

    xp = x_prompt.reshape(batch * seq, D_MODEL)
    xs = x_sample.reshape(dec_batch * dec_seq, D_MODEL)
    tm_p = min(TOKEN_TILE, batch * seq)
    row_p = _row_fn(0, batch * seq, tm_p)
    row_s = _row_fn(1, dec_seq, TOKEN_TILE)

    c_rows = jnp.zeros((MOD_ROWS, D_MODEL), F32).at[0].set(c_ctx).at[1:1 + dec_batch].set(c)
    mods = _adaln(c_rows, w_mod, b_mod)
    rope_tabs = _rope_tables(dec_seq)

    src = {}
    for l in range(depth):
        for s in range(2):
            src["w13", l, s] = (ffn_w13, (l, s))
            src["w2", l, s] = (ffn_w2, (l, s))
        names, w_in_all, w_out_all = (("ein", "eout"), even_w_in, even_w_out) if l % 2 == 0 else (
            ("oin", "oout"), odd_w_in, odd_w_out)
        src[names[0], l // 2] = (w_in_all, (l // 2,))
        src[names[1], l // 2] = (w_out_all, (l // 2,))
    wb = {("w13", 0, 0): ffn_w13[0, 0].astype(BF16), ("w2", 0, 0): ffn_w2[0, 0].astype(BF16)}
    first = [("w13", 0, 1), ("w2", 0, 1), ("ein", 0), ("eout", 0)]
    rest = [key for key in src if key not in wb and key not in first]

    even_k, even_v, odd_k, odd_v = [], [], [], []
    for l in range(depth):
        mod = mods[l]
        g = [norm_g[l, s].reshape(1, D_MODEL) for s in range(3)]

        ffn_p = functools.partial(_ffn, mod=mod, row_fn=row_p, tm=tm_p)
        ffn_s = functools.partial(_ffn, mod=mod, row_fn=row_s, tm=TOKEN_TILE)
        xp = ffn_p(xp, g=g[0], w13=wb["w13", l, 0], w2=wb["w2", l, 0])
        if l == 0:
            xs, copies = ffn_s(xs, g=g[0], w13=wb["w13", l, 0], w2=wb["w2", l, 0], casts=[src[key] for key in first])
            wb.update(zip(first, copies))
        else:
            xs = ffn_s(xs, g=g[0], w13=wb["w13", l, 0], w2=wb["w2", l, 0])
        w13b, w2b = wb["w13", l, 1], wb["w2", l, 1]

        if l % 2 == 0:
            e = l // 2
            lam_init = 0.8 - 0.6 * math.exp(-0.3 * l)
            w_in, w_out = wb["ein", e], wb["eout", e]
            qkg = jnp.tile(even_qk_norm[e], (1, 2))
            sub = even_subln[e].reshape(1, A_VDIM)

            q, k, v, bg, u = _even_in(xp, mod, g[1], w_in, qkg, None, row_fn=row_p, tm=tm_p, kv_dtype=F32)
            o = _even_attn_p(q, k, v, even_lambda[e], sub, seq=seq, lam_init=lam_init)
            xp = ffn_p(xp, g=g[2], w13=w13b, w2=w2b, mixer=(w_out, o, bg, u, even_conv_w[e], seq))
            even_k.append(k.reshape(batch, seq, A_HEADS, 2 * HEAD_DIM))
            even_v.append(v.reshape(batch, seq, A_HEADS, A_VDIM))

            q, k, v, bg, u = _even_in(xs, mod, g[1], w_in, qkg, rope_tabs, row_fn=row_s, tm=TOKEN_TILE,
                                      kv_dtype=BF16)
            kc = cache_even_k[:, e].reshape(dec_batch * past, A_QK)
            vc = cache_even_v[:, e].reshape(dec_batch * past, A_V)
            casts = [src[key] for key in rest] if l == 0 else []
            o = _even_attn_s(q, k, v, kc, vc, even_lambda[e], sub, seq=dec_seq, past=past, tq=EVEN_Q_TILE,
                             lam_init=lam_init, casts=casts)
            if casts:
                o, copies = o
                wb.update(zip(rest, copies))
            xs = ffn_s(xs, g=g[2], w13=w13b, w2=w2b, mixer=(w_out, o, bg, u, even_conv_w[e], dec_seq))
        else:
            e = l // 2
            w_in, w_out = wb["oin", e], wb["oout", e]
            qkg = jnp.tile(odd_qk_norm[e], (1, 2))
            sink = odd_sink[e]

            q, kd, vd, k, v = _odd_in(xp, mod, g[1], w_in, qkg, None, row_fn=row_p, tm=tm_p, keep_kv=True)
            o = _odd_attn_p(q, kd, vd, sink, seq=seq)
            xp = ffn_p(xp, g=g[2], w13=w13b, w2=w2b, mixer=(w_out, o))
            odd_k.append(k.reshape(batch, seq, C_KV_HEADS, HEAD_DIM))
            odd_v.append(v.reshape(batch, seq, C_KV_HEADS, HEAD_DIM))

            q, kd, vd = _odd_in(xs, mod, g[1], w_in, qkg, rope_tabs, row_fn=row_s, tm=TOKEN_TILE, keep_kv=False)
            kc = cache_odd_k[:, e]
            vc = cache_odd_v[:, e]
            o = _odd_attn_s(q, kd, vd, kc, vc, sink, seq=dec_seq, past=past, tq=ODD_Q_TILE)
            xs = ffn_s(xs, g=g[2], w13=w13b, w2=w2b, mixer=(w_out, o))

    return (xp.reshape(batch, seq, D_MODEL), xs.reshape(dec_batch, dec_seq, D_MODEL),
            jnp.stack(even_k, axis=1), jnp.stack(even_v, axis=1),
            jnp.stack(odd_k, axis=1), jnp.stack(odd_v, axis=1))
```

```python
import functools
import math

import jax
import jax.numpy as jnp
import numpy as np
from jax import lax
from jax.experimental import pallas as pl
from jax.experimental.pallas import tpu as pltpu

F32 = jnp.float32
BF16 = jnp.bfloat16

D_MODEL = 1024
GRID_W = 64
HEAD_DIM = 64
AXIS_DIM = HEAD_DIM // 2
A_HEADS = 4
A_VDIM = 2 * HEAD_DIM
CONV_DIM = 512
C_HEADS = 16
C_KV_HEADS = 4
GROUP = C_HEADS // C_KV_HEADS
WINDOW = 128
D_FF = 2816
ROPE_BASE = 10000.0
N_MOD = 9
EPS = 1e-6
A_QK = A_HEADS * 2 * HEAD_DIM
A_V = A_HEADS * A_VDIM
EVEN_IN = 2 * A_QK + A_V + 3 * CONV_DIM
ODD_IN = (C_HEADS + 2 * C_KV_HEADS) * HEAD_DIM
LOG2E = math.log2(math.e)
Q_PRESCALE = HEAD_DIM ** -0.5 * LOG2E

LANES = 128
SUBLANES = 8
BF16_SUBLANES = 16
MOD_ROWS = 16
VMEM_LIMIT_BYTES = 56 * 2 ** 20

TOKEN_TILE = 512
FFN_ROW_PARTS = 2
EVEN_Q_TILE = 512
EVEN_Q_SUB = 512
ODD_Q_TILE = 512
ODD_Q_SUB = WINDOW
ODD_P_SEQS = 2


def _params(*semantics):
    return pltpu.CompilerParams(dimension_semantics=semantics, vmem_limit_bytes=VMEM_LIMIT_BYTES)


def _sigmoid(x):
    return 1.0 / (1.0 + jnp.exp(-x))


def _mod_norm(x, g, shift, scale):
    y = x * lax.rsqrt(jnp.mean(x * x, axis=-1, keepdims=True) + EPS)
    return (y * g) * (1.0 + scale) + shift


def _mod_row(mod_ref, k, r):
    return mod_ref[k, pl.ds(r, 1), :]


def _row_fn(row0, tokens_per_row, tile):
    return lambda i: row0 + (i * tile) // tokens_per_row


def _lane_lo(shape):
    return lax.broadcasted_iota(jnp.int32, shape, len(shape) - 1) < HEAD_DIM


def _seg_rms(xs, gain):
    lo = _lane_lo(xs.shape)
    sq = xs * xs
    s_lo = jnp.sum(jnp.where(lo, sq, 0.0), axis=-1, keepdims=True)
    s_hi = jnp.sum(jnp.where(lo, 0.0, sq), axis=-1, keepdims=True)
    inv = lax.rsqrt(jnp.where(lo, s_lo, s_hi) * (1.0 / HEAD_DIM) + EPS)
    return (xs * inv) * gain


def _half_sum_matrix():
    r = lax.broadcasted_iota(jnp.int32, (2 * LANES, LANES), 0)
    c = lax.broadcasted_iota(jnp.int32, (2 * LANES, LANES), 1)
    return jnp.where((r & HEAD_DIM) == (c & HEAD_DIM), 1.0, 0.0).astype(BF16)


def _seg_rms_mxu(xs, gain, ones2):
    sq = xs * xs
    head = sq.astype(BF16)
    rest = (sq - head.astype(F32)).astype(BF16)
    ssum = _dot(jnp.concatenate([head, rest], axis=-1), ones2)
    return (xs * lax.rsqrt(ssum * (1.0 / HEAD_DIM) + EPS)) * gain


def _rope(xs, cos, sin_signed):
    lane = lax.broadcasted_iota(jnp.int32, xs.shape, 1)
    first = (lane & (AXIS_DIM // 2)) == 0
    partner = jnp.where(first, pltpu.roll(xs, LANES - AXIS_DIM // 2, 1), pltpu.roll(xs, AXIS_DIM // 2, 1))
    return xs * cos + partner * sin_signed


def _dot_t(a, b):
    return lax.dot_general(a, b, (((1,), (1,)), ((), ())), preferred_element_type=F32)


def _dot(a, b):
    return jnp.dot(a, b, preferred_element_type=F32)


def _adaln_kernel(c_ref, w_ref, b_ref, o_ref):
    c = c_ref[...]
    s = c * _sigmoid(c)
    o_ref[0, 0] = _dot(s.astype(BF16), w_ref[0].astype(BF16)) + b_ref[0]


def _adaln(c_rows, w_mod, b_mod):
    depth = w_mod.shape[0]
    b3 = b_mod.reshape(depth * N_MOD, 1, D_MODEL)
    return pl.pallas_call(
        _adaln_kernel,
        out_shape=jax.ShapeDtypeStruct((depth, N_MOD, MOD_ROWS, D_MODEL), F32),
        grid=(depth, N_MOD),
        in_specs=[
            pl.BlockSpec((MOD_ROWS, D_MODEL), lambda l, j: (0, 0)),
            pl.BlockSpec((1, D_MODEL, D_MODEL), lambda l, j: (l, 0, j)),
            pl.BlockSpec((1, 1, D_MODEL), lambda l, j: (l * N_MOD + j, 0, 0)),
        ],
        out_specs=pl.BlockSpec((1, 1, MOD_ROWS, D_MODEL), lambda l, j: (l, j, 0, 0)),
        compiler_params=_params("parallel", "parallel"),
        name="adaln",
    )(c_rows, w_mod, b3)


def _ffn_tail(x, r, mod_ref, g_ref, w13_ref, w2_ref, k0):
    part = x.shape[0] // FFN_ROW_PARTS
    xs = [x[p * part:(p + 1) * part] for p in range(FFN_ROW_PARTS)]
    hs = [_mod_norm(xr, g_ref[...], _mod_row(mod_ref, k0, r), _mod_row(mod_ref, k0 + 1, r)) for xr in xs]
    a_s = [_dot(h.astype(BF16), w13_ref[...]) for h in hs]
    acts = [(a[:, :D_FF] * _sigmoid(a[:, :D_FF])) * a[:, D_FF:] for a in a_s]
    ys = [_dot(act.astype(BF16), w2_ref[...]) for act in acts]
    gate = 0.5 * _mod_row(mod_ref, k0 + 2, r)
    return jnp.concatenate([xr + gate * y for xr, y in zip(xs, ys)], axis=0)


def _ffn_kernel(x_ref, mod_ref, g_ref, w13_ref, w2_ref, o_ref, *, row_fn):
    o_ref[...] = _ffn_tail(x_ref[...], row_fn(pl.program_id(0)), mod_ref, g_ref, w13_ref, w2_ref, 0)


def _conv_gate(bg_ref, u_ref, up_ref, un_ref, cw_ref, i, seq, tm):
    u = u_ref[...]
    row = lax.broadcasted_iota(jnp.int32, (tm, 1), 0)
    pos = (row + i * tm) % seq
    u_dn = jnp.where(row == 0, up_ref[SUBLANES - 1:SUBLANES, :], pltpu.roll(u, 1, 0))
    u_dn = jnp.where(pos == 0, 0.0, u_dn)
    u_up = jnp.where(row == tm - 1, un_ref[0:1, :], pltpu.roll(u, tm - 1, 0))
    u_up = jnp.where(pos == seq - 1, 0.0, u_up)
    return bg_ref[...] * (cw_ref[0:1, :] * u_dn + cw_ref[1:2, :] * u + cw_ref[2:3, :] * u_up)


def _even_out_ffn_kernel(x_ref, a_ref, bg_ref, u_ref, up_ref, un_ref, cw_ref, wo_ref, mod_ref, g_ref, w13_ref, w2_ref,
                         o_ref, *, row_fn, seq, tm):
    i = pl.program_id(0)
    r = row_fn(i)
    y = _conv_gate(bg_ref, u_ref, up_ref, un_ref, cw_ref, i, seq, tm)
    mix = _dot(jnp.concatenate([a_ref[...], y.astype(BF16)], axis=-1), wo_ref[...])
    x = x_ref[...] + _mod_row(mod_ref, 5, r) * mix
    o_ref[...] = _ffn_tail(x, r, mod_ref, g_ref, w13_ref, w2_ref, 6)


def _odd_out_ffn_kernel(x_ref, a_ref, wo_ref, mod_ref, g_ref, w13_ref, w2_ref, o_ref, *, row_fn):
    r = row_fn(pl.program_id(0))
    x = x_ref[...] + _mod_row(mod_ref, 5, r) * _dot(a_ref[...], wo_ref[...])
    o_ref[...] = _ffn_tail(x, r, mod_ref, g_ref, w13_ref, w2_ref, 6)


def _resident(shape, index=None):
    index = (0,) * len(shape) if index is None else index
    return pl.BlockSpec(shape, lambda *_: index, pipeline_mode=pl.Buffered(1))


def _cast_specs(casts, steps, flat_step=lambda i: i):
    in_specs, out_specs, out_shapes = [], [], []
    for arr, lead in casts:
        rows_total, cols = arr.shape[-2:]
        nblk = next(d for d in range(steps, 0, -1)
                    if steps % d == 0 and rows_total % d == 0 and (rows_total // d) % BF16_SUBLANES == 0)
        rows, rep = rows_total // nblk, steps // nblk
        in_specs.append(pl.BlockSpec((1,) * len(lead) + (rows, cols),
                                     lambda *idx, lead=lead, rep=rep: lead + (flat_step(*idx) // rep, 0)))
        out_specs.append(pl.BlockSpec((rows, cols), lambda *idx, rep=rep: (flat_step(*idx) // rep, 0)))
        out_shapes.append(jax.ShapeDtypeStruct((rows_total, cols), BF16))
    return in_specs, out_specs, out_shapes


def _with_casts(body, n_in, n_out, n_cast):
    if n_cast == 0:
        return body

    def kernel(*refs):
        outs = n_in + n_cast
        body(*refs[:n_in], *refs[outs:outs + n_out], *refs[outs + n_out + n_cast:])
        for src, dst in zip(refs[n_in:outs], refs[outs + n_out:outs + n_out + n_cast]):
            dst[...] = src[(0,) * (len(src.shape) - 2)].astype(dst.dtype)

    return kernel


def _ffn(x, mod, g, w13, w2, *, row_fn, tm, mixer=None, casts=()):
    t = x.shape[0]
    tile = lambda n: pl.BlockSpec((tm, n), lambda i: (i, 0))
    ffn_specs = [
        _resident((N_MOD, MOD_ROWS, D_MODEL)),
        _resident((1, D_MODEL)),
        _resident((D_MODEL, 2 * D_FF)),
        _resident((D_FF, D_MODEL)),
    ]
    ffn_args = [mod, g, w13, w2]
    if mixer is None:
        body = functools.partial(_ffn_kernel, row_fn=row_fn)
        in_specs, args, name = [tile(D_MODEL)], [x], "ffn"
    elif len(mixer) == 2:
        w_out, attn = mixer
        body = functools.partial(_odd_out_ffn_kernel, row_fn=row_fn)
        in_specs = [tile(D_MODEL), tile(attn.shape[1]), _resident(w_out.shape)]
        args, name = [x, attn, w_out], "odd_out_ffn"
    else:
        w_out, attn, bg, u, conv_w, seq = mixer
        nb = tm // SUBLANES
        last = t // SUBLANES - 1
        body = functools.partial(_even_out_ffn_kernel, row_fn=row_fn, seq=seq, tm=tm)
        in_specs = [
            tile(D_MODEL), tile(A_V), tile(CONV_DIM), tile(CONV_DIM),
            pl.BlockSpec((SUBLANES, CONV_DIM), lambda i: (jnp.maximum(i * nb - 1, 0), 0)),
            pl.BlockSpec((SUBLANES, CONV_DIM), lambda i: (jnp.minimum((i + 1) * nb, last), 0)),
            _resident(conv_w.shape), _resident(w_out.shape),
        ]
        args, name = [x, attn, bg, u, u, u, conv_w, w_out], "even_out_ffn"
    steps = t // tm
    c_in, c_out, c_shapes = _cast_specs(casts, steps)
    in_specs = in_specs + ffn_specs
    out = pl.pallas_call(
        _with_casts(body, len(in_specs), 1, len(casts)),
        out_shape=(jax.ShapeDtypeStruct((t, D_MODEL), F32), *c_shapes),
        grid=(steps,),
        in_specs=in_specs + c_in,
        out_specs=(tile(D_MODEL), *c_out),
        compiler_params=_params("parallel"),
        name=name,
    )(*args, *ffn_args, *[arr for arr, _ in casts])
    return (out[0], list(out[1:])) if casts else out[0]


def _even_in_kernel(x_ref, mod_ref, g_ref, w_ref, qkg_ref, *rest, row_fn, rope):
    if rope:
        cos_ref, sin_ref, q_ref, k_ref, v_ref, bg_ref, u_ref = rest
    else:
        q_ref, k_ref, v_ref, bg_ref, u_ref = rest
    r = row_fn(pl.program_id(0))
    q_gain = qkg_ref[0:1, :] * Q_PRESCALE
    tm = x_ref.shape[0]
    halves = (slice(0, tm // 2), slice(tm // 2, tm))
    ys = []
    for rows in halves:
        h = _mod_norm(x_ref[rows, :], g_ref[...], _mod_row(mod_ref, 3, r), _mod_row(mod_ref, 4, r))
        ys.append(_dot(h.astype(BF16), w_ref[...]))
    for rows, y in zip(halves, ys):
        for hh in range(A_HEADS):
            sl = slice(hh * LANES, (hh + 1) * LANES)
            qs = _seg_rms(y[:, sl], q_gain)
            ks = _seg_rms(y[:, A_QK + hh * LANES:A_QK + (hh + 1) * LANES], qkg_ref[1:2, :])
            if rope:
                qs = _rope(qs, cos_ref[rows, :], sin_ref[rows, :])
                ks = _rope(ks, cos_ref[rows, :], sin_ref[rows, :])
            q_ref[rows, sl] = qs.astype(q_ref.dtype)
            k_ref[rows, sl] = ks.astype(k_ref.dtype)
        v_ref[rows, :] = y[:, 2 * A_QK:2 * A_QK + A_V].astype(v_ref.dtype)
        o = 2 * A_QK + A_V
        bg_ref[rows, :] = y[:, o:o + CONV_DIM]
        u_ref[rows, :] = y[:, o + CONV_DIM:o + 2 * CONV_DIM] * y[:, o + 2 * CONV_DIM:o + 3 * CONV_DIM]


def _even_in(x, mod, g, w_in, qkg, rope_tabs, *, row_fn, tm, kv_dtype):
    t = x.shape[0]
    rope = rope_tabs is not None
    block = tm
    in_specs = [
        pl.BlockSpec((block, D_MODEL), lambda i: (i, 0)),
        _resident((N_MOD, MOD_ROWS, D_MODEL)),
        _resident((1, D_MODEL)),
        _resident((D_MODEL, EVEN_IN)),
        _resident((2, LANES)),
    ]
    args = [x, mod, g, w_in, qkg]
    if rope:
        nt = rope_tabs[0].shape[0] // block
        in_specs += [pl.BlockSpec((block, LANES), lambda i: (i % nt, 0))] * 2
        args += list(rope_tabs)
    wide = lambda dt: jax.ShapeDtypeStruct((t, A_QK), dt)
    spec = pl.BlockSpec((block, A_QK), lambda i: (i, 0))
    return pl.pallas_call(
        functools.partial(_even_in_kernel, row_fn=row_fn, rope=rope),
        out_shape=(wide(BF16), wide(kv_dtype), wide(kv_dtype), wide(F32), wide(F32)),
        grid=(t // block,),
        in_specs=in_specs,
        out_specs=(spec,) * 5,
        compiler_params=_params("parallel"),
        name="even_in",
    )(*args)


def _lambda(lam_ref, lam_init):
    lf = lam_ref[...]
    a = jnp.sum(lf[0:1] * lf[1:2], axis=-1, keepdims=True)
    b = jnp.sum(lf[2:3] * lf[3:4], axis=-1, keepdims=True)
    return jnp.exp(a) - jnp.exp(b) + lam_init


def _subln(o, sub, lam_init):
    y = o * lax.rsqrt(jnp.mean(o * o, axis=-1, keepdims=True) + EPS)
    return (y * sub) * (1.0 - lam_init)


def _softmax_rows_pv(s, v1):
    m = jnp.max(s, axis=-1, keepdims=True)
    r = _dot(jnp.exp2(s - m).astype(BF16), v1)
    return r[:, :LANES] * (1.0 / r[:, LANES:])


def _diff_attn_heads(q_ref, head_kv, lam_ref, sub_ref, o_ref, lam_init):
    lam = _lambda(lam_ref, lam_init)
    tq = q_ref.shape[0]
    sub_rows = min(tq, EVEN_Q_SUB)
    units = [(slice(rb * sub_rows, (rb + 1) * sub_rows), h)
             for rb in range(tq // sub_rows) for h in range(A_HEADS)]

    def scores(unit):
        rows, h = unit
        q = q_ref[rows, h * LANES:(h + 1) * LANES]
        k, _ = head_kv(h)
        lo = _lane_lo(q.shape)
        zero = jnp.zeros_like(q)
        return _dot_t(jnp.where(lo, q, zero), k), _dot_t(jnp.where(lo, zero, q), k)

    s_next = scores(units[0])
    for n, (rows, h) in enumerate(units):
        s1, s2 = s_next
        if n + 1 < len(units):
            s_next = scores(units[n + 1])
        _, v1 = head_kv(h)
        o = _softmax_rows_pv(s1, v1) - lam * _softmax_rows_pv(s2, v1)
        o_ref[rows, h * LANES:(h + 1) * LANES] = _subln(o, sub_ref[...], lam_init).astype(o_ref.dtype)


def _even_attn_p_kernel(q_ref, k_ref, v_ref, lam_ref, sub_ref, o_ref, *, lam_init):
    ones = jnp.ones((k_ref.shape[0], LANES), BF16)

    def head_kv(h):
        sl = slice(h * LANES, (h + 1) * LANES)
        return k_ref[:, sl].astype(BF16), jnp.concatenate([v_ref[:, sl].astype(BF16), ones], axis=-1)

    _diff_attn_heads(q_ref, head_kv, lam_ref, sub_ref, o_ref, lam_init)


def _even_attn_p(q, k, v, lam_vec, sub, *, seq, lam_init):
    t = q.shape[0]
    blk = pl.BlockSpec((seq, A_QK), lambda b: (b, 0))
    return pl.pallas_call(
        functools.partial(_even_attn_p_kernel, lam_init=lam_init),
        out_shape=jax.ShapeDtypeStruct((t, A_V), BF16),
        grid=(t // seq,),
        in_specs=[blk, blk, blk, _resident((4, HEAD_DIM)), _resident((1, A_VDIM))],
        out_specs=blk,
        compiler_params=_params("parallel"),
        name="even_attn_prompt",
    )(q, k, v, lam_vec, sub)


def _even_attn_s_kernel(q_ref, kl_ref, vl_ref, kc_ref, vc_ref, lam_ref, sub_ref, o_ref, k_scr, v1_scr, *, past, lam_init):
    @pl.when(pl.program_id(1) == 0)
    def _():
        k_scr[:past, :] = kc_ref[...].astype(BF16)
        k_scr[past:, :] = kl_ref[...]
        for h in range(A_HEADS):
            sl = slice(h * LANES, (h + 1) * LANES)
            v1_scr[h, :past, :LANES] = vc_ref[:, sl].astype(BF16)
            v1_scr[h, past:, :LANES] = vl_ref[:, sl]
            v1_scr[h, :, LANES:] = jnp.ones((v1_scr.shape[1], LANES), BF16)

    def head_kv(h):
        return k_scr[:, h * LANES:(h + 1) * LANES], v1_scr[h]

    _diff_attn_heads(q_ref, head_kv, lam_ref, sub_ref, o_ref, lam_init)


def _even_attn_s(q, kl, vl, kc, vc, lam_vec, sub, *, seq, past, tq, lam_init, casts=()):
    t = q.shape[0]
    nq = seq // tq
    qblk = pl.BlockSpec((tq, A_QK), lambda b, i: (b * nq + i, 0))
    lat = pl.BlockSpec((seq, A_QK), lambda b, i: (b, 0))
    ctx = pl.BlockSpec((past, A_QK), lambda b, i: (b, 0))
    in_specs = [qblk, lat, lat, ctx, ctx, _resident((4, HEAD_DIM)), _resident((1, A_VDIM))]
    c_in, c_out, c_shapes = _cast_specs(casts, (t // seq) * nq, lambda b, i: b * nq + i)
    body = functools.partial(_even_attn_s_kernel, past=past, lam_init=lam_init)
    out = pl.pallas_call(
        _with_casts(body, len(in_specs), 1, len(casts)),
        out_shape=(jax.ShapeDtypeStruct((t, A_V), BF16), *c_shapes),
        grid=(t // seq, nq),
        in_specs=in_specs + c_in,
        out_specs=(qblk, *c_out),
        scratch_shapes=[pltpu.VMEM((past + seq, A_QK), BF16),
                        pltpu.VMEM((A_HEADS, past + seq, 2 * LANES), BF16)],
        compiler_params=_params("parallel", "arbitrary"),
        name="even_attn_sample",
    )(q, kl, vl, kc, vc, lam_vec, sub, *[arr for arr, _ in casts])
    return (out[0], list(out[1:])) if casts else out[0]


def _dup_halves(x):
    lo = _lane_lo(x.shape)
    sw = pltpu.roll(x, HEAD_DIM, 1)
    return jnp.where(lo, x, sw), jnp.where(lo, sw, x)


def _ones_halves(x):
    lo = _lane_lo(x.shape)
    return jnp.where(lo, x, 1.0), jnp.where(lo, pltpu.roll(x, HEAD_DIM, 1), 1.0)


def _odd_in_kernel(x_ref, mod_ref, g_ref, w_ref, qkg_ref, *rest, row_fn, rope, keep_kv):
    rest = list(rest)
    if rope:
        cos_ref, sin_ref = rest[:2]
        rest = rest[2:]
    if keep_kv:
        q_ref, kd_ref, vd_ref, k_ref, v_ref = rest
    else:
        q_ref, kd_ref, vd_ref = rest
    nq = C_HEADS * HEAD_DIM
    nk = C_KV_HEADS * HEAD_DIM
    q_gain = qkg_ref[0:1, :] * Q_PRESCALE
    tm = x_ref.shape[0]
    ones2 = _half_sum_matrix()
    r = row_fn(pl.program_id(0))
    for rows in (slice(0, tm // 2), slice(tm // 2, tm)):
        h = _mod_norm(x_ref[rows, :], g_ref[...], _mod_row(mod_ref, 3, r), _mod_row(mod_ref, 4, r))
        y = _dot(h.astype(BF16), w_ref[...])
        if rope:
            cos, sin = cos_ref[rows, :], sin_ref[rows, :]
        for c in range(nq // LANES):
            sl = slice(c * LANES, (c + 1) * LANES)
            qs = _seg_rms_mxu(y[:, sl], q_gain, ones2)
            if rope:
                qs = _rope(qs, cos, sin)
            q_ref[rows, sl] = qs.astype(q_ref.dtype)
        for c in range(nk // LANES):
            sl = slice(c * LANES, (c + 1) * LANES)
            ks = _seg_rms_mxu(y[:, nq + c * LANES:nq + (c + 1) * LANES], qkg_ref[1:2, :], ones2)
            vs = y[:, nq + nk + c * LANES:nq + nk + (c + 1) * LANES]
            if keep_kv:
                k_ref[rows, sl] = ks
                v_ref[rows, sl] = vs
            if rope:
                ks = _rope(ks, cos, sin)
            for (d0, d1), ref in ((_dup_halves(ks), kd_ref), (_ones_halves(vs), vd_ref)):
                ref[rows, 2 * c * LANES:(2 * c + 1) * LANES] = d0.astype(ref.dtype)
                ref[rows, (2 * c + 1) * LANES:(2 * c + 2) * LANES] = d1.astype(ref.dtype)


def _odd_in(x, mod, g, w_in, qkg, rope_tabs, *, row_fn, tm, keep_kv):
    t = x.shape[0]
    rope = rope_tabs is not None
    block = tm
    in_specs = [
        pl.BlockSpec((block, D_MODEL), lambda i: (i, 0)),
        _resident((N_MOD, MOD_ROWS, D_MODEL)),
        _resident((1, D_MODEL)),
        _resident((D_MODEL, ODD_IN)),
        _resident((2, LANES)),
    ]
    args = [x, mod, g, w_in, qkg]
    if rope:
        nt = rope_tabs[0].shape[0] // block
        in_specs += [pl.BlockSpec((block, LANES), lambda i: (i % nt, 0))] * 2
        args += list(rope_tabs)
    nq = C_HEADS * HEAD_DIM
    nd = C_KV_HEADS * LANES
    nk = C_KV_HEADS * HEAD_DIM
    out_shape = [jax.ShapeDtypeStruct((t, nq), BF16), jax.ShapeDtypeStruct((t, nd), BF16),
                 jax.ShapeDtypeStruct((t, nd), BF16)]
    out_specs = [pl.BlockSpec((block, nq), lambda i: (i, 0)), pl.BlockSpec((block, nd), lambda i: (i, 0)),
                 pl.BlockSpec((block, nd), lambda i: (i, 0))]
    if keep_kv:
        out_shape += [jax.ShapeDtypeStruct((t, nk), F32)] * 2
        out_specs += [pl.BlockSpec((block, nk), lambda i: (i, 0))] * 2
    return pl.pallas_call(
        functools.partial(_odd_in_kernel, row_fn=row_fn, rope=rope, keep_kv=keep_kv),
        out_shape=tuple(out_shape),
        grid=(t // block,),
        in_specs=in_specs,
        out_specs=tuple(out_specs),
        compiler_params=_params("parallel"),
        name="odd_in",
    )(*args)


def _stack_heads(q_ref, g, rows):
    parts = []
    for rr in range(GROUP):
        c = g * (GROUP // 2) + rr // 2
        qc = q_ref[rows, c * LANES:(c + 1) * LANES]
        lo = _lane_lo(qc.shape)
        keep = lo if rr % 2 == 0 else jnp.logical_not(lo)
        parts.append(jnp.where(keep, qc, jnp.zeros_like(qc)))
    return jnp.concatenate(parts, axis=0)


def _gqa_group(scores, biases, vals, sinks, rows):
    es = [[] for _ in scores]
    sink_terms = []
    for rr in range(GROUP):
        rs = slice(rr * rows, (rr + 1) * rows)
        parts = [s[rs] if b is None else s[rs] + b for s, b in zip(scores, biases)]
        m = functools.reduce(jnp.maximum, [jnp.max(p, axis=-1, keepdims=True) for p in parts])
        m = jnp.maximum(m, sinks[rr])
        for j, p in enumerate(parts):
            es[j].append(jnp.exp2(p - m).astype(BF16))
        sink_terms.append(jnp.exp2(sinks[rr] - m))
    r = functools.reduce(jnp.add, [_dot(jnp.concatenate(e, axis=0), v) for e, v in zip(es, vals)])
    heads = []
    for rr in range(GROUP):
        rh = r[rr * rows:(rr + 1) * rows]
        heads.append(rh * (1.0 / (pltpu.roll(rh, HEAD_DIM, 1) + sink_terms[rr])))
    lo = _lane_lo((rows, LANES))
    slabs = [jnp.where(lo, heads[2 * c], pltpu.roll(heads[2 * c + 1], HEAD_DIM, 1)) for c in range(GROUP // 2)]
    return jnp.concatenate(slabs, axis=-1)


def _odd_attn_p_kernel(sink_ref, q_ref, kd_ref, vd_ref, o_ref, *, seq):
    g = pl.program_id(1)
    sinks = [sink_ref[g * GROUP + rr] * LOG2E for rr in range(GROUP)]
    units = [slice(u * seq, (u + 1) * seq) for u in range(q_ref.shape[0] // seq)]

    def scores(rows):
        return [_dot_t(_stack_heads(q_ref, 0, rows), kd_ref[rows, :])]

    s_next = scores(units[0])
    for n, rows in enumerate(units):
        s_cur = s_next
        if n + 1 < len(units):
            s_next = scores(units[n + 1])
        o_ref[rows, :] = _gqa_group(s_cur, [None], [vd_ref[rows, :]], sinks, seq).astype(o_ref.dtype)


def _odd_attn_p(q, kd, vd, sink, *, seq):
    t = q.shape[0]
    gw = GROUP * HEAD_DIM
    rows = seq * (ODD_P_SEQS if (t // seq) % ODD_P_SEQS == 0 else 1)
    return pl.pallas_call(
        functools.partial(_odd_attn_p_kernel, seq=seq),
        out_shape=jax.ShapeDtypeStruct((t, C_HEADS * HEAD_DIM), BF16),
        grid=(t // rows, C_KV_HEADS),
        in_specs=[
            pl.BlockSpec(memory_space=pltpu.SMEM),
            pl.BlockSpec((rows, gw), lambda b, g: (b, g)),
            pl.BlockSpec((rows, LANES), lambda b, g: (b, g)),
            pl.BlockSpec((rows, LANES), lambda b, g: (b, g)),
        ],
        out_specs=pl.BlockSpec((rows, gw), lambda b, g: (b, g)),
        compiler_params=_params("parallel", "parallel"),
        name="odd_attn_prompt",
    )(sink, q, kd, vd)


def _odd_attn_s_kernel(sink_ref, q_ref, kd_ref, vd_ref, kc_ref, vc_ref, bias_ref, o_ref, kcd_scr, vcd_scr, *, seq, tq):
    i = pl.program_id(1)
    sub = ODD_Q_SUB
    span = sub + 2 * WINDOW

    @pl.when(i == 0)
    def _():
        ones = jnp.ones((kc_ref.shape[1], HEAD_DIM), F32)
        for g in range(C_KV_HEADS):
            gl = slice(g * LANES, (g + 1) * LANES)
            k_g = kc_ref[0, :, g, :]
            kcd_scr[:, gl] = jnp.concatenate([k_g, k_g], axis=-1).astype(BF16)
            vcd_scr[:, gl] = jnp.concatenate([vc_ref[0, :, g, :], ones], axis=-1).astype(BF16)

    units = []
    for jb in range(tq // sub):
        q0 = i * tq + jb * sub
        start = pl.multiple_of(jnp.clip(q0 - WINDOW, 0, seq - span), WINDOW)
        bias = bias_ref[(q0 - start) // WINDOW]
        units += [(slice(jb * sub, (jb + 1) * sub), start, bias, g) for g in range(C_KV_HEADS)]

    def scores(unit):
        rows, start, _, g = unit
        gl = slice(g * LANES, (g + 1) * LANES)
        q4 = _stack_heads(q_ref, g, rows)
        return [_dot_t(q4, kcd_scr[:, gl]), _dot_t(q4, kd_ref[pl.ds(start, span), gl])]

    s_next = scores(units[0])
    for n, (rows, start, bias, g) in enumerate(units):
        s_cur = s_next
        if n + 1 < len(units):
            s_next = scores(units[n + 1])
        gl = slice(g * LANES, (g + 1) * LANES)
        sinks = [sink_ref[g * GROUP + rr] * LOG2E for rr in range(GROUP)]
        og = _gqa_group(s_cur, [None, bias], [vcd_scr[:, gl], vd_ref[pl.ds(start, span), gl]], sinks, sub)
        o_ref[rows, g * GROUP * HEAD_DIM:(g + 1) * GROUP * HEAD_DIM] = og.astype(o_ref.dtype)


def _odd_attn_s(q, kd, vd, kc, vc, sink, *, seq, past, tq):
    t = q.shape[0]
    nq = seq // tq
    nqw = C_HEADS * HEAD_DIM
    nd = C_KV_HEADS * LANES
    nk = C_KV_HEADS * HEAD_DIM
    span = ODD_Q_SUB + 2 * WINDOW
    rel = (np.arange(3)[:, None, None] * WINDOW + np.arange(ODD_Q_SUB)[None, :, None]
           - np.arange(span)[None, None, :])
    bias = jnp.asarray(np.where(np.abs(rel) <= WINDOW, 0.0, -np.inf).astype(np.float32))
    return pl.pallas_call(
        functools.partial(_odd_attn_s_kernel, seq=seq, tq=tq),
        out_shape=jax.ShapeDtypeStruct((t, nqw), BF16),
        grid=(t // seq, nq),
        in_specs=[
            pl.BlockSpec(memory_space=pltpu.SMEM),
            pl.BlockSpec((tq, nqw), lambda b, i: (b * nq + i, 0)),
            pl.BlockSpec((seq, nd), lambda b, i: (b, 0)),
            pl.BlockSpec((seq, nd), lambda b, i: (b, 0)),
            pl.BlockSpec((1, past, C_KV_HEADS, HEAD_DIM), lambda b, i: (b, 0, 0, 0)),
            pl.BlockSpec((1, past, C_KV_HEADS, HEAD_DIM), lambda b, i: (b, 0, 0, 0)),
            _resident(bias.shape),
        ],
        out_specs=pl.BlockSpec((tq, nqw), lambda b, i: (b * nq + i, 0)),
        scratch_shapes=[pltpu.VMEM((past, nd), BF16), pltpu.VMEM((past, nd), BF16)],
        compiler_params=_params("parallel", "arbitrary"),
        name="odd_attn_sample",
    )(sink, q, kd, vd, kc, vc, bias)


def _rope_tables(n):
    rows = n // GRID_W
    row = np.repeat(np.arange(rows, dtype=np.float64), GRID_W)
    col = np.tile(np.arange(GRID_W, dtype=np.float64), rows)
    inv = ROPE_BASE ** (-np.arange(0, AXIS_DIM, 2, dtype=np.float64) / AXIS_DIM)
    ang_r = row[:, None] * inv[None, :]
    ang_c = col[:, None] * inv[None, :]
    ang = np.concatenate([ang_r, ang_r, ang_c, ang_c], axis=-1)
    sign = np.where((np.arange(HEAD_DIM) & (AXIS_DIM // 2)) == 0, -1.0, 1.0)
    cos = np.tile(np.cos(ang), (1, 2)).astype(np.float32)
    sin = np.tile(np.sin(ang) * sign[None, :], (1, 2)).astype(np.float32)
    return jnp.asarray(cos), jnp.asarray(sin)


def kernel(x_prompt, x_sample, cache_even_k, cache_even_v, cache_odd_k, cache_odd_v, c, c_ctx, w_mod, b_mod, norm_g, ffn_w13, ffn_w2, even_w_in, even_w_out, even_qk_norm, even_lambda, even_subln, even_conv_w, odd_w_in, odd_w_out, odd_qk_norm, odd_sink):
    batch, seq, _ = x_prompt.shape
    dec_batch, dec_seq, _ = x_sample.shape
    past = cache_even_k.shape[2]
    depth = w_mod.shape[0]
    assert 1 + dec_batch <= MOD_ROWS and seq % SUBLANES == 0 and dec_seq % TOKEN_TILE == 0

    xp = x_prompt.reshape(batch * seq, D_MODEL)
    xs = x_sample.reshape(dec_batch * dec_seq, D_MODEL)
    tm_p = min(TOKEN_TILE, batch * seq)
    row_p = _row_fn(0, batch * seq, tm_p)
    row_s = _row_fn(1, dec_seq, TOKEN_TILE)

    c_rows = jnp.zeros((MOD_ROWS, D_MODEL), F32).at[0].set(c_ctx).at[1:1 + dec_batch].set(c)
    mods = _adaln(c_rows, w_mod, b_mod)
    rope_tabs = _rope_tables(dec_seq)

    src = {}
    for l in range(depth):
        for s in range(2):
            src["w13", l, s] = (ffn_w13, (l, s))
            src["w2", l, s] = (ffn_w2, (l, s))
        names, w_in_all, w_out_all = (("ein", "eout"), even_w_in, even_w_out) if l % 2 == 0 else (
            ("oin", "oout"), odd_w_in, odd_w_out)
        src[names[0], l // 2] = (w_in_all, (l // 2,))
        src[names[1], l // 2] = (w_out_all, (l // 2,))
    wb = {("w13", 0, 0): ffn_w13[0, 0].astype(BF16), ("w2", 0, 0): ffn_w2[0, 0].astype(BF16)}
    first = [("w13", 0, 1), ("w2", 0, 1), ("ein", 0), ("eout", 0)]
    rest = [key for key in src if key not in wb and key not in first]

    even_k, even_v, odd_k, odd_v = [], [], [], []
    for l in range(depth):
        mod = mods[l]
        g = [norm_g[l, s].reshape(1, D_MODEL) for s in range(3)]

        ffn_p = functools.partial(_ffn, mod=mod, row_fn=row_p, tm=tm_p)
        ffn_s = functools.partial(_ffn, mod=mod, row_fn=row_s, tm=TOKEN_TILE)
        xp = ffn_p(xp, g=g[0], w13=wb["w13", l, 0], w2=wb["w2", l, 0])
        if l == 0:
            xs, copies = ffn_s(xs, g=g[0], w13=wb["w13", l, 0], w2=wb["w2", l, 0], casts=[src[key] for key in first])
            wb.update(zip(first, copies))
        else:
            xs = ffn_s(xs, g=g[0], w13=wb["w13", l, 0], w2=wb["w2", l, 0])
        w13b, w2b = wb["w13", l, 1], wb["w2", l, 1]

        if l % 2 == 0:
            e = l // 2
            lam_init = 0.8 - 0.6 * math.exp(-0.3 * l)
            w_in, w_out = wb["ein", e], wb["eout", e]
            qkg = jnp.tile(even_qk_norm[e], (1, 2))
            sub = even_subln[e].reshape(1, A_VDIM)

            q, k, v, bg, u = _even_in(xp, mod, g[1], w_in, qkg, None, row_fn=row_p, tm=tm_p, kv_dtype=F32)
            o = _even_attn_p(q, k, v, even_lambda[e], sub, seq=seq, lam_init=lam_init)
            xp = ffn_p(xp, g=g[2], w13=w13b, w2=w2b, mixer=(w_out, o, bg, u, even_conv_w[e], seq))
            even_k.append(k.reshape(batch, seq, A_HEADS, 2 * HEAD_DIM))
            even_v.append(v.reshape(batch, seq, A_HEADS, A_VDIM))

            q, k, v, bg, u = _even_in(xs, mod, g[1], w_in, qkg, rope_tabs, row_fn=row_s, tm=TOKEN_TILE,
                                      kv_dtype=BF16)
            kc = cache_even_k[:, e].reshape(dec_batch * past, A_QK)
            vc = cache_even_v[:, e].reshape(dec_batch * past, A_V)
            casts = [src[key] for key in rest] if l == 0 else []
            o = _even_attn_s(q, k, v, kc, vc, even_lambda[e], sub, seq=dec_seq, past=past, tq=EVEN_Q_TILE,
                             lam_init=lam_init, casts=casts)
            if casts:
                o, copies = o
                wb.update(zip(rest, copies))
            xs = ffn_s(xs, g=g[2], w13=w13b, w2=w2b, mixer=(w_out, o, bg, u, even_conv_w[e], dec_seq))
        else:
            e = l // 2
            w_in, w_out = wb["oin", e], wb["oout", e]
            qkg = jnp.tile(odd_qk_norm[e], (1, 2))
            sink = odd_sink[e]

            q, kd, vd, k, v = _odd_in(xp, mod, g[1], w_in, qkg, None, row_fn=row_p, tm=tm_p, keep_kv=True)
            o = _odd_attn_p(q, kd, vd, sink, seq=seq)
            xp = ffn_p(xp, g=g[2], w13=w13b, w2=w2b, mixer=(w_out, o))
            odd_k.append(k.reshape(batch, seq, C_KV_HEADS, HEAD_DIM))
            odd_v.append(v.reshape(batch, seq, C_KV_HEADS, HEAD_DIM))

            q, kd, vd = _odd_in(xs, mod, g[1], w_in, qkg, rope_tabs, row_fn=row_s, tm=TOKEN_TILE, keep_kv=False)
            kc = cache_odd_k[:, e]
            vc = cache_odd_v[:, e]
            o = _odd_attn_s(q, kd, vd, kc, vc, sink, seq=dec_seq, past=past, tq=ODD_Q_TILE)
            xs = ffn_s(xs, g=g[2], w13=w13b, w2=w2b, mixer=(w_out, o))

    return (xp.reshape(batch, seq, D_MODEL), xs.reshape(dec_batch, dec_seq, D_MODEL),
            jnp.stack(even_k, axis=1), jnp.stack(even_v, axis=1),
            jnp.stack(odd_k, axis=1), jnp.stack(odd_v, axis=1))
```

```python
import functools
import math

import jax
import jax.numpy as jnp
import numpy as np
from jax import lax
from jax.experimental import pallas as pl
from jax.experimental.pallas import tpu as pltpu

F32 = jnp.float32
BF16 = jnp.bfloat16

D_MODEL = 1024
GRID_W = 64
HEAD_DIM = 64
AXIS_DIM = HEAD_DIM // 2
A_HEADS = 4
A_VDIM = 2 * HEAD_DIM
CONV_DIM = 512
C_HEADS = 16
C_KV_HEADS = 4
GROUP = C_HEADS // C_KV_HEADS
WINDOW = 128
D_FF = 2816
ROPE_BASE = 10000.0
N_MOD = 9
EPS = 1e-6
A_QK = A_HEADS * 2 * HEAD_DIM
A_V = A_HEADS * A_VDIM
EVEN_IN = 2 * A_QK + A_V + 3 * CONV_DIM
ODD_IN = (C_HEADS + 2 * C_KV_HEADS) * HEAD_DIM
LOG2E = math.log2(math.e)
Q_PRESCALE = HEAD_DIM ** -0.5 * LOG2E

LANES = 128
SUBLANES = 8
BF16_SUBLANES = 16
MOD_ROWS = 16
VMEM_LIMIT_BYTES = 56 * 2 ** 20

ADALN_CHUNKS = 3
TOKEN_TILE = 512
EVEN_Q_TILE = 512
EVEN_Q_SUB = 512
ODD_Q_TILE = 512
ODD_Q_SUB = WINDOW
ODD_P_SEQS = 2


def _params(*semantics):
    return pltpu.CompilerParams(dimension_semantics=semantics, vmem_limit_bytes=VMEM_LIMIT_BYTES)


def _sigmoid(x):
    return 1.0 / (1.0 + jnp.exp(-x))


def _mod_norm(x, g, shift, scale):
    y = x * lax.rsqrt(jnp.mean(x * x, axis=-1, keepdims=True) + EPS)
    return (y * g) * (1.0 + scale) + shift


def _mod_row(mod_ref, k, r):
    return mod_ref[k, pl.ds(r, 1), :]


def _row_fn(row0, tokens_per_row, tile):
    return lambda i: row0 + (i * tile) // tokens_per_row


def _lane_lo(shape):
    return lax.broadcasted_iota(jnp.int32, shape, len(shape) - 1) < HEAD_DIM


def _seg_rms(xs, gain):
    lo = _lane_lo(xs.shape)
    sq = xs * xs
    s_lo = jnp.sum(jnp.where(lo, sq, 0.0), axis=-1, keepdims=True)
    s_hi = jnp.sum(jnp.where(lo, 0.0, sq), axis=-1, keepdims=True)
    inv = lax.rsqrt(jnp.where(lo, s_lo, s_hi) * (1.0 / HEAD_DIM) + EPS)
    return (xs * inv) * gain


def _half_sum_matrix():
    r = lax.broadcasted_iota(jnp.int32, (2 * LANES, LANES), 0)
    c = lax.broadcasted_iota(jnp.int32, (2 * LANES, LANES), 1)
    return jnp.where((r & HEAD_DIM) == (c & HEAD_DIM), 1.0, 0.0).astype(BF16)


def _seg_rms_mxu(xs, gain, ones2):
    sq = xs * xs
    head = sq.astype(BF16)
    rest = (sq - head.astype(F32)).astype(BF16)
    ssum = _dot(jnp.concatenate([head, rest], axis=-1), ones2)
    return (xs * lax.rsqrt(ssum * (1.0 / HEAD_DIM) + EPS)) * gain


def _rope(xs, cos, sin_signed):
    lane = lax.broadcasted_iota(jnp.int32, xs.shape, 1)
    first = (lane & (AXIS_DIM // 2)) == 0
    partner = jnp.where(first, pltpu.roll(xs, LANES - AXIS_DIM // 2, 1), pltpu.roll(xs, AXIS_DIM // 2, 1))
    return xs * cos + partner * sin_signed


def _dot_t(a, b):
    return lax.dot_general(a, b, (((1,), (1,)), ((), ())), preferred_element_type=F32)


def _dot(a, b):
    return jnp.dot(a, b, preferred_element_type=F32)


def _adaln_kernel(c_ref, w_ref, b_ref, o_ref):
    c = c_ref[...]
    s = c * _sigmoid(c)
    m = _dot(s.astype(BF16), w_ref[0].astype(BF16)) + b_ref[0]
    for k in range(ADALN_CHUNKS):
        o_ref[0, k] = m[:, k * D_MODEL:(k + 1) * D_MODEL]


def _adaln(c_rows, w_mod, b_mod):
    depth = w_mod.shape[0]
    steps = N_MOD // ADALN_CHUNKS
    cols = ADALN_CHUNKS * D_MODEL
    b3 = b_mod.reshape(depth * steps, 1, cols)
    return pl.pallas_call(
        _adaln_kernel,
        out_shape=jax.ShapeDtypeStruct((depth, N_MOD, MOD_ROWS, D_MODEL), F32),
        grid=(depth, steps),
        in_specs=[
            pl.BlockSpec((MOD_ROWS, D_MODEL), lambda l, j: (0, 0)),
            pl.BlockSpec((1, D_MODEL, cols), lambda l, j: (l, 0, j)),
            pl.BlockSpec((1, 1, cols), lambda l, j: (l * steps + j, 0, 0)),
        ],
        out_specs=pl.BlockSpec((1, ADALN_CHUNKS, MOD_ROWS, D_MODEL), lambda l, j: (l, j, 0, 0)),
        compiler_params=_params("parallel", "parallel"),
        name="adaln",
    )(c_rows, w_mod, b3)


def _ffn_tail(x, r, mod_ref, g_ref, w13_ref, w2_ref, k0):
    half = x.shape[0] // 2
    xs = [x[:half], x[half:]]
    hs = [_mod_norm(xr, g_ref[...], _mod_row(mod_ref, k0, r), _mod_row(mod_ref, k0 + 1, r)) for xr in xs]
    a_s = [_dot(h.astype(BF16), w13_ref[...]) for h in hs]
    acts = [(a[:, :D_FF] * _sigmoid(a[:, :D_FF])) * a[:, D_FF:] for a in a_s]
    ys = [_dot(act.astype(BF16), w2_ref[...]) for act in acts]
    gate = 0.5 * _mod_row(mod_ref, k0 + 2, r)
    return jnp.concatenate([xr + gate * y for xr, y in zip(xs, ys)], axis=0)


def _ffn_kernel(x_ref, mod_ref, g_ref, w13_ref, w2_ref, o_ref, *, row_fn):
    o_ref[...] = _ffn_tail(x_ref[...], row_fn(pl.program_id(0)), mod_ref, g_ref, w13_ref, w2_ref, 0)


def _conv_gate(bg_ref, u_ref, up_ref, un_ref, cw_ref, i, seq, tm):
    u = u_ref[...]
    row = lax.broadcasted_iota(jnp.int32, (tm, 1), 0)
    pos = (row + i * tm) % seq
    u_dn = jnp.where(row == 0, up_ref[SUBLANES - 1:SUBLANES, :], pltpu.roll(u, 1, 0))
    u_dn = jnp.where(pos == 0, 0.0, u_dn)
    u_up = jnp.where(row == tm - 1, un_ref[0:1, :], pltpu.roll(u, tm - 1, 0))
    u_up = jnp.where(pos == seq - 1, 0.0, u_up)
    return bg_ref[...] * (cw_ref[0:1, :] * u_dn + cw_ref[1:2, :] * u + cw_ref[2:3, :] * u_up)


def _even_out_ffn_kernel(x_ref, a_ref, bg_ref, u_ref, up_ref, un_ref, cw_ref, wo_ref, mod_ref, g_ref, w13_ref, w2_ref,
                         o_ref, *, row_fn, seq, tm):
    i = pl.program_id(0)
    r = row_fn(i)
    y = _conv_gate(bg_ref, u_ref, up_ref, un_ref, cw_ref, i, seq, tm)
    mix = _dot(jnp.concatenate([a_ref[...], y.astype(BF16)], axis=-1), wo_ref[...])
    x = x_ref[...] + _mod_row(mod_ref, 5, r) * mix
    o_ref[...] = _ffn_tail(x, r, mod_ref, g_ref, w13_ref, w2_ref, 6)


def _odd_out_ffn_kernel(x_ref, a_ref, wo_ref, mod_ref, g_ref, w13_ref, w2_ref, o_ref, *, row_fn):
    r = row_fn(pl.program_id(0))
    x = x_ref[...] + _mod_row(mod_ref, 5, r) * _dot(a_ref[...], wo_ref[...])
    o_ref[...] = _ffn_tail(x, r, mod_ref, g_ref, w13_ref, w2_ref, 6)


def _resident(shape, index=None):
    index = (0,) * len(shape) if index is None else index
    return pl.BlockSpec(shape, lambda *_: index, pipeline_mode=pl.Buffered(1))


def _cast_specs(casts, steps, flat_step=lambda i: i):
    in_specs, out_specs, out_shapes = [], [], []
    for arr, lead in casts:
        rows_total, cols = arr.shape[-2:]
        nblk = next(d for d in range(steps, 0, -1)
                    if steps % d == 0 and rows_total % d == 0 and (rows_total // d) % BF16_SUBLANES == 0)
        rows, rep = rows_total // nblk, steps // nblk
        in_specs.append(pl.BlockSpec((1,) * len(lead) + (rows, cols),
                                     lambda *idx, lead=lead, rep=rep: lead + (flat_step(*idx) // rep, 0)))
        out_specs.append(pl.BlockSpec((rows, cols), lambda *idx, rep=rep: (flat_step(*idx) // rep, 0)))
        out_shapes.append(jax.ShapeDtypeStruct((rows_total, cols), BF16))
    return in_specs, out_specs, out_shapes


def _with_casts(body, n_in, n_out, n_cast):
    if n_cast == 0:
        return body

    def kernel(*refs):
        outs = n_in + n_cast
        body(*refs[:n_in], *refs[outs:outs + n_out], *refs[outs + n_out + n_cast:])
        for src, dst in zip(refs[n_in:outs], refs[outs + n_out:outs + n_out + n_cast]):
            dst[...] = src[(0,) * (len(src.shape) - 2)].astype(dst.dtype)

    return kernel


def _ffn(x, mod, g, w13, w2, *, row_fn, tm, mixer=None, casts=()):
    t = x.shape[0]
    tile = lambda n: pl.BlockSpec((tm, n), lambda i: (i, 0))
    ffn_specs = [
        _resident((N_MOD, MOD_ROWS, D_MODEL)),
        _resident((1, D_MODEL)),
        _resident((D_MODEL, 2 * D_FF)),
        _resident((D_FF, D_MODEL)),
    ]
    ffn_args = [mod, g, w13, w2]
    if mixer is None:
        body = functools.partial(_ffn_kernel, row_fn=row_fn)
        in_specs, args, name = [tile(D_MODEL)], [x], "ffn"
    elif len(mixer) == 2:
        w_out, attn = mixer
        body = functools.partial(_odd_out_ffn_kernel, row_fn=row_fn)
        in_specs = [tile(D_MODEL), tile(attn.shape[1]), _resident(w_out.shape)]
        args, name = [x, attn, w_out], "odd_out_ffn"
    else:
        w_out, attn, bg, u, conv_w, seq = mixer
        nb = tm // SUBLANES
        last = t // SUBLANES - 1
        body = functools.partial(_even_out_ffn_kernel, row_fn=row_fn, seq=seq, tm=tm)
        in_specs = [
            tile(D_MODEL), tile(A_V), tile(CONV_DIM), tile(CONV_DIM),
            pl.BlockSpec((SUBLANES, CONV_DIM), lambda i: (jnp.maximum(i * nb - 1, 0), 0)),
            pl.BlockSpec((SUBLANES, CONV_DIM), lambda i: (jnp.minimum((i + 1) * nb, last), 0)),
            _resident(conv_w.shape), _resident(w_out.shape),
        ]
        args, name = [x, attn, bg, u, u, u, conv_w, w_out], "even_out_ffn"
    steps = t // tm
    c_in, c_out, c_shapes = _cast_specs(casts, steps)
    in_specs = in_specs + ffn_specs
    out = pl.pallas_call(
        _with_casts(body, len(in_specs), 1, len(casts)),
        out_shape=(jax.ShapeDtypeStruct((t, D_MODEL), F32), *c_shapes),
        grid=(steps,),
        in_specs=in_specs + c_in,
        out_specs=(tile(D_MODEL), *c_out),
        compiler_params=_params("parallel"),
        name=name,
    )(*args, *ffn_args, *[arr for arr, _ in casts])
    return (out[0], list(out[1:])) if casts else out[0]


def _even_in_kernel(x_ref, mod_ref, g_ref, w_ref, qkg_ref, *rest, row_fn, rope):
    if rope:
        cos_ref, sin_ref, q_ref, k_ref, v_ref, bg_ref, u_ref = rest
    else:
        q_ref, k_ref, v_ref, bg_ref, u_ref = rest
    r = row_fn(pl.program_id(0))
    h = _mod_norm(x_ref[...], g_ref[...], _mod_row(mod_ref, 3, r), _mod_row(mod_ref, 4, r))
    y = _dot(h.astype(BF16), w_ref[...])
    q_gain = qkg_ref[0:1, :] * Q_PRESCALE
    for hh in range(A_HEADS):
        sl = slice(hh * LANES, (hh + 1) * LANES)
        qs = _seg_rms(y[:, sl], q_gain)
        ks = _seg_rms(y[:, A_QK + hh * LANES:A_QK + (hh + 1) * LANES], qkg_ref[1:2, :])
        if rope:
            qs = _rope(qs, cos_ref[...], sin_ref[...])
            ks = _rope(ks, cos_ref[...], sin_ref[...])
        q_ref[:, sl] = qs.astype(q_ref.dtype)
        k_ref[:, sl] = ks.astype(k_ref.dtype)
    v_ref[...] = y[:, 2 * A_QK:2 * A_QK + A_V].astype(v_ref.dtype)
    o = 2 * A_QK + A_V
    bg_ref[...] = y[:, o:o + CONV_DIM]
    u_ref[...] = y[:, o + CONV_DIM:o + 2 * CONV_DIM] * y[:, o + 2 * CONV_DIM:o + 3 * CONV_DIM]


def _even_in(x, mod, g, w_in, qkg, rope_tabs, *, row_fn, tm, kv_dtype):
    t = x.shape[0]
    rope = rope_tabs is not None
    block = tm
    in_specs = [
        pl.BlockSpec((block, D_MODEL), lambda i: (i, 0)),
        _resident((N_MOD, MOD_ROWS, D_MODEL)),
        _resident((1, D_MODEL)),
        _resident((D_MODEL, EVEN_IN)),
        _resident((2, LANES)),
    ]
    args = [x, mod, g, w_in, qkg]
    if rope:
        nt = rope_tabs[0].shape[0] // block
        in_specs += [pl.BlockSpec((block, LANES), lambda i: (i % nt, 0))] * 2
        args += list(rope_tabs)
    wide = lambda dt: jax.ShapeDtypeStruct((t, A_QK), dt)
    spec = pl.BlockSpec((block, A_QK), lambda i: (i, 0))
    return pl.pallas_call(
        functools.partial(_even_in_kernel, row_fn=row_fn, rope=rope),
        out_shape=(wide(BF16), wide(kv_dtype), wide(kv_dtype), wide(F32), wide(F32)),
        grid=(t // block,),
        in_specs=in_specs,
        out_specs=(spec,) * 5,
        compiler_params=_params("parallel"),
        name="even_in",
    )(*args)


def _lambda(lam_ref, lam_init):
    lf = lam_ref[...]
    a = jnp.sum(lf[0:1] * lf[1:2], axis=-1, keepdims=True)
    b = jnp.sum(lf[2:3] * lf[3:4], axis=-1, keepdims=True)
    return jnp.exp(a) - jnp.exp(b) + lam_init


def _subln(o, sub, lam_init):
    y = o * lax.rsqrt(jnp.mean(o * o, axis=-1, keepdims=True) + EPS)
    return (y * sub) * (1.0 - lam_init)


def _softmax_rows_pv(s, v1):
    m = jnp.max(s, axis=-1, keepdims=True)
    r = _dot(jnp.exp2(s - m).astype(BF16), v1)
    return r[:, :LANES] * (1.0 / r[:, LANES:])


def _diff_attn_heads(q_ref, head_kv, lam_ref, sub_ref, o_ref, lam_init):
    lam = _lambda(lam_ref, lam_init)
    tq = q_ref.shape[0]
    sub_rows = min(tq, EVEN_Q_SUB)
    units = [(slice(rb * sub_rows, (rb + 1) * sub_rows), h)
             for rb in range(tq // sub_rows) for h in range(A_HEADS)]

    def scores(unit):
        rows, h = unit
        q = q_ref[rows, h * LANES:(h + 1) * LANES]
        k, _ = head_kv(h)
        lo = _lane_lo(q.shape)
        zero = jnp.zeros_like(q)
        return _dot_t(jnp.where(lo, q, zero), k), _dot_t(jnp.where(lo, zero, q), k)

    s_next = scores(units[0])
    for n, (rows, h) in enumerate(units):
        s1, s2 = s_next
        if n + 1 < len(units):
            s_next = scores(units[n + 1])
        _, v1 = head_kv(h)
        o = _softmax_rows_pv(s1, v1) - lam * _softmax_rows_pv(s2, v1)
        o_ref[rows, h * LANES:(h + 1) * LANES] = _subln(o, sub_ref[...], lam_init).astype(o_ref.dtype)


def _even_attn_p_kernel(q_ref, k_ref, v_ref, lam_ref, sub_ref, o_ref, *, lam_init):
    ones = jnp.ones((k_ref.shape[0], LANES), BF16)

    def head_kv(h):
        sl = slice(h * LANES, (h + 1) * LANES)
        return k_ref[:, sl].astype(BF16), jnp.concatenate([v_ref[:, sl].astype(BF16), ones], axis=-1)

    _diff_attn_heads(q_ref, head_kv, lam_ref, sub_ref, o_ref, lam_init)


def _even_attn_p(q, k, v, lam_vec, sub, *, seq, lam_init):
    t = q.shape[0]
    blk = pl.BlockSpec((seq, A_QK), lambda b: (b, 0))
    return pl.pallas_call(
        functools.partial(_even_attn_p_kernel, lam_init=lam_init),
        out_shape=jax.ShapeDtypeStruct((t, A_V), BF16),
        grid=(t // seq,),
        in_specs=[blk, blk, blk, _resident((4, HEAD_DIM)), _resident((1, A_VDIM))],
        out_specs=blk,
        compiler_params=_params("parallel"),
        name="even_attn_prompt",
    )(q, k, v, lam_vec, sub)


def _even_attn_s_kernel(q_ref, kl_ref, vl_ref, kc_ref, vc_ref, lam_ref, sub_ref, o_ref, k_scr, v1_scr, *, past, lam_init):
    @pl.when(pl.program_id(1) == 0)
    def _():
        k_scr[:past, :] = kc_ref[...].astype(BF16)
        k_scr[past:, :] = kl_ref[...]
        for h in range(A_HEADS):
            sl = slice(h * LANES, (h + 1) * LANES)
            v1_scr[h, :past, :LANES] = vc_ref[:, sl].astype(BF16)
            v1_scr[h, past:, :LANES] = vl_ref[:, sl]
            v1_scr[h, :, LANES:] = jnp.ones((v1_scr.shape[1], LANES), BF16)

    def head_kv(h):
        return k_scr[:, h * LANES:(h + 1) * LANES], v1_scr[h]

    _diff_attn_heads(q_ref, head_kv, lam_ref, sub_ref, o_ref, lam_init)


def _even_attn_s(q, kl, vl, kc, vc, lam_vec, sub, *, seq, past, tq, lam_init, casts=()):
    t = q.shape[0]
    nq = seq // tq
    qblk = pl.BlockSpec((tq, A_QK), lambda b, i: (b * nq + i, 0))
    lat = pl.BlockSpec((seq, A_QK), lambda b, i: (b, 0))
    ctx = pl.BlockSpec((past, A_QK), lambda b, i: (b, 0))
    in_specs = [qblk, lat, lat, ctx, ctx, _resident((4, HEAD_DIM)), _resident((1, A_VDIM))]
    c_in, c_out, c_shapes = _cast_specs(casts, (t // seq) * nq, lambda b, i: b * nq + i)
    body = functools.partial(_even_attn_s_kernel, past=past, lam_init=lam_init)
    out = pl.pallas_call(
        _with_casts(body, len(in_specs), 1, len(casts)),
        out_shape=(jax.ShapeDtypeStruct((t, A_V), BF16), *c_shapes),
        grid=(t // seq, nq),
        in_specs=in_specs + c_in,
        out_specs=(qblk, *c_out),
        scratch_shapes=[pltpu.VMEM((past + seq, A_QK), BF16),
                        pltpu.VMEM((A_HEADS, past + seq, 2 * LANES), BF16)],
        compiler_params=_params("parallel", "arbitrary"),
        name="even_attn_sample",
    )(q, kl, vl, kc, vc, lam_vec, sub, *[arr for arr, _ in casts])
    return (out[0], list(out[1:])) if casts else out[0]


def _dup_halves(x):
    lo = _lane_lo(x.shape)
    sw = pltpu.roll(x, HEAD_DIM, 1)
    return jnp.where(lo, x, sw), jnp.where(lo, sw, x)


def _ones_halves(x):
    lo = _lane_lo(x.shape)
    return jnp.where(lo, x, 1.0), jnp.where(lo, pltpu.roll(x, HEAD_DIM, 1), 1.0)


def _odd_in_kernel(x_ref, mod_ref, g_ref, w_ref, qkg_ref, *rest, row_fn, rope, keep_kv):
    rest = list(rest)
    if rope:
        cos_ref, sin_ref = rest[:2]
        rest = rest[2:]
    if keep_kv:
        q_ref, kd_ref, vd_ref, k_ref, v_ref = rest
    else:
        q_ref, kd_ref, vd_ref = rest
    nq = C_HEADS * HEAD_DIM
    nk = C_KV_HEADS * HEAD_DIM
    q_gain = qkg_ref[0:1, :] * Q_PRESCALE
    tm = x_ref.shape[0]
    ones2 = _half_sum_matrix()
    r = row_fn(pl.program_id(0))
    for rows in (slice(0, tm // 2), slice(tm // 2, tm)):
        h = _mod_norm(x_ref[rows, :], g_ref[...], _mod_row(mod_ref, 3, r), _mod_row(mod_ref, 4, r))
        y = _dot(h.astype(BF16), w_ref[...])
        if rope:
            cos, sin = cos_ref[rows, :], sin_ref[rows, :]
        for c in range(nq // LANES):
            sl = slice(c * LANES, (c + 1) * LANES)
            qs = _seg_rms_mxu(y[:, sl], q_gain, ones2)
            if rope:
                qs = _rope(qs, cos, sin)
            q_ref[rows, sl] = qs.astype(q_ref.dtype)
        for c in range(nk // LANES):
            sl = slice(c * LANES, (c + 1) * LANES)
            ks = _seg_rms_mxu(y[:, nq + c * LANES:nq + (c + 1) * LANES], qkg_ref[1:2, :], ones2)
            vs = y[:, nq + nk + c * LANES:nq + nk + (c + 1) * LANES]
            if keep_kv:
                k_ref[rows, sl] = ks
                v_ref[rows, sl] = vs
            if rope:
                ks = _rope(ks, cos, sin)
            for (d0, d1), ref in ((_dup_halves(ks), kd_ref), (_ones_halves(vs), vd_ref)):
                ref[rows, 2 * c * LANES:(2 * c + 1) * LANES] = d0.astype(ref.dtype)
                ref[rows, (2 * c + 1) * LANES:(2 * c + 2) * LANES] = d1.astype(ref.dtype)


def _odd_in(x, mod, g, w_in, qkg, rope_tabs, *, row_fn, tm, keep_kv):
    t = x.shape[0]
    rope = rope_tabs is not None
    block = tm
    in_specs = [
        pl.BlockSpec((block, D_MODEL), lambda i: (i, 0)),
        _resident((N_MOD, MOD_ROWS, D_MODEL)),
        _resident((1, D_MODEL)),
        _resident((D_MODEL, ODD_IN)),
        _resident((2, LANES)),
    ]
    args = [x, mod, g, w_in, qkg]
    if rope:
        nt = rope_tabs[0].shape[0] // block
        in_specs += [pl.BlockSpec((block, LANES), lambda i: (i % nt, 0))] * 2
        args += list(rope_tabs)
    nq = C_HEADS * HEAD_DIM
    nd = C_KV_HEADS * LANES
    nk = C_KV_HEADS * HEAD_DIM
    out_shape = [jax.ShapeDtypeStruct((t, nq), BF16), jax.ShapeDtypeStruct((t, nd), BF16),
                 jax.ShapeDtypeStruct((t, nd), BF16)]
    out_specs = [pl.BlockSpec((block, nq), lambda i: (i, 0)), pl.BlockSpec((block, nd), lambda i: (i, 0)),
                 pl.BlockSpec((block, nd), lambda i: (i, 0))]
    if keep_kv:
        out_shape += [jax.ShapeDtypeStruct((t, nk), F32)] * 2
        out_specs += [pl.BlockSpec((block, nk), lambda i: (i, 0))] * 2
    return pl.pallas_call(
        functools.partial(_odd_in_kernel, row_fn=row_fn, rope=rope, keep_kv=keep_kv),
        out_shape=tuple(out_shape),
        grid=(t // block,),
        in_specs=in_specs,
        out_specs=tuple(out_specs),
        compiler_params=_params("parallel"),
        name="odd_in",
    )(*args)


def _stack_heads(q_ref, g, rows):
    parts = []
    for rr in range(GROUP):
        c = g * (GROUP // 2) + rr // 2
        qc = q_ref[rows, c * LANES:(c + 1) * LANES]
        lo = _lane_lo(qc.shape)
        keep = lo if rr % 2 == 0 else jnp.logical_not(lo)
        parts.append(jnp.where(keep, qc, jnp.zeros_like(qc)))
    return jnp.concatenate(parts, axis=0)


def _gqa_group(scores, biases, vals, sinks, rows):
    es = [[] for _ in scores]
    sink_terms = []
    for rr in range(GROUP):
        rs = slice(rr * rows, (rr + 1) * rows)
        parts = [s[rs] if b is None else s[rs] + b for s, b in zip(scores, biases)]
        m = functools.reduce(jnp.maximum, [jnp.max(p, axis=-1, keepdims=True) for p in parts])
        m = jnp.maximum(m, sinks[rr])
        for j, p in enumerate(parts):
            es[j].append(jnp.exp2(p - m).astype(BF16))
        sink_terms.append(jnp.exp2(sinks[rr] - m))
    r = functools.reduce(jnp.add, [_dot(jnp.concatenate(e, axis=0), v) for e, v in zip(es, vals)])
    heads = []
    for rr in range(GROUP):
        rh = r[rr * rows:(rr + 1) * rows]
        heads.append(rh * (1.0 / (pltpu.roll(rh, HEAD_DIM, 1) + sink_terms[rr])))
    lo = _lane_lo((rows, LANES))
    slabs = [jnp.where(lo, heads[2 * c], pltpu.roll(heads[2 * c + 1], HEAD_DIM, 1)) for c in range(GROUP // 2)]
    return jnp.concatenate(slabs, axis=-1)


def _odd_attn_p_kernel(sink_ref, q_ref, kd_ref, vd_ref, o_ref, *, seq):
    g = pl.program_id(1)
    sinks = [sink_ref[g * GROUP + rr] * LOG2E for rr in range(GROUP)]
    units = [slice(u * seq, (u + 1) * seq) for u in range(q_ref.shape[0] // seq)]

    def scores(rows):
        return [_dot_t(_stack_heads(q_ref, 0, rows), kd_ref[rows, :])]

    s_next = scores(units[0])
    for n, rows in enumerate(units):
        s_cur = s_next
        if n + 1 < len(units):
            s_next = scores(units[n + 1])
        o_ref[rows, :] = _gqa_group(s_cur, [None], [vd_ref[rows, :]], sinks, seq).astype(o_ref.dtype)


def _odd_attn_p(q, kd, vd, sink, *, seq):
    t = q.shape[0]
    gw = GROUP * HEAD_DIM
    rows = seq * (ODD_P_SEQS if (t // seq) % ODD_P_SEQS == 0 else 1)
    return pl.pallas_call(
        functools.partial(_odd_attn_p_kernel, seq=seq),
        out_shape=jax.ShapeDtypeStruct((t, C_HEADS * HEAD_DIM), BF16),
        grid=(t // rows, C_KV_HEADS),
        in_specs=[
            pl.BlockSpec(memory_space=pltpu.SMEM),
            pl.BlockSpec((rows, gw), lambda b, g: (b, g)),
            pl.BlockSpec((rows, LANES), lambda b, g: (b, g)),
            pl.BlockSpec((rows, LANES), lambda b, g: (b, g)),
        ],
        out_specs=pl.BlockSpec((rows, gw), lambda b, g: (b, g)),
        compiler_params=_params("parallel", "parallel"),
        name="odd_attn_prompt",
    )(sink, q, kd, vd)


def _odd_attn_s_kernel(sink_ref, q_ref, kd_ref, vd_ref, kc_ref, vc_ref, bias_ref, o_ref, kcd_scr, vcd_scr, *, seq, tq):
    i = pl.program_id(1)
    sub = ODD_Q_SUB
    span = sub + 2 * WINDOW

    @pl.when(i == 0)
    def _():
        ones = jnp.ones((kc_ref.shape[1], HEAD_DIM), F32)
        for g in range(C_KV_HEADS):
            gl = slice(g * LANES, (g + 1) * LANES)
            k_g = kc_ref[0, :, g, :]
            kcd_scr[:, gl] = jnp.concatenate([k_g, k_g], axis=-1).astype(BF16)
            vcd_scr[:, gl] = jnp.concatenate([vc_ref[0, :, g, :], ones], axis=-1).astype(BF16)

    units = []
    for jb in range(tq // sub):
        q0 = i * tq + jb * sub
        start = pl.multiple_of(jnp.clip(q0 - WINDOW, 0, seq - span), WINDOW)
        bias = bias_ref[(q0 - start) // WINDOW]
        units += [(slice(jb * sub, (jb + 1) * sub), start, bias, g) for g in range(C_KV_HEADS)]

    def scores(unit):
        rows, start, _, g = unit
        gl = slice(g * LANES, (g + 1) * LANES)
        q4 = _stack_heads(q_ref, g, rows)
        return [_dot_t(q4, kcd_scr[:, gl]), _dot_t(q4, kd_ref[pl.ds(start, span), gl])]

    s_next = scores(units[0])
    for n, (rows, start, bias, g) in enumerate(units):
        s_cur = s_next
        if n + 1 < len(units):
            s_next = scores(units[n + 1])
        gl = slice(g * LANES, (g + 1) * LANES)
        sinks = [sink_ref[g * GROUP + rr] * LOG2E for rr in range(GROUP)]
        og = _gqa_group(s_cur, [None, bias], [vcd_scr[:, gl], vd_ref[pl.ds(start, span), gl]], sinks, sub)
        o_ref[rows, g * GROUP * HEAD_DIM:(g + 1) * GROUP * HEAD_DIM] = og.astype(o_ref.dtype)


def _odd_attn_s(q, kd, vd, kc, vc, sink, *, seq, past, tq):
    t = q.shape[0]
    nq = seq // tq
    nqw = C_HEADS * HEAD_DIM
    nd = C_KV_HEADS * LANES
    nk = C_KV_HEADS * HEAD_DIM
    span = ODD_Q_SUB + 2 * WINDOW
    rel = (np.arange(3)[:, None, None] * WINDOW + np.arange(ODD_Q_SUB)[None, :, None]
           - np.arange(span)[None, None, :])
    bias = jnp.asarray(np.where(np.abs(rel) <= WINDOW, 0.0, -np.inf).astype(np.float32))
    return pl.pallas_call(
        functools.partial(_odd_attn_s_kernel, seq=seq, tq=tq),
        out_shape=jax.ShapeDtypeStruct((t, nqw), BF16),
        grid=(t // seq, nq),
        in_specs=[
            pl.BlockSpec(memory_space=pltpu.SMEM),
            pl.BlockSpec((tq, nqw), lambda b, i: (b * nq + i, 0)),
            pl.BlockSpec((seq, nd), lambda b, i: (b, 0)),
            pl.BlockSpec((seq, nd), lambda b, i: (b, 0)),
            pl.BlockSpec((1, past, C_KV_HEADS, HEAD_DIM), lambda b, i: (b, 0, 0, 0)),
            pl.BlockSpec((1, past, C_KV_HEADS, HEAD_DIM), lambda b, i: (b, 0, 0, 0)),
            _resident(bias.shape),
        ],
        out_specs=pl.BlockSpec((tq, nqw), lambda b, i: (b * nq + i, 0)),
        scratch_shapes=[pltpu.VMEM((past, nd), BF16), pltpu.VMEM((past, nd), BF16)],
        compiler_params=_params("parallel", "arbitrary"),
        name="odd_attn_sample",
    )(sink, q, kd, vd, kc, vc, bias)


def _rope_tables(n):
    rows = n // GRID_W
    row = np.repeat(np.arange(rows, dtype=np.float64), GRID_W)
    col = np.tile(np.arange(GRID_W, dtype=np.float64), rows)
    inv = ROPE_BASE ** (-np.arange(0, AXIS_DIM, 2, dtype=np.float64) / AXIS_DIM)
    ang_r = row[:, None] * inv[None, :]
    ang_c = col[:, None] * inv[None, :]
    ang = np.concatenate([ang_r, ang_r, ang_c, ang_c], axis=-1)
    sign = np.where((np.arange(HEAD_DIM) & (AXIS_DIM // 2)) == 0, -1.0, 1.0)
    cos = np.tile(np.cos(ang), (1, 2)).astype(np.float32)
    sin = np.tile(np.sin(ang) * sign[None, :], (1, 2)).astype(np.float32)
    return jnp.asarray(cos), jnp.asarray(sin)


def kernel(x_prompt, x_sample, cache_even_k, cache_even_v, cache_odd_k, cache_odd_v, c, c_ctx, w_mod, b_mod, norm_g, ffn_w13, ffn_w2, even_w_in, even_w_out, even_qk_norm, even_lambda, even_subln, even_conv_w, odd_w_in, odd_w_out, odd_qk_norm, odd_sink):
    batch, seq, _ = x_prompt.shape
    dec_batch, dec_seq, _ = x_sample.shape
    past = cache_even_k.shape[2]
    depth = w_mod.shape[0]
    assert 1 + dec_batch <= MOD_ROWS and seq % SUBLANES == 0 and dec_seq % TOKEN_TILE == 0

    xp = x_prompt.reshape(batch * seq, D_MODEL)
    xs = x_sample.reshape(dec_batch * dec_seq, D_MODEL)
    tm_p = min(TOKEN_TILE, batch * seq)
    row_p = _row_fn(0, batch * seq, tm_p)
    row_s = _row_fn(1, dec_seq, TOKEN_TILE)

    c_rows = jnp.zeros((MOD_ROWS, D_MODEL), F32).at[0].set(c_ctx).at[1:1 + dec_batch].set(c)
    mods = _adaln(c_rows, w_mod, b_mod)
    rope_tabs = _rope_tables(dec_seq)

    src = {}
    for l in range(depth):
        for s in range(2):
            src["w13", l, s] = (ffn_w13, (l, s))
            src["w2", l, s] = (ffn_w2, (l, s))
        names, w_in_all, w_out_all = (("ein", "eout"), even_w_in, even_w_out) if l % 2 == 0 else (
            ("oin", "oout"), odd_w_in, odd_w_out)
        src[names[0], l // 2] = (w_in_all, (l // 2,))
        src[names[1], l // 2] = (w_out_all, (l // 2,))
    wb = {("w13", 0, 0): ffn_w13[0, 0].astype(BF16), ("w2", 0, 0): ffn_w2[0, 0].astype(BF16)}
    first = [("w13", 0, 1), ("w2", 0, 1), ("ein", 0), ("eout", 0)]
    rest = [key for key in src if key not in wb and key not in first]

    even_k, even_v, odd_k, odd_v = [], [], [], []
    for l in range(depth):
        mod = mods[l]
        g = [norm_g[l, s].reshape(1, D_MODEL) for s in range(3)]

        ffn_p = functools.partial(_ffn, mod=mod, row_fn=row_p, tm=tm_p)
        ffn_s = functools.partial(_ffn, mod=mod, row_fn=row_s, tm=TOKEN_TILE)
        xp = ffn_p(xp, g=g[0], w13=wb["w13", l, 0], w2=wb["w2", l, 0])
        if l == 0:
            xs, copies = ffn_s(xs, g=g[0], w13=wb["w13", l, 0], w2=wb["w2", l, 0], casts=[src[key] for key in first])
            wb.update(zip(first, copies))
        else:
            xs = ffn_s(xs, g=g[0], w13=wb["w13", l, 0], w2=wb["w2", l, 0])
        w13b, w2b = wb["w13", l, 1], wb["w2", l, 1]

        if l % 2 == 0:
            e = l // 2
            lam_init = 0.8 - 0.6 * math.exp(-0.3 * l)
            w_in, w_out = wb["ein", e], wb["eout", e]
            qkg = jnp.tile(even_qk_norm[e], (1, 2))
            sub = even_subln[e].reshape(1, A_VDIM)

            q, k, v, bg, u = _even_in(xp, mod, g[1], w_in, qkg, None, row_fn=row_p, tm=tm_p, kv_dtype=F32)
            o = _even_attn_p(q, k, v, even_lambda[e], sub, seq=seq, lam_init=lam_init)
            xp = ffn_p(xp, g=g[2], w13=w13b, w2=w2b, mixer=(w_out, o, bg, u, even_conv_w[e], seq))
            even_k.append(k.reshape(batch, seq, A_HEADS, 2 * HEAD_DIM))
            even_v.append(v.reshape(batch, seq, A_HEADS, A_VDIM))

            q, k, v, bg, u = _even_in(xs, mod, g[1], w_in, qkg, rope_tabs, row_fn=row_s, tm=TOKEN_TILE,
                                      kv_dtype=BF16)
            kc = cache_even_k[:, e].reshape(dec_batch * past, A_QK)
            vc = cache_even_v[:, e].reshape(dec_batch * past, A_V)
            casts = [src[key] for key in rest] if l == 0 else []
            o = _even_attn_s(q, k, v, kc, vc, even_lambda[e], sub, seq=dec_seq, past=past, tq=EVEN_Q_TILE,
                             lam_init=lam_init, casts=casts)
            if casts:
                o, copies = o
                wb.update(zip(rest, copies))
            xs = ffn_s(xs, g=g[2], w13=w13b, w2=w2b, mixer=(w_out, o, bg, u, even_conv_w[e], dec_seq))
        else:
            e = l // 2
            w_in, w_out = wb["oin", e], wb["oout", e]
            qkg = jnp.tile(odd_qk_norm[e], (1, 2))
            sink = odd_sink[e]

            q, kd, vd, k, v = _odd_in(xp, mod, g[1], w_in, qkg, None, row_fn=row_p, tm=tm_p, keep_kv=True)
            o = _odd_attn_p(q, kd, vd, sink, seq=seq)
            xp = ffn_p(xp, g=g[2], w13=w13b, w2=w2b, mixer=(w_out, o))
            odd_k.append(k.reshape(batch, seq, C_KV_HEADS, HEAD_DIM))
            odd_v.append(v.reshape(batch, seq, C_KV_HEADS, HEAD_DIM))

            q, kd, vd = _odd_in(xs, mod, g[1], w_in, qkg, rope_tabs, row_fn=row_s, tm=TOKEN_TILE, keep_kv=False)
            kc = cache_odd_k[:, e]
            vc = cache_odd_v[:, e]
            o = _odd_attn_s(q, kd, vd, kc, vc, sink, seq=dec_seq, past=past, tq=ODD_Q_TILE)
            xs = ffn_s(xs, g=g[2], w13=w13b, w2=w2b, mixer=(w_out, o))

    return (xp.reshape(batch, seq, D_MODEL), xs.reshape(dec_batch, dec_seq, D_MODEL),
            jnp.stack(even_k, axis=1), jnp.stack(even_v, axis=1),
            jnp.stack(odd_k, axis=1), jnp.stack(odd_v, axis=1))
```

```python
import functools
import math

import jax
import jax.numpy as jnp
import numpy as np
from jax import lax
from jax.experimental import pallas as pl
from jax.experimental.pallas import tpu as pltpu

F32 = jnp.float32
BF16 = jnp.bfloat16

D_MODEL = 1024
GRID_W = 64
HEAD_DIM = 64
AXIS_DIM = HEAD_DIM // 2
A_HEADS = 4
A_VDIM = 2 * HEAD_DIM
CONV_DIM = 512
C_HEADS = 16
C_KV_HEADS = 4
GROUP = C_HEADS // C_KV_HEADS
WINDOW = 128
D_FF = 2816
ROPE_BASE = 10000.0
N_MOD = 9
EPS = 1e-6
A_QK = A_HEADS * 2 * HEAD_DIM
A_V = A_HEADS * A_VDIM
EVEN_IN = 2 * A_QK + A_V + 3 * CONV_DIM
ODD_IN = (C_HEADS + 2 * C_KV_HEADS) * HEAD_DIM
LOG2E = math.log2(math.e)
Q_PRESCALE = HEAD_DIM ** -0.5 * LOG2E

LANES = 128
SUBLANES = 8
BF16_SUBLANES = 16
MOD_ROWS = 16
VMEM_LIMIT_BYTES = 56 * 2 ** 20

TOKEN_TILE = 512
EVEN_Q_TILE = 512
EVEN_Q_SUB = 256
ODD_Q_TILE = 512
ODD_Q_SUB = WINDOW
ODD_P_SEQS = 2


def _params(*semantics):
    return pltpu.CompilerParams(dimension_semantics=semantics, vmem_limit_bytes=VMEM_LIMIT_BYTES)


def _sigmoid(x):
    return 1.0 / (1.0 + jnp.exp(-x))


def _mod_norm(x, g, shift, scale):
    y = x * lax.rsqrt(jnp.mean(x * x, axis=-1, keepdims=True) + EPS)
    return (y * g) * (1.0 + scale) + shift


def _mod_row(mod_ref, k, r):
    return mod_ref[k, pl.ds(r, 1), :]


def _row_fn(row0, tokens_per_row, tile):
    return lambda i: row0 + (i * tile) // tokens_per_row


def _lane_lo(shape):
    return lax.broadcasted_iota(jnp.int32, shape, len(shape) - 1) < HEAD_DIM


def _seg_rms(xs, gain):
    lo = _lane_lo(xs.shape)
    sq = xs * xs
    s_lo = jnp.sum(jnp.where(lo, sq, 0.0), axis=-1, keepdims=True)
    s_hi = jnp.sum(jnp.where(lo, 0.0, sq), axis=-1, keepdims=True)
    inv = lax.rsqrt(jnp.where(lo, s_lo, s_hi) * (1.0 / HEAD_DIM) + EPS)
    return (xs * inv) * gain


def _half_sum_matrix():
    r = lax.broadcasted_iota(jnp.int32, (2 * LANES, LANES), 0)
    c = lax.broadcasted_iota(jnp.int32, (2 * LANES, LANES), 1)
    return jnp.where((r & HEAD_DIM) == (c & HEAD_DIM), 1.0, 0.0).astype(BF16)


def _seg_rms_mxu(xs, gain, ones2):
    sq = xs * xs
    head = sq.astype(BF16)
    rest = (sq - head.astype(F32)).astype(BF16)
    ssum = _dot(jnp.concatenate([head, rest], axis=-1), ones2)
    return (xs * lax.rsqrt(ssum * (1.0 / HEAD_DIM) + EPS)) * gain


def _rope(xs, cos, sin_signed):
    lane = lax.broadcasted_iota(jnp.int32, xs.shape, 1)
    first = (lane & (AXIS_DIM // 2)) == 0
    partner = jnp.where(first, pltpu.roll(xs, LANES - AXIS_DIM // 2, 1), pltpu.roll(xs, AXIS_DIM // 2, 1))
    return xs * cos + partner * sin_signed


def _dot_t(a, b):
    return lax.dot_general(a, b, (((1,), (1,)), ((), ())), preferred_element_type=F32)


def _dot(a, b):
    return jnp.dot(a, b, preferred_element_type=F32)


def _adaln_kernel(c_ref, w_ref, b_ref, o_ref):
    c = c_ref[...]
    s = c * _sigmoid(c)
    o_ref[0, 0] = _dot(s.astype(BF16), w_ref[0].astype(BF16)) + b_ref[0]


def _adaln(c_rows, w_mod, b_mod):
    depth = w_mod.shape[0]
    b3 = b_mod.reshape(depth * N_MOD, 1, D_MODEL)
    return pl.pallas_call(
        _adaln_kernel,
        out_shape=jax.ShapeDtypeStruct((depth, N_MOD, MOD_ROWS, D_MODEL), F32),
        grid=(depth, N_MOD),
        in_specs=[
            pl.BlockSpec((MOD_ROWS, D_MODEL), lambda l, j: (0, 0)),
            pl.BlockSpec((1, D_MODEL, D_MODEL), lambda l, j: (l, 0, j)),
            pl.BlockSpec((1, 1, D_MODEL), lambda l, j: (l * N_MOD + j, 0, 0)),
        ],
        out_specs=pl.BlockSpec((1, 1, MOD_ROWS, D_MODEL), lambda l, j: (l, j, 0, 0)),
        compiler_params=_params("parallel", "parallel"),
        name="adaln",
    )(c_rows, w_mod, b3)


def _ffn_tail(x, r, mod_ref, g_ref, w13_ref, w2_ref, k0):
    half = x.shape[0] // 2
    xs = [x[:half], x[half:]]
    hs = [_mod_norm(xr, g_ref[...], _mod_row(mod_ref, k0, r), _mod_row(mod_ref, k0 + 1, r)) for xr in xs]
    a_s = [_dot(h.astype(BF16), w13_ref[...]) for h in hs]
    acts = [(a[:, :D_FF] * _sigmoid(a[:, :D_FF])) * a[:, D_FF:] for a in a_s]
    ys = [_dot(act.astype(BF16), w2_ref[...]) for act in acts]
    gate = 0.5 * _mod_row(mod_ref, k0 + 2, r)
    return jnp.concatenate([xr + gate * y for xr, y in zip(xs, ys)], axis=0)


def _ffn_kernel(x_ref, mod_ref, g_ref, w13_ref, w2_ref, o_ref, *, row_fn):
    o_ref[...] = _ffn_tail(x_ref[...], row_fn(pl.program_id(0)), mod_ref, g_ref, w13_ref, w2_ref, 0)


def _conv_gate(bg_ref, u_ref, up_ref, un_ref, cw_ref, i, seq, tm):
    u = u_ref[...]
    row = lax.broadcasted_iota(jnp.int32, (tm, 1), 0)
    pos = (row + i * tm) % seq
    u_dn = jnp.where(row == 0, up_ref[SUBLANES - 1:SUBLANES, :], pltpu.roll(u, 1, 0))
    u_dn = jnp.where(pos == 0, 0.0, u_dn)
    u_up = jnp.where(row == tm - 1, un_ref[0:1, :], pltpu.roll(u, tm - 1, 0))
    u_up = jnp.where(pos == seq - 1, 0.0, u_up)
    return bg_ref[...] * (cw_ref[0:1, :] * u_dn + cw_ref[1:2, :] * u + cw_ref[2:3, :] * u_up)


def _even_out_ffn_kernel(x_ref, a_ref, bg_ref, u_ref, up_ref, un_ref, cw_ref, wo_ref, mod_ref, g_ref, w13_ref, w2_ref,
                         o_ref, *, row_fn, seq, tm):
    i = pl.program_id(0)
    r = row_fn(i)
    y = _conv_gate(bg_ref, u_ref, up_ref, un_ref, cw_ref, i, seq, tm)
    mix = _dot(jnp.concatenate([a_ref[...], y.astype(BF16)], axis=-1), wo_ref[...])
    x = x_ref[...] + _mod_row(mod_ref, 5, r) * mix
    o_ref[...] = _ffn_tail(x, r, mod_ref, g_ref, w13_ref, w2_ref, 6)


def _odd_out_ffn_kernel(x_ref, a_ref, wo_ref, mod_ref, g_ref, w13_ref, w2_ref, o_ref, *, row_fn):
    r = row_fn(pl.program_id(0))
    x = x_ref[...] + _mod_row(mod_ref, 5, r) * _dot(a_ref[...], wo_ref[...])
    o_ref[...] = _ffn_tail(x, r, mod_ref, g_ref, w13_ref, w2_ref, 6)


def _resident(shape, index=None):
    index = (0,) * len(shape) if index is None else index
    return pl.BlockSpec(shape, lambda *_: index, pipeline_mode=pl.Buffered(1))


def _cast_specs(casts, steps, flat_step=lambda i: i):
    in_specs, out_specs, out_shapes = [], [], []
    for arr, lead in casts:
        rows_total, cols = arr.shape[-2:]
        nblk = next(d for d in range(steps, 0, -1)
                    if steps % d == 0 and rows_total % d == 0 and (rows_total // d) % BF16_SUBLANES == 0)
        rows, rep = rows_total // nblk, steps // nblk
        in_specs.append(pl.BlockSpec((1,) * len(lead) + (rows, cols),
                                     lambda *idx, lead=lead, rep=rep: lead + (flat_step(*idx) // rep, 0)))
        out_specs.append(pl.BlockSpec((rows, cols), lambda *idx, rep=rep: (flat_step(*idx) // rep, 0)))
        out_shapes.append(jax.ShapeDtypeStruct((rows_total, cols), BF16))
    return in_specs, out_specs, out_shapes


def _with_casts(body, n_in, n_out, n_cast):
    if n_cast == 0:
        return body

    def kernel(*refs):
        outs = n_in + n_cast
        body(*refs[:n_in], *refs[outs:outs + n_out], *refs[outs + n_out + n_cast:])
        for src, dst in zip(refs[n_in:outs], refs[outs + n_out:outs + n_out + n_cast]):
            dst[...] = src[(0,) * (len(src.shape) - 2)].astype(dst.dtype)

    return kernel


def _ffn(x, mod, g, w13, w2, *, row_fn, tm, mixer=None, casts=()):
    t = x.shape[0]
    tile = lambda n: pl.BlockSpec((tm, n), lambda i: (i, 0))
    ffn_specs = [
        _resident((N_MOD, MOD_ROWS, D_MODEL)),
        _resident((1, D_MODEL)),
        _resident((D_MODEL, 2 * D_FF)),
        _resident((D_FF, D_MODEL)),
    ]
    ffn_args = [mod, g, w13, w2]
    if mixer is None:
        body = functools.partial(_ffn_kernel, row_fn=row_fn)
        in_specs, args, name = [tile(D_MODEL)], [x], "ffn"
    elif len(mixer) == 2:
        w_out, attn = mixer
        body = functools.partial(_odd_out_ffn_kernel, row_fn=row_fn)
        in_specs = [tile(D_MODEL), tile(attn.shape[1]), _resident(w_out.shape)]
        args, name = [x, attn, w_out], "odd_out_ffn"
    else:
        w_out, attn, bg, u, conv_w, seq = mixer
        nb = tm // SUBLANES
        last = t // SUBLANES - 1
        body = functools.partial(_even_out_ffn_kernel, row_fn=row_fn, seq=seq, tm=tm)
        in_specs = [
            tile(D_MODEL), tile(A_V), tile(CONV_DIM), tile(CONV_DIM),
            pl.BlockSpec((SUBLANES, CONV_DIM), lambda i: (jnp.maximum(i * nb - 1, 0), 0)),
            pl.BlockSpec((SUBLANES, CONV_DIM), lambda i: (jnp.minimum((i + 1) * nb, last), 0)),
            _resident(conv_w.shape), _resident(w_out.shape),
        ]
        args, name = [x, attn, bg, u, u, u, conv_w, w_out], "even_out_ffn"
    steps = t // tm
    c_in, c_out, c_shapes = _cast_specs(casts, steps)
    in_specs = in_specs + ffn_specs
    out = pl.pallas_call(
        _with_casts(body, len(in_specs), 1, len(casts)),
        out_shape=(jax.ShapeDtypeStruct((t, D_MODEL), F32), *c_shapes),
        grid=(steps,),
        in_specs=in_specs + c_in,
        out_specs=(tile(D_MODEL), *c_out),
        compiler_params=_params("parallel"),
        name=name,
    )(*args, *ffn_args, *[arr for arr, _ in casts])
    return (out[0], list(out[1:])) if casts else out[0]


def _even_in_kernel(x_ref, mod_ref, g_ref, w_ref, qkg_ref, *rest, row_fn, rope):
    if rope:
        cos_ref, sin_ref, q_ref, k_ref, v_ref, bg_ref, u_ref = rest
    else:
        q_ref, k_ref, v_ref, bg_ref, u_ref = rest
    r = row_fn(pl.program_id(0))
    h = _mod_norm(x_ref[...], g_ref[...], _mod_row(mod_ref, 3, r), _mod_row(mod_ref, 4, r))
    y = _dot(h.astype(BF16), w_ref[...])
    q_gain = qkg_ref[0:1, :] * Q_PRESCALE
    for hh in range(A_HEADS):
        sl = slice(hh * LANES, (hh + 1) * LANES)
        qs = _seg_rms(y[:, sl], q_gain)
        ks = _seg_rms(y[:, A_QK + hh * LANES:A_QK + (hh + 1) * LANES], qkg_ref[1:2, :])
        if rope:
            qs = _rope(qs, cos_ref[...], sin_ref[...])
            ks = _rope(ks, cos_ref[...], sin_ref[...])
        q_ref[:, sl] = qs.astype(q_ref.dtype)
        k_ref[:, sl] = ks.astype(k_ref.dtype)
    v_ref[...] = y[:, 2 * A_QK:2 * A_QK + A_V].astype(v_ref.dtype)
    o = 2 * A_QK + A_V
    bg_ref[...] = y[:, o:o + CONV_DIM]
    u_ref[...] = y[:, o + CONV_DIM:o + 2 * CONV_DIM] * y[:, o + 2 * CONV_DIM:o + 3 * CONV_DIM]


def _even_in(x, mod, g, w_in, qkg, rope_tabs, *, row_fn, tm, kv_dtype):
    t = x.shape[0]
    rope = rope_tabs is not None
    block = tm
    in_specs = [
        pl.BlockSpec((block, D_MODEL), lambda i: (i, 0)),
        _resident((N_MOD, MOD_ROWS, D_MODEL)),
        _resident((1, D_MODEL)),
        _resident((D_MODEL, EVEN_IN)),
        _resident((2, LANES)),
    ]
    args = [x, mod, g, w_in, qkg]
    if rope:
        nt = rope_tabs[0].shape[0] // block
        in_specs += [pl.BlockSpec((block, LANES), lambda i: (i % nt, 0))] * 2
        args += list(rope_tabs)
    wide = lambda dt: jax.ShapeDtypeStruct((t, A_QK), dt)
    spec = pl.BlockSpec((block, A_QK), lambda i: (i, 0))
    return pl.pallas_call(
        functools.partial(_even_in_kernel, row_fn=row_fn, rope=rope),
        out_shape=(wide(BF16), wide(kv_dtype), wide(kv_dtype), wide(F32), wide(F32)),
        grid=(t // block,),
        in_specs=in_specs,
        out_specs=(spec,) * 5,
        compiler_params=_params("parallel"),
        name="even_in",
    )(*args)


def _lambda(lam_ref, lam_init):
    lf = lam_ref[...]
    a = jnp.sum(lf[0:1] * lf[1:2], axis=-1, keepdims=True)
    b = jnp.sum(lf[2:3] * lf[3:4], axis=-1, keepdims=True)
    return jnp.exp(a) - jnp.exp(b) + lam_init


def _subln(o, sub, lam_init):
    y = o * lax.rsqrt(jnp.mean(o * o, axis=-1, keepdims=True) + EPS)
    return (y * sub) * (1.0 - lam_init)


def _softmax_rows_pv(s, v1):
    m = jnp.max(s, axis=-1, keepdims=True)
    r = _dot(jnp.exp2(s - m).astype(BF16), v1)
    return r[:, :LANES] * (1.0 / r[:, LANES:])


def _diff_attn_heads(q_ref, head_kv, lam_ref, sub_ref, o_ref, lam_init):
    lam = _lambda(lam_ref, lam_init)
    tq = q_ref.shape[0]
    sub_rows = min(tq, EVEN_Q_SUB)
    units = [(slice(rb * sub_rows, (rb + 1) * sub_rows), h)
             for rb in range(tq // sub_rows) for h in range(A_HEADS)]

    def scores(unit):
        rows, h = unit
        q = q_ref[rows, h * LANES:(h + 1) * LANES]
        k, _ = head_kv(h)
        lo = _lane_lo(q.shape)
        zero = jnp.zeros_like(q)
        return _dot_t(jnp.where(lo, q, zero), k), _dot_t(jnp.where(lo, zero, q), k)

    ahead = 2
    queue = [scores(unit) for unit in units[:ahead]]
    for n, (rows, h) in enumerate(units):
        s1, s2 = queue.pop(0)
        if n + ahead < len(units):
            queue.append(scores(units[n + ahead]))
        _, v1 = head_kv(h)
        o = _softmax_rows_pv(s1, v1) - lam * _softmax_rows_pv(s2, v1)
        o_ref[rows, h * LANES:(h + 1) * LANES] = _subln(o, sub_ref[...], lam_init).astype(o_ref.dtype)


def _even_attn_p_kernel(q_ref, k_ref, v_ref, lam_ref, sub_ref, o_ref, *, lam_init):
    ones = jnp.ones((k_ref.shape[0], LANES), BF16)

    def head_kv(h):
        sl = slice(h * LANES, (h + 1) * LANES)
        return k_ref[:, sl].astype(BF16), jnp.concatenate([v_ref[:, sl].astype(BF16), ones], axis=-1)

    _diff_attn_heads(q_ref, head_kv, lam_ref, sub_ref, o_ref, lam_init)


def _even_attn_p(q, k, v, lam_vec, sub, *, seq, lam_init):
    t = q.shape[0]
    blk = pl.BlockSpec((seq, A_QK), lambda b: (b, 0))
    return pl.pallas_call(
        functools.partial(_even_attn_p_kernel, lam_init=lam_init),
        out_shape=jax.ShapeDtypeStruct((t, A_V), BF16),
        grid=(t // seq,),
        in_specs=[blk, blk, blk, _resident((4, HEAD_DIM)), _resident((1, A_VDIM))],
        out_specs=blk,
        compiler_params=_params("parallel"),
        name="even_attn_prompt",
    )(q, k, v, lam_vec, sub)


def _even_attn_s_kernel(q_ref, kl_ref, vl_ref, kc_ref, vc_ref, lam_ref, sub_ref, o_ref, k_scr, v1_scr, *, past, lam_init):
    @pl.when(pl.program_id(1) == 0)
    def _():
        k_scr[:past, :] = kc_ref[...].astype(BF16)
        k_scr[past:, :] = kl_ref[...]
        for h in range(A_HEADS):
            sl = slice(h * LANES, (h + 1) * LANES)
            v1_scr[h, :past, :LANES] = vc_ref[:, sl].astype(BF16)
            v1_scr[h, past:, :LANES] = vl_ref[:, sl]
            v1_scr[h, :, LANES:] = jnp.ones((v1_scr.shape[1], LANES), BF16)

    def head_kv(h):
        return k_scr[:, h * LANES:(h + 1) * LANES], v1_scr[h]

    _diff_attn_heads(q_ref, head_kv, lam_ref, sub_ref, o_ref, lam_init)


def _even_attn_s(q, kl, vl, kc, vc, lam_vec, sub, *, seq, past, tq, lam_init, casts=()):
    t = q.shape[0]
    nq = seq // tq
    qblk = pl.BlockSpec((tq, A_QK), lambda b, i: (b * nq + i, 0))
    lat = pl.BlockSpec((seq, A_QK), lambda b, i: (b, 0))
    ctx = pl.BlockSpec((past, A_QK), lambda b, i: (b, 0))
    in_specs = [qblk, lat, lat, ctx, ctx, _resident((4, HEAD_DIM)), _resident((1, A_VDIM))]
    c_in, c_out, c_shapes = _cast_specs(casts, (t // seq) * nq, lambda b, i: b * nq + i)
    body = functools.partial(_even_attn_s_kernel, past=past, lam_init=lam_init)
    out = pl.pallas_call(
        _with_casts(body, len(in_specs), 1, len(casts)),
        out_shape=(jax.ShapeDtypeStruct((t, A_V), BF16), *c_shapes),
        grid=(t // seq, nq),
        in_specs=in_specs + c_in,
        out_specs=(qblk, *c_out),
        scratch_shapes=[pltpu.VMEM((past + seq, A_QK), BF16),
                        pltpu.VMEM((A_HEADS, past + seq, 2 * LANES), BF16)],
        compiler_params=_params("parallel", "arbitrary"),
        name="even_attn_sample",
    )(q, kl, vl, kc, vc, lam_vec, sub, *[arr for arr, _ in casts])
    return (out[0], list(out[1:])) if casts else out[0]


def _dup_halves(x):
    lo = _lane_lo(x.shape)
    sw = pltpu.roll(x, HEAD_DIM, 1)
    return jnp.where(lo, x, sw), jnp.where(lo, sw, x)


def _ones_halves(x):
    lo = _lane_lo(x.shape)
    return jnp.where(lo, x, 1.0), jnp.where(lo, pltpu.roll(x, HEAD_DIM, 1), 1.0)


def _odd_in_kernel(x_ref, mod_ref, g_ref, w_ref, qkg_ref, *rest, row_fn, rope, keep_kv):
    rest = list(rest)
    if rope:
        cos_ref, sin_ref = rest[:2]
        rest = rest[2:]
    if keep_kv:
        q_ref, kd_ref, vd_ref, k_ref, v_ref = rest
    else:
        q_ref, kd_ref, vd_ref = rest
    nq = C_HEADS * HEAD_DIM
    nk = C_KV_HEADS * HEAD_DIM
    q_gain = qkg_ref[0:1, :] * Q_PRESCALE
    tm = x_ref.shape[0]
    ones2 = _half_sum_matrix()
    r = row_fn(pl.program_id(0))
    for rows in (slice(0, tm // 2), slice(tm // 2, tm)):
        h = _mod_norm(x_ref[rows, :], g_ref[...], _mod_row(mod_ref, 3, r), _mod_row(mod_ref, 4, r))
        y = _dot(h.astype(BF16), w_ref[...])
        if rope:
            cos, sin = cos_ref[rows, :], sin_ref[rows, :]
        for c in range(nq // LANES):
            sl = slice(c * LANES, (c + 1) * LANES)
            qs = _seg_rms_mxu(y[:, sl], q_gain, ones2)
            if rope:
                qs = _rope(qs, cos, sin)
            q_ref[rows, sl] = qs.astype(q_ref.dtype)
        for c in range(nk // LANES):
            sl = slice(c * LANES, (c + 1) * LANES)
            ks = _seg_rms_mxu(y[:, nq + c * LANES:nq + (c + 1) * LANES], qkg_ref[1:2, :], ones2)
            vs = y[:, nq + nk + c * LANES:nq + nk + (c + 1) * LANES]
            if keep_kv:
                k_ref[rows, sl] = ks
                v_ref[rows, sl] = vs
            if rope:
                ks = _rope(ks, cos, sin)
            for (d0, d1), ref in ((_dup_halves(ks), kd_ref), (_ones_halves(vs), vd_ref)):
                ref[rows, 2 * c * LANES:(2 * c + 1) * LANES] = d0.astype(ref.dtype)
                ref[rows, (2 * c + 1) * LANES:(2 * c + 2) * LANES] = d1.astype(ref.dtype)


def _odd_in(x, mod, g, w_in, qkg, rope_tabs, *, row_fn, tm, keep_kv):
    t = x.shape[0]
    rope = rope_tabs is not None
    block = tm
    in_specs = [
        pl.BlockSpec((block, D_MODEL), lambda i: (i, 0)),
        _resident((N_MOD, MOD_ROWS, D_MODEL)),
        _resident((1, D_MODEL)),
        _resident((D_MODEL, ODD_IN)),
        _resident((2, LANES)),
    ]
    args = [x, mod, g, w_in, qkg]
    if rope:
        nt = rope_tabs[0].shape[0] // block
        in_specs += [pl.BlockSpec((block, LANES), lambda i: (i % nt, 0))] * 2
        args += list(rope_tabs)
    nq = C_HEADS * HEAD_DIM
    nd = C_KV_HEADS * LANES
    nk = C_KV_HEADS * HEAD_DIM
    out_shape = [jax.ShapeDtypeStruct((t, nq), BF16), jax.ShapeDtypeStruct((t, nd), BF16),
                 jax.ShapeDtypeStruct((t, nd), BF16)]
    out_specs = [pl.BlockSpec((block, nq), lambda i: (i, 0)), pl.BlockSpec((block, nd), lambda i: (i, 0)),
                 pl.BlockSpec((block, nd), lambda i: (i, 0))]
    if keep_kv:
        out_shape += [jax.ShapeDtypeStruct((t, nk), F32)] * 2
        out_specs += [pl.BlockSpec((block, nk), lambda i: (i, 0))] * 2
    return pl.pallas_call(
        functools.partial(_odd_in_kernel, row_fn=row_fn, rope=rope, keep_kv=keep_kv),
        out_shape=tuple(out_shape),
        grid=(t // block,),
        in_specs=in_specs,
        out_specs=tuple(out_specs),
        compiler_params=_params("parallel"),
        name="odd_in",
    )(*args)


def _stack_heads(q_ref, g, rows):
    parts = []
    for rr in range(GROUP):
        c = g * (GROUP // 2) + rr // 2
        qc = q_ref[rows, c * LANES:(c + 1) * LANES]
        lo = _lane_lo(qc.shape)
        keep = lo if rr % 2 == 0 else jnp.logical_not(lo)
        parts.append(jnp.where(keep, qc, jnp.zeros_like(qc)))
    return jnp.concatenate(parts, axis=0)


def _gqa_group(scores, biases, vals, sinks, rows):
    es = [[] for _ in scores]
    sink_terms = []
    for rr in range(GROUP):
        rs = slice(rr * rows, (rr + 1) * rows)
        parts = [s[rs] if b is None else s[rs] + b for s, b in zip(scores, biases)]
        m = functools.reduce(jnp.maximum, [jnp.max(p, axis=-1, keepdims=True) for p in parts])
        m = jnp.maximum(m, sinks[rr])
        for j, p in enumerate(parts):
            es[j].append(jnp.exp2(p - m).astype(BF16))
        sink_terms.append(jnp.exp2(sinks[rr] - m))
    r = functools.reduce(jnp.add, [_dot(jnp.concatenate(e, axis=0), v) for e, v in zip(es, vals)])
    heads = []
    for rr in range(GROUP):
        rh = r[rr * rows:(rr + 1) * rows]
        heads.append(rh * (1.0 / (pltpu.roll(rh, HEAD_DIM, 1) + sink_terms[rr])))
    lo = _lane_lo((rows, LANES))
    slabs = [jnp.where(lo, heads[2 * c], pltpu.roll(heads[2 * c + 1], HEAD_DIM, 1)) for c in range(GROUP // 2)]
    return jnp.concatenate(slabs, axis=-1)


def _odd_attn_p_kernel(sink_ref, q_ref, kd_ref, vd_ref, o_ref, *, seq):
    g = pl.program_id(1)
    sinks = [sink_ref[g * GROUP + rr] * LOG2E for rr in range(GROUP)]
    units = [slice(u * seq, (u + 1) * seq) for u in range(q_ref.shape[0] // seq)]

    def scores(rows):
        return [_dot_t(_stack_heads(q_ref, 0, rows), kd_ref[rows, :])]

    s_next = scores(units[0])
    for n, rows in enumerate(units):
        s_cur = s_next
        if n + 1 < len(units):
            s_next = scores(units[n + 1])
        o_ref[rows, :] = _gqa_group(s_cur, [None], [vd_ref[rows, :]], sinks, seq).astype(o_ref.dtype)


def _odd_attn_p(q, kd, vd, sink, *, seq):
    t = q.shape[0]
    gw = GROUP * HEAD_DIM
    rows = seq * (ODD_P_SEQS if (t // seq) % ODD_P_SEQS == 0 else 1)
    return pl.pallas_call(
        functools.partial(_odd_attn_p_kernel, seq=seq),
        out_shape=jax.ShapeDtypeStruct((t, C_HEADS * HEAD_DIM), BF16),
        grid=(t // rows, C_KV_HEADS),
        in_specs=[
            pl.BlockSpec(memory_space=pltpu.SMEM),
            pl.BlockSpec((rows, gw), lambda b, g: (b, g)),
            pl.BlockSpec((rows, LANES), lambda b, g: (b, g)),
            pl.BlockSpec((rows, LANES), lambda b, g: (b, g)),
        ],
        out_specs=pl.BlockSpec((rows, gw), lambda b, g: (b, g)),
        compiler_params=_params("parallel", "parallel"),
        name="odd_attn_prompt",
    )(sink, q, kd, vd)


def _odd_attn_s_kernel(sink_ref, q_ref, kd_ref, vd_ref, kc_ref, vc_ref, bias_ref, o_ref, kcd_scr, vcd_scr, *, seq, tq):
    i = pl.program_id(1)
    sub = ODD_Q_SUB
    span = sub + 2 * WINDOW

    @pl.when(i == 0)
    def _():
        ones = jnp.ones((kc_ref.shape[1], HEAD_DIM), F32)
        for g in range(C_KV_HEADS):
            gl = slice(g * LANES, (g + 1) * LANES)
            k_g = kc_ref[0, :, g, :]
            kcd_scr[:, gl] = jnp.concatenate([k_g, k_g], axis=-1).astype(BF16)
            vcd_scr[:, gl] = jnp.concatenate([vc_ref[0, :, g, :], ones], axis=-1).astype(BF16)

    units = []
    for jb in range(tq // sub):
        q0 = i * tq + jb * sub
        start = pl.multiple_of(jnp.clip(q0 - WINDOW, 0, seq - span), WINDOW)
        bias = bias_ref[(q0 - start) // WINDOW]
        units += [(slice(jb * sub, (jb + 1) * sub), start, bias, g) for g in range(C_KV_HEADS)]

    def scores(unit):
        rows, start, _, g = unit
        gl = slice(g * LANES, (g + 1) * LANES)
        q4 = _stack_heads(q_ref, g, rows)
        return [_dot_t(q4, kcd_scr[:, gl]), _dot_t(q4, kd_ref[pl.ds(start, span), gl])]

    ahead = 2
    queue = [scores(unit) for unit in units[:ahead]]
    for n, (rows, start, bias, g) in enumerate(units):
        s_cur = queue.pop(0)
        if n + ahead < len(units):
            queue.append(scores(units[n + ahead]))
        gl = slice(g * LANES, (g + 1) * LANES)
        sinks = [sink_ref[g * GROUP + rr] * LOG2E for rr in range(GROUP)]
        og = _gqa_group(s_cur, [None, bias], [vcd_scr[:, gl], vd_ref[pl.ds(start, span), gl]], sinks, sub)
        o_ref[rows, g * GROUP * HEAD_DIM:(g + 1) * GROUP * HEAD_DIM] = og.astype(o_ref.dtype)


def _odd_attn_s(q, kd, vd, kc, vc, sink, *, seq, past, tq):
    t = q.shape[0]
    nq = seq // tq
    nqw = C_HEADS * HEAD_DIM
    nd = C_KV_HEADS * LANES
    nk = C_KV_HEADS * HEAD_DIM
    span = ODD_Q_SUB + 2 * WINDOW
    rel = (np.arange(3)[:, None, None] * WINDOW + np.arange(ODD_Q_SUB)[None, :, None]
           - np.arange(span)[None, None, :])
    bias = jnp.asarray(np.where(np.abs(rel) <= WINDOW, 0.0, -np.inf).astype(np.float32))
    return pl.pallas_call(
        functools.partial(_odd_attn_s_kernel, seq=seq, tq=tq),
        out_shape=jax.ShapeDtypeStruct((t, nqw), BF16),
        grid=(t // seq, nq),
        in_specs=[
            pl.BlockSpec(memory_space=pltpu.SMEM),
            pl.BlockSpec((tq, nqw), lambda b, i: (b * nq + i, 0)),
            pl.BlockSpec((seq, nd), lambda b, i: (b, 0)),
            pl.BlockSpec((seq, nd), lambda b, i: (b, 0)),
            pl.BlockSpec((1, past, C_KV_HEADS, HEAD_DIM), lambda b, i: (b, 0, 0, 0)),
            pl.BlockSpec((1, past, C_KV_HEADS, HEAD_DIM), lambda b, i: (b, 0, 0, 0)),
            _resident(bias.shape),
        ],
        out_specs=pl.BlockSpec((tq, nqw), lambda b, i: (b * nq + i, 0)),
        scratch_shapes=[pltpu.VMEM((past, nd), BF16), pltpu.VMEM((past, nd), BF16)],
        compiler_params=_params("parallel", "arbitrary"),
        name="odd_attn_sample",
    )(sink, q, kd, vd, kc, vc, bias)


def _rope_tables(n):
    rows = n // GRID_W
    row = np.repeat(np.arange(rows, dtype=np.float64), GRID_W)
    col = np.tile(np.arange(GRID_W, dtype=np.float64), rows)
    inv = ROPE_BASE ** (-np.arange(0, AXIS_DIM, 2, dtype=np.float64) / AXIS_DIM)
    ang_r = row[:, None] * inv[None, :]
    ang_c = col[:, None] * inv[None, :]
    ang = np.concatenate([ang_r, ang_r, ang_c, ang_c], axis=-1)
    sign = np.where((np.arange(HEAD_DIM) & (AXIS_DIM // 2)) == 0, -1.0, 1.0)
    cos = np.tile(np.cos(ang), (1, 2)).astype(np.float32)
    sin = np.tile(np.sin(ang) * sign[None, :], (1, 2)).astype(np.float32)
    return jnp.asarray(cos), jnp.asarray(sin)


def kernel(x_prompt, x_sample, cache_even_k, cache_even_v, cache_odd_k, cache_odd_v, c, c_ctx, w_mod, b_mod, norm_g, ffn_w13, ffn_w2, even_w_in, even_w_out, even_qk_norm, even_lambda, even_subln, even_conv_w, odd_w_in, odd_w_out, odd_qk_norm, odd_sink):
    batch, seq, _ = x_prompt.shape
    dec_batch, dec_seq, _ = x_sample.shape
    past = cache_even_k.shape[2]
    depth = w_mod.shape[0]
    assert 1 + dec_batch <= MOD_ROWS and seq % SUBLANES == 0 and dec_seq % TOKEN_TILE == 0

    xp = x_prompt.reshape(batch * seq, D_MODEL)
    xs = x_sample.reshape(dec_batch * dec_seq, D_MODEL)
    tm_p = min(TOKEN_TILE, batch * seq)
    row_p = _row_fn(0, batch * seq, tm_p)
    row_s = _row_fn(1, dec_seq, TOKEN_TILE)

    c_rows = jnp.zeros((MOD_ROWS, D_MODEL), F32).at[0].set(c_ctx).at[1:1 + dec_batch].set(c)
    mods = _adaln(c_rows, w_mod, b_mod)
    rope_tabs = _rope_tables(dec_seq)

    src = {}
    for l in range(depth):
        for s in range(2):
            src["w13", l, s] = (ffn_w13, (l, s))
            src["w2", l, s] = (ffn_w2, (l, s))
        names, w_in_all, w_out_all = (("ein", "eout"), even_w_in, even_w_out) if l % 2 == 0 else (
            ("oin", "oout"), odd_w_in, odd_w_out)
        src[names[0], l // 2] = (w_in_all, (l // 2,))
        src[names[1], l // 2] = (w_out_all, (l // 2,))
    wb = {("w13", 0, 0): ffn_w13[0, 0].astype(BF16), ("w2", 0, 0): ffn_w2[0, 0].astype(BF16)}
    first = [("w13", 0, 1), ("w2", 0, 1), ("ein", 0), ("eout", 0)]
    rest = [key for key in src if key not in wb and key not in first]

    even_k, even_v, odd_k, odd_v = [], [], [], []
    for l in range(depth):
        mod = mods[l]
        g = [norm_g[l, s].reshape(1, D_MODEL) for s in range(3)]

        ffn_p = functools.partial(_ffn, mod=mod, row_fn=row_p, tm=tm_p)
        ffn_s = functools.partial(_ffn, mod=mod, row_fn=row_s, tm=TOKEN_TILE)
        xp = ffn_p(xp, g=g[0], w13=wb["w13", l, 0], w2=wb["w2", l, 0])
        if l == 0:
            xs, copies = ffn_s(xs, g=g[0], w13=wb["w13", l, 0], w2=wb["w2", l, 0], casts=[src[key] for key in first])
            wb.update(zip(first, copies))
        else:
            xs = ffn_s(xs, g=g[0], w13=wb["w13", l, 0], w2=wb["w2", l, 0])
        w13b, w2b = wb["w13", l, 1], wb["w2", l, 1]

        if l % 2 == 0:
            e = l // 2
            lam_init = 0.8 - 0.6 * math.exp(-0.3 * l)
            w_in, w_out = wb["ein", e], wb["eout", e]
            qkg = jnp.tile(even_qk_norm[e], (1, 2))
            sub = even_subln[e].reshape(1, A_VDIM)

            q, k, v, bg, u = _even_in(xp, mod, g[1], w_in, qkg, None, row_fn=row_p, tm=tm_p, kv_dtype=F32)
            o = _even_attn_p(q, k, v, even_lambda[e], sub, seq=seq, lam_init=lam_init)
            xp = ffn_p(xp, g=g[2], w13=w13b, w2=w2b, mixer=(w_out, o, bg, u, even_conv_w[e], seq))
            even_k.append(k.reshape(batch, seq, A_HEADS, 2 * HEAD_DIM))
            even_v.append(v.reshape(batch, seq, A_HEADS, A_VDIM))

            q, k, v, bg, u = _even_in(xs, mod, g[1], w_in, qkg, rope_tabs, row_fn=row_s, tm=TOKEN_TILE,
                                      kv_dtype=BF16)
            kc = cache_even_k[:, e].reshape(dec_batch * past, A_QK)
            vc = cache_even_v[:, e].reshape(dec_batch * past, A_V)
            casts = [src[key] for key in rest] if l == 0 else []
            o = _even_attn_s(q, k, v, kc, vc, even_lambda[e], sub, seq=dec_seq, past=past, tq=EVEN_Q_TILE,
                             lam_init=lam_init, casts=casts)
            if casts:
                o, copies = o
                wb.update(zip(rest, copies))
            xs = ffn_s(xs, g=g[2], w13=w13b, w2=w2b, mixer=(w_out, o, bg, u, even_conv_w[e], dec_seq))
        else:
            e = l // 2
            w_in, w_out = wb["oin", e], wb["oout", e]
            qkg = jnp.tile(odd_qk_norm[e], (1, 2))
            sink = odd_sink[e]

            q, kd, vd, k, v = _odd_in(xp, mod, g[1], w_in, qkg, None, row_fn=row_p, tm=tm_p, keep_kv=True)
            o = _odd_attn_p(q, kd, vd, sink, seq=seq)
            xp = ffn_p(xp, g=g[2], w13=w13b, w2=w2b, mixer=(w_out, o))
            odd_k.append(k.reshape(batch, seq, C_KV_HEADS, HEAD_DIM))
            odd_v.append(v.reshape(batch, seq, C_KV_HEADS, HEAD_DIM))

            q, kd, vd = _odd_in(xs, mod, g[1], w_in, qkg, rope_tabs, row_fn=row_s, tm=TOKEN_TILE, keep_kv=False)
            kc = cache_odd_k[:, e]
            vc = cache_odd_v[:, e]
            o = _odd_attn_s(q, kd, vd, kc, vc, sink, seq=dec_seq, past=past, tq=ODD_Q_TILE)
            xs = ffn_s(xs, g=g[2], w13=w13b, w2=w2b, mixer=(w_out, o))

    return (xp.reshape(batch, seq, D_MODEL), xs.reshape(dec_batch, dec_seq, D_MODEL),
            jnp.stack(even_k, axis=1), jnp.stack(even_v, axis=1),
            jnp.stack(odd_k, axis=1), jnp.stack(odd_v, axis=1))
```
